```python
import jax, jax.numpy as jnp
from jax import lax
import numpy as np

D_MODEL = 1024
BATCH = 8
SEQ = 2048
DEPTH = 2
DEC_BATCH = 128
DEC_SEQ = 1
PAST_LEN = 16384
PAGE_SIZE = 128

RET_HEADS = 4
RET_WIDTH = D_MODEL // 2
RET_DV = RET_WIDTH // RET_HEADS
RET_DK = RET_DV // 2
CONV_A_WIDTH = D_MODEL // 4
CONV_A_K = 31
CONV_C_WIDTH = D_MODEL - RET_WIDTH - CONV_A_WIDTH
CONV_C_K = 3
MIX_WIDTH = RET_WIDTH + CONV_A_WIDTH + CONV_C_WIDTH
IN_SPLITS = (RET_HEADS * RET_DK, RET_HEADS * RET_DK, RET_WIDTH, RET_WIDTH,
             CONV_A_WIDTH, CONV_A_WIDTH, CONV_C_WIDTH, CONV_C_WIDTH, CONV_C_WIDTH)
IN_WIDTH = sum(IN_SPLITS)
D_FF = ((8 * D_MODEL // 3 + 255) // 256) * 256
RET_CHUNK = 128
ROPE_BASE = 10000.0
EPS = 1e-6

kernel_name = "hybrid_retention_conformer_shortconv_decoder_step"


def _rmsnorm(x, g):
    xf = x.astype(jnp.float32)
    y = xf * lax.rsqrt(jnp.mean(xf * xf, axis=-1, keepdims=True) + EPS)
    return (y * g.astype(jnp.float32)).astype(x.dtype)


def _layernorm(x, g, b):
    xf = x.astype(jnp.float32)
    mu = jnp.mean(xf, axis=-1, keepdims=True)
    var = jnp.mean(jnp.square(xf - mu), axis=-1, keepdims=True)
    y = (xf - mu) * lax.rsqrt(var + EPS)
    return (y * g.astype(jnp.float32) + b.astype(jnp.float32)).astype(x.dtype)


def _rope(x, pos):
    half = x.shape[-1] // 2
    inv = ROPE_BASE ** (-jnp.arange(half, dtype=jnp.float32) / half)
    ang = pos[:, None] * inv[None, :]
    cos = jnp.cos(ang)[None, :, None, :]
    sin = jnp.sin(ang)[None, :, None, :]
    xf = x.astype(jnp.float32)
    x1, x2 = xf[..., :half], xf[..., half:]
    return jnp.concatenate([x1 * cos - x2 * sin, x1 * sin + x2 * cos], axis=-1).astype(x.dtype)


def _retention(q, k, v, s0, chunk):
    B, L, H, DK = q.shape
    DV = v.shape[-1]
    nc = L // chunk
    dt = q.dtype
    log_g = jnp.log(1.0 - jnp.exp2(-5.0 - jnp.arange(H, dtype=jnp.float32)))
    idx = jnp.arange(chunk, dtype=jnp.float32)
    diff = idx[:, None] - idx[None, :]
    dmat = jnp.where(diff[None] >= 0, jnp.exp(jnp.maximum(diff, 0.0)[None] * log_g[:, None, None]), 0.0).astype(dt)
    read_dec = jnp.exp((idx + 1.0)[:, None] * log_g[None, :]).astype(dt)
    upd_dec = jnp.exp((chunk - 1.0 - idx)[:, None] * log_g[None, :]).astype(dt)
    chunk_dec = jnp.exp(chunk * log_g).astype(dt)

    def to_chunks(t):
        return jnp.moveaxis(t.reshape(B, nc, chunk, H, t.shape[-1]), 1, 0)

    def step(s, inp):
        qi, ki, vi = inp
        scores = jnp.einsum('bnhd,bmhd->bhnm', qi, ki) * dmat[None]
        o = (jnp.einsum('bhnm,bmhe->bnhe', scores, vi)
             + jnp.einsum('bnhd,bhde->bnhe', qi, s) * read_dec[None, :, :, None])
        s = (s * chunk_dec[None, :, None, None]
             + jnp.einsum('bmhd,bmhe->bhde', ki * upd_dec[None, :, :, None], vi))
        return s, o

    s_new, o = lax.scan(step, s0, (to_chunks(q), to_chunks(k), to_chunks(v)))
    o = jnp.moveaxis(o, 0, 1).reshape(B, L, H, DV)
    return o, s_new


def _causal_dwconv(x, buf, w):
    K, C = w.shape
    xp = jnp.concatenate([buf, x], axis=1)
    y = lax.conv_general_dilated(xp, w[:, None, :], window_strides=(1,), padding='VALID',
                                 dimension_numbers=('NWC', 'WIO', 'NWC'), feature_group_count=C)
    return y, xp[:, -(K - 1):, :]


def _layer(x, c, s_ret, buf_a, buf_c, pos,
           ada_w, ada_b, g_pre_mix, g_post_mix, g_pre_ffn, g_post_ffn,
           w_in, w_out, ret_gn_g, conv_a_w, conv_a_b, conv_a_ln_g, conv_a_ln_b,
           conv_c_w, ffn_w1, ffn_w3, ffn_w2):
    B, L, _ = x.shape
    mod = jax.nn.silu(c) @ ada_w + ada_b
    sh_m, sc_m, gt_m, sh_f, sc_f, gt_f = [m[:, None, :] for m in jnp.split(mod, 6, axis=-1)]

    h = _rmsnorm(x, g_pre_mix) * (1.0 + sc_m) + sh_m
    proj = h @ w_in
    bounds = np.cumsum(IN_SPLITS)[:-1].tolist()
    q, k, v, g, a_val, a_gate, cb, cc, cx = jnp.split(proj, bounds, axis=-1)

    q = _rope(q.reshape(B, L, RET_HEADS, RET_DK), pos)
    k = _rope(k.reshape(B, L, RET_HEADS, RET_DK), pos) * (RET_DK ** -0.5)
    v = v.reshape(B, L, RET_HEADS, RET_DV)
    chunk = RET_CHUNK if L % RET_CHUNK == 0 else L
    o_ret, s_ret_new = _retention(q, k, v, s_ret, chunk)
    o_ret = _layernorm(o_ret, ret_gn_g.reshape(RET_HEADS, RET_DV), jnp.zeros_like(ret_gn_g).reshape(RET_HEADS, RET_DV))
    o_ret = o_ret.reshape(B, L, RET_WIDTH) * jax.nn.silu(g)

    u = a_val * jax.nn.sigmoid(a_gate)
    ua, buf_a_new = _causal_dwconv(u, buf_a, conv_a_w)
    o_a = jax.nn.silu(_layernorm(ua + conv_a_b, conv_a_ln_g, conv_a_ln_b))

    z = cc * cx
    zc, buf_c_new = _causal_dwconv(z, buf_c, conv_c_w)
    o_c = cb * zc

    mix = jnp.concatenate([o_ret, o_a, o_c], axis=-1) @ w_out
    x = x + gt_m * _rmsnorm(mix, g_post_mix)

    h = _rmsnorm(x, g_pre_ffn) * (1.0 + sc_f) + sh_f
    f = (jax.nn.silu(h @ ffn_w1) * (h @ ffn_w3)) @ ffn_w2
    x = x + gt_f * _rmsnorm(f, g_post_ffn)
    return x, s_ret_new, buf_a_new, buf_c_new


def setup_inputs(seed: int = 0) -> dict:
    key = jax.random.key(seed)
    ks = jax.random.split(key, 24)
    f32 = jnp.float32
    n = lambda k, shape, s: (jax.random.normal(k, shape, f32) * s)
    return {
        "x_prompt": n(ks[0], (BATCH, SEQ, D_MODEL), 1.0),
        "x_sample": n(ks[1], (DEC_BATCH, DEC_SEQ, D_MODEL), 1.0),
        "c_prompt": n(ks[2], (BATCH, D_MODEL), 1.0),
        "c_sample": n(ks[3], (DEC_BATCH, D_MODEL), 1.0),
        "state_ret": n(ks[4], (DEPTH, DEC_BATCH, RET_HEADS, RET_DK, RET_DV), 0.5),
        "state_conv_a": n(ks[5], (DEPTH, DEC_BATCH, CONV_A_K - 1, CONV_A_WIDTH), 0.5),
        "state_conv_c": n(ks[6], (DEPTH, DEC_BATCH, CONV_C_K - 1, CONV_C_WIDTH), 0.5),
        "ada_w": n(ks[7], (DEPTH, D_MODEL, 6 * D_MODEL), 0.5 * D_MODEL ** -0.5),
        "ada_b": n(ks[8], (DEPTH, 6 * D_MODEL), 0.01),
        "norm_pre_mix": 1.0 + n(ks[9], (DEPTH, D_MODEL), 0.05),
        "norm_post_mix": 1.0 + n(ks[10], (DEPTH, D_MODEL), 0.05),
        "norm_pre_ffn": 1.0 + n(ks[11], (DEPTH, D_MODEL), 0.05),
        "norm_post_ffn": 1.0 + n(ks[12], (DEPTH, D_MODEL), 0.05),
        "w_in": n(ks[13], (DEPTH, D_MODEL, IN_WIDTH), D_MODEL ** -0.5),
        "w_out": n(ks[14], (DEPTH, MIX_WIDTH, D_MODEL), MIX_WIDTH ** -0.5),
        "ret_gn_g": 1.0 + n(ks[15], (DEPTH, RET_WIDTH), 0.05),
        "conv_a_w": n(ks[16], (DEPTH, CONV_A_K, CONV_A_WIDTH), CONV_A_K ** -0.5),
        "conv_a_b": n(ks[17], (DEPTH, CONV_A_WIDTH), 0.01),
        "conv_a_ln_g": 1.0 + n(ks[18], (DEPTH, CONV_A_WIDTH), 0.05),
        "conv_a_ln_b": n(ks[19], (DEPTH, CONV_A_WIDTH), 0.01),
        "conv_c_w": n(ks[20], (DEPTH, CONV_C_K, CONV_C_WIDTH), CONV_C_K ** -0.5),
        "ffn_w1": n(ks[21], (DEPTH, D_MODEL, D_FF), D_MODEL ** -0.5),
        "ffn_w3": n(ks[22], (DEPTH, D_MODEL, D_FF), D_MODEL ** -0.5),
        "ffn_w2": n(ks[23], (DEPTH, D_FF, D_MODEL), D_FF ** -0.5),
    }


def reference(x_prompt, x_sample, c_prompt, c_sample, state_ret, state_conv_a, state_conv_c,
              ada_w, ada_b, norm_pre_mix, norm_post_mix, norm_pre_ffn, norm_post_ffn,
              w_in, w_out, ret_gn_g, conv_a_w, conv_a_b, conv_a_ln_g, conv_a_ln_b,
              conv_c_w, ffn_w1, ffn_w3, ffn_w2):
    Bp, Lp, _ = x_prompt.shape
    Ls = x_sample.shape[1]
    dt = x_prompt.dtype
    pos_p = jnp.arange(Lp, dtype=jnp.float32)
    pos_s = PAST_LEN + jnp.arange(Ls, dtype=jnp.float32)
    yp, ys = x_prompt, x_sample
    ret_p, ca_p, cc_p, ret_s, ca_s, cc_s = [], [], [], [], [], []
    for l in range(DEPTH):
        params = (ada_w[l], ada_b[l], norm_pre_mix[l], norm_post_mix[l], norm_pre_ffn[l],
                  norm_post_ffn[l], w_in[l], w_out[l], ret_gn_g[l], conv_a_w[l], conv_a_b[l],
                  conv_a_ln_g[l], conv_a_ln_b[l], conv_c_w[l], ffn_w1[l], ffn_w3[l], ffn_w2[l])
        s0 = jnp.zeros((Bp, RET_HEADS, RET_DK, RET_DV), dt)
        ba0 = jnp.zeros((Bp, CONV_A_K - 1, CONV_A_WIDTH), dt)
        bc0 = jnp.zeros((Bp, CONV_C_K - 1, CONV_C_WIDTH), dt)
        yp, sr, ba, bc = _layer(yp, c_prompt, s0, ba0, bc0, pos_p, *params)
        ret_p.append(sr); ca_p.append(ba); cc_p.append(bc)
        ys, sr, ba, bc = _layer(ys, c_sample, state_ret[l], state_conv_a[l], state_conv_c[l], pos_s, *params)
        ret_s.append(sr); ca_s.append(ba); cc_s.append(bc)
    return (yp, ys, jnp.stack(ret_p), jnp.stack(ca_p), jnp.stack(cc_p),
            jnp.stack(ret_s), jnp.stack(ca_s), jnp.stack(cc_s))
```

```python
import functools

import numpy as np
import jax
import jax.numpy as jnp
from jax import lax
from jax.experimental import pallas as pl
from jax.experimental.pallas import tpu as pltpu

D_MODEL = 1024
DEPTH = 2
PAST_LEN = 16384
RET_HEADS = 4
RET_WIDTH = D_MODEL // 2
RET_DV = RET_WIDTH // RET_HEADS
RET_DK = RET_DV // 2
QK_WIDTH = RET_HEADS * RET_DK
CONV_A_WIDTH = D_MODEL // 4
CONV_A_K = 31
CONV_C_WIDTH = D_MODEL - RET_WIDTH - CONV_A_WIDTH
CONV_C_K = 3
IN_WIDTH = 2 * QK_WIDTH + 2 * RET_WIDTH + 2 * CONV_A_WIDTH + 3 * CONV_C_WIDTH
D_FF = ((8 * D_MODEL // 3 + 255) // 256) * 256
RET_CHUNK = 128
ROPE_BASE = 10000.0
EPS = 1e-6

OFF_Q = 0
OFF_K = OFF_Q + QK_WIDTH
OFF_V = OFF_K + QK_WIDTH
OFF_G = OFF_V + RET_WIDTH
OFF_AV = OFF_G + RET_WIDTH
OFF_AG = OFF_AV + CONV_A_WIDTH
OFF_CB = OFF_AG + CONV_A_WIDTH
OFF_CC = OFF_CB + CONV_C_WIDTH
OFF_CX = OFF_CC + CONV_C_WIDTH
MIX_A = RET_WIDTH
MIX_C = RET_WIDTH + CONV_A_WIDTH

LANES = 128
V7X_VMEM_LIMIT_BYTES = 56 * 1024 * 1024

TOK_TILE = 512
ROW_BLK = 32
CONV_BLK = 64
HIST_A = 32
HIST_C = 8
SEQ_BLK = 8
FF_BLK = 256

F32 = jnp.float32
BF16 = jnp.bfloat16


def _sigmoid(x):
    return jax.nn.sigmoid(x)


def _silu(x):
    return x * _sigmoid(x)


def _rms(x):
    return x * lax.rsqrt(jnp.mean(x * x, axis=-1, keepdims=True) + EPS)


def _layernorm(x, g, b=None):
    mu = jnp.mean(x, axis=-1, keepdims=True)
    d = x - mu
    var = jnp.mean(d * d, axis=-1, keepdims=True)
    y = d * lax.rsqrt(var + EPS) * g
    return y if b is None else y + b


def _rope_half(x, cos, sin, first_half):
    partner = jnp.where(first_half, pltpu.roll(x, 96, 1), pltpu.roll(x, 32, 1))
    return x * cos + partner * sin


def _first_half_mask(rows):
    lane = lax.broadcasted_iota(jnp.int32, (rows, LANES), 1)
    return (lane & (RET_DK - 1)) < (RET_DK // 2), lane < RET_DK


def _rope_tables(pos, k_scale):
    half = RET_DK // 2
    inv = ROPE_BASE ** (-np.arange(half, dtype=np.float64) / half)
    ang = np.asarray(pos, np.float64)[:, None] * inv[None, :]
    cos = np.tile(np.cos(ang), (1, 4))
    sin = np.tile(np.concatenate([-np.sin(ang), np.sin(ang)], axis=1), (1, 2))
    return (cos * k_scale).astype(np.float32), (sin * k_scale).astype(np.float32)


def _decay_tables(chunk):
    log_g = np.log(1.0 - np.exp2(-5.0 - np.arange(RET_HEADS, dtype=np.float64)))
    idx = np.arange(chunk, dtype=np.float64)
    diff = idx[:, None] - idx[None, :]
    dmat = np.where(diff[None] >= 0, np.exp(np.maximum(diff, 0.0)[None] * log_g[:, None, None]), 0.0)
    read_dec = np.exp((idx + 1.0)[:, None] * log_g[None, :])
    upd_dec = np.exp((chunk - 1.0 - idx)[:, None] * log_g[None, :])
    chunk_dec = np.exp(chunk * log_g)
    rd = np.broadcast_to(read_dec.T[:, :, None], (RET_HEADS, chunk, RET_DV))
    upd = np.repeat(upd_dec, RET_DK, axis=1).reshape(chunk, 2, LANES).transpose(1, 0, 2)
    cd = np.broadcast_to(chunk_dec[:, None, None], (RET_HEADS, 1, RET_DV))
    f = lambda a: np.ascontiguousarray(a, dtype=np.float32)
    return f(dmat), f(rd), f(upd), f(cd)


def _ada_kernel(c_ref, w_ref, b_ref, o_ref):
    s = _silu(c_ref[...]).astype(BF16)
    o_ref[...] = jnp.dot(s, w_ref[...].astype(BF16), preferred_element_type=F32) + b_ref[...]


def _ada_modulation(c_all, ada_w, ada_b):
    rows = c_all.shape[0]
    ncol = 6 * D_MODEL
    blk = D_MODEL
    return pl.pallas_call(
        _ada_kernel,
        grid=(DEPTH, ncol // blk),
        in_specs=[
            pl.BlockSpec((rows, D_MODEL), lambda l, j: (0, 0)),
            pl.BlockSpec((None, D_MODEL, blk), lambda l, j: (l, 0, j)),
            pl.BlockSpec((None, 1, blk), lambda l, j: (l, 0, j)),
        ],
        out_specs=pl.BlockSpec((None, rows, blk), lambda l, j: (l, 0, j)),
        out_shape=jax.ShapeDtypeStruct((DEPTH, rows, ncol), F32),
        compiler_params=pltpu.CompilerParams(
            dimension_semantics=("arbitrary", "arbitrary"), vmem_limit_bytes=V7X_VMEM_LIMIT_BYTES),
        name="ada_mod",
    )(c_all, ada_w, ada_b.reshape(DEPTH, 1, ncol))


def _prompt_mix_kernel(x_ref, mod_ref, gpre_ref, gpost_ref, win_ref, wout_ref,
                       cq_ref, sq_ref, ck_ref, sk_ref, dmat_ref, rd_ref, upd_ref, cd_ref,
                       gn_ref, caw_ref, cab_ref, lng_ref, lnb_ref, ccw_ref,
                       y_ref, sret_ref, sca_ref, scc_ref,
                       h_scr, proj_scr, mixin_scr, ubuf, zbuf, *, tile, n_tiles):
    c = pl.program_id(1)

    @pl.when(c == 0)
    def _():
        sret_ref[...] = jnp.zeros_like(sret_ref)
        ubuf[0:HIST_A, :] = jnp.zeros((HIST_A, CONV_A_WIDTH), F32)
        zbuf[0:HIST_C, :] = jnp.zeros((HIST_C, CONV_C_WIDTH), F32)

    sh = mod_ref[0:1, :]
    sc1 = 1.0 + mod_ref[1:2, :]
    gt = mod_ref[2:3, :]

    for r in range(tile // ROW_BLK):
        rows = pl.ds(r * ROW_BLK, ROW_BLK)
        h = _rms(x_ref[rows, :]) * gpre_ref[...] * sc1 + sh
        h_scr[rows, :] = h.astype(BF16)

    proj_scr[...] = jnp.dot(h_scr[...], win_ref[...], preferred_element_type=F32)

    for r in range(tile // CONV_BLK):
        rows = pl.ds(r * CONV_BLK, CONV_BLK)
        u = proj_scr[rows, OFF_AV:OFF_AV + CONV_A_WIDTH] * _sigmoid(proj_scr[rows, OFF_AG:OFF_AG + CONV_A_WIDTH])
        ubuf[pl.ds(HIST_A + r * CONV_BLK, CONV_BLK), :] = u
    first_tap = HIST_A - (CONV_A_K - 1)
    for r in range(tile // CONV_BLK):
        acc = jnp.zeros((CONV_BLK, CONV_A_WIDTH), F32)
        for j in range(CONV_A_K):
            acc = acc + caw_ref[j:j + 1, :] * ubuf[pl.ds(r * CONV_BLK + first_tap + j, CONV_BLK), :]
        o_a = _silu(_layernorm(acc + cab_ref[...], lng_ref[...], lnb_ref[...]))
        mixin_scr[pl.ds(r * CONV_BLK, CONV_BLK), MIX_A:MIX_A + CONV_A_WIDTH] = o_a.astype(BF16)

    @pl.when(c == n_tiles - 1)
    def _():
        sca_ref[...] = ubuf[pl.ds(tile + first_tap, CONV_A_K - 1), :]

    ubuf[0:HIST_A, :] = ubuf[pl.ds(tile, HIST_A), :]

    for r in range(tile // CONV_BLK):
        rows = pl.ds(r * CONV_BLK, CONV_BLK)
        z = proj_scr[rows, OFF_CC:OFF_CC + CONV_C_WIDTH] * proj_scr[rows, OFF_CX:OFF_CX + CONV_C_WIDTH]
        zbuf[pl.ds(HIST_C + r * CONV_BLK, CONV_BLK), :] = z
    first_tap_c = HIST_C - (CONV_C_K - 1)
    for r in range(tile // CONV_BLK):
        rows = pl.ds(r * CONV_BLK, CONV_BLK)
        zc = jnp.zeros((CONV_BLK, CONV_C_WIDTH), F32)
        for j in range(CONV_C_K):
            zc = zc + ccw_ref[j:j + 1, :] * zbuf[pl.ds(r * CONV_BLK + first_tap_c + j, CONV_BLK), :]
        o_c = proj_scr[rows, OFF_CB:OFF_CB + CONV_C_WIDTH] * zc
        mixin_scr[rows, MIX_C:MIX_C + CONV_C_WIDTH] = o_c.astype(BF16)

    @pl.when(c == n_tiles - 1)
    def _():
        scc_ref[...] = zbuf[pl.ds(tile + first_tap_c, CONV_C_K - 1), :]

    zbuf[0:HIST_C, :] = zbuf[pl.ds(tile, HIST_C), :]

    first_half, head_lo = _first_half_mask(RET_CHUNK)
    nt = (((1,), (1,)), ((), ()))
    for i in range(tile // RET_CHUNK):
        rows = pl.ds(i * RET_CHUNK, RET_CHUNK)
        cq, sq, ck, sk = cq_ref[rows, :], sq_ref[rows, :], ck_ref[rows, :], sk_ref[rows, :]
        for pair in range(RET_HEADS // 2):
            qr = _rope_half(proj_scr[rows, pl.ds(OFF_Q + pair * LANES, LANES)], cq, sq, first_half)
            kr = _rope_half(proj_scr[rows, pl.ds(OFF_K + pair * LANES, LANES)], ck, sk, first_half)
            kr_b = kr.astype(BF16)
            ku_t = (kr * upd_ref[pair]).T.astype(BF16)
            s_pair = sret_ref[2 * pair:2 * pair + 2].reshape(2 * RET_DK, RET_DV).astype(BF16)
            for hl in range(2):
                h = 2 * pair + hl
                qm = jnp.where(head_lo if hl == 0 else jnp.logical_not(head_lo), qr, 0.0).astype(BF16)
                scores = lax.dot_general(qm, kr_b, nt, preferred_element_type=F32) * dmat_ref[h]
                vh = proj_scr[rows, pl.ds(OFF_V + h * RET_DV, RET_DV)].astype(BF16)
                o = (jnp.dot(scores.astype(BF16), vh, preferred_element_type=F32)
                     + jnp.dot(qm, s_pair, preferred_element_type=F32) * rd_ref[h])
                sret_ref[h] = (sret_ref[h] * cd_ref[h]
                               + jnp.dot(ku_t[hl * RET_DK:(hl + 1) * RET_DK, :], vh, preferred_element_type=F32))
                gate = proj_scr[rows, pl.ds(OFF_G + h * RET_DV, RET_DV)]
                o = _layernorm(o, gn_ref[:, pl.ds(h * RET_DV, RET_DV)]) * _silu(gate)
                mixin_scr[rows, pl.ds(h * RET_DV, RET_DV)] = o.astype(BF16)

    proj_scr[:, 0:D_MODEL] = jnp.dot(mixin_scr[...], wout_ref[...], preferred_element_type=F32)
    for r in range(tile // ROW_BLK):
        rows = pl.ds(r * ROW_BLK, ROW_BLK)
        y_ref[rows, :] = x_ref[rows, :] + gt * (_rms(proj_scr[rows, 0:D_MODEL]) * gpost_ref[...])


def _const_spec(shape, grid_rank, single_buffer=True):
    zeros = (0,) * len(shape)
    idx = (lambda b, c: zeros) if grid_rank == 2 else (lambda j: zeros)
    if single_buffer:
        return pl.BlockSpec(shape, idx, pipeline_mode=pl.Buffered(1))
    return pl.BlockSpec(shape, idx)


def _prompt_mix(x, mod_p, gpre, gpost, w_in, w_out, tables, gn, caw, cab, lng, lnb, ccw):
    bsz, seq, _ = x.shape
    tile = TOK_TILE
    n_tiles = seq // tile
    cq, sq, ck, sk, dmat, rd, upd, cd = tables
    tok = pl.BlockSpec((None, tile, D_MODEL), lambda b, c: (b, c, 0))
    rope = pl.BlockSpec((tile, LANES), lambda b, c: (c, 0))
    cs = functools.partial(_const_spec, grid_rank=2)
    kern = functools.partial(_prompt_mix_kernel, tile=tile, n_tiles=n_tiles)
    return pl.pallas_call(
        kern,
        grid=(bsz, n_tiles),
        in_specs=[
            tok,
            pl.BlockSpec((None, 6, D_MODEL), lambda b, c: (b, 0, 0)),
            cs((1, D_MODEL)), cs((1, D_MODEL)),
            cs((D_MODEL, IN_WIDTH)), cs((D_MODEL, D_MODEL)),
            rope, rope, rope, rope,
            cs(dmat.shape), cs(rd.shape), cs(upd.shape), cs(cd.shape),
            cs((1, RET_WIDTH)), cs((CONV_A_K, CONV_A_WIDTH)), cs((1, CONV_A_WIDTH)),
            cs((1, CONV_A_WIDTH)), cs((1, CONV_A_WIDTH)), cs((CONV_C_K, CONV_C_WIDTH)),
        ],
        out_specs=[
            tok,
            pl.BlockSpec((None, RET_HEADS, RET_DK, RET_DV), lambda b, c: (b, 0, 0, 0)),
            pl.BlockSpec((None, CONV_A_K - 1, CONV_A_WIDTH), lambda b, c: (b, 0, 0)),
            pl.BlockSpec((None, CONV_C_K - 1, CONV_C_WIDTH), lambda b, c: (b, 0, 0)),
        ],
        out_shape=[
            jax.ShapeDtypeStruct(x.shape, F32),
            jax.ShapeDtypeStruct((bsz, RET_HEADS, RET_DK, RET_DV), F32),
            jax.ShapeDtypeStruct((bsz, CONV_A_K - 1, CONV_A_WIDTH), F32),
            jax.ShapeDtypeStruct((bsz, CONV_C_K - 1, CONV_C_WIDTH), F32),
        ],
        scratch_shapes=[
            pltpu.VMEM((tile, D_MODEL), BF16),
            pltpu.VMEM((tile, IN_WIDTH), F32),
            pltpu.VMEM((tile, D_MODEL), BF16),
            pltpu.VMEM((tile + HIST_A, CONV_A_WIDTH), F32),
            pltpu.VMEM((tile + HIST_C, CONV_C_WIDTH), F32),
        ],
        compiler_params=pltpu.CompilerParams(
            dimension_semantics=("arbitrary", "arbitrary"), vmem_limit_bytes=V7X_VMEM_LIMIT_BYTES),
        name="prompt_mix",
    )(x, mod_p, gpre, gpost, w_in, w_out, cq, sq, ck, sk, dmat, rd, upd, cd, gn, caw, cab, lng, lnb, ccw)


def _swiglu_cols(h, w13_ref, col0, ncols):
    r = jnp.dot(h, w13_ref[:, pl.ds(2 * col0, 2 * ncols)], preferred_element_type=F32)
    outs = []
    for t in range(ncols // LANES):
        a = r[:, 2 * t * LANES:(2 * t + 1) * LANES]
        b = r[:, (2 * t + 1) * LANES:(2 * t + 2) * LANES]
        outs.append((_silu(a) * b).astype(BF16))
    return outs


def _prompt_ffn_kernel(x_ref, mod_ref, gpre_ref, gpost_ref, w13_ref, w2_ref, y_ref,
                       h_scr, p_scr, f_scr, *, tile):
    sh = mod_ref[3:4, :]
    sc1 = 1.0 + mod_ref[4:5, :]
    gt = mod_ref[5:6, :]
    for r in range(tile // ROW_BLK):
        rows = pl.ds(r * ROW_BLK, ROW_BLK)
        h = _rms(x_ref[rows, :]) * gpre_ref[...] * sc1 + sh
        h_scr[rows, :] = h.astype(BF16)
    blk = 2 * LANES
    for j in range(D_FF // blk):
        outs = _swiglu_cols(h_scr[...], w13_ref, j * blk, blk)
        for t, p in enumerate(outs):
            p_scr[:, pl.ds(j * blk + t * LANES, LANES)] = p
    f_scr[...] = jnp.dot(p_scr[...], w2_ref[...], preferred_element_type=F32)
    for r in range(tile // ROW_BLK):
        rows = pl.ds(r * ROW_BLK, ROW_BLK)
        y_ref[rows, :] = x_ref[rows, :] + gt * (_rms(f_scr[rows, :]) * gpost_ref[...])


def _prompt_ffn(x, mod_p, gpre, gpost, w13, w2):
    bsz, seq, _ = x.shape
    tile = TOK_TILE
    tok = pl.BlockSpec((None, tile, D_MODEL), lambda b, c: (b, c, 0))
    cs = functools.partial(_const_spec, grid_rank=2)
    return pl.pallas_call(
        functools.partial(_prompt_ffn_kernel, tile=tile),
        grid=(bsz, seq // tile),
        in_specs=[
            tok,
            pl.BlockSpec((None, 6, D_MODEL), lambda b, c: (b, 0, 0)),
            cs((1, D_MODEL)), cs((1, D_MODEL)),
            cs((D_MODEL, 2 * D_FF)), cs((D_FF, D_MODEL)),
        ],
        out_specs=tok,
        out_shape=jax.ShapeDtypeStruct(x.shape, F32),
        scratch_shapes=[
            pltpu.VMEM((tile, D_MODEL), BF16),
            pltpu.VMEM((tile, D_FF), BF16),
            pltpu.VMEM((tile, D_MODEL), F32),
        ],
        compiler_params=pltpu.CompilerParams(
            dimension_semantics=("arbitrary", "arbitrary"), vmem_limit_bytes=V7X_VMEM_LIMIT_BYTES),
        name="prompt_ffn",
    )(x, mod_p, gpre, gpost, w13, w2)


def _sample_pre_kernel(x_ref, mod_ref, gpre_ref, win_ref, rope_ref, proj_ref, qt_ref, kt_ref):
    n = x_ref.shape[0]
    sh = mod_ref[:, 0:D_MODEL]
    sc1 = 1.0 + mod_ref[:, D_MODEL:2 * D_MODEL]
    h = (_rms(x_ref[...]) * gpre_ref[...] * sc1 + sh).astype(BF16)
    proj_ref[...] = jnp.dot(h, win_ref[...], preferred_element_type=F32)
    first_half, _ = _first_half_mask(n)
    cq, sq, ck, sk = rope_ref[0:1, :], rope_ref[1:2, :], rope_ref[2:3, :], rope_ref[3:4, :]
    for pair in range(RET_HEADS // 2):
        ql = pl.ds(OFF_Q + pair * LANES, LANES)
        kl = pl.ds(OFF_K + pair * LANES, LANES)
        qr = _rope_half(proj_ref[:, ql], cq, sq, first_half)
        kr = _rope_half(proj_ref[:, kl], ck, sk, first_half)
        qt_ref[pl.ds(pair * LANES, LANES), :] = qr.T
        kt_ref[pl.ds(pair * LANES, LANES), :] = kr.T


def _sample_pre(xs, mod_s, gpre, w_in, rope_s):
    n = xs.shape[0]
    cs = functools.partial(_const_spec, grid_rank=1, single_buffer=False)
    return pl.pallas_call(
        _sample_pre_kernel,
        grid=(1,),
        in_specs=[cs((n, D_MODEL)), cs((n, 6 * D_MODEL)), cs((1, D_MODEL)),
                  cs((D_MODEL, IN_WIDTH)), cs(rope_s.shape)],
        out_specs=[cs((n, IN_WIDTH)), cs((QK_WIDTH, n)), cs((QK_WIDTH, n))],
        out_shape=[jax.ShapeDtypeStruct((n, IN_WIDTH), F32),
                   jax.ShapeDtypeStruct((QK_WIDTH, n), F32),
                   jax.ShapeDtypeStruct((QK_WIDTH, n), F32)],
        compiler_params=pltpu.CompilerParams(
            dimension_semantics=("arbitrary",), vmem_limit_bytes=V7X_VMEM_LIMIT_BYTES),
        name="sample_pre",
    )(xs, mod_s, gpre, w_in, rope_s)


def _sample_state_kernel(proj_ref, qt_ref, kt_ref, sin_ref, bufa_ref, bufc_ref, cd_ref,
                         gn_ref, caw_ref, cab_ref, lng_ref, lnb_ref, ccw_ref,
                         mix_ref, sout_ref, outa_ref, outc_ref,
                         o_scr, ua_scr, zc_scr):
    blk = pl.program_id(0)
    n = qt_ref.shape[1]
    lane = lax.broadcasted_iota(jnp.int32, (QK_WIDTH, n), 1)

    for bl in range(SEQ_BLK):
        onehot = lane == (blk * SEQ_BLK + bl)
        qcol = jnp.sum(jnp.where(onehot, qt_ref[...], 0.0), axis=1, keepdims=True)
        kcol = jnp.sum(jnp.where(onehot, kt_ref[...], 0.0), axis=1, keepdims=True)
        row = pl.ds(bl, 1)
        for h in range(RET_HEADS):
            vrow = proj_ref[row, pl.ds(OFF_V + h * RET_DV, RET_DV)]
            s_new = (sin_ref[bl, h] * cd_ref[h]
                     + kcol[h * RET_DK:(h + 1) * RET_DK, :] * vrow)
            sout_ref[bl, h] = s_new
            o_scr[row, pl.ds(h * RET_DV, RET_DV)] = jnp.sum(
                qcol[h * RET_DK:(h + 1) * RET_DK, :] * s_new, axis=0, keepdims=True)
        urow = proj_ref[row, OFF_AV:OFF_AV + CONV_A_WIDTH] * _sigmoid(proj_ref[row, OFF_AG:OFF_AG + CONV_A_WIDTH])
        ba = bufa_ref[bl]
        ua_scr[row, :] = (jnp.sum(ba * caw_ref[0:CONV_A_K - 1, :], axis=0, keepdims=True)
                          + caw_ref[CONV_A_K - 1:CONV_A_K, :] * urow)
        outa_ref[bl, 0:CONV_A_K - 2, :] = ba[1:CONV_A_K - 1, :]
        outa_ref[bl, CONV_A_K - 2:CONV_A_K - 1, :] = urow
        zrow = proj_ref[row, OFF_CC:OFF_CC + CONV_C_WIDTH] * proj_ref[row, OFF_CX:OFF_CX + CONV_C_WIDTH]
        bc = bufc_ref[bl]
        zc_scr[row, :] = (ccw_ref[0:1, :] * bc[0:1, :] + ccw_ref[1:2, :] * bc[1:2, :] + ccw_ref[2:3, :] * zrow)
        outc_ref[bl, 0:1, :] = bc[1:2, :]
        outc_ref[bl, 1:2, :] = zrow

    for h in range(RET_HEADS):
        cols = pl.ds(h * RET_DV, RET_DV)
        o = _layernorm(o_scr[:, cols], gn_ref[:, cols]) * _silu(proj_ref[:, pl.ds(OFF_G + h * RET_DV, RET_DV)])
        mix_ref[:, cols] = o
    mix_ref[:, MIX_A:MIX_A + CONV_A_WIDTH] = _silu(
        _layernorm(ua_scr[...] + cab_ref[...], lng_ref[...], lnb_ref[...]))
    mix_ref[:, MIX_C:MIX_C + CONV_C_WIDTH] = proj_ref[:, OFF_CB:OFF_CB + CONV_C_WIDTH] * zc_scr[...]


def _sample_state(proj, qt, kt, s_ret, buf_a, buf_c, cd, gn, caw, cab, lng, lnb, ccw):
    n = proj.shape[0]
    cs = functools.partial(_const_spec, grid_rank=1, single_buffer=False)
    return pl.pallas_call(
        _sample_state_kernel,
        grid=(n // SEQ_BLK,),
        in_specs=[
            pl.BlockSpec((SEQ_BLK, IN_WIDTH), lambda j: (j, 0)),
            cs((QK_WIDTH, n)), cs((QK_WIDTH, n)),
            pl.BlockSpec((SEQ_BLK, RET_HEADS, RET_DK, RET_DV), lambda j: (j, 0, 0, 0)),
            pl.BlockSpec((SEQ_BLK, CONV_A_K - 1, CONV_A_WIDTH), lambda j: (j, 0, 0)),
            pl.BlockSpec((SEQ_BLK, CONV_C_K - 1, CONV_C_WIDTH), lambda j: (j, 0, 0)),
            cs(cd.shape), cs((1, RET_WIDTH)), cs((CONV_A_K, CONV_A_WIDTH)), cs((1, CONV_A_WIDTH)),
            cs((1, CONV_A_WIDTH)), cs((1, CONV_A_WIDTH)), cs((CONV_C_K, CONV_C_WIDTH)),
        ],
        out_specs=[
            pl.BlockSpec((SEQ_BLK, D_MODEL), lambda j: (j, 0)),
            pl.BlockSpec((SEQ_BLK, RET_HEADS, RET_DK, RET_DV), lambda j: (j, 0, 0, 0)),
            pl.BlockSpec((SEQ_BLK, CONV_A_K - 1, CONV_A_WIDTH), lambda j: (j, 0, 0)),
            pl.BlockSpec((SEQ_BLK, CONV_C_K - 1, CONV_C_WIDTH), lambda j: (j, 0, 0)),
        ],
        out_shape=[
            jax.ShapeDtypeStruct((n, D_MODEL), F32),
            jax.ShapeDtypeStruct(s_ret.shape, F32),
            jax.ShapeDtypeStruct(buf_a.shape, F32),
            jax.ShapeDtypeStruct(buf_c.shape, F32),
        ],
        scratch_shapes=[
            pltpu.VMEM((SEQ_BLK, RET_WIDTH), F32),
            pltpu.VMEM((SEQ_BLK, CONV_A_WIDTH), F32),
            pltpu.VMEM((SEQ_BLK, CONV_C_WIDTH), F32),
        ],
        compiler_params=pltpu.CompilerParams(
            dimension_semantics=("arbitrary",), vmem_limit_bytes=V7X_VMEM_LIMIT_BYTES),
        name="sample_state",
    )(proj, qt, kt, s_ret, buf_a, buf_c, cd, gn, caw, cab, lng, lnb, ccw)


def _sample_post_kernel(x_ref, mod_ref, mix_ref, wout_ref, gpm_ref, gpf_ref, gqf_ref, w13_ref, w2_ref,
                        y_ref, x1_scr, h_scr, f_scr, *, n_steps):
    j = pl.program_id(0)

    @pl.when(j == 0)
    def _():
        gt_m = mod_ref[:, 2 * D_MODEL:3 * D_MODEL]
        mix = jnp.dot(mix_ref[...].astype(BF16), wout_ref[...], preferred_element_type=F32)
        x1 = x_ref[...] + gt_m * (_rms(mix) * gpm_ref[...])
        x1_scr[...] = x1
        sh = mod_ref[:, 3 * D_MODEL:4 * D_MODEL]
        sc1 = 1.0 + mod_ref[:, 4 * D_MODEL:5 * D_MODEL]
        h_scr[...] = (_rms(x1) * gpf_ref[...] * sc1 + sh).astype(BF16)
        f_scr[...] = jnp.zeros_like(f_scr)

    outs = _swiglu_cols(h_scr[...], w13_ref, 0, FF_BLK)
    p = jnp.concatenate(outs, axis=-1)
    f_scr[...] += jnp.dot(p, w2_ref[...], preferred_element_type=F32)

    @pl.when(j == n_steps - 1)
    def _():
        gt_f = mod_ref[:, 5 * D_MODEL:6 * D_MODEL]
        y_ref[...] = x1_scr[...] + gt_f * (_rms(f_scr[...]) * gqf_ref[...])


def _sample_post(xs, mod_s, mix, w_out, gpost_m, gpre_f, gpost_f, w13, w2):
    n = xs.shape[0]
    n_steps = D_FF // FF_BLK
    cs = functools.partial(_const_spec, grid_rank=1, single_buffer=False)
    return pl.pallas_call(
        functools.partial(_sample_post_kernel, n_steps=n_steps),
        grid=(n_steps,),
        in_specs=[
            cs((n, D_MODEL)), cs((n, 6 * D_MODEL)), cs((n, D_MODEL)), cs((D_MODEL, D_MODEL)),
            cs((1, D_MODEL)), cs((1, D_MODEL)), cs((1, D_MODEL)),
            pl.BlockSpec((D_MODEL, 2 * FF_BLK), lambda j: (0, j)),
            pl.BlockSpec((FF_BLK, D_MODEL), lambda j: (j, 0)),
        ],
        out_specs=cs((n, D_MODEL)),
        out_shape=jax.ShapeDtypeStruct((n, D_MODEL), F32),
        scratch_shapes=[
            pltpu.VMEM((n, D_MODEL), F32),
            pltpu.VMEM((n, D_MODEL), BF16),
            pltpu.VMEM((n, D_MODEL), F32),
        ],
        compiler_params=pltpu.CompilerParams(
            dimension_semantics=("arbitrary",), vmem_limit_bytes=V7X_VMEM_LIMIT_BYTES),
        name="sample_post",
    )(xs, mod_s, mix, w_out, gpost_m, gpre_f, gpost_f, w13, w2)


def _interleave_w13(w1, w3):
    d = w1.shape[0]
    nb = D_FF // LANES
    return jnp.stack([w1.reshape(d, nb, LANES), w3.reshape(d, nb, LANES)], axis=2).reshape(d, 2 * D_FF)


def kernel(x_prompt, x_sample, c_prompt, c_sample, state_ret, state_conv_a, state_conv_c, ada_w, ada_b, norm_pre_mix, norm_post_mix, norm_pre_ffn, norm_post_ffn, w_in, w_out, ret_gn_g, conv_a_w, conv_a_b, conv_a_ln_g, conv_a_ln_b, conv_c_w, ffn_w1, ffn_w3, ffn_w2):
    bp, lp, _ = x_prompt.shape
    ns = x_sample.shape[0]
    assert x_sample.shape[1] == 1 and lp % TOK_TILE == 0 and TOK_TILE % RET_CHUNK == 0

    k_scale = RET_DK ** -0.5
    cq, sq = _rope_tables(np.arange(lp), 1.0)
    ck, sk = _rope_tables(np.arange(lp), k_scale)
    tables_p = (cq, sq, ck, sk) + _decay_tables(RET_CHUNK)
    cqs, sqs = _rope_tables([PAST_LEN], 1.0)
    cks, sks = _rope_tables([PAST_LEN], k_scale)
    rope_s = np.concatenate([cqs, sqs, cks, sks, np.zeros((4, LANES), np.float32)], axis=0)
    cd_s = _decay_tables(1)[3]

    mod = _ada_modulation(jnp.concatenate([c_prompt, c_sample], axis=0), ada_w, ada_b)

    row = lambda a: a.reshape(1, -1)
    yp = x_prompt
    ys = x_sample.reshape(ns, D_MODEL)
    outs = [[] for _ in range(6)]
    for l in range(DEPTH):
        w_in_b = w_in[l].astype(BF16)
        w_out_b = w_out[l].astype(BF16)
        w13_b = _interleave_w13(ffn_w1[l].astype(BF16), ffn_w3[l].astype(BF16))
        w2_b = ffn_w2[l].astype(BF16)
        gn, cab, lng, lnb = row(ret_gn_g[l]), row(conv_a_b[l]), row(conv_a_ln_g[l]), row(conv_a_ln_b[l])
        g_pm, g_qm = row(norm_pre_mix[l]), row(norm_post_mix[l])
        g_pf, g_qf = row(norm_pre_ffn[l]), row(norm_post_ffn[l])

        mod_p = mod[l, :bp].reshape(bp, 6, D_MODEL)
        yp, sr, ba, bc = _prompt_mix(yp, mod_p, g_pm, g_qm, w_in_b, w_out_b, tables_p,
                                     gn, conv_a_w[l], cab, lng, lnb, conv_c_w[l])
        yp = _prompt_ffn(yp, mod_p, g_pf, g_qf, w13_b, w2_b)
        outs[0].append(sr); outs[1].append(ba); outs[2].append(bc)

        mod_s = mod[l, bp:]
        proj, qt, kt = _sample_pre(ys, mod_s, g_pm, w_in_b, rope_s)
        mix, sr, ba, bc = _sample_state(proj, qt, kt, state_ret[l], state_conv_a[l], state_conv_c[l], cd_s,
                                        gn, conv_a_w[l], cab, lng, lnb, conv_c_w[l])
        ys = _sample_post(ys, mod_s, mix, w_out_b, g_qm, g_pf, g_qf, w13_b, w2_b)
        outs[3].append(sr); outs[4].append(ba); outs[5].append(bc)

    return (yp, ys.reshape(ns, 1, D_MODEL)) + tuple(jnp.stack(o) for o in outs)
```

```python
import functools

import numpy as np
import jax
import jax.numpy as jnp
from jax import lax
from jax.experimental import pallas as pl
from jax.experimental.pallas import tpu as pltpu

D_MODEL = 1024
DEPTH = 2
PAST_LEN = 16384
RET_HEADS = 4
RET_WIDTH = D_MODEL // 2
RET_DV = RET_WIDTH // RET_HEADS
RET_DK = RET_DV // 2
QK_WIDTH = RET_HEADS * RET_DK
CONV_A_WIDTH = D_MODEL // 4
CONV_A_K = 31
CONV_C_WIDTH = D_MODEL - RET_WIDTH - CONV_A_WIDTH
CONV_C_K = 3
IN_WIDTH = 2 * QK_WIDTH + 2 * RET_WIDTH + 2 * CONV_A_WIDTH + 3 * CONV_C_WIDTH
D_FF = ((8 * D_MODEL // 3 + 255) // 256) * 256
RET_CHUNK = 128
ROPE_BASE = 10000.0
EPS = 1e-6

OFF_Q = 0
OFF_K = OFF_Q + QK_WIDTH
OFF_V = OFF_K + QK_WIDTH
OFF_G = OFF_V + RET_WIDTH
OFF_AV = OFF_G + RET_WIDTH
OFF_AG = OFF_AV + CONV_A_WIDTH
OFF_CB = OFF_AG + CONV_A_WIDTH
OFF_CC = OFF_CB + CONV_C_WIDTH
OFF_CX = OFF_CC + CONV_C_WIDTH
MIX_A = RET_WIDTH
MIX_C = RET_WIDTH + CONV_A_WIDTH

LANES = 128
SUBLANES = 8
V7X_VMEM_LIMIT_BYTES = 56 * 1024 * 1024

TOK_TILE = 512
ROW_BLK = 32
CONV_BLK = 64
CONV_STRIDE = 4
HIST_A = 32
HIST_C = 8
SEQ_BLK = 8
FF_BLK = 256

F32 = jnp.float32
BF16 = jnp.bfloat16


def _sigmoid(x):
    return jax.nn.sigmoid(x)


def _silu(x):
    return x * _sigmoid(x)


def _rms(x):
    return x * lax.rsqrt(jnp.mean(x * x, axis=-1, keepdims=True) + EPS)


def _layernorm(x, g, b=None):
    mu = jnp.mean(x, axis=-1, keepdims=True)
    d = x - mu
    var = jnp.mean(d * d, axis=-1, keepdims=True)
    y = d * lax.rsqrt(var + EPS) * g
    return y if b is None else y + b


def _rope_half(x, cos, sin, first_half):
    partner = jnp.where(first_half, pltpu.roll(x, 96, 1), pltpu.roll(x, 32, 1))
    return x * cos + partner * sin


def _first_half_mask(rows):
    lane = lax.broadcasted_iota(jnp.int32, (rows, LANES), 1)
    return (lane & (RET_DK - 1)) < (RET_DK // 2), lane < RET_DK


def _swiglu_block(h, w1_ref, w3_ref, col0):
    cols = pl.ds(col0, FF_BLK)
    a = jnp.dot(h, w1_ref[:, cols], preferred_element_type=F32)
    b = jnp.dot(h, w3_ref[:, cols], preferred_element_type=F32)
    return (_silu(a) * b).astype(BF16)


def _rope_tables(pos, k_scale):
    half = RET_DK // 2
    inv = ROPE_BASE ** (-np.arange(half, dtype=np.float64) / half)
    ang = np.asarray(pos, np.float64)[:, None] * inv[None, :]
    cos = np.tile(np.cos(ang), (1, 4))
    sin = np.tile(np.concatenate([-np.sin(ang), np.sin(ang)], axis=1), (1, 2))
    return (cos * k_scale).astype(np.float32), (sin * k_scale).astype(np.float32)


def _decay_tables(chunk):
    log_g = np.log(1.0 - np.exp2(-5.0 - np.arange(RET_HEADS, dtype=np.float64)))
    idx = np.arange(chunk, dtype=np.float64)
    diff = idx[:, None] - idx[None, :]
    dmat = np.where(diff[None] >= 0, np.exp(np.maximum(diff, 0.0)[None] * log_g[:, None, None]), 0.0)
    read_dec = np.exp((idx + 1.0)[:, None] * log_g[None, :])
    upd_dec = np.exp((chunk - 1.0 - idx)[:, None] * log_g[None, :])
    chunk_dec = np.exp(chunk * log_g)
    rd = np.broadcast_to(read_dec.T[:, :, None], (RET_HEADS, chunk, RET_DV))
    upd = np.repeat(upd_dec, RET_DK, axis=1).reshape(chunk, 2, LANES).transpose(1, 0, 2)
    cd = np.broadcast_to(chunk_dec[:, None, None], (RET_HEADS, 1, RET_DV))
    f = lambda a: np.ascontiguousarray(a, dtype=np.float32)
    return f(dmat), f(rd), f(upd), f(cd)


def _const_spec(shape, grid_rank, single_buffer=True):
    zeros = (0,) * len(shape)
    idx = (lambda b, c: zeros) if grid_rank == 2 else (lambda j: zeros)
    if single_buffer:
        return pl.BlockSpec(shape, idx, pipeline_mode=pl.Buffered(1))
    return pl.BlockSpec(shape, idx)


def _params(n_axes):
    return pltpu.CompilerParams(dimension_semantics=("arbitrary",) * n_axes,
                                vmem_limit_bytes=V7X_VMEM_LIMIT_BYTES)


def _stacked_call(kern, *, name, grid, in_specs, args, out_specs, out_shape, n_stacked, prev, scratch_shapes):
    aliases = {}
    if prev is not None:
        n_in = len(args)
        n_out = len(out_shape)
        in_specs = list(in_specs) + [pl.BlockSpec(memory_space=pl.ANY)] * n_stacked
        args = list(args) + list(prev)
        aliases = {n_in + i: n_out - n_stacked + i for i in range(n_stacked)}
    return pl.pallas_call(
        kern, grid=grid, in_specs=in_specs, out_specs=out_specs, out_shape=out_shape,
        scratch_shapes=scratch_shapes, input_output_aliases=aliases,
        compiler_params=_params(len(grid)), name=name)(*args)


def _ada_kernel(c_ref, w_ref, b_ref, o_ref):
    s = _silu(c_ref[...]).astype(BF16)
    o_ref[...] = jnp.dot(s, w_ref[...].astype(BF16), preferred_element_type=F32) + b_ref[...]


def _ada_modulation(c_all, ada_w, ada_b):
    rows = c_all.shape[0]
    ncol = 6 * D_MODEL
    blk = D_MODEL
    return pl.pallas_call(
        _ada_kernel,
        grid=(DEPTH, ncol // blk),
        in_specs=[
            pl.BlockSpec((rows, D_MODEL), lambda l, j: (0, 0)),
            pl.BlockSpec((None, D_MODEL, blk), lambda l, j: (l, 0, j)),
            pl.BlockSpec((None, 1, blk), lambda l, j: (l, 0, j)),
        ],
        out_specs=pl.BlockSpec((None, rows, blk), lambda l, j: (l, 0, j)),
        out_shape=jax.ShapeDtypeStruct((DEPTH, rows, ncol), F32),
        compiler_params=_params(2),
        name="ada_mod",
    )(c_all, ada_w, ada_b.reshape(DEPTH, 1, ncol))


def _prompt_mix_kernel(*refs, tile, n_in):
    (x_ref, mod_ref, gpre_ref, gpost_ref, win_ref, wout_ref,
     cq_ref, sq_ref, ck_ref, sk_ref, dmat_ref, rd_ref, upd_ref, cd_ref,
     gn_ref, caw_ref, cab_ref, lng_ref, lnb_ref, ccw_ref) = refs[:20]
    (y_ref, sret_ref, sca_ref, scc_ref,
     h_scr, proj_scr, mixin_scr, ubuf, ua_scr, zbuf) = refs[n_in:]
    c = pl.program_id(1)

    @pl.when(c == 0)
    def _():
        sret_ref[...] = jnp.zeros_like(sret_ref)
        ubuf[:, 0:HIST_A, :] = jnp.zeros((2, HIST_A, LANES), F32)
        zbuf[0:HIST_C, :] = jnp.zeros((HIST_C, CONV_C_WIDTH), F32)

    sh = mod_ref[0:1, :]
    pre_scale = gpre_ref[...] * (1.0 + mod_ref[1:2, :])
    post_scale = gpost_ref[...] * mod_ref[2:3, :]

    for r in range(tile // ROW_BLK):
        rows = pl.ds(r * ROW_BLK, ROW_BLK)
        h_scr[rows, :] = (_rms(x_ref[rows, :]) * pre_scale + sh).astype(BF16)

    proj_scr[...] = jnp.dot(h_scr[...], win_ref[...], preferred_element_type=F32)

    for r in range(tile // CONV_BLK):
        rows = pl.ds(r * CONV_BLK, CONV_BLK)
        dst = pl.ds(HIST_A + r * CONV_BLK, CONV_BLK)
        for half in range(2):
            lo = half * LANES
            u = (proj_scr[rows, pl.ds(OFF_AV + lo, LANES)]
                 * _sigmoid(proj_scr[rows, pl.ds(OFF_AG + lo, LANES)]))
            ubuf[half, dst, :] = u
    first_tap = HIST_A - (CONV_A_K - 1)
    span = SUBLANES * CONV_STRIDE
    for half in range(2):
        taps = [jnp.broadcast_to(caw_ref[j:j + 1, pl.ds(half * LANES, LANES)], (SUBLANES, LANES))
                for j in range(CONV_A_K)]
        for m in range(tile // span):
            acc = [None] * CONV_STRIDE
            for k in range(CONV_STRIDE + CONV_A_K - 1):
                win = ubuf[half, pl.ds(m * span + first_tap + k, SUBLANES, stride=CONV_STRIDE), :]
                for t in range(CONV_STRIDE):
                    j = k - t
                    if 0 <= j < CONV_A_K:
                        term = taps[j] * win
                        acc[t] = term if acc[t] is None else acc[t] + term
            for t in range(CONV_STRIDE):
                ua_scr[half, pl.ds(m * span + t, SUBLANES, stride=CONV_STRIDE), :] = acc[t]
    for r in range(tile // CONV_BLK):
        rows = pl.ds(r * CONV_BLK, CONV_BLK)
        ua = jnp.concatenate([ua_scr[0, rows, :], ua_scr[1, rows, :]], axis=-1) + cab_ref[...]
        o_a = _silu(_layernorm(ua, lng_ref[...], lnb_ref[...]))
        mixin_scr[rows, MIX_A:MIX_A + CONV_A_WIDTH] = o_a.astype(BF16)

    for r in range(tile // CONV_BLK):
        rows = pl.ds(r * CONV_BLK, CONV_BLK)
        z = proj_scr[rows, OFF_CC:OFF_CC + CONV_C_WIDTH] * proj_scr[rows, OFF_CX:OFF_CX + CONV_C_WIDTH]
        zbuf[pl.ds(HIST_C + r * CONV_BLK, CONV_BLK), :] = z
    first_tap_c = HIST_C - (CONV_C_K - 1)
    for r in range(tile // CONV_BLK):
        rows = pl.ds(r * CONV_BLK, CONV_BLK)
        zc = ccw_ref[0:1, :] * zbuf[pl.ds(r * CONV_BLK + first_tap_c, CONV_BLK), :]
        for j in range(1, CONV_C_K):
            zc = zc + ccw_ref[j:j + 1, :] * zbuf[pl.ds(r * CONV_BLK + first_tap_c + j, CONV_BLK), :]
        o_c = proj_scr[rows, OFF_CB:OFF_CB + CONV_C_WIDTH] * zc
        mixin_scr[rows, MIX_C:MIX_C + CONV_C_WIDTH] = o_c.astype(BF16)

    first_half, head_lo = _first_half_mask(RET_CHUNK)
    nt = (((1,), (1,)), ((), ()))
    for i in range(tile // RET_CHUNK):
        rows = pl.ds(i * RET_CHUNK, RET_CHUNK)
        cq, sq, ck, sk = cq_ref[rows, :], sq_ref[rows, :], ck_ref[rows, :], sk_ref[rows, :]
        for pair in range(RET_HEADS // 2):
            qr = _rope_half(proj_scr[rows, pl.ds(OFF_Q + pair * LANES, LANES)], cq, sq, first_half)
            kr = _rope_half(proj_scr[rows, pl.ds(OFF_K + pair * LANES, LANES)], ck, sk, first_half)
            kr_b = kr.astype(BF16)
            ku_t = (kr * upd_ref[pair]).T.astype(BF16)
            s_pair = sret_ref[2 * pair:2 * pair + 2].reshape(2 * RET_DK, RET_DV).astype(BF16)
            for hl in range(2):
                h = 2 * pair + hl
                qm = jnp.where(head_lo if hl == 0 else jnp.logical_not(head_lo), qr, 0.0).astype(BF16)
                scores = lax.dot_general(qm, kr_b, nt, preferred_element_type=F32) * dmat_ref[h]
                vh = proj_scr[rows, pl.ds(OFF_V + h * RET_DV, RET_DV)].astype(BF16)
                o = (jnp.dot(scores.astype(BF16), vh, preferred_element_type=F32)
                     + jnp.dot(qm, s_pair, preferred_element_type=F32) * rd_ref[h])
                sret_ref[h] = (sret_ref[h] * cd_ref[h]
                               + jnp.dot(ku_t[hl * RET_DK:(hl + 1) * RET_DK, :], vh, preferred_element_type=F32))
                gate = proj_scr[rows, pl.ds(OFF_G + h * RET_DV, RET_DV)]
                o = _layernorm(o, gn_ref[:, pl.ds(h * RET_DV, RET_DV)]) * _silu(gate)
                mixin_scr[rows, pl.ds(h * RET_DV, RET_DV)] = o.astype(BF16)

    proj_scr[:, 0:D_MODEL] = jnp.dot(mixin_scr[...], wout_ref[...], preferred_element_type=F32)
    for r in range(tile // ROW_BLK):
        rows = pl.ds(r * ROW_BLK, ROW_BLK)
        y_ref[rows, :] = x_ref[rows, :] + _rms(proj_scr[rows, 0:D_MODEL]) * post_scale

    hist_a = pl.ds(tile + first_tap, CONV_A_K - 1)
    sca_ref[...] = jnp.concatenate([ubuf[0, hist_a, :], ubuf[1, hist_a, :]], axis=-1)
    scc_ref[...] = zbuf[pl.ds(tile + first_tap_c, CONV_C_K - 1), :]
    ubuf[:, 0:HIST_A, :] = ubuf[:, pl.ds(tile, HIST_A), :]
    zbuf[0:HIST_C, :] = zbuf[pl.ds(tile, HIST_C), :]


def _prompt_mix(layer, prev, x, mod_p, gpre, gpost, w_in, w_out, tables, gn, caw, cab, lng, lnb, ccw):
    bsz, seq, _ = x.shape
    tile = TOK_TILE
    cq, sq, ck, sk, dmat, rd, upd, cd = tables
    tok = pl.BlockSpec((None, tile, D_MODEL), lambda b, c: (b, c, 0))
    rope = pl.BlockSpec((tile, LANES), lambda b, c: (c, 0))
    cs = functools.partial(_const_spec, grid_rank=2)
    args = (x, mod_p, gpre, gpost, w_in, w_out, cq, sq, ck, sk, dmat, rd, upd, cd, gn, caw, cab, lng, lnb, ccw)
    n_in = len(args) + (3 if prev is not None else 0)
    return _stacked_call(
        functools.partial(_prompt_mix_kernel, tile=tile, n_in=n_in),
        name="prompt_mix", grid=(bsz, seq // tile), args=args, n_stacked=3, prev=prev,
        in_specs=[
            tok,
            pl.BlockSpec((None, 6, D_MODEL), lambda b, c: (b, 0, 0)),
            cs((1, D_MODEL)), cs((1, D_MODEL)),
            cs((D_MODEL, IN_WIDTH)), cs((D_MODEL, D_MODEL)),
            rope, rope, rope, rope,
            cs(dmat.shape), cs(rd.shape), cs(upd.shape), cs(cd.shape),
            cs((1, RET_WIDTH)), cs((CONV_A_K, CONV_A_WIDTH)), cs((1, CONV_A_WIDTH)),
            cs((1, CONV_A_WIDTH)), cs((1, CONV_A_WIDTH)), cs((CONV_C_K, CONV_C_WIDTH)),
        ],
        out_specs=[
            tok,
            pl.BlockSpec((None, None, RET_HEADS, RET_DK, RET_DV), lambda b, c: (layer, b, 0, 0, 0)),
            pl.BlockSpec((None, None, CONV_A_K - 1, CONV_A_WIDTH), lambda b, c: (layer, b, 0, 0)),
            pl.BlockSpec((None, None, CONV_C_K - 1, CONV_C_WIDTH), lambda b, c: (layer, b, 0, 0)),
        ],
        out_shape=[
            jax.ShapeDtypeStruct(x.shape, F32),
            jax.ShapeDtypeStruct((DEPTH, bsz, RET_HEADS, RET_DK, RET_DV), F32),
            jax.ShapeDtypeStruct((DEPTH, bsz, CONV_A_K - 1, CONV_A_WIDTH), F32),
            jax.ShapeDtypeStruct((DEPTH, bsz, CONV_C_K - 1, CONV_C_WIDTH), F32),
        ],
        scratch_shapes=[
            pltpu.VMEM((tile, D_MODEL), BF16),
            pltpu.VMEM((tile, IN_WIDTH), F32),
            pltpu.VMEM((tile, D_MODEL), BF16),
            pltpu.VMEM((2, tile + HIST_A, LANES), F32),
            pltpu.VMEM((2, tile, LANES), F32),
            pltpu.VMEM((tile + HIST_C, CONV_C_WIDTH), F32),
        ])


def _prompt_ffn_kernel(x_ref, mod_ref, gpre_ref, gpost_ref, w1_ref, w3_ref, w2_ref, y_ref,
                       h_scr, p_scr, f_scr, *, tile):
    sh = mod_ref[3:4, :]
    pre_scale = gpre_ref[...] * (1.0 + mod_ref[4:5, :])
    post_scale = gpost_ref[...] * mod_ref[5:6, :]
    for r in range(tile // ROW_BLK):
        rows = pl.ds(r * ROW_BLK, ROW_BLK)
        h_scr[rows, :] = (_rms(x_ref[rows, :]) * pre_scale + sh).astype(BF16)
    for j in range(D_FF // FF_BLK):
        p_scr[:, pl.ds(j * FF_BLK, FF_BLK)] = _swiglu_block(h_scr[...], w1_ref, w3_ref, j * FF_BLK)
    f_scr[...] = jnp.dot(p_scr[...], w2_ref[...], preferred_element_type=F32)
    for r in range(tile // ROW_BLK):
        rows = pl.ds(r * ROW_BLK, ROW_BLK)
        y_ref[rows, :] = x_ref[rows, :] + _rms(f_scr[rows, :]) * post_scale


def _prompt_ffn(x, mod_p, gpre, gpost, w1, w3, w2):
    bsz, seq, _ = x.shape
    tile = TOK_TILE
    tok = pl.BlockSpec((None, tile, D_MODEL), lambda b, c: (b, c, 0))
    cs = functools.partial(_const_spec, grid_rank=2)
    return pl.pallas_call(
        functools.partial(_prompt_ffn_kernel, tile=tile),
        grid=(bsz, seq // tile),
        in_specs=[
            tok,
            pl.BlockSpec((None, 6, D_MODEL), lambda b, c: (b, 0, 0)),
            cs((1, D_MODEL)), cs((1, D_MODEL)),
            cs((D_MODEL, D_FF)), cs((D_MODEL, D_FF)), cs((D_FF, D_MODEL)),
        ],
        out_specs=tok,
        out_shape=jax.ShapeDtypeStruct(x.shape, F32),
        scratch_shapes=[
            pltpu.VMEM((tile, D_MODEL), BF16),
            pltpu.VMEM((tile, D_FF), BF16),
            pltpu.VMEM((tile, D_MODEL), F32),
        ],
        compiler_params=_params(2),
        name="prompt_ffn",
    )(x, mod_p, gpre, gpost, w1, w3, w2)


def _sample_pre_kernel(x_ref, mod_ref, gpre_ref, win_ref, rope_ref, proj_ref, qt_ref, kt_ref):
    n = x_ref.shape[0]
    sh = mod_ref[:, 0:D_MODEL]
    sc1 = 1.0 + mod_ref[:, D_MODEL:2 * D_MODEL]
    h = (_rms(x_ref[...]) * (gpre_ref[...] * sc1) + sh).astype(BF16)
    proj_ref[...] = jnp.dot(h, win_ref[...], preferred_element_type=F32)
    first_half, _ = _first_half_mask(n)
    cq, sq, ck, sk = rope_ref[0:1, :], rope_ref[1:2, :], rope_ref[2:3, :], rope_ref[3:4, :]
    for pair in range(RET_HEADS // 2):
        ql = pl.ds(OFF_Q + pair * LANES, LANES)
        kl = pl.ds(OFF_K + pair * LANES, LANES)
        qr = _rope_half(proj_ref[:, ql], cq, sq, first_half)
        kr = _rope_half(proj_ref[:, kl], ck, sk, first_half)
        qt_ref[pl.ds(pair * LANES, LANES), :] = qr.T
        kt_ref[pl.ds(pair * LANES, LANES), :] = kr.T


def _sample_pre(xs, mod_s, gpre, w_in, rope_s):
    n = xs.shape[0]
    cs = functools.partial(_const_spec, grid_rank=1, single_buffer=False)
    return pl.pallas_call(
        _sample_pre_kernel,
        grid=(1,),
        in_specs=[cs((n, D_MODEL)), cs((n, 6 * D_MODEL)), cs((1, D_MODEL)),
                  cs((D_MODEL, IN_WIDTH)), cs(rope_s.shape)],
        out_specs=[cs((n, IN_WIDTH)), cs((QK_WIDTH, n)), cs((QK_WIDTH, n))],
        out_shape=[jax.ShapeDtypeStruct((n, IN_WIDTH), F32),
                   jax.ShapeDtypeStruct((QK_WIDTH, n), F32),
                   jax.ShapeDtypeStruct((QK_WIDTH, n), F32)],
        compiler_params=_params(1),
        name="sample_pre",
    )(xs, mod_s, gpre, w_in, rope_s)


def _sample_state_kernel(*refs, n_in):
    (proj_ref, qt_ref, kt_ref, sin_ref, bufa_ref, bufc_ref, cd_ref,
     gn_ref, caw_ref, cab_ref, lng_ref, lnb_ref, ccw_ref) = refs[:13]
    mix_ref, sout_ref, outa_ref, outc_ref, o_scr, ua_scr, zc_scr = refs[n_in:]
    blk = pl.program_id(0)
    n = qt_ref.shape[1]
    lane = lax.broadcasted_iota(jnp.int32, (QK_WIDTH, n), 1)

    for bl in range(SEQ_BLK):
        onehot = lane == (blk * SEQ_BLK + bl)
        qcol = jnp.sum(jnp.where(onehot, qt_ref[...], 0.0), axis=1, keepdims=True)
        kcol = jnp.sum(jnp.where(onehot, kt_ref[...], 0.0), axis=1, keepdims=True)
        row = pl.ds(bl, 1)
        for h in range(RET_HEADS):
            vrow = proj_ref[row, pl.ds(OFF_V + h * RET_DV, RET_DV)]
            s_new = (sin_ref[bl, h] * cd_ref[h]
                     + kcol[h * RET_DK:(h + 1) * RET_DK, :] * vrow)
            sout_ref[bl, h] = s_new
            o_scr[row, pl.ds(h * RET_DV, RET_DV)] = jnp.sum(
                qcol[h * RET_DK:(h + 1) * RET_DK, :] * s_new, axis=0, keepdims=True)
        urow = proj_ref[row, OFF_AV:OFF_AV + CONV_A_WIDTH] * _sigmoid(proj_ref[row, OFF_AG:OFF_AG + CONV_A_WIDTH])
        ba = bufa_ref[bl]
        ua_scr[row, :] = (jnp.sum(ba * caw_ref[0:CONV_A_K - 1, :], axis=0, keepdims=True)
                          + caw_ref[CONV_A_K - 1:CONV_A_K, :] * urow)
        outa_ref[bl, 0:CONV_A_K - 2, :] = ba[1:CONV_A_K - 1, :]
        outa_ref[bl, CONV_A_K - 2:CONV_A_K - 1, :] = urow
        zrow = proj_ref[row, OFF_CC:OFF_CC + CONV_C_WIDTH] * proj_ref[row, OFF_CX:OFF_CX + CONV_C_WIDTH]
        bc = bufc_ref[bl]
        zc_scr[row, :] = (ccw_ref[0:1, :] * bc[0:1, :] + ccw_ref[1:2, :] * bc[1:2, :] + ccw_ref[2:3, :] * zrow)
        outc_ref[bl, 0:1, :] = bc[1:2, :]
        outc_ref[bl, 1:2, :] = zrow

    for h in range(RET_HEADS):
        cols = pl.ds(h * RET_DV, RET_DV)
        o = _layernorm(o_scr[:, cols], gn_ref[:, cols]) * _silu(proj_ref[:, pl.ds(OFF_G + h * RET_DV, RET_DV)])
        mix_ref[:, cols] = o
    mix_ref[:, MIX_A:MIX_A + CONV_A_WIDTH] = _silu(
        _layernorm(ua_scr[...] + cab_ref[...], lng_ref[...], lnb_ref[...]))
    mix_ref[:, MIX_C:MIX_C + CONV_C_WIDTH] = proj_ref[:, OFF_CB:OFF_CB + CONV_C_WIDTH] * zc_scr[...]


def _sample_state(layer, prev, proj, qt, kt, s_ret, buf_a, buf_c, cd, gn, caw, cab, lng, lnb, ccw):
    n = proj.shape[0]
    cs = functools.partial(_const_spec, grid_rank=1, single_buffer=False)
    s_spec = pl.BlockSpec((None, SEQ_BLK, RET_HEADS, RET_DK, RET_DV), lambda j: (layer, j, 0, 0, 0))
    a_spec = pl.BlockSpec((None, SEQ_BLK, CONV_A_K - 1, CONV_A_WIDTH), lambda j: (layer, j, 0, 0))
    c_spec = pl.BlockSpec((None, SEQ_BLK, CONV_C_K - 1, CONV_C_WIDTH), lambda j: (layer, j, 0, 0))
    args = (proj, qt, kt, s_ret, buf_a, buf_c, cd, gn, caw, cab, lng, lnb, ccw)
    n_in = len(args) + (3 if prev is not None else 0)
    return _stacked_call(
        functools.partial(_sample_state_kernel, n_in=n_in),
        name="sample_state", grid=(n // SEQ_BLK,), args=args, n_stacked=3, prev=prev,
        in_specs=[
            pl.BlockSpec((SEQ_BLK, IN_WIDTH), lambda j: (j, 0)),
            cs((QK_WIDTH, n)), cs((QK_WIDTH, n)),
            s_spec, a_spec, c_spec,
            cs(cd.shape), cs((1, RET_WIDTH)), cs((CONV_A_K, CONV_A_WIDTH)), cs((1, CONV_A_WIDTH)),
            cs((1, CONV_A_WIDTH)), cs((1, CONV_A_WIDTH)), cs((CONV_C_K, CONV_C_WIDTH)),
        ],
        out_specs=[pl.BlockSpec((SEQ_BLK, D_MODEL), lambda j: (j, 0)), s_spec, a_spec, c_spec],
        out_shape=[
            jax.ShapeDtypeStruct((n, D_MODEL), F32),
            jax.ShapeDtypeStruct(s_ret.shape, F32),
            jax.ShapeDtypeStruct(buf_a.shape, F32),
            jax.ShapeDtypeStruct(buf_c.shape, F32),
        ],
        scratch_shapes=[
            pltpu.VMEM((SEQ_BLK, RET_WIDTH), F32),
            pltpu.VMEM((SEQ_BLK, CONV_A_WIDTH), F32),
            pltpu.VMEM((SEQ_BLK, CONV_C_WIDTH), F32),
        ])


def _sample_post_kernel(x_ref, mod_ref, mix_ref, wout_ref, gpm_ref, gpf_ref, gqf_ref, w1_ref, w3_ref, w2_ref,
                        y_ref, x1_scr, h_scr, f_scr, *, n_steps):
    j = pl.program_id(0)

    @pl.when(j == 0)
    def _():
        gt_m = mod_ref[:, 2 * D_MODEL:3 * D_MODEL]
        mix = jnp.dot(mix_ref[...].astype(BF16), wout_ref[...], preferred_element_type=F32)
        x1 = x_ref[...] + _rms(mix) * (gpm_ref[...] * gt_m)
        x1_scr[...] = x1
        sh = mod_ref[:, 3 * D_MODEL:4 * D_MODEL]
        sc1 = 1.0 + mod_ref[:, 4 * D_MODEL:5 * D_MODEL]
        h_scr[...] = (_rms(x1) * (gpf_ref[...] * sc1) + sh).astype(BF16)
        f_scr[...] = jnp.zeros_like(f_scr)

    p = _swiglu_block(h_scr[...], w1_ref, w3_ref, 0)
    f_scr[...] += jnp.dot(p, w2_ref[...], preferred_element_type=F32)

    @pl.when(j == n_steps - 1)
    def _():
        gt_f = mod_ref[:, 5 * D_MODEL:6 * D_MODEL]
        y_ref[...] = x1_scr[...] + _rms(f_scr[...]) * (gqf_ref[...] * gt_f)


def _sample_post(xs, mod_s, mix, w_out, gpost_m, gpre_f, gpost_f, w1, w3, w2):
    n = xs.shape[0]
    n_steps = D_FF // FF_BLK
    cs = functools.partial(_const_spec, grid_rank=1, single_buffer=False)
    return pl.pallas_call(
        functools.partial(_sample_post_kernel, n_steps=n_steps),
        grid=(n_steps,),
        in_specs=[
            cs((n, D_MODEL)), cs((n, 6 * D_MODEL)), cs((n, D_MODEL)), cs((D_MODEL, D_MODEL)),
            cs((1, D_MODEL)), cs((1, D_MODEL)), cs((1, D_MODEL)),
            pl.BlockSpec((D_MODEL, FF_BLK), lambda j: (0, j)),
            pl.BlockSpec((D_MODEL, FF_BLK), lambda j: (0, j)),
            pl.BlockSpec((FF_BLK, D_MODEL), lambda j: (j, 0)),
        ],
        out_specs=cs((n, D_MODEL)),
        out_shape=jax.ShapeDtypeStruct((n, D_MODEL), F32),
        scratch_shapes=[
            pltpu.VMEM((n, D_MODEL), F32),
            pltpu.VMEM((n, D_MODEL), BF16),
            pltpu.VMEM((n, D_MODEL), F32),
        ],
        compiler_params=_params(1),
        name="sample_post",
    )(xs, mod_s, mix, w_out, gpost_m, gpre_f, gpost_f, w1, w3, w2)


def kernel(x_prompt, x_sample, c_prompt, c_sample, state_ret, state_conv_a, state_conv_c, ada_w, ada_b, norm_pre_mix, norm_post_mix, norm_pre_ffn, norm_post_ffn, w_in, w_out, ret_gn_g, conv_a_w, conv_a_b, conv_a_ln_g, conv_a_ln_b, conv_c_w, ffn_w1, ffn_w3, ffn_w2):
    bp, lp, _ = x_prompt.shape
    ns = x_sample.shape[0]
    assert x_sample.shape[1] == 1 and ns % SEQ_BLK == 0
    assert lp % TOK_TILE == 0 and TOK_TILE % RET_CHUNK == 0 and TOK_TILE % (SUBLANES * CONV_STRIDE) == 0

    k_scale = RET_DK ** -0.5
    cq, sq = _rope_tables(np.arange(lp), 1.0)
    ck, sk = _rope_tables(np.arange(lp), k_scale)
    tables_p = (cq, sq, ck, sk) + _decay_tables(RET_CHUNK)
    cqs, sqs = _rope_tables([PAST_LEN], 1.0)
    cks, sks = _rope_tables([PAST_LEN], k_scale)
    rope_s = np.concatenate([cqs, sqs, cks, sks, np.zeros((4, LANES), np.float32)], axis=0)
    cd_s = _decay_tables(1)[3]

    mod = _ada_modulation(jnp.concatenate([c_prompt, c_sample], axis=0), ada_w, ada_b)

    row = lambda a: a.reshape(1, -1)
    yp = x_prompt
    ys = x_sample.reshape(ns, D_MODEL)
    st_p = None
    st_s = None
    for l in range(DEPTH):
        w_in_b = w_in[l].astype(BF16)
        w_out_b = w_out[l].astype(BF16)
        w1_b, w3_b, w2_b = ffn_w1[l].astype(BF16), ffn_w3[l].astype(BF16), ffn_w2[l].astype(BF16)
        gn, cab, lng, lnb = row(ret_gn_g[l]), row(conv_a_b[l]), row(conv_a_ln_g[l]), row(conv_a_ln_b[l])
        g_pm, g_qm = row(norm_pre_mix[l]), row(norm_post_mix[l])
        g_pf, g_qf = row(norm_pre_ffn[l]), row(norm_post_ffn[l])

        mod_p = mod[l, :bp].reshape(bp, 6, D_MODEL)
        yp, *st_p = _prompt_mix(l, st_p, yp, mod_p, g_pm, g_qm, w_in_b, w_out_b, tables_p,
                                gn, conv_a_w[l], cab, lng, lnb, conv_c_w[l])
        yp = _prompt_ffn(yp, mod_p, g_pf, g_qf, w1_b, w3_b, w2_b)

        mod_s = mod[l, bp:]
        proj, qt, kt = _sample_pre(ys, mod_s, g_pm, w_in_b, rope_s)
        mix, *st_s = _sample_state(l, st_s, proj, qt, kt, state_ret, state_conv_a, state_conv_c, cd_s,
                                   gn, conv_a_w[l], cab, lng, lnb, conv_c_w[l])
        ys = _sample_post(ys, mod_s, mix, w_out_b, g_qm, g_pf, g_qf, w1_b, w3_b, w2_b)

    return (yp, ys.reshape(ns, 1, D_MODEL)) + tuple(st_p) + tuple(st_s)
```

```python
import functools

import numpy as np
import jax
import jax.numpy as jnp
from jax import lax
from jax.experimental import pallas as pl
from jax.experimental.pallas import tpu as pltpu

D_MODEL = 1024
DEPTH = 2
PAST_LEN = 16384
RET_HEADS = 4
RET_WIDTH = D_MODEL // 2
RET_DV = RET_WIDTH // RET_HEADS
RET_DK = RET_DV // 2
QK_WIDTH = RET_HEADS * RET_DK
CONV_A_WIDTH = D_MODEL // 4
CONV_A_K = 31
CONV_C_WIDTH = D_MODEL - RET_WIDTH - CONV_A_WIDTH
CONV_C_K = 3
IN_WIDTH = 2 * QK_WIDTH + 2 * RET_WIDTH + 2 * CONV_A_WIDTH + 3 * CONV_C_WIDTH
D_FF = ((8 * D_MODEL // 3 + 255) // 256) * 256
RET_CHUNK = 128
ROPE_BASE = 10000.0
EPS = 1e-6

OFF_Q = 0
OFF_K = OFF_Q + QK_WIDTH
OFF_V = OFF_K + QK_WIDTH
OFF_G = OFF_V + RET_WIDTH
OFF_AV = OFF_G + RET_WIDTH
OFF_AG = OFF_AV + CONV_A_WIDTH
OFF_CB = OFF_AG + CONV_A_WIDTH
OFF_CC = OFF_CB + CONV_C_WIDTH
OFF_CX = OFF_CC + CONV_C_WIDTH
MIX_A = RET_WIDTH
MIX_C = RET_WIDTH + CONV_A_WIDTH

LANES = 128
SUBLANES = 8
V7X_VMEM_LIMIT_BYTES = 56 * 1024 * 1024

TOK_TILE = 256
ROW_BLK = 32
CONV_BLK = 64
CONV_STRIDE = 4
HIST_A = 32
HIST_C = 8
SEQ_BLK = 8
FF_BLK = 256
IN_BLK = 256

F32 = jnp.float32
BF16 = jnp.bfloat16


def _sigmoid(x):
    return jax.nn.sigmoid(x)


def _silu(x):
    return x * _sigmoid(x)


def _rms(x):
    return x * lax.rsqrt(jnp.mean(x * x, axis=-1, keepdims=True) + EPS)


def _layernorm(x, g, b=None):
    mu = jnp.mean(x, axis=-1, keepdims=True)
    d = x - mu
    var = jnp.mean(d * d, axis=-1, keepdims=True)
    y = d * lax.rsqrt(var + EPS) * g
    return y if b is None else y + b


def _rope_half(x, cos, sin, first_half):
    partner = jnp.where(first_half, pltpu.roll(x, 96, 1), pltpu.roll(x, 32, 1))
    return x * cos + partner * sin


def _first_half_mask(rows):
    lane = lax.broadcasted_iota(jnp.int32, (rows, LANES), 1)
    return (lane & (RET_DK - 1)) < (RET_DK // 2), lane < RET_DK


def _swiglu_block(h, w1_ref, w3_ref, col0):
    cols = pl.ds(col0, FF_BLK)
    a = jnp.dot(h, w1_ref[:, cols], preferred_element_type=F32)
    b = jnp.dot(h, w3_ref[:, cols], preferred_element_type=F32)
    return (_silu(a) * b).astype(BF16)


def _emit(items):
    for item in items:
        item()


def _emit_interleaved(primary, secondary):
    n, m = len(primary), len(secondary)
    done = 0
    for i, item in enumerate(primary):
        item()
        upto = ((i + 1) * m) // n
        _emit(secondary[done:upto])
        done = upto


def _rope_tables(pos, k_scale):
    half = RET_DK // 2
    inv = ROPE_BASE ** (-np.arange(half, dtype=np.float64) / half)
    ang = np.asarray(pos, np.float64)[:, None] * inv[None, :]
    cos = np.tile(np.cos(ang), (1, 4))
    sin = np.tile(np.concatenate([-np.sin(ang), np.sin(ang)], axis=1), (1, 2))
    return (cos * k_scale).astype(np.float32), (sin * k_scale).astype(np.float32)


def _decay_tables(chunk):
    log_g = np.log(1.0 - np.exp2(-5.0 - np.arange(RET_HEADS, dtype=np.float64)))
    idx = np.arange(chunk, dtype=np.float64)
    diff = idx[:, None] - idx[None, :]
    dmat = np.where(diff[None] >= 0, np.exp(np.maximum(diff, 0.0)[None] * log_g[:, None, None]), 0.0)
    read_dec = np.exp((idx + 1.0)[:, None] * log_g[None, :])
    upd_dec = np.exp((chunk - 1.0 - idx)[:, None] * log_g[None, :])
    chunk_dec = np.exp(chunk * log_g)
    rd = np.broadcast_to(read_dec.T[:, :, None], (RET_HEADS, chunk, RET_DV))
    upd = np.repeat(upd_dec, RET_DK, axis=1).reshape(chunk, 2, LANES).transpose(1, 0, 2)
    cd = np.broadcast_to(chunk_dec[:, None, None], (RET_HEADS, 1, RET_DV))
    f = lambda a: np.ascontiguousarray(a, dtype=np.float32)
    return f(dmat), f(rd), f(upd), f(cd)


def _const_spec(shape, grid_rank, single_buffer=True):
    zeros = (0,) * len(shape)
    idx = (lambda b, c: zeros) if grid_rank == 2 else (lambda j: zeros)
    if single_buffer:
        return pl.BlockSpec(shape, idx, pipeline_mode=pl.Buffered(1))
    return pl.BlockSpec(shape, idx)


def _params(n_axes):
    return pltpu.CompilerParams(dimension_semantics=("arbitrary",) * n_axes,
                                vmem_limit_bytes=V7X_VMEM_LIMIT_BYTES)


def _stacked_call(kern, *, name, grid, in_specs, args, out_specs, out_shape, n_stacked, prev, scratch_shapes):
    aliases = {}
    if prev is not None:
        n_in = len(args)
        n_out = len(out_shape)
        in_specs = list(in_specs) + [pl.BlockSpec(memory_space=pl.ANY)] * n_stacked
        args = list(args) + list(prev)
        aliases = {n_in + i: n_out - n_stacked + i for i in range(n_stacked)}
    return pl.pallas_call(
        kern, grid=grid, in_specs=in_specs, out_specs=out_specs, out_shape=out_shape,
        scratch_shapes=scratch_shapes, input_output_aliases=aliases,
        compiler_params=_params(len(grid)), name=name)(*args)


def _ada_kernel(c_ref, w_ref, b_ref, o_ref):
    s = _silu(c_ref[...]).astype(BF16)
    o_ref[...] = jnp.dot(s, w_ref[...].astype(BF16), preferred_element_type=F32) + b_ref[...]


def _ada_modulation(c_all, ada_w, ada_b):
    rows = c_all.shape[0]
    ncol = 6 * D_MODEL
    blk = D_MODEL
    return pl.pallas_call(
        _ada_kernel,
        grid=(DEPTH, ncol // blk),
        in_specs=[
            pl.BlockSpec((rows, D_MODEL), lambda l, j: (0, 0)),
            pl.BlockSpec((None, D_MODEL, blk), lambda l, j: (l, 0, j)),
            pl.BlockSpec((None, 1, blk), lambda l, j: (l, 0, j)),
        ],
        out_specs=pl.BlockSpec((None, rows, blk), lambda l, j: (l, 0, j)),
        out_shape=jax.ShapeDtypeStruct((DEPTH, rows, ncol), F32),
        compiler_params=_params(2),
        name="ada_mod",
    )(c_all, ada_w, ada_b.reshape(DEPTH, 1, ncol))


def _prompt_layer_kernel(*refs, tile, n_tiles, tiles_per_seq, n_in):
    (xc_ref, xp_ref, modc_ref, modp_ref, gpm_ref, gqm_ref, gpf_ref, gqf_ref,
     win_ref, wout_ref, w1_ref, w3_ref, w2_ref,
     cq_ref, sq_ref, ck_ref, sk_ref, dmat_ref, rd_ref, upd_ref, cd_ref,
     gn_ref, caw_ref, cab_ref, lng_ref, lnb_ref, ccw_ref) = refs[:27]
    (y_ref, sret_ref, sca_ref, scc_ref,
     h_scr, proj_scr, mixin_scr, mix_scr, x1_scr, hf_scr, p_scr, f_scr, ubuf, ua_scr, zbuf,
     sc_scr, qs_scr) = refs[n_in:]
    step = pl.program_id(0)
    row_blocks = [pl.ds(r * ROW_BLK, ROW_BLK) for r in range(tile // ROW_BLK)]
    first_tap = HIST_A - (CONV_A_K - 1)
    first_tap_c = HIST_C - (CONV_C_K - 1)
    span = SUBLANES * CONV_STRIDE

    @pl.when((step < n_tiles) & (lax.rem(step, tiles_per_seq) == 0))
    def _():
        sret_ref[...] = jnp.zeros_like(sret_ref)
        ubuf[:, 0:HIST_A, :] = jnp.zeros((2, HIST_A, LANES), F32)
        zbuf[0:HIST_C, :] = jnp.zeros((HIST_C, CONV_C_WIDTH), F32)

    def mix_prenorm():
        sh = modc_ref[0:1, :]
        pre_scale = gpm_ref[...] * (1.0 + modc_ref[1:2, :])
        for rows in row_blocks:
            h_scr[rows, :] = (_rms(xc_ref[rows, :]) * pre_scale + sh).astype(BF16)

    def in_proj(n):
        cols = pl.ds(n * IN_BLK, IN_BLK)
        proj_scr[:, cols] = jnp.dot(h_scr[...], win_ref[:, cols], preferred_element_type=F32)

    def conv_groups():
        taps = [[jnp.broadcast_to(caw_ref[j:j + 1, pl.ds(half * LANES, LANES)], (SUBLANES, LANES))
                 for j in range(CONV_A_K)] for half in range(2)]
        for r in range(tile // CONV_BLK):
            rows = pl.ds(r * CONV_BLK, CONV_BLK)
            dst = pl.ds(HIST_A + r * CONV_BLK, CONV_BLK)
            for half in range(2):
                lo = half * LANES
                u = (proj_scr[rows, pl.ds(OFF_AV + lo, LANES)]
                     * _sigmoid(proj_scr[rows, pl.ds(OFF_AG + lo, LANES)]))
                ubuf[half, dst, :] = u
        for half in range(2):
            for m in range(tile // span):
                base = m * span
                acc = [None] * CONV_STRIDE
                for k in range(CONV_STRIDE + CONV_A_K - 1):
                    win = ubuf[half, pl.ds(base + first_tap + k, SUBLANES, stride=CONV_STRIDE), :]
                    for t in range(CONV_STRIDE):
                        j = k - t
                        if 0 <= j < CONV_A_K:
                            term = taps[half][j] * win
                            acc[t] = term if acc[t] is None else acc[t] + term
                for t in range(CONV_STRIDE):
                    ua_scr[half, pl.ds(base + t, SUBLANES, stride=CONV_STRIDE), :] = acc[t]
        for r in range(tile // CONV_BLK):
            rows = pl.ds(r * CONV_BLK, CONV_BLK)
            ua = jnp.concatenate([ua_scr[0, rows, :], ua_scr[1, rows, :]], axis=-1) + cab_ref[...]
            o_a = _silu(_layernorm(ua, lng_ref[...], lnb_ref[...]))
            mixin_scr[rows, MIX_A:MIX_A + CONV_A_WIDTH] = o_a.astype(BF16)
        for r in range(tile // CONV_BLK):
            rows = pl.ds(r * CONV_BLK, CONV_BLK)
            z = proj_scr[rows, OFF_CC:OFF_CC + CONV_C_WIDTH] * proj_scr[rows, OFF_CX:OFF_CX + CONV_C_WIDTH]
            zbuf[pl.ds(HIST_C + r * CONV_BLK, CONV_BLK), :] = z
        for r in range(tile // CONV_BLK):
            rows = pl.ds(r * CONV_BLK, CONV_BLK)
            zc = ccw_ref[0:1, :] * zbuf[pl.ds(r * CONV_BLK + first_tap_c, CONV_BLK), :]
            for j in range(1, CONV_C_K):
                zc = zc + ccw_ref[j:j + 1, :] * zbuf[pl.ds(r * CONV_BLK + first_tap_c + j, CONV_BLK), :]
            o_c = proj_scr[rows, OFF_CB:OFF_CB + CONV_C_WIDTH] * zc
            mixin_scr[rows, MIX_C:MIX_C + CONV_C_WIDTH] = o_c.astype(BF16)
        hist_a = pl.ds(tile + first_tap, CONV_A_K - 1)
        sca_ref[...] = jnp.concatenate([ubuf[0, hist_a, :], ubuf[1, hist_a, :]], axis=-1)
        scc_ref[...] = zbuf[pl.ds(tile + first_tap_c, CONV_C_K - 1), :]
        ubuf[:, 0:HIST_A, :] = ubuf[:, pl.ds(tile, HIST_A), :]
        zbuf[0:HIST_C, :] = zbuf[pl.ds(tile, HIST_C), :]

    def retention_scores(i, pair):
        first_half, head_lo = _first_half_mask(RET_CHUNK)
        nt = (((1,), (1,)), ((), ()))
        rows = pl.ds(i * RET_CHUNK, RET_CHUNK)
        cq, sq, ck, sk = cq_ref[rows, :], sq_ref[rows, :], ck_ref[rows, :], sk_ref[rows, :]
        qr = _rope_half(proj_scr[rows, pl.ds(OFF_Q + pair * LANES, LANES)], cq, sq, first_half)
        kr = _rope_half(proj_scr[rows, pl.ds(OFF_K + pair * LANES, LANES)], ck, sk, first_half)
        kr_b = kr.astype(BF16)
        ku_t = (kr * upd_ref[pair]).T.astype(BF16)
        s_pair = sret_ref[2 * pair:2 * pair + 2].reshape(2 * RET_DK, RET_DV).astype(BF16)
        for hl in range(2):
            h = 2 * pair + hl
            qm = jnp.where(head_lo if hl == 0 else jnp.logical_not(head_lo), qr, 0.0).astype(BF16)
            scores = lax.dot_general(qm, kr_b, nt, preferred_element_type=F32) * dmat_ref[h]
            sc_scr[hl] = scores.astype(BF16)
            qs_scr[hl] = jnp.dot(qm, s_pair, preferred_element_type=F32) * rd_ref[h]
            vh = proj_scr[rows, pl.ds(OFF_V + h * RET_DV, RET_DV)].astype(BF16)
            sret_ref[h] = (sret_ref[h] * cd_ref[h]
                           + jnp.dot(ku_t[hl * RET_DK:(hl + 1) * RET_DK, :], vh, preferred_element_type=F32))

    def retention_values(i, pair):
        rows = pl.ds(i * RET_CHUNK, RET_CHUNK)
        for hl in range(2):
            h = 2 * pair + hl
            vh = proj_scr[rows, pl.ds(OFF_V + h * RET_DV, RET_DV)].astype(BF16)
            o = jnp.dot(sc_scr[hl], vh, preferred_element_type=F32) + qs_scr[hl]
            gate = proj_scr[rows, pl.ds(OFF_G + h * RET_DV, RET_DV)]
            o = _layernorm(o, gn_ref[:, pl.ds(h * RET_DV, RET_DV)]) * _silu(gate)
            mixin_scr[rows, pl.ds(h * RET_DV, RET_DV)] = o.astype(BF16)

    def out_proj():
        mix_scr[...] = jnp.dot(mixin_scr[...], wout_ref[...], preferred_element_type=F32)

    in_tiles = [functools.partial(in_proj, n) for n in range(IN_WIDTH // IN_BLK)]
    ret_items = [functools.partial(phase, i, pair)
                 for i in range(tile // RET_CHUNK) for pair in range(RET_HEADS // 2)
                 for phase in (retention_scores, retention_values)]

    def ffn_norms():
        post_scale = gqm_ref[...] * modp_ref[2:3, :]
        sh = modp_ref[3:4, :]
        pre_scale = gpf_ref[...] * (1.0 + modp_ref[4:5, :])
        for rows in row_blocks:
            x1 = xp_ref[rows, :] + _rms(mix_scr[rows, :]) * post_scale
            x1_scr[rows, :] = x1
            hf_scr[rows, :] = (_rms(x1) * pre_scale + sh).astype(BF16)

    def hidden(j):
        p_scr[:, pl.ds(j * FF_BLK, FF_BLK)] = _swiglu_block(hf_scr[...], w1_ref, w3_ref, j * FF_BLK)

    def down(n):
        cols = pl.ds(n * FF_BLK, FF_BLK)
        f_scr[:, cols] = jnp.dot(p_scr[...], w2_ref[:, cols], preferred_element_type=F32)

    def ffn_finish():
        post_scale = gqf_ref[...] * modp_ref[5:6, :]
        for rows in row_blocks:
            y_ref[rows, :] = x1_scr[rows, :] + _rms(f_scr[rows, :]) * post_scale

    hidden_blocks = [functools.partial(hidden, j) for j in range(D_FF // FF_BLK)]
    down_tiles = [functools.partial(down, n) for n in range(D_MODEL // FF_BLK)]

    @pl.when(step == 0)
    def _():
        mix_prenorm()
        _emit(in_tiles)
        _emit(ret_items)
        conv_groups()
        out_proj()

    @pl.when((step > 0) & (step < n_tiles))
    def _():
        mix_prenorm()
        ffn_norms()
        _emit(in_tiles)
        _emit_interleaved(hidden_blocks, ret_items)
        conv_groups()
        _emit(down_tiles)
        out_proj()
        ffn_finish()

    @pl.when(step == n_tiles)
    def _():
        ffn_norms()
        _emit(hidden_blocks)
        _emit(down_tiles)
        ffn_finish()


def _prompt_layer(layer, prev, x, mod_p, g_pm, g_qm, g_pf, g_qf, w_in, w_out, w1, w3, w2, tables,
                  gn, caw, cab, lng, lnb, ccw):
    bsz, seq, _ = x.shape
    tile = TOK_TILE
    tps = seq // tile
    n_tiles = bsz * tps
    cq, sq, ck, sk, dmat, rd, upd, cd = tables
    cur = lambda i: jnp.minimum(i, n_tiles - 1)
    prv = lambda i: jnp.maximum(i - 1, 0)
    tok_cur = pl.BlockSpec((None, tile, D_MODEL), lambda i: (cur(i) // tps, cur(i) % tps, 0))
    tok_prev = pl.BlockSpec((None, tile, D_MODEL), lambda i: (prv(i) // tps, prv(i) % tps, 0))
    mod_cur = pl.BlockSpec((None, 6, D_MODEL), lambda i: (cur(i) // tps, 0, 0))
    mod_prev = pl.BlockSpec((None, 6, D_MODEL), lambda i: (prv(i) // tps, 0, 0))
    rope = pl.BlockSpec((tile, LANES), lambda i: (cur(i) % tps, 0))
    cs = functools.partial(_const_spec, grid_rank=1)
    args = (x, x, mod_p, mod_p, g_pm, g_qm, g_pf, g_qf, w_in, w_out, w1, w3, w2,
            cq, sq, ck, sk, dmat, rd, upd, cd, gn, caw, cab, lng, lnb, ccw)
    n_in = len(args) + (3 if prev is not None else 0)
    return _stacked_call(
        functools.partial(_prompt_layer_kernel, tile=tile, n_tiles=n_tiles, tiles_per_seq=tps, n_in=n_in),
        name="prompt_layer", grid=(n_tiles + 1,), args=args, n_stacked=3, prev=prev,
        in_specs=[
            tok_cur, tok_prev, mod_cur, mod_prev,
            cs((1, D_MODEL)), cs((1, D_MODEL)), cs((1, D_MODEL)), cs((1, D_MODEL)),
            cs((D_MODEL, IN_WIDTH)), cs((D_MODEL, D_MODEL)),
            cs((D_MODEL, D_FF)), cs((D_MODEL, D_FF)), cs((D_FF, D_MODEL)),
            rope, rope, rope, rope,
            cs(dmat.shape), cs(rd.shape), cs(upd.shape), cs(cd.shape),
            cs((1, RET_WIDTH)), cs((CONV_A_K, CONV_A_WIDTH)), cs((1, CONV_A_WIDTH)),
            cs((1, CONV_A_WIDTH)), cs((1, CONV_A_WIDTH)), cs((CONV_C_K, CONV_C_WIDTH)),
        ],
        out_specs=[
            tok_prev,
            pl.BlockSpec((None, None, RET_HEADS, RET_DK, RET_DV), lambda i: (layer, cur(i) // tps, 0, 0, 0)),
            pl.BlockSpec((None, None, CONV_A_K - 1, CONV_A_WIDTH), lambda i: (layer, cur(i) // tps, 0, 0)),
            pl.BlockSpec((None, None, CONV_C_K - 1, CONV_C_WIDTH), lambda i: (layer, cur(i) // tps, 0, 0)),
        ],
        out_shape=[
            jax.ShapeDtypeStruct(x.shape, F32),
            jax.ShapeDtypeStruct((DEPTH, bsz, RET_HEADS, RET_DK, RET_DV), F32),
            jax.ShapeDtypeStruct((DEPTH, bsz, CONV_A_K - 1, CONV_A_WIDTH), F32),
            jax.ShapeDtypeStruct((DEPTH, bsz, CONV_C_K - 1, CONV_C_WIDTH), F32),
        ],
        scratch_shapes=[
            pltpu.VMEM((tile, D_MODEL), BF16),
            pltpu.VMEM((tile, IN_WIDTH), F32),
            pltpu.VMEM((tile, D_MODEL), BF16),
            pltpu.VMEM((tile, D_MODEL), F32),
            pltpu.VMEM((tile, D_MODEL), F32),
            pltpu.VMEM((tile, D_MODEL), BF16),
            pltpu.VMEM((tile, D_FF), BF16),
            pltpu.VMEM((tile, D_MODEL), F32),
            pltpu.VMEM((2, tile + HIST_A, LANES), F32),
            pltpu.VMEM((2, tile, LANES), F32),
            pltpu.VMEM((tile + HIST_C, CONV_C_WIDTH), F32),
            pltpu.VMEM((2, RET_CHUNK, RET_CHUNK), BF16),
            pltpu.VMEM((2, RET_CHUNK, RET_DV), F32),
        ])


def _sample_pre_kernel(x_ref, mod_ref, gpre_ref, win_ref, rope_ref, proj_ref, qt_ref, kt_ref):
    n = x_ref.shape[0]
    sh = mod_ref[:, 0:D_MODEL]
    sc1 = 1.0 + mod_ref[:, D_MODEL:2 * D_MODEL]
    h = (_rms(x_ref[...]) * (gpre_ref[...] * sc1) + sh).astype(BF16)
    proj_ref[...] = jnp.dot(h, win_ref[...], preferred_element_type=F32)
    first_half, _ = _first_half_mask(n)
    cq, sq, ck, sk = rope_ref[0:1, :], rope_ref[1:2, :], rope_ref[2:3, :], rope_ref[3:4, :]
    for pair in range(RET_HEADS // 2):
        ql = pl.ds(OFF_Q + pair * LANES, LANES)
        kl = pl.ds(OFF_K + pair * LANES, LANES)
        qr = _rope_half(proj_ref[:, ql], cq, sq, first_half)
        kr = _rope_half(proj_ref[:, kl], ck, sk, first_half)
        qt_ref[pl.ds(pair * LANES, LANES), :] = qr.T
        kt_ref[pl.ds(pair * LANES, LANES), :] = kr.T


def _sample_pre(xs, mod_s, gpre, w_in, rope_s):
    n = xs.shape[0]
    cs = functools.partial(_const_spec, grid_rank=1, single_buffer=False)
    return pl.pallas_call(
        _sample_pre_kernel,
        grid=(1,),
        in_specs=[cs((n, D_MODEL)), cs((n, 6 * D_MODEL)), cs((1, D_MODEL)),
                  cs((D_MODEL, IN_WIDTH)), cs(rope_s.shape)],
        out_specs=[cs((n, IN_WIDTH)), cs((QK_WIDTH, n)), cs((QK_WIDTH, n))],
        out_shape=[jax.ShapeDtypeStruct((n, IN_WIDTH), F32),
                   jax.ShapeDtypeStruct((QK_WIDTH, n), F32),
                   jax.ShapeDtypeStruct((QK_WIDTH, n), F32)],
        compiler_params=_params(1),
        name="sample_pre",
    )(xs, mod_s, gpre, w_in, rope_s)


def _sample_state_kernel(*refs, n_in):
    (proj_ref, qt_ref, kt_ref, sin_ref, bufa_ref, bufc_ref, cd_ref,
     gn_ref, caw_ref, cab_ref, lng_ref, lnb_ref, ccw_ref) = refs[:13]
    mix_ref, sout_ref, outa_ref, outc_ref, o_scr, ua_scr, zc_scr = refs[n_in:]
    blk = pl.program_id(0)
    n = qt_ref.shape[1]
    lane = lax.broadcasted_iota(jnp.int32, (QK_WIDTH, n), 1)

    for bl in range(SEQ_BLK):
        onehot = lane == (blk * SEQ_BLK + bl)
        qcol = jnp.sum(jnp.where(onehot, qt_ref[...], 0.0), axis=1, keepdims=True)
        kcol = jnp.sum(jnp.where(onehot, kt_ref[...], 0.0), axis=1, keepdims=True)
        row = pl.ds(bl, 1)
        for h in range(RET_HEADS):
            vrow = proj_ref[row, pl.ds(OFF_V + h * RET_DV, RET_DV)]
            s_new = (sin_ref[bl, h] * cd_ref[h]
                     + kcol[h * RET_DK:(h + 1) * RET_DK, :] * vrow)
            sout_ref[bl, h] = s_new
            o_scr[row, pl.ds(h * RET_DV, RET_DV)] = jnp.sum(
                qcol[h * RET_DK:(h + 1) * RET_DK, :] * s_new, axis=0, keepdims=True)
        urow = proj_ref[row, OFF_AV:OFF_AV + CONV_A_WIDTH] * _sigmoid(proj_ref[row, OFF_AG:OFF_AG + CONV_A_WIDTH])
        ba = bufa_ref[bl]
        ua_scr[row, :] = (jnp.sum(ba * caw_ref[0:CONV_A_K - 1, :], axis=0, keepdims=True)
                          + caw_ref[CONV_A_K - 1:CONV_A_K, :] * urow)
        outa_ref[bl, 0:CONV_A_K - 2, :] = ba[1:CONV_A_K - 1, :]
        outa_ref[bl, CONV_A_K - 2:CONV_A_K - 1, :] = urow
        zrow = proj_ref[row, OFF_CC:OFF_CC + CONV_C_WIDTH] * proj_ref[row, OFF_CX:OFF_CX + CONV_C_WIDTH]
        bc = bufc_ref[bl]
        zc_scr[row, :] = (ccw_ref[0:1, :] * bc[0:1, :] + ccw_ref[1:2, :] * bc[1:2, :] + ccw_ref[2:3, :] * zrow)
        outc_ref[bl, 0:1, :] = bc[1:2, :]
        outc_ref[bl, 1:2, :] = zrow

    for h in range(RET_HEADS):
        cols = pl.ds(h * RET_DV, RET_DV)
        o = _layernorm(o_scr[:, cols], gn_ref[:, cols]) * _silu(proj_ref[:, pl.ds(OFF_G + h * RET_DV, RET_DV)])
        mix_ref[:, cols] = o
    mix_ref[:, MIX_A:MIX_A + CONV_A_WIDTH] = _silu(
        _layernorm(ua_scr[...] + cab_ref[...], lng_ref[...], lnb_ref[...]))
    mix_ref[:, MIX_C:MIX_C + CONV_C_WIDTH] = proj_ref[:, OFF_CB:OFF_CB + CONV_C_WIDTH] * zc_scr[...]


def _sample_state(layer, prev, proj, qt, kt, s_ret, buf_a, buf_c, cd, gn, caw, cab, lng, lnb, ccw):
    n = proj.shape[0]
    cs = functools.partial(_const_spec, grid_rank=1, single_buffer=False)
    s_spec = pl.BlockSpec((None, SEQ_BLK, RET_HEADS, RET_DK, RET_DV), lambda j: (layer, j, 0, 0, 0))
    a_spec = pl.BlockSpec((None, SEQ_BLK, CONV_A_K - 1, CONV_A_WIDTH), lambda j: (layer, j, 0, 0))
    c_spec = pl.BlockSpec((None, SEQ_BLK, CONV_C_K - 1, CONV_C_WIDTH), lambda j: (layer, j, 0, 0))
    args = (proj, qt, kt, s_ret, buf_a, buf_c, cd, gn, caw, cab, lng, lnb, ccw)
    n_in = len(args) + (3 if prev is not None else 0)
    return _stacked_call(
        functools.partial(_sample_state_kernel, n_in=n_in),
        name="sample_state", grid=(n // SEQ_BLK,), args=args, n_stacked=3, prev=prev,
        in_specs=[
            pl.BlockSpec((SEQ_BLK, IN_WIDTH), lambda j: (j, 0)),
            cs((QK_WIDTH, n)), cs((QK_WIDTH, n)),
            s_spec, a_spec, c_spec,
            cs(cd.shape), cs((1, RET_WIDTH)), cs((CONV_A_K, CONV_A_WIDTH)), cs((1, CONV_A_WIDTH)),
            cs((1, CONV_A_WIDTH)), cs((1, CONV_A_WIDTH)), cs((CONV_C_K, CONV_C_WIDTH)),
        ],
        out_specs=[pl.BlockSpec((SEQ_BLK, D_MODEL), lambda j: (j, 0)), s_spec, a_spec, c_spec],
        out_shape=[
            jax.ShapeDtypeStruct((n, D_MODEL), F32),
            jax.ShapeDtypeStruct(s_ret.shape, F32),
            jax.ShapeDtypeStruct(buf_a.shape, F32),
            jax.ShapeDtypeStruct(buf_c.shape, F32),
        ],
        scratch_shapes=[
            pltpu.VMEM((SEQ_BLK, RET_WIDTH), F32),
            pltpu.VMEM((SEQ_BLK, CONV_A_WIDTH), F32),
            pltpu.VMEM((SEQ_BLK, CONV_C_WIDTH), F32),
        ])


def _sample_post_kernel(x_ref, mod_ref, mix_ref, wout_ref, gpm_ref, gpf_ref, gqf_ref, w1_ref, w3_ref, w2_ref,
                        y_ref, x1_scr, h_scr, f_scr, *, n_steps):
    j = pl.program_id(0)

    @pl.when(j == 0)
    def _():
        gt_m = mod_ref[:, 2 * D_MODEL:3 * D_MODEL]
        mix = jnp.dot(mix_ref[...].astype(BF16), wout_ref[...], preferred_element_type=F32)
        x1 = x_ref[...] + _rms(mix) * (gpm_ref[...] * gt_m)
        x1_scr[...] = x1
        sh = mod_ref[:, 3 * D_MODEL:4 * D_MODEL]
        sc1 = 1.0 + mod_ref[:, 4 * D_MODEL:5 * D_MODEL]
        h_scr[...] = (_rms(x1) * (gpf_ref[...] * sc1) + sh).astype(BF16)
        f_scr[...] = jnp.zeros_like(f_scr)

    p = _swiglu_block(h_scr[...], w1_ref, w3_ref, 0)
    f_scr[...] += jnp.dot(p, w2_ref[...], preferred_element_type=F32)

    @pl.when(j == n_steps - 1)
    def _():
        gt_f = mod_ref[:, 5 * D_MODEL:6 * D_MODEL]
        y_ref[...] = x1_scr[...] + _rms(f_scr[...]) * (gqf_ref[...] * gt_f)


def _sample_post(xs, mod_s, mix, w_out, gpost_m, gpre_f, gpost_f, w1, w3, w2):
    n = xs.shape[0]
    n_steps = D_FF // FF_BLK
    cs = functools.partial(_const_spec, grid_rank=1, single_buffer=False)
    return pl.pallas_call(
        functools.partial(_sample_post_kernel, n_steps=n_steps),
        grid=(n_steps,),
        in_specs=[
            cs((n, D_MODEL)), cs((n, 6 * D_MODEL)), cs((n, D_MODEL)), cs((D_MODEL, D_MODEL)),
            cs((1, D_MODEL)), cs((1, D_MODEL)), cs((1, D_MODEL)),
            pl.BlockSpec((D_MODEL, FF_BLK), lambda j: (0, j)),
            pl.BlockSpec((D_MODEL, FF_BLK), lambda j: (0, j)),
            pl.BlockSpec((FF_BLK, D_MODEL), lambda j: (j, 0)),
        ],
        out_specs=cs((n, D_MODEL)),
        out_shape=jax.ShapeDtypeStruct((n, D_MODEL), F32),
        scratch_shapes=[
            pltpu.VMEM((n, D_MODEL), F32),
            pltpu.VMEM((n, D_MODEL), BF16),
            pltpu.VMEM((n, D_MODEL), F32),
        ],
        compiler_params=_params(1),
        name="sample_post",
    )(xs, mod_s, mix, w_out, gpost_m, gpre_f, gpost_f, w1, w3, w2)


def kernel(x_prompt, x_sample, c_prompt, c_sample, state_ret, state_conv_a, state_conv_c, ada_w, ada_b, norm_pre_mix, norm_post_mix, norm_pre_ffn, norm_post_ffn, w_in, w_out, ret_gn_g, conv_a_w, conv_a_b, conv_a_ln_g, conv_a_ln_b, conv_c_w, ffn_w1, ffn_w3, ffn_w2):
    bp, lp, _ = x_prompt.shape
    ns = x_sample.shape[0]
    assert x_sample.shape[1] == 1 and ns % SEQ_BLK == 0
    assert lp % TOK_TILE == 0 and TOK_TILE % RET_CHUNK == 0
    assert TOK_TILE % CONV_BLK == 0 and CONV_BLK % (SUBLANES * CONV_STRIDE) == 0

    k_scale = RET_DK ** -0.5
    cq, sq = _rope_tables(np.arange(lp), 1.0)
    ck, sk = _rope_tables(np.arange(lp), k_scale)
    tables_p = (cq, sq, ck, sk) + _decay_tables(RET_CHUNK)
    cqs, sqs = _rope_tables([PAST_LEN], 1.0)
    cks, sks = _rope_tables([PAST_LEN], k_scale)
    rope_s = np.concatenate([cqs, sqs, cks, sks, np.zeros((4, LANES), np.float32)], axis=0)
    cd_s = _decay_tables(1)[3]

    mod = _ada_modulation(jnp.concatenate([c_prompt, c_sample], axis=0), ada_w, ada_b)

    row = lambda a: a.reshape(1, -1)
    yp = x_prompt
    ys = x_sample.reshape(ns, D_MODEL)
    st_p = None
    st_s = None
    for l in range(DEPTH):
        w_in_b = w_in[l].astype(BF16)
        w_out_b = w_out[l].astype(BF16)
        w1_b, w3_b, w2_b = ffn_w1[l].astype(BF16), ffn_w3[l].astype(BF16), ffn_w2[l].astype(BF16)
        gn, cab, lng, lnb = row(ret_gn_g[l]), row(conv_a_b[l]), row(conv_a_ln_g[l]), row(conv_a_ln_b[l])
        g_pm, g_qm = row(norm_pre_mix[l]), row(norm_post_mix[l])
        g_pf, g_qf = row(norm_pre_ffn[l]), row(norm_post_ffn[l])

        mod_p = mod[l, :bp].reshape(bp, 6, D_MODEL)
        yp, *st_p = _prompt_layer(l, st_p, yp, mod_p, g_pm, g_qm, g_pf, g_qf, w_in_b, w_out_b, w1_b, w3_b, w2_b,
                                  tables_p, gn, conv_a_w[l], cab, lng, lnb, conv_c_w[l])

        mod_s = mod[l, bp:]
        proj, qt, kt = _sample_pre(ys, mod_s, g_pm, w_in_b, rope_s)
        mix, *st_s = _sample_state(l, st_s, proj, qt, kt, state_ret, state_conv_a, state_conv_c, cd_s,
                                   gn, conv_a_w[l], cab, lng, lnb, conv_c_w[l])
        ys = _sample_post(ys, mod_s, mix, w_out_b, g_qm, g_pf, g_qf, w1_b, w3_b, w2_b)

    return (yp, ys.reshape(ns, 1, D_MODEL)) + tuple(st_p) + tuple(st_s)
```

```python
import functools

import numpy as np
import jax
import jax.numpy as jnp
from jax import lax
from jax.experimental import pallas as pl
from jax.experimental.pallas import tpu as pltpu

D_MODEL = 1024
DEPTH = 2
PAST_LEN = 16384
RET_HEADS = 4
RET_WIDTH = D_MODEL // 2
RET_DV = RET_WIDTH // RET_HEADS
RET_DK = RET_DV // 2
QK_WIDTH = RET_HEADS * RET_DK
CONV_A_WIDTH = D_MODEL // 4
CONV_A_K = 31
CONV_C_WIDTH = D_MODEL - RET_WIDTH - CONV_A_WIDTH
CONV_C_K = 3
IN_WIDTH = 2 * QK_WIDTH + 2 * RET_WIDTH + 2 * CONV_A_WIDTH + 3 * CONV_C_WIDTH
D_FF = ((8 * D_MODEL // 3 + 255) // 256) * 256
RET_CHUNK = 128
ROPE_BASE = 10000.0
EPS = 1e-6

OFF_Q = 0
OFF_K = OFF_Q + QK_WIDTH
OFF_V = OFF_K + QK_WIDTH
OFF_G = OFF_V + RET_WIDTH
OFF_AV = OFF_G + RET_WIDTH
OFF_AG = OFF_AV + CONV_A_WIDTH
OFF_CB = OFF_AG + CONV_A_WIDTH
OFF_CC = OFF_CB + CONV_C_WIDTH
OFF_CX = OFF_CC + CONV_C_WIDTH
MIX_A = RET_WIDTH
MIX_C = RET_WIDTH + CONV_A_WIDTH

LANES = 128
SUBLANES = 8
V7X_VMEM_LIMIT_BYTES = 56 * 1024 * 1024

TOK_TILE = 256
ROW_BLK = 32
CONV_BLK = 64
CONV_STRIDE = 4
HIST_A = 32
HIST_C = 8
SEQ_BLK = 8
FF_BLK = 256
IN_BLK = 256

F32 = jnp.float32
BF16 = jnp.bfloat16


def _sigmoid(x):
    return jax.nn.sigmoid(x)


def _silu(x):
    return x * _sigmoid(x)


def _rms(x):
    return x * lax.rsqrt(jnp.mean(x * x, axis=-1, keepdims=True) + EPS)


def _layernorm(x, g, b=None):
    mu = jnp.mean(x, axis=-1, keepdims=True)
    d = x - mu
    var = jnp.mean(d * d, axis=-1, keepdims=True)
    y = d * lax.rsqrt(var + EPS) * g
    return y if b is None else y + b


def _rope_half(x, cos, sin, first_half):
    partner = jnp.where(first_half, pltpu.roll(x, 96, 1), pltpu.roll(x, 32, 1))
    return x * cos + partner * sin


def _first_half_mask(rows):
    lane = lax.broadcasted_iota(jnp.int32, (rows, LANES), 1)
    return (lane & (RET_DK - 1)) < (RET_DK // 2), lane < RET_DK


def _swiglu_block(h, w1_ref, w3_ref, col0):
    cols = pl.ds(col0, FF_BLK)
    a = jnp.dot(h, w1_ref[:, cols], preferred_element_type=F32)
    b = jnp.dot(h, w3_ref[:, cols], preferred_element_type=F32)
    return (_silu(a) * b).astype(BF16)


def _emit(items):
    for item in items:
        item()


def _emit_interleaved(primary, secondary):
    n, m = len(primary), len(secondary)
    done = 0
    for i, item in enumerate(primary):
        item()
        upto = ((i + 1) * m) // n
        _emit(secondary[done:upto])
        done = upto


def _rope_tables(pos, k_scale):
    half = RET_DK // 2
    inv = ROPE_BASE ** (-np.arange(half, dtype=np.float64) / half)
    ang = np.asarray(pos, np.float64)[:, None] * inv[None, :]
    cos = np.tile(np.cos(ang), (1, 4))
    sin = np.tile(np.concatenate([-np.sin(ang), np.sin(ang)], axis=1), (1, 2))
    return (cos * k_scale).astype(np.float32), (sin * k_scale).astype(np.float32)


def _decay_tables(chunk):
    log_g = np.log(1.0 - np.exp2(-5.0 - np.arange(RET_HEADS, dtype=np.float64)))
    idx = np.arange(chunk, dtype=np.float64)
    diff = idx[:, None] - idx[None, :]
    dmat = np.where(diff[None] >= 0, np.exp(np.maximum(diff, 0.0)[None] * log_g[:, None, None]), 0.0)
    read_dec = np.exp((idx + 1.0)[:, None] * log_g[None, :])
    upd_dec = np.exp((chunk - 1.0 - idx)[:, None] * log_g[None, :])
    chunk_dec = np.exp(chunk * log_g)
    rd = np.broadcast_to(read_dec.T[:, :, None], (RET_HEADS, chunk, RET_DV))
    upd = np.repeat(upd_dec, RET_DK, axis=1).reshape(chunk, 2, LANES).transpose(1, 0, 2)
    cd = np.broadcast_to(chunk_dec[:, None, None], (RET_HEADS, 1, RET_DV))
    f = lambda a: np.ascontiguousarray(a, dtype=np.float32)
    return f(dmat), f(rd), f(upd), f(cd)


def _const_spec(shape, grid_rank, single_buffer=True):
    zeros = (0,) * len(shape)
    idx = (lambda b, c: zeros) if grid_rank == 2 else (lambda j: zeros)
    if single_buffer:
        return pl.BlockSpec(shape, idx, pipeline_mode=pl.Buffered(1))
    return pl.BlockSpec(shape, idx)


def _layer_spec(shape, layer, single_buffer=True):
    zeros = (0,) * len(shape)
    mode = dict(pipeline_mode=pl.Buffered(1)) if single_buffer else {}
    return pl.BlockSpec((None,) + tuple(shape), lambda j: (layer,) + zeros, **mode)


def _params(n_axes):
    return pltpu.CompilerParams(dimension_semantics=("arbitrary",) * n_axes,
                                vmem_limit_bytes=V7X_VMEM_LIMIT_BYTES)


def _stacked_call(kern, *, name, grid, in_specs, args, out_specs, out_shape, n_stacked, prev, scratch_shapes):
    aliases = {}
    if prev is not None:
        n_in = len(args)
        n_out = len(out_shape)
        in_specs = list(in_specs) + [pl.BlockSpec(memory_space=pl.ANY)] * n_stacked
        args = list(args) + list(prev)
        aliases = {n_in + i: n_out - n_stacked + i for i in range(n_stacked)}
    return pl.pallas_call(
        kern, grid=grid, in_specs=in_specs, out_specs=out_specs, out_shape=out_shape,
        scratch_shapes=scratch_shapes, input_output_aliases=aliases,
        compiler_params=_params(len(grid)), name=name)(*args)


def _ada_kernel(c_ref, w_ref, b_ref, o_ref):
    s = _silu(c_ref[...]).astype(BF16)
    o_ref[...] = jnp.dot(s, w_ref[...].astype(BF16), preferred_element_type=F32) + b_ref[...]


def _ada_modulation(c_all, ada_w, ada_b):
    rows = c_all.shape[0]
    ncol = 6 * D_MODEL
    blk = D_MODEL
    return pl.pallas_call(
        _ada_kernel,
        grid=(DEPTH, ncol // blk),
        in_specs=[
            pl.BlockSpec((rows, D_MODEL), lambda l, j: (0, 0)),
            pl.BlockSpec((None, D_MODEL, blk), lambda l, j: (l, 0, j)),
            pl.BlockSpec((None, 1, blk), lambda l, j: (l, 0, j)),
        ],
        out_specs=pl.BlockSpec((None, rows, blk), lambda l, j: (l, 0, j)),
        out_shape=jax.ShapeDtypeStruct((DEPTH, rows, ncol), F32),
        compiler_params=_params(2),
        name="ada_mod",
    )(c_all, ada_w, ada_b.reshape(DEPTH, 1, ncol))


def _prompt_layer_kernel(*refs, tile, n_tiles, tiles_per_seq, n_in):
    (xc_ref, xp_ref, modc_ref, modp_ref, gpm_ref, gqm_ref, gpf_ref, gqf_ref,
     win_ref, wout_ref, w1_ref, w3_ref, w2_ref,
     cq_ref, sq_ref, ck_ref, sk_ref, dmat_ref, rd_ref, upd_ref, cd_ref,
     gn_ref, caw_ref, cab_ref, lng_ref, lnb_ref, ccw_ref) = refs[:27]
    (y_ref, sret_ref, sca_ref, scc_ref,
     h_scr, proj_scr, mixin_scr, mix_scr, x1_scr, hf_scr, p_scr, f_scr, ubuf, ua_scr, zbuf,
     sc_scr, qs_scr) = refs[n_in:]
    step = pl.program_id(0)
    row_blocks = [pl.ds(r * ROW_BLK, ROW_BLK) for r in range(tile // ROW_BLK)]
    first_tap = HIST_A - (CONV_A_K - 1)
    first_tap_c = HIST_C - (CONV_C_K - 1)
    span = SUBLANES * CONV_STRIDE

    @pl.when((step < n_tiles) & (lax.rem(step, tiles_per_seq) == 0))
    def _():
        sret_ref[...] = jnp.zeros_like(sret_ref)
        ubuf[:, 0:HIST_A, :] = jnp.zeros((2, HIST_A, LANES), F32)
        zbuf[0:HIST_C, :] = jnp.zeros((HIST_C, CONV_C_WIDTH), F32)

    def mix_prenorm():
        sh = modc_ref[0:1, :]
        pre_scale = gpm_ref[...] * (1.0 + modc_ref[1:2, :])
        for rows in row_blocks:
            h_scr[rows, :] = (_rms(xc_ref[rows, :]) * pre_scale + sh).astype(BF16)

    def in_proj(n):
        cols = pl.ds(n * IN_BLK, IN_BLK)
        proj_scr[:, cols] = jnp.dot(h_scr[...], win_ref[:, cols], preferred_element_type=F32)

    def conv_groups():
        taps = [[jnp.broadcast_to(caw_ref[j:j + 1, pl.ds(half * LANES, LANES)], (SUBLANES, LANES))
                 for j in range(CONV_A_K)] for half in range(2)]
        for r in range(tile // CONV_BLK):
            rows = pl.ds(r * CONV_BLK, CONV_BLK)
            dst = pl.ds(HIST_A + r * CONV_BLK, CONV_BLK)
            for half in range(2):
                lo = half * LANES
                u = (proj_scr[rows, pl.ds(OFF_AV + lo, LANES)]
                     * _sigmoid(proj_scr[rows, pl.ds(OFF_AG + lo, LANES)]))
                ubuf[half, dst, :] = u
        for half in range(2):
            for m in range(tile // span):
                base = m * span
                acc = [None] * CONV_STRIDE
                for k in range(CONV_STRIDE + CONV_A_K - 1):
                    win = ubuf[half, pl.ds(base + first_tap + k, SUBLANES, stride=CONV_STRIDE), :]
                    for t in range(CONV_STRIDE):
                        j = k - t
                        if 0 <= j < CONV_A_K:
                            term = taps[half][j] * win
                            acc[t] = term if acc[t] is None else acc[t] + term
                for t in range(CONV_STRIDE):
                    ua_scr[half, pl.ds(base + t, SUBLANES, stride=CONV_STRIDE), :] = acc[t]
        for r in range(tile // CONV_BLK):
            rows = pl.ds(r * CONV_BLK, CONV_BLK)
            ua = jnp.concatenate([ua_scr[0, rows, :], ua_scr[1, rows, :]], axis=-1) + cab_ref[...]
            o_a = _silu(_layernorm(ua, lng_ref[...], lnb_ref[...]))
            mixin_scr[rows, MIX_A:MIX_A + CONV_A_WIDTH] = o_a.astype(BF16)
        for r in range(tile // CONV_BLK):
            rows = pl.ds(r * CONV_BLK, CONV_BLK)
            z = proj_scr[rows, OFF_CC:OFF_CC + CONV_C_WIDTH] * proj_scr[rows, OFF_CX:OFF_CX + CONV_C_WIDTH]
            zbuf[pl.ds(HIST_C + r * CONV_BLK, CONV_BLK), :] = z
        for r in range(tile // CONV_BLK):
            rows = pl.ds(r * CONV_BLK, CONV_BLK)
            zc = ccw_ref[0:1, :] * zbuf[pl.ds(r * CONV_BLK + first_tap_c, CONV_BLK), :]
            for j in range(1, CONV_C_K):
                zc = zc + ccw_ref[j:j + 1, :] * zbuf[pl.ds(r * CONV_BLK + first_tap_c + j, CONV_BLK), :]
            o_c = proj_scr[rows, OFF_CB:OFF_CB + CONV_C_WIDTH] * zc
            mixin_scr[rows, MIX_C:MIX_C + CONV_C_WIDTH] = o_c.astype(BF16)
        hist_a = pl.ds(tile + first_tap, CONV_A_K - 1)
        sca_ref[...] = jnp.concatenate([ubuf[0, hist_a, :], ubuf[1, hist_a, :]], axis=-1)
        scc_ref[...] = zbuf[pl.ds(tile + first_tap_c, CONV_C_K - 1), :]
        ubuf[:, 0:HIST_A, :] = ubuf[:, pl.ds(tile, HIST_A), :]
        zbuf[0:HIST_C, :] = zbuf[pl.ds(tile, HIST_C), :]

    def retention_scores(i, pair):
        first_half, head_lo = _first_half_mask(RET_CHUNK)
        nt = (((1,), (1,)), ((), ()))
        rows = pl.ds(i * RET_CHUNK, RET_CHUNK)
        cq, sq, ck, sk = cq_ref[rows, :], sq_ref[rows, :], ck_ref[rows, :], sk_ref[rows, :]
        qr = _rope_half(proj_scr[rows, pl.ds(OFF_Q + pair * LANES, LANES)], cq, sq, first_half)
        kr = _rope_half(proj_scr[rows, pl.ds(OFF_K + pair * LANES, LANES)], ck, sk, first_half)
        kr_b = kr.astype(BF16)
        ku_t = (kr * upd_ref[pair]).T.astype(BF16)
        s_pair = sret_ref[2 * pair:2 * pair + 2].reshape(2 * RET_DK, RET_DV).astype(BF16)
        for hl in range(2):
            h = 2 * pair + hl
            qm = jnp.where(head_lo if hl == 0 else jnp.logical_not(head_lo), qr, 0.0).astype(BF16)
            scores = lax.dot_general(qm, kr_b, nt, preferred_element_type=F32) * dmat_ref[h]
            sc_scr[hl] = scores.astype(BF16)
            qs_scr[hl] = jnp.dot(qm, s_pair, preferred_element_type=F32) * rd_ref[h]
            vh = proj_scr[rows, pl.ds(OFF_V + h * RET_DV, RET_DV)].astype(BF16)
            sret_ref[h] = (sret_ref[h] * cd_ref[h]
                           + jnp.dot(ku_t[hl * RET_DK:(hl + 1) * RET_DK, :], vh, preferred_element_type=F32))

    def retention_values(i, pair):
        rows = pl.ds(i * RET_CHUNK, RET_CHUNK)
        for hl in range(2):
            h = 2 * pair + hl
            vh = proj_scr[rows, pl.ds(OFF_V + h * RET_DV, RET_DV)].astype(BF16)
            o = jnp.dot(sc_scr[hl], vh, preferred_element_type=F32) + qs_scr[hl]
            gate = proj_scr[rows, pl.ds(OFF_G + h * RET_DV, RET_DV)]
            o = _layernorm(o, gn_ref[:, pl.ds(h * RET_DV, RET_DV)]) * _silu(gate)
            mixin_scr[rows, pl.ds(h * RET_DV, RET_DV)] = o.astype(BF16)

    def out_proj():
        mix_scr[...] = jnp.dot(mixin_scr[...], wout_ref[...], preferred_element_type=F32)

    in_tiles = [functools.partial(in_proj, n) for n in range(IN_WIDTH // IN_BLK)]
    ret_items = [functools.partial(phase, i, pair)
                 for i in range(tile // RET_CHUNK) for pair in range(RET_HEADS // 2)
                 for phase in (retention_scores, retention_values)]

    def ffn_norms():
        post_scale = gqm_ref[...] * modp_ref[2:3, :]
        sh = modp_ref[3:4, :]
        pre_scale = gpf_ref[...] * (1.0 + modp_ref[4:5, :])
        for rows in row_blocks:
            x1 = xp_ref[rows, :] + _rms(mix_scr[rows, :]) * post_scale
            x1_scr[rows, :] = x1
            hf_scr[rows, :] = (_rms(x1) * pre_scale + sh).astype(BF16)

    def hidden(j):
        p_scr[:, pl.ds(j * FF_BLK, FF_BLK)] = _swiglu_block(hf_scr[...], w1_ref, w3_ref, j * FF_BLK)

    def down(n):
        cols = pl.ds(n * FF_BLK, FF_BLK)
        f_scr[:, cols] = jnp.dot(p_scr[...], w2_ref[:, cols], preferred_element_type=F32)

    def ffn_finish():
        post_scale = gqf_ref[...] * modp_ref[5:6, :]
        for rows in row_blocks:
            y_ref[rows, :] = x1_scr[rows, :] + _rms(f_scr[rows, :]) * post_scale

    hidden_blocks = [functools.partial(hidden, j) for j in range(D_FF // FF_BLK)]
    down_tiles = [functools.partial(down, n) for n in range(D_MODEL // FF_BLK)]

    @pl.when(step == 0)
    def _():
        mix_prenorm()
        _emit(in_tiles)
        _emit(ret_items)
        conv_groups()
        out_proj()

    @pl.when((step > 0) & (step < n_tiles))
    def _():
        mix_prenorm()
        ffn_norms()
        _emit(in_tiles)
        _emit_interleaved(hidden_blocks, ret_items)
        conv_groups()
        _emit(down_tiles)
        out_proj()
        ffn_finish()

    @pl.when(step == n_tiles)
    def _():
        ffn_norms()
        _emit(hidden_blocks)
        _emit(down_tiles)
        ffn_finish()


def _prompt_layer(layer, prev, x, mod_p, g_pm, g_qm, g_pf, g_qf, w_in, w_out, w1, w3, w2, tables,
                  gn, caw, cab, lng, lnb, ccw):
    bsz, seq, _ = x.shape
    tile = TOK_TILE
    tps = seq // tile
    n_tiles = bsz * tps
    cq, sq, ck, sk, dmat, rd, upd, cd = tables
    cur = lambda i: jnp.minimum(i, n_tiles - 1)
    prv = lambda i: jnp.maximum(i - 1, 0)
    tok_cur = pl.BlockSpec((None, tile, D_MODEL), lambda i: (cur(i) // tps, cur(i) % tps, 0))
    tok_prev = pl.BlockSpec((None, tile, D_MODEL), lambda i: (prv(i) // tps, prv(i) % tps, 0))
    mod_cur = pl.BlockSpec((None, 6, D_MODEL), lambda i: (cur(i) // tps, 0, 0))
    mod_prev = pl.BlockSpec((None, 6, D_MODEL), lambda i: (prv(i) // tps, 0, 0))
    rope = pl.BlockSpec((tile, LANES), lambda i: (cur(i) % tps, 0))
    cs = functools.partial(_const_spec, grid_rank=1)
    ls = functools.partial(_layer_spec, layer=layer)
    args = (x, x, mod_p, mod_p, g_pm, g_qm, g_pf, g_qf, w_in, w_out, w1, w3, w2,
            cq, sq, ck, sk, dmat, rd, upd, cd, gn, caw, cab, lng, lnb, ccw)
    n_in = len(args) + (3 if prev is not None else 0)
    return _stacked_call(
        functools.partial(_prompt_layer_kernel, tile=tile, n_tiles=n_tiles, tiles_per_seq=tps, n_in=n_in),
        name="prompt_layer", grid=(n_tiles + 1,), args=args, n_stacked=3, prev=prev,
        in_specs=[
            tok_cur, tok_prev, mod_cur, mod_prev,
            ls((1, D_MODEL)), ls((1, D_MODEL)), ls((1, D_MODEL)), ls((1, D_MODEL)),
            ls((D_MODEL, IN_WIDTH)), ls((D_MODEL, D_MODEL)),
            ls((D_MODEL, D_FF)), ls((D_MODEL, D_FF)), ls((D_FF, D_MODEL)),
            rope, rope, rope, rope,
            cs(dmat.shape), cs(rd.shape), cs(upd.shape), cs(cd.shape),
            ls((1, RET_WIDTH)), ls((CONV_A_K, CONV_A_WIDTH)), ls((1, CONV_A_WIDTH)),
            ls((1, CONV_A_WIDTH)), ls((1, CONV_A_WIDTH)), ls((CONV_C_K, CONV_C_WIDTH)),
        ],
        out_specs=[
            tok_prev,
            pl.BlockSpec((None, None, RET_HEADS, RET_DK, RET_DV), lambda i: (layer, cur(i) // tps, 0, 0, 0)),
            pl.BlockSpec((None, None, CONV_A_K - 1, CONV_A_WIDTH), lambda i: (layer, cur(i) // tps, 0, 0)),
            pl.BlockSpec((None, None, CONV_C_K - 1, CONV_C_WIDTH), lambda i: (layer, cur(i) // tps, 0, 0)),
        ],
        out_shape=[
            jax.ShapeDtypeStruct(x.shape, F32),
            jax.ShapeDtypeStruct((DEPTH, bsz, RET_HEADS, RET_DK, RET_DV), F32),
            jax.ShapeDtypeStruct((DEPTH, bsz, CONV_A_K - 1, CONV_A_WIDTH), F32),
            jax.ShapeDtypeStruct((DEPTH, bsz, CONV_C_K - 1, CONV_C_WIDTH), F32),
        ],
        scratch_shapes=[
            pltpu.VMEM((tile, D_MODEL), BF16),
            pltpu.VMEM((tile, IN_WIDTH), F32),
            pltpu.VMEM((tile, D_MODEL), BF16),
            pltpu.VMEM((tile, D_MODEL), F32),
            pltpu.VMEM((tile, D_MODEL), F32),
            pltpu.VMEM((tile, D_MODEL), BF16),
            pltpu.VMEM((tile, D_FF), BF16),
            pltpu.VMEM((tile, D_MODEL), F32),
            pltpu.VMEM((2, tile + HIST_A, LANES), F32),
            pltpu.VMEM((2, tile, LANES), F32),
            pltpu.VMEM((tile + HIST_C, CONV_C_WIDTH), F32),
            pltpu.VMEM((2, RET_CHUNK, RET_CHUNK), BF16),
            pltpu.VMEM((2, RET_CHUNK, RET_DV), F32),
        ])


def _sample_pre_kernel(x_ref, mod_ref, gpre_ref, win_ref, rope_ref, proj_ref, qt_ref, kt_ref):
    n = x_ref.shape[0]
    sh = mod_ref[:, 0:D_MODEL]
    sc1 = 1.0 + mod_ref[:, D_MODEL:2 * D_MODEL]
    h = (_rms(x_ref[...]) * (gpre_ref[...] * sc1) + sh).astype(BF16)
    proj_ref[...] = jnp.dot(h, win_ref[...], preferred_element_type=F32)
    first_half, _ = _first_half_mask(n)
    cq, sq, ck, sk = rope_ref[0:1, :], rope_ref[1:2, :], rope_ref[2:3, :], rope_ref[3:4, :]
    for pair in range(RET_HEADS // 2):
        ql = pl.ds(OFF_Q + pair * LANES, LANES)
        kl = pl.ds(OFF_K + pair * LANES, LANES)
        qr = _rope_half(proj_ref[:, ql], cq, sq, first_half)
        kr = _rope_half(proj_ref[:, kl], ck, sk, first_half)
        qt_ref[pl.ds(pair * LANES, LANES), :] = qr.T
        kt_ref[pl.ds(pair * LANES, LANES), :] = kr.T


def _sample_pre(layer, xs, mod, gpre, w_in, rope_s):
    n = xs.shape[0]
    cs = functools.partial(_const_spec, grid_rank=1, single_buffer=False)
    ls = functools.partial(_layer_spec, layer=layer, single_buffer=False)
    return pl.pallas_call(
        _sample_pre_kernel,
        grid=(1,),
        in_specs=[cs((n, D_MODEL)), ls((n, 6 * D_MODEL)), ls((1, D_MODEL)),
                  ls((D_MODEL, IN_WIDTH)), cs(rope_s.shape)],
        out_specs=[cs((n, IN_WIDTH)), cs((QK_WIDTH, n)), cs((QK_WIDTH, n))],
        out_shape=[jax.ShapeDtypeStruct((n, IN_WIDTH), F32),
                   jax.ShapeDtypeStruct((QK_WIDTH, n), F32),
                   jax.ShapeDtypeStruct((QK_WIDTH, n), F32)],
        compiler_params=_params(1),
        name="sample_pre",
    )(xs, mod, gpre, w_in, rope_s)


def _sample_state_kernel(*refs, n_in):
    (proj_ref, qt_ref, kt_ref, sin_ref, bufa_ref, bufc_ref, cd_ref,
     gn_ref, caw_ref, cab_ref, lng_ref, lnb_ref, ccw_ref) = refs[:13]
    mix_ref, sout_ref, outa_ref, outc_ref, o_scr = refs[n_in:]
    blk = pl.program_id(0)
    n = qt_ref.shape[1]
    lane = lax.broadcasted_iota(jnp.int32, (QK_WIDTH, n), 1)

    for bl in range(SEQ_BLK):
        onehot = lane == (blk * SEQ_BLK + bl)
        qcol = jnp.sum(jnp.where(onehot, qt_ref[...], 0.0), axis=1, keepdims=True)
        kcol = jnp.sum(jnp.where(onehot, kt_ref[...], 0.0), axis=1, keepdims=True)
        row = pl.ds(bl, 1)
        for h in range(RET_HEADS):
            vrow = proj_ref[row, pl.ds(OFF_V + h * RET_DV, RET_DV)]
            s_new = (sin_ref[bl, h] * cd_ref[h]
                     + kcol[h * RET_DK:(h + 1) * RET_DK, :] * vrow)
            sout_ref[bl, h] = s_new
            o_scr[row, pl.ds(h * RET_DV, RET_DV)] = jnp.sum(
                qcol[h * RET_DK:(h + 1) * RET_DK, :] * s_new, axis=0, keepdims=True)
    for h in range(RET_HEADS):
        cols = pl.ds(h * RET_DV, RET_DV)
        o = _layernorm(o_scr[:, cols], gn_ref[:, cols]) * _silu(proj_ref[:, pl.ds(OFF_G + h * RET_DV, RET_DV)])
        mix_ref[:, cols] = o

    u = proj_ref[:, OFF_AV:OFF_AV + CONV_A_WIDTH] * _sigmoid(proj_ref[:, OFF_AG:OFF_AG + CONV_A_WIDTH])
    ua = caw_ref[CONV_A_K - 1:CONV_A_K, :] * u
    for j in range(CONV_A_K - 1):
        ua = ua + caw_ref[j:j + 1, :] * bufa_ref[j]
    outa_ref[0:CONV_A_K - 2] = bufa_ref[1:CONV_A_K - 1]
    outa_ref[CONV_A_K - 2] = u
    mix_ref[:, MIX_A:MIX_A + CONV_A_WIDTH] = _silu(_layernorm(ua + cab_ref[...], lng_ref[...], lnb_ref[...]))

    z = proj_ref[:, OFF_CC:OFF_CC + CONV_C_WIDTH] * proj_ref[:, OFF_CX:OFF_CX + CONV_C_WIDTH]
    zc = ccw_ref[0:1, :] * bufc_ref[0] + ccw_ref[1:2, :] * bufc_ref[1] + ccw_ref[2:3, :] * z
    outc_ref[0] = bufc_ref[1]
    outc_ref[1] = z
    mix_ref[:, MIX_C:MIX_C + CONV_C_WIDTH] = proj_ref[:, OFF_CB:OFF_CB + CONV_C_WIDTH] * zc


def _sample_state(layer, prev, proj, qt, kt, s_ret, buf_a_t, buf_c_t, cd, gn, caw, cab, lng, lnb, ccw):
    n = proj.shape[0]
    cs = functools.partial(_const_spec, grid_rank=1, single_buffer=False)
    ls = functools.partial(_layer_spec, layer=layer, single_buffer=False)
    s_spec = pl.BlockSpec((None, SEQ_BLK, RET_HEADS, RET_DK, RET_DV), lambda j: (layer, j, 0, 0, 0))
    a_spec = pl.BlockSpec((None, CONV_A_K - 1, SEQ_BLK, CONV_A_WIDTH), lambda j: (layer, 0, j, 0))
    c_spec = pl.BlockSpec((None, CONV_C_K - 1, SEQ_BLK, CONV_C_WIDTH), lambda j: (layer, 0, j, 0))
    args = (proj, qt, kt, s_ret, buf_a_t, buf_c_t, cd, gn, caw, cab, lng, lnb, ccw)
    n_in = len(args) + (3 if prev is not None else 0)
    return _stacked_call(
        functools.partial(_sample_state_kernel, n_in=n_in),
        name="sample_state", grid=(n // SEQ_BLK,), args=args, n_stacked=3, prev=prev,
        in_specs=[
            pl.BlockSpec((SEQ_BLK, IN_WIDTH), lambda j: (j, 0)),
            cs((QK_WIDTH, n)), cs((QK_WIDTH, n)),
            s_spec, a_spec, c_spec,
            cs(cd.shape), ls((1, RET_WIDTH)), ls((CONV_A_K, CONV_A_WIDTH)), ls((1, CONV_A_WIDTH)),
            ls((1, CONV_A_WIDTH)), ls((1, CONV_A_WIDTH)), ls((CONV_C_K, CONV_C_WIDTH)),
        ],
        out_specs=[pl.BlockSpec((SEQ_BLK, D_MODEL), lambda j: (j, 0)), s_spec, a_spec, c_spec],
        out_shape=[
            jax.ShapeDtypeStruct((n, D_MODEL), F32),
            jax.ShapeDtypeStruct(s_ret.shape, F32),
            jax.ShapeDtypeStruct(buf_a_t.shape, F32),
            jax.ShapeDtypeStruct(buf_c_t.shape, F32),
        ],
        scratch_shapes=[pltpu.VMEM((SEQ_BLK, RET_WIDTH), F32)])


def _sample_post_kernel(x_ref, mod_ref, mix_ref, wout_ref, gpm_ref, gpf_ref, gqf_ref, w1_ref, w3_ref, w2_ref,
                        y_ref, x1_scr, h_scr, f_scr, *, n_steps):
    j = pl.program_id(0)

    @pl.when(j == 0)
    def _():
        gt_m = mod_ref[:, 2 * D_MODEL:3 * D_MODEL]
        mix = jnp.dot(mix_ref[...].astype(BF16), wout_ref[...], preferred_element_type=F32)
        x1 = x_ref[...] + _rms(mix) * (gpm_ref[...] * gt_m)
        x1_scr[...] = x1
        sh = mod_ref[:, 3 * D_MODEL:4 * D_MODEL]
        sc1 = 1.0 + mod_ref[:, 4 * D_MODEL:5 * D_MODEL]
        h_scr[...] = (_rms(x1) * (gpf_ref[...] * sc1) + sh).astype(BF16)
        f_scr[...] = jnp.zeros_like(f_scr)

    p = _swiglu_block(h_scr[...], w1_ref, w3_ref, 0)
    f_scr[...] += jnp.dot(p, w2_ref[...], preferred_element_type=F32)

    @pl.when(j == n_steps - 1)
    def _():
        gt_f = mod_ref[:, 5 * D_MODEL:6 * D_MODEL]
        y_ref[...] = x1_scr[...] + _rms(f_scr[...]) * (gqf_ref[...] * gt_f)


def _sample_post(layer, xs, mod, mix, w_out, gpost_m, gpre_f, gpost_f, w1, w3, w2):
    n = xs.shape[0]
    n_steps = D_FF // FF_BLK
    cs = functools.partial(_const_spec, grid_rank=1, single_buffer=False)
    ls = functools.partial(_layer_spec, layer=layer, single_buffer=False)
    return pl.pallas_call(
        functools.partial(_sample_post_kernel, n_steps=n_steps),
        grid=(n_steps,),
        in_specs=[
            cs((n, D_MODEL)), ls((n, 6 * D_MODEL)), cs((n, D_MODEL)), ls((D_MODEL, D_MODEL)),
            ls((1, D_MODEL)), ls((1, D_MODEL)), ls((1, D_MODEL)),
            pl.BlockSpec((None, D_MODEL, FF_BLK), lambda j: (layer, 0, j)),
            pl.BlockSpec((None, D_MODEL, FF_BLK), lambda j: (layer, 0, j)),
            pl.BlockSpec((None, FF_BLK, D_MODEL), lambda j: (layer, j, 0)),
        ],
        out_specs=cs((n, D_MODEL)),
        out_shape=jax.ShapeDtypeStruct((n, D_MODEL), F32),
        scratch_shapes=[
            pltpu.VMEM((n, D_MODEL), F32),
            pltpu.VMEM((n, D_MODEL), BF16),
            pltpu.VMEM((n, D_MODEL), F32),
        ],
        compiler_params=_params(1),
        name="sample_post",
    )(xs, mod, mix, w_out, gpost_m, gpre_f, gpost_f, w1, w3, w2)


def kernel(x_prompt, x_sample, c_prompt, c_sample, state_ret, state_conv_a, state_conv_c, ada_w, ada_b, norm_pre_mix, norm_post_mix, norm_pre_ffn, norm_post_ffn, w_in, w_out, ret_gn_g, conv_a_w, conv_a_b, conv_a_ln_g, conv_a_ln_b, conv_c_w, ffn_w1, ffn_w3, ffn_w2):
    bp, lp, _ = x_prompt.shape
    ns = x_sample.shape[0]
    assert x_sample.shape[1] == 1 and ns % SEQ_BLK == 0
    assert lp % TOK_TILE == 0 and TOK_TILE % RET_CHUNK == 0
    assert TOK_TILE % CONV_BLK == 0 and CONV_BLK % (SUBLANES * CONV_STRIDE) == 0

    k_scale = RET_DK ** -0.5
    cq, sq = _rope_tables(np.arange(lp), 1.0)
    ck, sk = _rope_tables(np.arange(lp), k_scale)
    tables_p = (cq, sq, ck, sk) + _decay_tables(RET_CHUNK)
    cqs, sqs = _rope_tables([PAST_LEN], 1.0)
    cks, sks = _rope_tables([PAST_LEN], k_scale)
    rope_s = np.concatenate([cqs, sqs, cks, sks, np.zeros((4, LANES), np.float32)], axis=0)
    cd_s = _decay_tables(1)[3]

    mod = _ada_modulation(jnp.concatenate([c_sample, c_prompt], axis=0), ada_w, ada_b)

    rows = lambda a: a.reshape(DEPTH, 1, -1)
    w_in_b, w_out_b = w_in.astype(BF16), w_out.astype(BF16)
    w1_b, w3_b, w2_b = ffn_w1.astype(BF16), ffn_w3.astype(BF16), ffn_w2.astype(BF16)
    gn, cab, lng, lnb = rows(ret_gn_g), rows(conv_a_b), rows(conv_a_ln_g), rows(conv_a_ln_b)
    g_pm, g_qm, g_pf, g_qf = rows(norm_pre_mix), rows(norm_post_mix), rows(norm_pre_ffn), rows(norm_post_ffn)
    conv_a_t = jnp.transpose(state_conv_a, (0, 2, 1, 3))
    conv_c_t = jnp.transpose(state_conv_c, (0, 2, 1, 3))

    yp = x_prompt
    ys = x_sample.reshape(ns, D_MODEL)
    st_p = None
    st_s = None
    for l in range(DEPTH):
        mod_p = mod[l, ns:].reshape(bp, 6, D_MODEL)
        yp, *st_p = _prompt_layer(l, st_p, yp, mod_p, g_pm, g_qm, g_pf, g_qf, w_in_b, w_out_b, w1_b, w3_b, w2_b,
                                  tables_p, gn, conv_a_w, cab, lng, lnb, conv_c_w)
        proj, qt, kt = _sample_pre(l, ys, mod, g_pm, w_in_b, rope_s)
        mix, *st_s = _sample_state(l, st_s, proj, qt, kt, state_ret, conv_a_t, conv_c_t, cd_s,
                                   gn, conv_a_w, cab, lng, lnb, conv_c_w)
        ys = _sample_post(l, ys, mod, mix, w_out_b, g_qm, g_pf, g_qf, w1_b, w3_b, w2_b)

    sret_s, sca_s, scc_s = st_s
    sca_s = jnp.transpose(sca_s, (0, 2, 1, 3))
    scc_s = jnp.transpose(scc_s, (0, 2, 1, 3))
    return (yp, ys.reshape(ns, 1, D_MODEL)) + tuple(st_p) + (sret_s, sca_s, scc_s)
```

```python
import functools

import numpy as np
import jax
import jax.numpy as jnp
from jax import lax
from jax.experimental import pallas as pl
from jax.experimental.pallas import tpu as pltpu

D_MODEL = 1024
DEPTH = 2
PAST_LEN = 16384
RET_HEADS = 4
RET_WIDTH = D_MODEL // 2
RET_DV = RET_WIDTH // RET_HEADS
RET_DK = RET_DV // 2
QK_WIDTH = RET_HEADS * RET_DK
CONV_A_WIDTH = D_MODEL // 4
CONV_A_K = 31
CONV_C_WIDTH = D_MODEL - RET_WIDTH - CONV_A_WIDTH
CONV_C_K = 3
IN_WIDTH = 2 * QK_WIDTH + 2 * RET_WIDTH + 2 * CONV_A_WIDTH + 3 * CONV_C_WIDTH
D_FF = ((8 * D_MODEL // 3 + 255) // 256) * 256
RET_CHUNK = 128
ROPE_BASE = 10000.0
EPS = 1e-6

OFF_Q = 0
OFF_K = OFF_Q + QK_WIDTH
OFF_V = OFF_K + QK_WIDTH
OFF_G = OFF_V + RET_WIDTH
OFF_AV = OFF_G + RET_WIDTH
OFF_AG = OFF_AV + CONV_A_WIDTH
OFF_CB = OFF_AG + CONV_A_WIDTH
OFF_CC = OFF_CB + CONV_C_WIDTH
OFF_CX = OFF_CC + CONV_C_WIDTH
MIX_A = RET_WIDTH
MIX_C = RET_WIDTH + CONV_A_WIDTH

LANES = 128
SUBLANES = 8
V7X_VMEM_LIMIT_BYTES = 60 * 1024 * 1024

TOK_TILE = 512
ROW_BLK = 32
CONV_BLK = 64
CONV_STRIDE = 4
HIST_A = 32
HIST_C = 8
SEQ_BLK = 8
FF_BLK = 256
IN_BLK = 256

F32 = jnp.float32
BF16 = jnp.bfloat16


def _sigmoid(x):
    return jax.nn.sigmoid(x)


def _silu(x):
    return x * _sigmoid(x)


def _rms(x):
    return x * lax.rsqrt(jnp.mean(x * x, axis=-1, keepdims=True) + EPS)


def _layernorm(x, g, b=None):
    mu = jnp.mean(x, axis=-1, keepdims=True)
    d = x - mu
    var = jnp.mean(d * d, axis=-1, keepdims=True)
    y = d * lax.rsqrt(var + EPS) * g
    return y if b is None else y + b


def _rope_half(x, cos, sin, first_half):
    partner = jnp.where(first_half, pltpu.roll(x, 96, 1), pltpu.roll(x, 32, 1))
    return x * cos + partner * sin


def _first_half_mask(rows):
    lane = lax.broadcasted_iota(jnp.int32, (rows, LANES), 1)
    return (lane & (RET_DK - 1)) < (RET_DK // 2), lane < RET_DK


def _swiglu_block(h, w1_ref, w3_ref, col0):
    cols = pl.ds(col0, FF_BLK)
    a = jnp.dot(h, w1_ref[:, cols], preferred_element_type=F32)
    b = jnp.dot(h, w3_ref[:, cols], preferred_element_type=F32)
    return (_silu(a) * b).astype(BF16)


def _emit(items):
    for item in items:
        item()


def _emit_interleaved(primary, secondary):
    n, m = len(primary), len(secondary)
    done = 0
    for i, item in enumerate(primary):
        item()
        upto = ((i + 1) * m) // n
        _emit(secondary[done:upto])
        done = upto


def _rope_tables(pos, k_scale):
    half = RET_DK // 2
    inv = ROPE_BASE ** (-np.arange(half, dtype=np.float64) / half)
    ang = np.asarray(pos, np.float64)[:, None] * inv[None, :]
    cos = np.tile(np.cos(ang), (1, 4))
    sin = np.tile(np.concatenate([-np.sin(ang), np.sin(ang)], axis=1), (1, 2))
    return (cos * k_scale).astype(np.float32), (sin * k_scale).astype(np.float32)


def _decay_tables(chunk):
    log_g = np.log(1.0 - np.exp2(-5.0 - np.arange(RET_HEADS, dtype=np.float64)))
    idx = np.arange(chunk, dtype=np.float64)
    diff = idx[:, None] - idx[None, :]
    dmat = np.where(diff[None] >= 0, np.exp(np.maximum(diff, 0.0)[None] * log_g[:, None, None]), 0.0)
    read_dec = np.exp((idx + 1.0)[:, None] * log_g[None, :])
    upd_dec = np.exp((chunk - 1.0 - idx)[:, None] * log_g[None, :])
    chunk_dec = np.exp(chunk * log_g)
    rd = np.broadcast_to(read_dec.T[:, :, None], (RET_HEADS, chunk, RET_DV))
    upd = np.repeat(upd_dec, RET_DK, axis=1).reshape(chunk, 2, LANES).transpose(1, 0, 2)
    cd = np.broadcast_to(chunk_dec[:, None, None], (RET_HEADS, 1, RET_DV))
    f = lambda a: np.ascontiguousarray(a, dtype=np.float32)
    return f(dmat), f(rd), f(upd), f(cd)


def _const_spec(shape, grid_rank, single_buffer=True):
    zeros = (0,) * len(shape)
    idx = (lambda b, c: zeros) if grid_rank == 2 else (lambda j: zeros)
    if single_buffer:
        return pl.BlockSpec(shape, idx, pipeline_mode=pl.Buffered(1))
    return pl.BlockSpec(shape, idx)


def _layer_spec(shape, layer, single_buffer=True):
    zeros = (0,) * len(shape)
    mode = dict(pipeline_mode=pl.Buffered(1)) if single_buffer else {}
    return pl.BlockSpec((None,) + tuple(shape), lambda j: (layer,) + zeros, **mode)


def _params(n_axes):
    return pltpu.CompilerParams(dimension_semantics=("arbitrary",) * n_axes,
                                vmem_limit_bytes=V7X_VMEM_LIMIT_BYTES)


def _stacked_call(kern, *, name, grid, in_specs, args, out_specs, out_shape, n_stacked, prev, scratch_shapes):
    aliases = {}
    if prev is not None:
        n_in = len(args)
        n_out = len(out_shape)
        in_specs = list(in_specs) + [pl.BlockSpec(memory_space=pl.ANY)] * n_stacked
        args = list(args) + list(prev)
        aliases = {n_in + i: n_out - n_stacked + i for i in range(n_stacked)}
    return pl.pallas_call(
        kern, grid=grid, in_specs=in_specs, out_specs=out_specs, out_shape=out_shape,
        scratch_shapes=scratch_shapes, input_output_aliases=aliases,
        compiler_params=_params(len(grid)), name=name)(*args)


def _ada_kernel(c_ref, w_ref, b_ref, o_ref):
    s = _silu(c_ref[...]).astype(BF16)
    o_ref[...] = jnp.dot(s, w_ref[...].astype(BF16), preferred_element_type=F32) + b_ref[...]


def _ada_modulation(c_all, ada_w, ada_b):
    rows = c_all.shape[0]
    ncol = 6 * D_MODEL
    blk = D_MODEL
    return pl.pallas_call(
        _ada_kernel,
        grid=(DEPTH, ncol // blk),
        in_specs=[
            pl.BlockSpec((rows, D_MODEL), lambda l, j: (0, 0)),
            pl.BlockSpec((None, D_MODEL, blk), lambda l, j: (l, 0, j)),
            pl.BlockSpec((None, 1, blk), lambda l, j: (l, 0, j)),
        ],
        out_specs=pl.BlockSpec((None, rows, blk), lambda l, j: (l, 0, j)),
        out_shape=jax.ShapeDtypeStruct((DEPTH, rows, ncol), F32),
        compiler_params=_params(2),
        name="ada_mod",
    )(c_all, ada_w, ada_b.reshape(DEPTH, 1, ncol))


def _prompt_layer_kernel(*refs, tile, n_tiles, tiles_per_seq, n_in):
    (xc_ref, xp_ref, modc_ref, modp_ref, gpm_ref, gqm_ref, gpf_ref, gqf_ref,
     win_ref, wout_ref, w1_ref, w3_ref, w2_ref,
     cq_ref, sq_ref, ck_ref, sk_ref, dmat_ref, rd_ref, upd_ref, cd_ref,
     gn_ref, caw_ref, cab_ref, lng_ref, lnb_ref, ccw_ref) = refs[:27]
    (y_ref, sret_ref, sca_ref, scc_ref,
     h_scr, proj_scr, mixin_scr, mix_scr, hf_scr, p_scr, ubuf, ua_scr, zbuf,
     sc_scr, qs_scr) = refs[n_in:]
    step = pl.program_id(0)
    row_blocks = [pl.ds(r * ROW_BLK, ROW_BLK) for r in range(tile // ROW_BLK)]
    first_tap = HIST_A - (CONV_A_K - 1)
    first_tap_c = HIST_C - (CONV_C_K - 1)
    span = SUBLANES * CONV_STRIDE

    @pl.when((step < n_tiles) & (lax.rem(step, tiles_per_seq) == 0))
    def _():
        sret_ref[...] = jnp.zeros_like(sret_ref)
        ubuf[:, 0:HIST_A, :] = jnp.zeros((2, HIST_A, LANES), F32)
        zbuf[0:HIST_C, :] = jnp.zeros((HIST_C, CONV_C_WIDTH), F32)

    def mix_prenorm():
        sh = modc_ref[0:1, :]
        pre_scale = gpm_ref[...] * (1.0 + modc_ref[1:2, :])
        for rows in row_blocks:
            h_scr[rows, :] = (_rms(xc_ref[rows, :]) * pre_scale + sh).astype(BF16)

    def in_proj(n):
        cols = pl.ds(n * IN_BLK, IN_BLK)
        proj_scr[:, cols] = jnp.dot(h_scr[...], win_ref[:, cols], preferred_element_type=F32)

    def conv_groups():
        taps = [[jnp.broadcast_to(caw_ref[j:j + 1, pl.ds(half * LANES, LANES)], (SUBLANES, LANES))
                 for j in range(CONV_A_K)] for half in range(2)]
        for r in range(tile // CONV_BLK):
            rows = pl.ds(r * CONV_BLK, CONV_BLK)
            dst = pl.ds(HIST_A + r * CONV_BLK, CONV_BLK)
            for half in range(2):
                lo = half * LANES
                u = (proj_scr[rows, pl.ds(OFF_AV + lo, LANES)]
                     * _sigmoid(proj_scr[rows, pl.ds(OFF_AG + lo, LANES)]))
                ubuf[half, dst, :] = u
        for half in range(2):
            for m in range(tile // span):
                base = m * span
                acc = [None] * CONV_STRIDE
                for k in range(CONV_STRIDE + CONV_A_K - 1):
                    win = ubuf[half, pl.ds(base + first_tap + k, SUBLANES, stride=CONV_STRIDE), :]
                    for t in range(CONV_STRIDE):
                        j = k - t
                        if 0 <= j < CONV_A_K:
                            term = taps[half][j] * win
                            acc[t] = term if acc[t] is None else acc[t] + term
                for t in range(CONV_STRIDE):
                    ua_scr[half, pl.ds(base + t, SUBLANES, stride=CONV_STRIDE), :] = acc[t]
        for r in range(tile // CONV_BLK):
            rows = pl.ds(r * CONV_BLK, CONV_BLK)
            ua = jnp.concatenate([ua_scr[0, rows, :], ua_scr[1, rows, :]], axis=-1) + cab_ref[...]
            o_a = _silu(_layernorm(ua, lng_ref[...], lnb_ref[...]))
            mixin_scr[rows, MIX_A:MIX_A + CONV_A_WIDTH] = o_a.astype(BF16)
        for r in range(tile // CONV_BLK):
            rows = pl.ds(r * CONV_BLK, CONV_BLK)
            z = proj_scr[rows, OFF_CC:OFF_CC + CONV_C_WIDTH] * proj_scr[rows, OFF_CX:OFF_CX + CONV_C_WIDTH]
            zbuf[pl.ds(HIST_C + r * CONV_BLK, CONV_BLK), :] = z
        for r in range(tile // CONV_BLK):
            rows = pl.ds(r * CONV_BLK, CONV_BLK)
            zc = ccw_ref[0:1, :] * zbuf[pl.ds(r * CONV_BLK + first_tap_c, CONV_BLK), :]
            for j in range(1, CONV_C_K):
                zc = zc + ccw_ref[j:j + 1, :] * zbuf[pl.ds(r * CONV_BLK + first_tap_c + j, CONV_BLK), :]
            o_c = proj_scr[rows, OFF_CB:OFF_CB + CONV_C_WIDTH] * zc
            mixin_scr[rows, MIX_C:MIX_C + CONV_C_WIDTH] = o_c.astype(BF16)
        hist_a = pl.ds(tile + first_tap, CONV_A_K - 1)
        sca_ref[...] = jnp.concatenate([ubuf[0, hist_a, :], ubuf[1, hist_a, :]], axis=-1)
        scc_ref[...] = zbuf[pl.ds(tile + first_tap_c, CONV_C_K - 1), :]
        ubuf[:, 0:HIST_A, :] = ubuf[:, pl.ds(tile, HIST_A), :]
        zbuf[0:HIST_C, :] = zbuf[pl.ds(tile, HIST_C), :]

    def retention_scores(i, pair):
        first_half, head_lo = _first_half_mask(RET_CHUNK)
        nt = (((1,), (1,)), ((), ()))
        rows = pl.ds(i * RET_CHUNK, RET_CHUNK)
        cq, sq, ck, sk = cq_ref[rows, :], sq_ref[rows, :], ck_ref[rows, :], sk_ref[rows, :]
        qr = _rope_half(proj_scr[rows, pl.ds(OFF_Q + pair * LANES, LANES)], cq, sq, first_half)
        kr = _rope_half(proj_scr[rows, pl.ds(OFF_K + pair * LANES, LANES)], ck, sk, first_half)
        kr_b = kr.astype(BF16)
        ku_t = (kr * upd_ref[pair]).T.astype(BF16)
        s_pair = sret_ref[2 * pair:2 * pair + 2].reshape(2 * RET_DK, RET_DV).astype(BF16)
        for hl in range(2):
            h = 2 * pair + hl
            qm = jnp.where(head_lo if hl == 0 else jnp.logical_not(head_lo), qr, 0.0).astype(BF16)
            scores = lax.dot_general(qm, kr_b, nt, preferred_element_type=F32) * dmat_ref[h]
            sc_scr[hl] = scores.astype(BF16)
            qs_scr[hl] = jnp.dot(qm, s_pair, preferred_element_type=F32) * rd_ref[h]
            vh = proj_scr[rows, pl.ds(OFF_V + h * RET_DV, RET_DV)].astype(BF16)
            sret_ref[h] = (sret_ref[h] * cd_ref[h]
                           + jnp.dot(ku_t[hl * RET_DK:(hl + 1) * RET_DK, :], vh, preferred_element_type=F32))

    def retention_values(i, pair):
        rows = pl.ds(i * RET_CHUNK, RET_CHUNK)
        for hl in range(2):
            h = 2 * pair + hl
            vh = proj_scr[rows, pl.ds(OFF_V + h * RET_DV, RET_DV)].astype(BF16)
            o = jnp.dot(sc_scr[hl], vh, preferred_element_type=F32) + qs_scr[hl]
            gate = proj_scr[rows, pl.ds(OFF_G + h * RET_DV, RET_DV)]
            o = _layernorm(o, gn_ref[:, pl.ds(h * RET_DV, RET_DV)]) * _silu(gate)
            mixin_scr[rows, pl.ds(h * RET_DV, RET_DV)] = o.astype(BF16)

    def out_proj():
        mix_scr[...] = jnp.dot(mixin_scr[...], wout_ref[...], preferred_element_type=F32)

    in_tiles = [functools.partial(in_proj, n) for n in range(IN_WIDTH // IN_BLK)]
    ret_items = [functools.partial(phase, i, pair)
                 for i in range(tile // RET_CHUNK) for pair in range(RET_HEADS // 2)
                 for phase in (retention_scores, retention_values)]

    def ffn_norms():
        post_scale = gqm_ref[...] * modp_ref[2:3, :]
        sh = modp_ref[3:4, :]
        pre_scale = gpf_ref[...] * (1.0 + modp_ref[4:5, :])
        for rows in row_blocks:
            x1 = xp_ref[rows, :] + _rms(mix_scr[rows, :]) * post_scale
            y_ref[rows, :] = x1
            hf_scr[rows, :] = (_rms(x1) * pre_scale + sh).astype(BF16)

    def hidden(j):
        p_scr[:, pl.ds(j * FF_BLK, FF_BLK)] = _swiglu_block(hf_scr[...], w1_ref, w3_ref, j * FF_BLK)

    def down(n):
        cols = pl.ds(n * FF_BLK, FF_BLK)
        proj_scr[:, cols] = jnp.dot(p_scr[...], w2_ref[:, cols], preferred_element_type=F32)

    def ffn_finish():
        post_scale = gqf_ref[...] * modp_ref[5:6, :]
        for rows in row_blocks:
            y_ref[rows, :] = y_ref[rows, :] + _rms(proj_scr[rows, 0:D_MODEL]) * post_scale

    hidden_blocks = [functools.partial(hidden, j) for j in range(D_FF // FF_BLK)]
    down_tiles = [functools.partial(down, n) for n in range(D_MODEL // FF_BLK)]

    @pl.when(step == 0)
    def _():
        mix_prenorm()
        _emit(in_tiles)
        _emit(ret_items)
        conv_groups()
        out_proj()

    @pl.when((step > 0) & (step < n_tiles))
    def _():
        mix_prenorm()
        ffn_norms()
        _emit(in_tiles)
        _emit_interleaved(hidden_blocks, ret_items)
        conv_groups()
        _emit(down_tiles)
        out_proj()
        ffn_finish()

    @pl.when(step == n_tiles)
    def _():
        ffn_norms()
        _emit(hidden_blocks)
        _emit(down_tiles)
        ffn_finish()


def _prompt_layer(layer, prev, x, mod_p, g_pm, g_qm, g_pf, g_qf, w_in, w_out, w1, w3, w2, tables,
                  gn, caw, cab, lng, lnb, ccw):
    bsz, seq, _ = x.shape
    tile = TOK_TILE
    tps = seq // tile
    n_tiles = bsz * tps
    cq, sq, ck, sk, dmat, rd, upd, cd = tables
    cur = lambda i: jnp.minimum(i, n_tiles - 1)
    prv = lambda i: jnp.maximum(i - 1, 0)
    tok_cur = pl.BlockSpec((None, tile, D_MODEL), lambda i: (cur(i) // tps, cur(i) % tps, 0))
    tok_prev = pl.BlockSpec((None, tile, D_MODEL), lambda i: (prv(i) // tps, prv(i) % tps, 0))
    mod_cur = pl.BlockSpec((None, 6, D_MODEL), lambda i: (cur(i) // tps, 0, 0))
    mod_prev = pl.BlockSpec((None, 6, D_MODEL), lambda i: (prv(i) // tps, 0, 0))
    rope = pl.BlockSpec((tile, LANES), lambda i: (cur(i) % tps, 0))
    cs = functools.partial(_const_spec, grid_rank=1)
    ls = functools.partial(_layer_spec, layer=layer)
    args = (x, x, mod_p, mod_p, g_pm, g_qm, g_pf, g_qf, w_in, w_out, w1, w3, w2,
            cq, sq, ck, sk, dmat, rd, upd, cd, gn, caw, cab, lng, lnb, ccw)
    n_in = len(args) + (3 if prev is not None else 0)
    return _stacked_call(
        functools.partial(_prompt_layer_kernel, tile=tile, n_tiles=n_tiles, tiles_per_seq=tps, n_in=n_in),
        name="prompt_layer", grid=(n_tiles + 1,), args=args, n_stacked=3, prev=prev,
        in_specs=[
            tok_cur, tok_prev, mod_cur, mod_prev,
            ls((1, D_MODEL)), ls((1, D_MODEL)), ls((1, D_MODEL)), ls((1, D_MODEL)),
            ls((D_MODEL, IN_WIDTH)), ls((D_MODEL, D_MODEL)),
            ls((D_MODEL, D_FF)), ls((D_MODEL, D_FF)), ls((D_FF, D_MODEL)),
            rope, rope, rope, rope,
            cs(dmat.shape), cs(rd.shape), cs(upd.shape), cs(cd.shape),
            ls((1, RET_WIDTH)), ls((CONV_A_K, CONV_A_WIDTH)), ls((1, CONV_A_WIDTH)),
            ls((1, CONV_A_WIDTH)), ls((1, CONV_A_WIDTH)), ls((CONV_C_K, CONV_C_WIDTH)),
        ],
        out_specs=[
            tok_prev,
            pl.BlockSpec((None, None, RET_HEADS, RET_DK, RET_DV), lambda i: (layer, cur(i) // tps, 0, 0, 0)),
            pl.BlockSpec((None, None, CONV_A_K - 1, CONV_A_WIDTH), lambda i: (layer, cur(i) // tps, 0, 0)),
            pl.BlockSpec((None, None, CONV_C_K - 1, CONV_C_WIDTH), lambda i: (layer, cur(i) // tps, 0, 0)),
        ],
        out_shape=[
            jax.ShapeDtypeStruct(x.shape, F32),
            jax.ShapeDtypeStruct((DEPTH, bsz, RET_HEADS, RET_DK, RET_DV), F32),
            jax.ShapeDtypeStruct((DEPTH, bsz, CONV_A_K - 1, CONV_A_WIDTH), F32),
            jax.ShapeDtypeStruct((DEPTH, bsz, CONV_C_K - 1, CONV_C_WIDTH), F32),
        ],
        scratch_shapes=[
            pltpu.VMEM((tile, D_MODEL), BF16),
            pltpu.VMEM((tile, IN_WIDTH), F32),
            pltpu.VMEM((tile, D_MODEL), BF16),
            pltpu.VMEM((tile, D_MODEL), F32),
            pltpu.VMEM((tile, D_MODEL), BF16),
            pltpu.VMEM((tile, D_FF), BF16),
            pltpu.VMEM((2, tile + HIST_A, LANES), F32),
            pltpu.VMEM((2, tile, LANES), F32),
            pltpu.VMEM((tile + HIST_C, CONV_C_WIDTH), F32),
            pltpu.VMEM((2, RET_CHUNK, RET_CHUNK), BF16),
            pltpu.VMEM((2, RET_CHUNK, RET_DV), F32),
        ])


def _sample_pre_kernel(x_ref, mod_ref, gpre_ref, win_ref, rope_ref, proj_ref, qt_ref, kt_ref):
    n = x_ref.shape[0]
    sh = mod_ref[:, 0:D_MODEL]
    sc1 = 1.0 + mod_ref[:, D_MODEL:2 * D_MODEL]
    h = (_rms(x_ref[...]) * (gpre_ref[...] * sc1) + sh).astype(BF16)
    proj_ref[...] = jnp.dot(h, win_ref[...], preferred_element_type=F32)
    first_half, _ = _first_half_mask(n)
    cq, sq, ck, sk = rope_ref[0:1, :], rope_ref[1:2, :], rope_ref[2:3, :], rope_ref[3:4, :]
    for pair in range(RET_HEADS // 2):
        ql = pl.ds(OFF_Q + pair * LANES, LANES)
        kl = pl.ds(OFF_K + pair * LANES, LANES)
        qr = _rope_half(proj_ref[:, ql], cq, sq, first_half)
        kr = _rope_half(proj_ref[:, kl], ck, sk, first_half)
        qt_ref[pl.ds(pair * LANES, LANES), :] = qr.T
        kt_ref[pl.ds(pair * LANES, LANES), :] = kr.T


def _sample_pre(layer, xs, mod, gpre, w_in, rope_s):
    n = xs.shape[0]
    cs = functools.partial(_const_spec, grid_rank=1, single_buffer=False)
    ls = functools.partial(_layer_spec, layer=layer, single_buffer=False)
    return pl.pallas_call(
        _sample_pre_kernel,
        grid=(1,),
        in_specs=[cs((n, D_MODEL)), ls((n, 6 * D_MODEL)), ls((1, D_MODEL)),
                  ls((D_MODEL, IN_WIDTH)), cs(rope_s.shape)],
        out_specs=[cs((n, IN_WIDTH)), cs((QK_WIDTH, n)), cs((QK_WIDTH, n))],
        out_shape=[jax.ShapeDtypeStruct((n, IN_WIDTH), F32),
                   jax.ShapeDtypeStruct((QK_WIDTH, n), F32),
                   jax.ShapeDtypeStruct((QK_WIDTH, n), F32)],
        compiler_params=_params(1),
        name="sample_pre",
    )(xs, mod, gpre, w_in, rope_s)


def _sample_state_kernel(*refs, n_in):
    (proj_ref, qt_ref, kt_ref, sin_ref, bufa_ref, bufc_ref, cd_ref,
     gn_ref, caw_ref, cab_ref, lng_ref, lnb_ref, ccw_ref) = refs[:13]
    mix_ref, sout_ref, outa_ref, outc_ref, o_scr = refs[n_in:]
    blk = pl.program_id(0)
    n = qt_ref.shape[1]
    lane = lax.broadcasted_iota(jnp.int32, (QK_WIDTH, n), 1)

    for bl in range(SEQ_BLK):
        onehot = lane == (blk * SEQ_BLK + bl)
        qcol = jnp.sum(jnp.where(onehot, qt_ref[...], 0.0), axis=1, keepdims=True)
        kcol = jnp.sum(jnp.where(onehot, kt_ref[...], 0.0), axis=1, keepdims=True)
        row = pl.ds(bl, 1)
        for h in range(RET_HEADS):
            vrow = proj_ref[row, pl.ds(OFF_V + h * RET_DV, RET_DV)]
            s_new = (sin_ref[bl, h] * cd_ref[h]
                     + kcol[h * RET_DK:(h + 1) * RET_DK, :] * vrow)
            sout_ref[bl, h] = s_new
            o_scr[row, pl.ds(h * RET_DV, RET_DV)] = jnp.sum(
                qcol[h * RET_DK:(h + 1) * RET_DK, :] * s_new, axis=0, keepdims=True)
    for h in range(RET_HEADS):
        cols = pl.ds(h * RET_DV, RET_DV)
        o = _layernorm(o_scr[:, cols], gn_ref[:, cols]) * _silu(proj_ref[:, pl.ds(OFF_G + h * RET_DV, RET_DV)])
        mix_ref[:, cols] = o

    u = proj_ref[:, OFF_AV:OFF_AV + CONV_A_WIDTH] * _sigmoid(proj_ref[:, OFF_AG:OFF_AG + CONV_A_WIDTH])
    ua = caw_ref[CONV_A_K - 1:CONV_A_K, :] * u
    for j in range(CONV_A_K - 1):
        ua = ua + caw_ref[j:j + 1, :] * bufa_ref[j]
    outa_ref[0:CONV_A_K - 2] = bufa_ref[1:CONV_A_K - 1]
    outa_ref[CONV_A_K - 2] = u
    mix_ref[:, MIX_A:MIX_A + CONV_A_WIDTH] = _silu(_layernorm(ua + cab_ref[...], lng_ref[...], lnb_ref[...]))

    z = proj_ref[:, OFF_CC:OFF_CC + CONV_C_WIDTH] * proj_ref[:, OFF_CX:OFF_CX + CONV_C_WIDTH]
    zc = ccw_ref[0:1, :] * bufc_ref[0] + ccw_ref[1:2, :] * bufc_ref[1] + ccw_ref[2:3, :] * z
    outc_ref[0] = bufc_ref[1]
    outc_ref[1] = z
    mix_ref[:, MIX_C:MIX_C + CONV_C_WIDTH] = proj_ref[:, OFF_CB:OFF_CB + CONV_C_WIDTH] * zc


def _sample_state(layer, prev, proj, qt, kt, s_ret, buf_a_t, buf_c_t, cd, gn, caw, cab, lng, lnb, ccw):
    n = proj.shape[0]
    cs = functools.partial(_const_spec, grid_rank=1, single_buffer=False)
    ls = functools.partial(_layer_spec, layer=layer, single_buffer=False)
    s_spec = pl.BlockSpec((None, SEQ_BLK, RET_HEADS, RET_DK, RET_DV), lambda j: (layer, j, 0, 0, 0))
    a_spec = pl.BlockSpec((None, CONV_A_K - 1, SEQ_BLK, CONV_A_WIDTH), lambda j: (layer, 0, j, 0))
    c_spec = pl.BlockSpec((None, CONV_C_K - 1, SEQ_BLK, CONV_C_WIDTH), lambda j: (layer, 0, j, 0))
    args = (proj, qt, kt, s_ret, buf_a_t, buf_c_t, cd, gn, caw, cab, lng, lnb, ccw)
    n_in = len(args) + (3 if prev is not None else 0)
    return _stacked_call(
        functools.partial(_sample_state_kernel, n_in=n_in),
        name="sample_state", grid=(n // SEQ_BLK,), args=args, n_stacked=3, prev=prev,
        in_specs=[
            pl.BlockSpec((SEQ_BLK, IN_WIDTH), lambda j: (j, 0)),
            cs((QK_WIDTH, n)), cs((QK_WIDTH, n)),
            s_spec, a_spec, c_spec,
            cs(cd.shape), ls((1, RET_WIDTH)), ls((CONV_A_K, CONV_A_WIDTH)), ls((1, CONV_A_WIDTH)),
            ls((1, CONV_A_WIDTH)), ls((1, CONV_A_WIDTH)), ls((CONV_C_K, CONV_C_WIDTH)),
        ],
        out_specs=[pl.BlockSpec((SEQ_BLK, D_MODEL), lambda j: (j, 0)), s_spec, a_spec, c_spec],
        out_shape=[
            jax.ShapeDtypeStruct((n, D_MODEL), F32),
            jax.ShapeDtypeStruct(s_ret.shape, F32),
            jax.ShapeDtypeStruct(buf_a_t.shape, F32),
            jax.ShapeDtypeStruct(buf_c_t.shape, F32),
        ],
        scratch_shapes=[pltpu.VMEM((SEQ_BLK, RET_WIDTH), F32)])


def _sample_post_kernel(x_ref, mod_ref, mix_ref, wout_ref, gpm_ref, gpf_ref, gqf_ref, w1_ref, w3_ref, w2_ref,
                        y_ref, x1_scr, h_scr, f_scr, *, n_steps):
    j = pl.program_id(0)

    @pl.when(j == 0)
    def _():
        gt_m = mod_ref[:, 2 * D_MODEL:3 * D_MODEL]
        mix = jnp.dot(mix_ref[...].astype(BF16), wout_ref[...], preferred_element_type=F32)
        x1 = x_ref[...] + _rms(mix) * (gpm_ref[...] * gt_m)
        x1_scr[...] = x1
        sh = mod_ref[:, 3 * D_MODEL:4 * D_MODEL]
        sc1 = 1.0 + mod_ref[:, 4 * D_MODEL:5 * D_MODEL]
        h_scr[...] = (_rms(x1) * (gpf_ref[...] * sc1) + sh).astype(BF16)
        f_scr[...] = jnp.zeros_like(f_scr)

    p = _swiglu_block(h_scr[...], w1_ref, w3_ref, 0)
    f_scr[...] += jnp.dot(p, w2_ref[...], preferred_element_type=F32)

    @pl.when(j == n_steps - 1)
    def _():
        gt_f = mod_ref[:, 5 * D_MODEL:6 * D_MODEL]
        y_ref[...] = x1_scr[...] + _rms(f_scr[...]) * (gqf_ref[...] * gt_f)


def _sample_post(layer, xs, mod, mix, w_out, gpost_m, gpre_f, gpost_f, w1, w3, w2):
    n = xs.shape[0]
    n_steps = D_FF // FF_BLK
    cs = functools.partial(_const_spec, grid_rank=1, single_buffer=False)
    ls = functools.partial(_layer_spec, layer=layer, single_buffer=False)
    return pl.pallas_call(
        functools.partial(_sample_post_kernel, n_steps=n_steps),
        grid=(n_steps,),
        in_specs=[
            cs((n, D_MODEL)), ls((n, 6 * D_MODEL)), cs((n, D_MODEL)), ls((D_MODEL, D_MODEL)),
            ls((1, D_MODEL)), ls((1, D_MODEL)), ls((1, D_MODEL)),
            pl.BlockSpec((None, D_MODEL, FF_BLK), lambda j: (layer, 0, j)),
            pl.BlockSpec((None, D_MODEL, FF_BLK), lambda j: (layer, 0, j)),
            pl.BlockSpec((None, FF_BLK, D_MODEL), lambda j: (layer, j, 0)),
        ],
        out_specs=cs((n, D_MODEL)),
        out_shape=jax.ShapeDtypeStruct((n, D_MODEL), F32),
        scratch_shapes=[
            pltpu.VMEM((n, D_MODEL), F32),
            pltpu.VMEM((n, D_MODEL), BF16),
            pltpu.VMEM((n, D_MODEL), F32),
        ],
        compiler_params=_params(1),
        name="sample_post",
    )(xs, mod, mix, w_out, gpost_m, gpre_f, gpost_f, w1, w3, w2)


def kernel(x_prompt, x_sample, c_prompt, c_sample, state_ret, state_conv_a, state_conv_c, ada_w, ada_b, norm_pre_mix, norm_post_mix, norm_pre_ffn, norm_post_ffn, w_in, w_out, ret_gn_g, conv_a_w, conv_a_b, conv_a_ln_g, conv_a_ln_b, conv_c_w, ffn_w1, ffn_w3, ffn_w2):
    bp, lp, _ = x_prompt.shape
    ns = x_sample.shape[0]
    assert x_sample.shape[1] == 1 and ns % SEQ_BLK == 0
    assert lp % TOK_TILE == 0 and TOK_TILE % RET_CHUNK == 0
    assert TOK_TILE % CONV_BLK == 0 and CONV_BLK % (SUBLANES * CONV_STRIDE) == 0

    k_scale = RET_DK ** -0.5
    cq, sq = _rope_tables(np.arange(lp), 1.0)
    ck, sk = _rope_tables(np.arange(lp), k_scale)
    tables_p = (cq, sq, ck, sk) + _decay_tables(RET_CHUNK)
    cqs, sqs = _rope_tables([PAST_LEN], 1.0)
    cks, sks = _rope_tables([PAST_LEN], k_scale)
    rope_s = np.concatenate([cqs, sqs, cks, sks, np.zeros((4, LANES), np.float32)], axis=0)
    cd_s = _decay_tables(1)[3]

    mod = _ada_modulation(jnp.concatenate([c_sample, c_prompt], axis=0), ada_w, ada_b)

    rows = lambda a: a.reshape(DEPTH, 1, -1)
    w_in_b, w_out_b = w_in.astype(BF16), w_out.astype(BF16)
    w1_b, w3_b, w2_b = ffn_w1.astype(BF16), ffn_w3.astype(BF16), ffn_w2.astype(BF16)
    gn, cab, lng, lnb = rows(ret_gn_g), rows(conv_a_b), rows(conv_a_ln_g), rows(conv_a_ln_b)
    g_pm, g_qm, g_pf, g_qf = rows(norm_pre_mix), rows(norm_post_mix), rows(norm_pre_ffn), rows(norm_post_ffn)
    conv_a_t = jnp.transpose(state_conv_a, (0, 2, 1, 3))
    conv_c_t = jnp.transpose(state_conv_c, (0, 2, 1, 3))

    yp = x_prompt
    ys = x_sample.reshape(ns, D_MODEL)
    st_p = None
    st_s = None
    for l in range(DEPTH):
        mod_p = mod[l, ns:].reshape(bp, 6, D_MODEL)
        yp, *st_p = _prompt_layer(l, st_p, yp, mod_p, g_pm, g_qm, g_pf, g_qf, w_in_b, w_out_b, w1_b, w3_b, w2_b,
                                  tables_p, gn, conv_a_w, cab, lng, lnb, conv_c_w)
        proj, qt, kt = _sample_pre(l, ys, mod, g_pm, w_in_b, rope_s)
        mix, *st_s = _sample_state(l, st_s, proj, qt, kt, state_ret, conv_a_t, conv_c_t, cd_s,
                                   gn, conv_a_w, cab, lng, lnb, conv_c_w)
        ys = _sample_post(l, ys, mod, mix, w_out_b, g_qm, g_pf, g_qf, w1_b, w3_b, w2_b)

    sret_s, sca_s, scc_s = st_s
    sca_s = jnp.transpose(sca_s, (0, 2, 1, 3))
    scc_s = jnp.transpose(scc_s, (0, 2, 1, 3))
    return (yp, ys.reshape(ns, 1, D_MODEL)) + tuple(st_p) + (sret_s, sca_s, scc_s)
```

```python
import functools

import numpy as np
import jax
import jax.numpy as jnp
from jax import lax
from jax.experimental import pallas as pl
from jax.experimental.pallas import tpu as pltpu

D_MODEL = 1024
DEPTH = 2
PAST_LEN = 16384
RET_HEADS = 4
RET_WIDTH = D_MODEL // 2
RET_DV = RET_WIDTH // RET_HEADS
RET_DK = RET_DV // 2
QK_WIDTH = RET_HEADS * RET_DK
CONV_A_WIDTH = D_MODEL // 4
CONV_A_K = 31
CONV_C_WIDTH = D_MODEL - RET_WIDTH - CONV_A_WIDTH
CONV_C_K = 3
IN_WIDTH = 2 * QK_WIDTH + 2 * RET_WIDTH + 2 * CONV_A_WIDTH + 3 * CONV_C_WIDTH
D_FF = ((8 * D_MODEL // 3 + 255) // 256) * 256
RET_CHUNK = 128
ROPE_BASE = 10000.0
EPS = 1e-6

OFF_Q = 0
OFF_K = OFF_Q + QK_WIDTH
OFF_V = OFF_K + QK_WIDTH
OFF_G = OFF_V + RET_WIDTH
OFF_AV = OFF_G + RET_WIDTH
OFF_AG = OFF_AV + CONV_A_WIDTH
OFF_CB = OFF_AG + CONV_A_WIDTH
OFF_CC = OFF_CB + CONV_C_WIDTH
OFF_CX = OFF_CC + CONV_C_WIDTH
MIX_A = RET_WIDTH
MIX_C = RET_WIDTH + CONV_A_WIDTH

LANES = 128
SUBLANES = 8
V7X_VMEM_LIMIT_BYTES = 60 * 1024 * 1024

TOK_TILE = 512
ROW_BLK = 32
CONV_BLK = 64
CONV_STRIDE = 4
HIST_A = 32
HIST_C = 8
SEQ_BLK = 8
FF_BLK = 256
IN_BLK = 256

F32 = jnp.float32
BF16 = jnp.bfloat16


def _sigmoid(x):
    return jax.nn.sigmoid(x)


def _silu(x):
    return x * _sigmoid(x)


def _rms(x):
    return x * lax.rsqrt(jnp.mean(x * x, axis=-1, keepdims=True) + EPS)


def _layernorm(x, g, b=None):
    mu = jnp.mean(x, axis=-1, keepdims=True)
    d = x - mu
    var = jnp.mean(d * d, axis=-1, keepdims=True)
    y = d * lax.rsqrt(var + EPS) * g
    return y if b is None else y + b


def _rope_half(x, cos, sin, first_half):
    partner = jnp.where(first_half, pltpu.roll(x, 96, 1), pltpu.roll(x, 32, 1))
    return x * cos + partner * sin


def _first_half_mask(rows):
    lane = lax.broadcasted_iota(jnp.int32, (rows, LANES), 1)
    return (lane & (RET_DK - 1)) < (RET_DK // 2), lane < RET_DK


def _swiglu_block(h, w1_ref, w3_ref, col0):
    cols = pl.ds(col0, FF_BLK)
    a = jnp.dot(h, w1_ref[:, cols], preferred_element_type=F32)
    b = jnp.dot(h, w3_ref[:, cols], preferred_element_type=F32)
    return (_silu(a) * b).astype(BF16)


def _emit(items):
    for item in items:
        item()


def _emit_interleaved(primary, secondary):
    n, m = len(primary), len(secondary)
    done = 0
    for i, item in enumerate(primary):
        item()
        upto = ((i + 1) * m) // n
        _emit(secondary[done:upto])
        done = upto


def _rope_tables(pos, k_scale):
    half = RET_DK // 2
    inv = ROPE_BASE ** (-np.arange(half, dtype=np.float64) / half)
    ang = np.asarray(pos, np.float64)[:, None] * inv[None, :]
    cos = np.tile(np.cos(ang), (1, 4))
    sin = np.tile(np.concatenate([-np.sin(ang), np.sin(ang)], axis=1), (1, 2))
    return (cos * k_scale).astype(np.float32), (sin * k_scale).astype(np.float32)


def _decay_tables(chunk):
    log_g = np.log(1.0 - np.exp2(-5.0 - np.arange(RET_HEADS, dtype=np.float64)))
    idx = np.arange(chunk, dtype=np.float64)
    diff = idx[:, None] - idx[None, :]
    dmat = np.where(diff[None] >= 0, np.exp(np.maximum(diff, 0.0)[None] * log_g[:, None, None]), 0.0)
    read_dec = np.exp((idx + 1.0)[:, None] * log_g[None, :])
    upd_dec = np.exp((chunk - 1.0 - idx)[:, None] * log_g[None, :])
    chunk_dec = np.exp(chunk * log_g)
    pairs = RET_HEADS // 2
    side_by_side = lambda a: a.reshape(pairs, 2, *a.shape[1:]).transpose(0, 2, 1, 3).reshape(pairs, a.shape[1], -1)
    dmat2 = side_by_side(dmat)
    rd2 = side_by_side(np.broadcast_to(read_dec.T[:, :, None], (RET_HEADS, chunk, RET_DV)))
    updt = np.repeat(upd_dec.T, RET_DK, axis=0).reshape(pairs, 2 * RET_DK, chunk)
    cd = np.broadcast_to(chunk_dec[:, None, None], (RET_HEADS, 1, RET_DV))
    cd2 = side_by_side(cd)
    f = lambda a: np.ascontiguousarray(a, dtype=np.float32)
    return f(dmat2), f(rd2), f(updt), f(cd2), f(cd)


def _const_spec(shape, grid_rank, single_buffer=True):
    zeros = (0,) * len(shape)
    idx = (lambda b, c: zeros) if grid_rank == 2 else (lambda j: zeros)
    if single_buffer:
        return pl.BlockSpec(shape, idx, pipeline_mode=pl.Buffered(1))
    return pl.BlockSpec(shape, idx)


def _layer_spec(shape, layer, single_buffer=True):
    zeros = (0,) * len(shape)
    mode = dict(pipeline_mode=pl.Buffered(1)) if single_buffer else {}
    return pl.BlockSpec((None,) + tuple(shape), lambda j: (layer,) + zeros, **mode)


def _params(n_axes):
    return pltpu.CompilerParams(dimension_semantics=("arbitrary",) * n_axes,
                                vmem_limit_bytes=V7X_VMEM_LIMIT_BYTES)


def _stacked_call(kern, *, name, grid, in_specs, args, out_specs, out_shape, layer, stacked, prev, scratch_shapes):
    def spec(n_layers, block, index_fn):
        return pl.BlockSpec((n_layers,) + tuple(block), lambda *g: (0,) + tuple(index_fn(*g)))

    in_specs, out_specs, out_shape, args = list(in_specs), list(out_specs), list(out_shape), list(args)
    for dims, block, index_fn in stacked:
        out_specs.append(spec(layer + 1, block, index_fn))
        out_shape.append(jax.ShapeDtypeStruct((layer + 1,) + tuple(dims), F32))
        if layer > 0:
            in_specs.append(spec(layer, block, index_fn))
    if layer > 0:
        args += list(prev)
    return pl.pallas_call(
        kern, grid=grid, in_specs=in_specs, out_specs=out_specs, out_shape=out_shape,
        scratch_shapes=scratch_shapes, compiler_params=_params(len(grid)), name=name)(*args)


def _ada_kernel(c_ref, w_ref, b_ref, o_ref):
    s = _silu(c_ref[...]).astype(BF16)
    o_ref[...] = jnp.dot(s, w_ref[...].astype(BF16), preferred_element_type=F32) + b_ref[...]


def _ada_modulation(c_all, ada_w, ada_b):
    rows = c_all.shape[0]
    ncol = 6 * D_MODEL
    blk = D_MODEL
    return pl.pallas_call(
        _ada_kernel,
        grid=(DEPTH, ncol // blk),
        in_specs=[
            pl.BlockSpec((rows, D_MODEL), lambda l, j: (0, 0)),
            pl.BlockSpec((None, D_MODEL, blk), lambda l, j: (l, 0, j)),
            pl.BlockSpec((None, 1, blk), lambda l, j: (l, 0, j)),
        ],
        out_specs=pl.BlockSpec((None, rows, blk), lambda l, j: (l, 0, j)),
        out_shape=jax.ShapeDtypeStruct((DEPTH, rows, ncol), F32),
        compiler_params=_params(2),
        name="ada_mod",
    )(c_all, ada_w, ada_b.reshape(DEPTH, 1, ncol))


def _prompt_layer_kernel(*refs, tile, n_tiles, tiles_per_seq, layer):
    (xc_ref, xp_ref, modc_ref, modp_ref, gpm_ref, gqm_ref, gpf_ref, gqf_ref,
     win_ref, wout_ref, w1_ref, w3_ref, w2_ref,
     cq_ref, sq_ref, ck_ref, sk_ref, dmat_ref, rd_ref, upd_ref, cd_ref,
     gn_ref, caw_ref, cab_ref, lng_ref, lnb_ref, ccw_ref) = refs[:27]
    n_prev = 3 if layer > 0 else 0
    prev_states = refs[27:27 + n_prev]
    (y_ref, sret_all, sca_all, scc_all,
     h_scr, proj_scr, mixin_scr, mix_scr, hf_scr, p_scr, ubuf, ua_scr, zbuf,
     sc_scr, qs_scr, sbd_scr) = refs[27 + n_prev:]
    sret_ref, sca_ref, scc_ref = sret_all.at[layer], sca_all.at[layer], scc_all.at[layer]
    step = pl.program_id(0)
    row_blocks = [pl.ds(r * ROW_BLK, ROW_BLK) for r in range(tile // ROW_BLK)]
    first_tap = HIST_A - (CONV_A_K - 1)
    first_tap_c = HIST_C - (CONV_C_K - 1)
    span = SUBLANES * CONV_STRIDE

    @pl.when((step < n_tiles) & (lax.rem(step, tiles_per_seq) == 0))
    def _():
        for prev_ref, all_ref in zip(prev_states, (sret_all, sca_all, scc_all)):
            all_ref[0:layer] = prev_ref[...]
        sret_ref[...] = jnp.zeros_like(sret_ref)
        sbd_scr[...] = jnp.zeros_like(sbd_scr)
        ubuf[:, 0:HIST_A, :] = jnp.zeros((2, HIST_A, LANES), F32)
        zbuf[0:HIST_C, :] = jnp.zeros((HIST_C, CONV_C_WIDTH), F32)

    def mix_prenorm():
        sh = modc_ref[0:1, :]
        pre_scale = gpm_ref[...] * (1.0 + modc_ref[1:2, :])
        for rows in row_blocks:
            h_scr[rows, :] = (_rms(xc_ref[rows, :]) * pre_scale + sh).astype(BF16)

    def in_proj(n):
        cols = pl.ds(n * IN_BLK, IN_BLK)
        proj_scr[:, cols] = jnp.dot(h_scr[...], win_ref[:, cols], preferred_element_type=F32)

    def conv_groups():
        taps = [[jnp.broadcast_to(caw_ref[j:j + 1, pl.ds(half * LANES, LANES)], (SUBLANES, LANES))
                 for j in range(CONV_A_K)] for half in range(2)]
        for r in range(tile // CONV_BLK):
            rows = pl.ds(r * CONV_BLK, CONV_BLK)
            dst = pl.ds(HIST_A + r * CONV_BLK, CONV_BLK)
            for half in range(2):
                lo = half * LANES
                u = (proj_scr[rows, pl.ds(OFF_AV + lo, LANES)]
                     * _sigmoid(proj_scr[rows, pl.ds(OFF_AG + lo, LANES)]))
                ubuf[half, dst, :] = u
        for half in range(2):
            for m in range(tile // span):
                base = m * span
                acc = [None] * CONV_STRIDE
                for k in range(CONV_STRIDE + CONV_A_K - 1):
                    win = ubuf[half, pl.ds(base + first_tap + k, SUBLANES, stride=CONV_STRIDE), :]
                    for t in range(CONV_STRIDE):
                        j = k - t
                        if 0 <= j < CONV_A_K:
                            term = taps[half][j] * win
                            acc[t] = term if acc[t] is None else acc[t] + term
                for t in range(CONV_STRIDE):
                    ua_scr[half, pl.ds(base + t, SUBLANES, stride=CONV_STRIDE), :] = acc[t]
        for r in range(tile // CONV_BLK):
            rows = pl.ds(r * CONV_BLK, CONV_BLK)
            ua = jnp.concatenate([ua_scr[0, rows, :], ua_scr[1, rows, :]], axis=-1) + cab_ref[...]
            o_a = _silu(_layernorm(ua, lng_ref[...], lnb_ref[...]))
            mixin_scr[rows, MIX_A:MIX_A + CONV_A_WIDTH] = o_a.astype(BF16)
        for r in range(tile // CONV_BLK):
            rows = pl.ds(r * CONV_BLK, CONV_BLK)
            z = proj_scr[rows, OFF_CC:OFF_CC + CONV_C_WIDTH] * proj_scr[rows, OFF_CX:OFF_CX + CONV_C_WIDTH]
            zbuf[pl.ds(HIST_C + r * CONV_BLK, CONV_BLK), :] = z
        for r in range(tile // CONV_BLK):
            rows = pl.ds(r * CONV_BLK, CONV_BLK)
            zc = ccw_ref[0:1, :] * zbuf[pl.ds(r * CONV_BLK + first_tap_c, CONV_BLK), :]
            for j in range(1, CONV_C_K):
                zc = zc + ccw_ref[j:j + 1, :] * zbuf[pl.ds(r * CONV_BLK + first_tap_c + j, CONV_BLK), :]
            o_c = proj_scr[rows, OFF_CB:OFF_CB + CONV_C_WIDTH] * zc
            mixin_scr[rows, MIX_C:MIX_C + CONV_C_WIDTH] = o_c.astype(BF16)
        hist_a = pl.ds(tile + first_tap, CONV_A_K - 1)
        sca_ref[...] = jnp.concatenate([ubuf[0, hist_a, :], ubuf[1, hist_a, :]], axis=-1)
        scc_ref[...] = zbuf[pl.ds(tile + first_tap_c, CONV_C_K - 1), :]
        ubuf[:, 0:HIST_A, :] = ubuf[:, pl.ds(tile, HIST_A), :]
        zbuf[0:HIST_C, :] = zbuf[pl.ds(tile, HIST_C), :]

    def retention_scores(i, pair):
        first_half, _ = _first_half_mask(RET_CHUNK)
        rows = pl.ds(i * RET_CHUNK, RET_CHUNK)
        cq, sq, ck, sk = cq_ref[rows, :], sq_ref[rows, :], ck_ref[rows, :], sk_ref[rows, :]
        qr = _rope_half(proj_scr[rows, pl.ds(OFF_Q + pair * LANES, LANES)], cq, sq, first_half)
        kr = _rope_half(proj_scr[rows, pl.ds(OFF_K + pair * LANES, LANES)], ck, sk, first_half)
        qr_b = qr.astype(BF16)
        k_t = kr.T
        head0_rows = lax.broadcasted_iota(jnp.int32, (2 * RET_DK, RET_CHUNK), 0) < RET_DK
        k_bd = jnp.concatenate([jnp.where(head0_rows, k_t, 0.0), jnp.where(head0_rows, 0.0, k_t)], axis=1)
        scores = jnp.dot(qr_b, k_bd.astype(BF16), preferred_element_type=F32) * dmat_ref[pair]
        sc_scr[...] = scores.astype(BF16)
        s_bd = sbd_scr[pair]
        qs_scr[...] = jnp.dot(qr_b, s_bd.astype(BF16), preferred_element_type=F32) * rd_ref[pair]
        v_pair = proj_scr[rows, pl.ds(OFF_V + 2 * pair * RET_DV, 2 * RET_DV)].astype(BF16)
        kv = jnp.dot((k_t * upd_ref[pair]).astype(BF16), v_pair, preferred_element_type=F32)
        r0 = lax.broadcasted_iota(jnp.int32, (2 * RET_DK, 2 * RET_DV), 0) < RET_DK
        c0 = lax.broadcasted_iota(jnp.int32, (2 * RET_DK, 2 * RET_DV), 1) < RET_DV
        s_new = s_bd * cd_ref[pair] + jnp.where(r0 == c0, kv, 0.0)
        sbd_scr[pair] = s_new
        sret_ref[2 * pair] = s_new[0:RET_DK, 0:RET_DV]
        sret_ref[2 * pair + 1] = s_new[RET_DK:2 * RET_DK, RET_DV:2 * RET_DV]

    def retention_values(i, pair):
        rows = pl.ds(i * RET_CHUNK, RET_CHUNK)
        for hl in range(2):
            h = 2 * pair + hl
            vh = proj_scr[rows, pl.ds(OFF_V + h * RET_DV, RET_DV)].astype(BF16)
            o = (jnp.dot(sc_scr[:, pl.ds(hl * RET_CHUNK, RET_CHUNK)], vh, preferred_element_type=F32)
                 + qs_scr[:, pl.ds(hl * RET_DV, RET_DV)])
            gate = proj_scr[rows, pl.ds(OFF_G + h * RET_DV, RET_DV)]
            o = _layernorm(o, gn_ref[:, pl.ds(h * RET_DV, RET_DV)]) * _silu(gate)
            mixin_scr[rows, pl.ds(h * RET_DV, RET_DV)] = o.astype(BF16)

    def out_proj():
        mix_scr[...] = jnp.dot(mixin_scr[...], wout_ref[...], preferred_element_type=F32)

    in_tiles = [functools.partial(in_proj, n) for n in range(IN_WIDTH // IN_BLK)]
    ret_items = [functools.partial(phase, i, pair)
                 for i in range(tile // RET_CHUNK) for pair in range(RET_HEADS // 2)
                 for phase in (retention_scores, retention_values)]

    def ffn_norms():
        post_scale = gqm_ref[...] * modp_ref[2:3, :]
        sh = modp_ref[3:4, :]
        pre_scale = gpf_ref[...] * (1.0 + modp_ref[4:5, :])
        for rows in row_blocks:
            x1 = xp_ref[rows, :] + _rms(mix_scr[rows, :]) * post_scale
            y_ref[rows, :] = x1
            hf_scr[rows, :] = (_rms(x1) * pre_scale + sh).astype(BF16)

    def hidden(j):
        p_scr[:, pl.ds(j * FF_BLK, FF_BLK)] = _swiglu_block(hf_scr[...], w1_ref, w3_ref, j * FF_BLK)

    def down(n):
        cols = pl.ds(n * FF_BLK, FF_BLK)
        proj_scr[:, cols] = jnp.dot(p_scr[...], w2_ref[:, cols], preferred_element_type=F32)

    def ffn_finish():
        post_scale = gqf_ref[...] * modp_ref[5:6, :]
        for rows in row_blocks:
            y_ref[rows, :] = y_ref[rows, :] + _rms(proj_scr[rows, 0:D_MODEL]) * post_scale

    hidden_blocks = [functools.partial(hidden, j) for j in range(D_FF // FF_BLK)]
    down_tiles = [functools.partial(down, n) for n in range(D_MODEL // FF_BLK)]

    @pl.when(step == 0)
    def _():
        mix_prenorm()
        _emit(in_tiles)
        _emit(ret_items)
        conv_groups()
        out_proj()

    @pl.when((step > 0) & (step < n_tiles))
    def _():
        mix_prenorm()
        ffn_norms()
        _emit(in_tiles)
        _emit_interleaved(hidden_blocks, ret_items)
        conv_groups()
        _emit(down_tiles)
        out_proj()
        ffn_finish()

    @pl.when(step == n_tiles)
    def _():
        ffn_norms()
        _emit(hidden_blocks)
        _emit(down_tiles)
        ffn_finish()


def _prompt_layer(layer, prev, x, mod_p, g_pm, g_qm, g_pf, g_qf, w_in, w_out, w1, w3, w2, tables,
                  gn, caw, cab, lng, lnb, ccw):
    bsz, seq, _ = x.shape
    tile = TOK_TILE
    tps = seq // tile
    n_tiles = bsz * tps
    cq, sq, ck, sk, dmat, rd, upd, cd, _ = tables
    cur = lambda i: jnp.minimum(i, n_tiles - 1)
    prv = lambda i: jnp.maximum(i - 1, 0)
    tok_cur = pl.BlockSpec((None, tile, D_MODEL), lambda i: (cur(i) // tps, cur(i) % tps, 0))
    tok_prev = pl.BlockSpec((None, tile, D_MODEL), lambda i: (prv(i) // tps, prv(i) % tps, 0))
    mod_cur = pl.BlockSpec((None, 6, D_MODEL), lambda i: (cur(i) // tps, 0, 0))
    mod_prev = pl.BlockSpec((None, 6, D_MODEL), lambda i: (prv(i) // tps, 0, 0))
    rope = pl.BlockSpec((tile, LANES), lambda i: (cur(i) % tps, 0))
    cs = functools.partial(_const_spec, grid_rank=1)
    ls = functools.partial(_layer_spec, layer=layer)
    args = (x, x, mod_p, mod_p, g_pm, g_qm, g_pf, g_qf, w_in, w_out, w1, w3, w2,
            cq, sq, ck, sk, dmat, rd, upd, cd, gn, caw, cab, lng, lnb, ccw)
    seq_block = lambda i: (cur(i) // tps,)
    return _stacked_call(
        functools.partial(_prompt_layer_kernel, tile=tile, n_tiles=n_tiles, tiles_per_seq=tps, layer=layer),
        name="prompt_layer", grid=(n_tiles + 1,), args=args, layer=layer, prev=prev,
        stacked=[
            ((bsz, RET_HEADS, RET_DK, RET_DV), (None, RET_HEADS, RET_DK, RET_DV),
             lambda i: seq_block(i) + (0, 0, 0)),
            ((bsz, CONV_A_K - 1, CONV_A_WIDTH), (None, CONV_A_K - 1, CONV_A_WIDTH),
             lambda i: seq_block(i) + (0, 0)),
            ((bsz, CONV_C_K - 1, CONV_C_WIDTH), (None, CONV_C_K - 1, CONV_C_WIDTH),
             lambda i: seq_block(i) + (0, 0)),
        ],
        in_specs=[
            tok_cur, tok_prev, mod_cur, mod_prev,
            ls((1, D_MODEL)), ls((1, D_MODEL)), ls((1, D_MODEL)), ls((1, D_MODEL)),
            ls((D_MODEL, IN_WIDTH)), ls((D_MODEL, D_MODEL)),
            ls((D_MODEL, D_FF)), ls((D_MODEL, D_FF)), ls((D_FF, D_MODEL)),
            rope, rope, rope, rope,
            cs(dmat.shape), cs(rd.shape), cs(upd.shape), cs(cd.shape),
            ls((1, RET_WIDTH)), ls((CONV_A_K, CONV_A_WIDTH)), ls((1, CONV_A_WIDTH)),
            ls((1, CONV_A_WIDTH)), ls((1, CONV_A_WIDTH)), ls((CONV_C_K, CONV_C_WIDTH)),
        ],
        out_specs=[tok_prev],
        out_shape=[jax.ShapeDtypeStruct(x.shape, F32)],
        scratch_shapes=[
            pltpu.VMEM((tile, D_MODEL), BF16),
            pltpu.VMEM((tile, IN_WIDTH), F32),
            pltpu.VMEM((tile, D_MODEL), BF16),
            pltpu.VMEM((tile, D_MODEL), F32),
            pltpu.VMEM((tile, D_MODEL), BF16),
            pltpu.VMEM((tile, D_FF), BF16),
            pltpu.VMEM((2, tile + HIST_A, LANES), F32),
            pltpu.VMEM((2, tile, LANES), F32),
            pltpu.VMEM((tile + HIST_C, CONV_C_WIDTH), F32),
            pltpu.VMEM((RET_CHUNK, 2 * RET_CHUNK), BF16),
            pltpu.VMEM((RET_CHUNK, 2 * RET_DV), F32),
            pltpu.VMEM((RET_HEADS // 2, 2 * RET_DK, 2 * RET_DV), F32),
        ])


def _sample_pre_kernel(x_ref, mod_ref, gpre_ref, win_ref, rope_ref, proj_ref, qt_ref, kt_ref):
    n = x_ref.shape[0]
    sh = mod_ref[:, 0:D_MODEL]
    sc1 = 1.0 + mod_ref[:, D_MODEL:2 * D_MODEL]
    h = (_rms(x_ref[...]) * (gpre_ref[...] * sc1) + sh).astype(BF16)
    proj_ref[...] = jnp.dot(h, win_ref[...], preferred_element_type=F32)
    first_half, _ = _first_half_mask(n)
    cq, sq, ck, sk = rope_ref[0:1, :], rope_ref[1:2, :], rope_ref[2:3, :], rope_ref[3:4, :]
    for pair in range(RET_HEADS // 2):
        ql = pl.ds(OFF_Q + pair * LANES, LANES)
        kl = pl.ds(OFF_K + pair * LANES, LANES)
        qr = _rope_half(proj_ref[:, ql], cq, sq, first_half)
        kr = _rope_half(proj_ref[:, kl], ck, sk, first_half)
        qt_ref[pl.ds(pair * LANES, LANES), :] = qr.T
        kt_ref[pl.ds(pair * LANES, LANES), :] = kr.T


def _sample_pre(layer, xs, mod, gpre, w_in, rope_s):
    n = xs.shape[0]
    cs = functools.partial(_const_spec, grid_rank=1, single_buffer=False)
    ls = functools.partial(_layer_spec, layer=layer, single_buffer=False)
    return pl.pallas_call(
        _sample_pre_kernel,
        grid=(1,),
        in_specs=[cs((n, D_MODEL)), ls((n, 6 * D_MODEL)), ls((1, D_MODEL)),
                  ls((D_MODEL, IN_WIDTH)), cs(rope_s.shape)],
        out_specs=[cs((n, IN_WIDTH)), cs((QK_WIDTH, n)), cs((QK_WIDTH, n))],
        out_shape=[jax.ShapeDtypeStruct((n, IN_WIDTH), F32),
                   jax.ShapeDtypeStruct((QK_WIDTH, n), F32),
                   jax.ShapeDtypeStruct((QK_WIDTH, n), F32)],
        compiler_params=_params(1),
        name="sample_pre",
    )(xs, mod, gpre, w_in, rope_s)


def _sample_state_kernel(*refs, layer):
    (proj_ref, qt_ref, kt_ref, sin_ref, bufa_ref, bufc_ref, cd_ref,
     gn_ref, caw_ref, cab_ref, lng_ref, lnb_ref, ccw_ref) = refs[:13]
    n_prev = 3 if layer > 0 else 0
    prev_states = refs[13:13 + n_prev]
    mix_ref, sout_all, outa_all, outc_all, o_scr = refs[13 + n_prev:]
    for prev_ref, all_ref in zip(prev_states, (sout_all, outa_all, outc_all)):
        all_ref[0:layer] = prev_ref[...]
    sout_ref, outa_ref, outc_ref = sout_all.at[layer], outa_all.at[layer], outc_all.at[layer]
    blk = pl.program_id(0)
    n = qt_ref.shape[1]
    lane = lax.broadcasted_iota(jnp.int32, (QK_WIDTH, n), 1)

    for bl in range(SEQ_BLK):
        onehot = lane == (blk * SEQ_BLK + bl)
        qcol = jnp.sum(jnp.where(onehot, qt_ref[...], 0.0), axis=1, keepdims=True)
        kcol = jnp.sum(jnp.where(onehot, kt_ref[...], 0.0), axis=1, keepdims=True)
        row = pl.ds(bl, 1)
        for h in range(RET_HEADS):
            vrow = proj_ref[row, pl.ds(OFF_V + h * RET_DV, RET_DV)]
            s_new = (sin_ref[bl, h] * cd_ref[h]
                     + kcol[h * RET_DK:(h + 1) * RET_DK, :] * vrow)
            sout_ref[bl, h] = s_new
            o_scr[row, pl.ds(h * RET_DV, RET_DV)] = jnp.sum(
                qcol[h * RET_DK:(h + 1) * RET_DK, :] * s_new, axis=0, keepdims=True)
    for h in range(RET_HEADS):
        cols = pl.ds(h * RET_DV, RET_DV)
        o = _layernorm(o_scr[:, cols], gn_ref[:, cols]) * _silu(proj_ref[:, pl.ds(OFF_G + h * RET_DV, RET_DV)])
        mix_ref[:, cols] = o

    u = proj_ref[:, OFF_AV:OFF_AV + CONV_A_WIDTH] * _sigmoid(proj_ref[:, OFF_AG:OFF_AG + CONV_A_WIDTH])
    ua = caw_ref[CONV_A_K - 1:CONV_A_K, :] * u
    for j in range(CONV_A_K - 1):
        ua = ua + caw_ref[j:j + 1, :] * bufa_ref[j]
    outa_ref[0:CONV_A_K - 2] = bufa_ref[1:CONV_A_K - 1]
    outa_ref[CONV_A_K - 2] = u
    mix_ref[:, MIX_A:MIX_A + CONV_A_WIDTH] = _silu(_layernorm(ua + cab_ref[...], lng_ref[...], lnb_ref[...]))

    z = proj_ref[:, OFF_CC:OFF_CC + CONV_C_WIDTH] * proj_ref[:, OFF_CX:OFF_CX + CONV_C_WIDTH]
    zc = ccw_ref[0:1, :] * bufc_ref[0] + ccw_ref[1:2, :] * bufc_ref[1] + ccw_ref[2:3, :] * z
    outc_ref[0] = bufc_ref[1]
    outc_ref[1] = z
    mix_ref[:, MIX_C:MIX_C + CONV_C_WIDTH] = proj_ref[:, OFF_CB:OFF_CB + CONV_C_WIDTH] * zc


def _sample_state(layer, prev, proj, qt, kt, s_ret, buf_a_t, buf_c_t, cd, gn, caw, cab, lng, lnb, ccw):
    n = proj.shape[0]
    cs = functools.partial(_const_spec, grid_rank=1, single_buffer=False)
    ls = functools.partial(_layer_spec, layer=layer, single_buffer=False)
    s_spec = pl.BlockSpec((None, SEQ_BLK, RET_HEADS, RET_DK, RET_DV), lambda j: (layer, j, 0, 0, 0))
    a_spec = pl.BlockSpec((None, CONV_A_K - 1, SEQ_BLK, CONV_A_WIDTH), lambda j: (layer, 0, j, 0))
    c_spec = pl.BlockSpec((None, CONV_C_K - 1, SEQ_BLK, CONV_C_WIDTH), lambda j: (layer, 0, j, 0))
    args = (proj, qt, kt, s_ret, buf_a_t, buf_c_t, cd, gn, caw, cab, lng, lnb, ccw)
    return _stacked_call(
        functools.partial(_sample_state_kernel, layer=layer),
        name="sample_state", grid=(n // SEQ_BLK,), args=args, layer=layer, prev=prev,
        stacked=[
            ((n, RET_HEADS, RET_DK, RET_DV), (SEQ_BLK, RET_HEADS, RET_DK, RET_DV), lambda j: (j, 0, 0, 0)),
            ((CONV_A_K - 1, n, CONV_A_WIDTH), (CONV_A_K - 1, SEQ_BLK, CONV_A_WIDTH), lambda j: (0, j, 0)),
            ((CONV_C_K - 1, n, CONV_C_WIDTH), (CONV_C_K - 1, SEQ_BLK, CONV_C_WIDTH), lambda j: (0, j, 0)),
        ],
        in_specs=[
            pl.BlockSpec((SEQ_BLK, IN_WIDTH), lambda j: (j, 0)),
            cs((QK_WIDTH, n)), cs((QK_WIDTH, n)),
            s_spec, a_spec, c_spec,
            cs(cd.shape), ls((1, RET_WIDTH)), ls((CONV_A_K, CONV_A_WIDTH)), ls((1, CONV_A_WIDTH)),
            ls((1, CONV_A_WIDTH)), ls((1, CONV_A_WIDTH)), ls((CONV_C_K, CONV_C_WIDTH)),
        ],
        out_specs=[pl.BlockSpec((SEQ_BLK, D_MODEL), lambda j: (j, 0))],
        out_shape=[jax.ShapeDtypeStruct((n, D_MODEL), F32)],
        scratch_shapes=[pltpu.VMEM((SEQ_BLK, RET_WIDTH), F32)])


def _sample_post_kernel(x_ref, mod_ref, mix_ref, wout_ref, gpm_ref, gpf_ref, gqf_ref, w1_ref, w3_ref, w2_ref,
                        y_ref, x1_scr, h_scr, f_scr, *, n_steps):
    j = pl.program_id(0)

    @pl.when(j == 0)
    def _():
        gt_m = mod_ref[:, 2 * D_MODEL:3 * D_MODEL]
        mix = jnp.dot(mix_ref[...].astype(BF16), wout_ref[...], preferred_element_type=F32)
        x1 = x_ref[...] + _rms(mix) * (gpm_ref[...] * gt_m)
        x1_scr[...] = x1
        sh = mod_ref[:, 3 * D_MODEL:4 * D_MODEL]
        sc1 = 1.0 + mod_ref[:, 4 * D_MODEL:5 * D_MODEL]
        h_scr[...] = (_rms(x1) * (gpf_ref[...] * sc1) + sh).astype(BF16)
        f_scr[...] = jnp.zeros_like(f_scr)

    p = _swiglu_block(h_scr[...], w1_ref, w3_ref, 0)
    f_scr[...] += jnp.dot(p, w2_ref[...], preferred_element_type=F32)

    @pl.when(j == n_steps - 1)
    def _():
        gt_f = mod_ref[:, 5 * D_MODEL:6 * D_MODEL]
        y_ref[...] = x1_scr[...] + _rms(f_scr[...]) * (gqf_ref[...] * gt_f)


def _sample_post(layer, xs, mod, mix, w_out, gpost_m, gpre_f, gpost_f, w1, w3, w2):
    n = xs.shape[0]
    n_steps = D_FF // FF_BLK
    cs = functools.partial(_const_spec, grid_rank=1, single_buffer=False)
    ls = functools.partial(_layer_spec, layer=layer, single_buffer=False)
    return pl.pallas_call(
        functools.partial(_sample_post_kernel, n_steps=n_steps),
        grid=(n_steps,),
        in_specs=[
            cs((n, D_MODEL)), ls((n, 6 * D_MODEL)), cs((n, D_MODEL)), ls((D_MODEL, D_MODEL)),
            ls((1, D_MODEL)), ls((1, D_MODEL)), ls((1, D_MODEL)),
            pl.BlockSpec((None, D_MODEL, FF_BLK), lambda j: (layer, 0, j)),
            pl.BlockSpec((None, D_MODEL, FF_BLK), lambda j: (layer, 0, j)),
            pl.BlockSpec((None, FF_BLK, D_MODEL), lambda j: (layer, j, 0)),
        ],
        out_specs=cs((n, D_MODEL)),
        out_shape=jax.ShapeDtypeStruct((n, D_MODEL), F32),
        scratch_shapes=[
            pltpu.VMEM((n, D_MODEL), F32),
            pltpu.VMEM((n, D_MODEL), BF16),
            pltpu.VMEM((n, D_MODEL), F32),
        ],
        compiler_params=_params(1),
        name="sample_post",
    )(xs, mod, mix, w_out, gpost_m, gpre_f, gpost_f, w1, w3, w2)


def kernel(x_prompt, x_sample, c_prompt, c_sample, state_ret, state_conv_a, state_conv_c, ada_w, ada_b, norm_pre_mix, norm_post_mix, norm_pre_ffn, norm_post_ffn, w_in, w_out, ret_gn_g, conv_a_w, conv_a_b, conv_a_ln_g, conv_a_ln_b, conv_c_w, ffn_w1, ffn_w3, ffn_w2):
    bp, lp, _ = x_prompt.shape
    ns = x_sample.shape[0]
    assert x_sample.shape[1] == 1 and ns % SEQ_BLK == 0
    assert lp % TOK_TILE == 0 and TOK_TILE % RET_CHUNK == 0
    assert TOK_TILE % CONV_BLK == 0 and CONV_BLK % (SUBLANES * CONV_STRIDE) == 0

    k_scale = RET_DK ** -0.5
    cq, sq = _rope_tables(np.arange(lp), 1.0)
    ck, sk = _rope_tables(np.arange(lp), k_scale)
    tables_p = (cq, sq, ck, sk) + _decay_tables(RET_CHUNK)
    cqs, sqs = _rope_tables([PAST_LEN], 1.0)
    cks, sks = _rope_tables([PAST_LEN], k_scale)
    rope_s = np.concatenate([cqs, sqs, cks, sks, np.zeros((4, LANES), np.float32)], axis=0)
    cd_s = _decay_tables(1)[4]

    mod = _ada_modulation(jnp.concatenate([c_sample, c_prompt], axis=0), ada_w, ada_b)

    rows = lambda a: a.reshape(DEPTH, 1, -1)
    w_in_b, w_out_b = w_in.astype(BF16), w_out.astype(BF16)
    w1_b, w3_b, w2_b = ffn_w1.astype(BF16), ffn_w3.astype(BF16), ffn_w2.astype(BF16)
    gn, cab, lng, lnb = rows(ret_gn_g), rows(conv_a_b), rows(conv_a_ln_g), rows(conv_a_ln_b)
    g_pm, g_qm, g_pf, g_qf = rows(norm_pre_mix), rows(norm_post_mix), rows(norm_pre_ffn), rows(norm_post_ffn)
    conv_a_t = jnp.transpose(state_conv_a, (0, 2, 1, 3))
    conv_c_t = jnp.transpose(state_conv_c, (0, 2, 1, 3))

    yp = x_prompt
    ys = x_sample.reshape(ns, D_MODEL)
    st_p = st_s = ()
    for l in range(DEPTH):
        mod_p = mod[l, ns:].reshape(bp, 6, D_MODEL)
        yp, *st_p = _prompt_layer(l, st_p, yp, mod_p, g_pm, g_qm, g_pf, g_qf, w_in_b, w_out_b, w1_b, w3_b, w2_b,
                                  tables_p, gn, conv_a_w, cab, lng, lnb, conv_c_w)
        proj, qt, kt = _sample_pre(l, ys, mod, g_pm, w_in_b, rope_s)
        mix, *st_s = _sample_state(l, st_s, proj, qt, kt, state_ret, conv_a_t, conv_c_t, cd_s,
                                   gn, conv_a_w, cab, lng, lnb, conv_c_w)
        ys = _sample_post(l, ys, mod, mix, w_out_b, g_qm, g_pf, g_qf, w1_b, w3_b, w2_b)

    sret_s, sca_s, scc_s = st_s
    sca_s = jnp.transpose(sca_s, (0, 2, 1, 3))
    scc_s = jnp.transpose(scc_s, (0, 2, 1, 3))
    return (yp, ys.reshape(ns, 1, D_MODEL)) + tuple(st_p) + (sret_s, sca_s, scc_s)
```

```python
import functools

import numpy as np
import jax
import jax.numpy as jnp
from jax import lax
from jax.experimental import pallas as pl
from jax.experimental.pallas import tpu as pltpu

D_MODEL = 1024
DEPTH = 2
PAST_LEN = 16384
RET_HEADS = 4
RET_WIDTH = D_MODEL // 2
RET_DV = RET_WIDTH // RET_HEADS
RET_DK = RET_DV // 2
QK_WIDTH = RET_HEADS * RET_DK
CONV_A_WIDTH = D_MODEL // 4
CONV_A_K = 31
CONV_C_WIDTH = D_MODEL - RET_WIDTH - CONV_A_WIDTH
CONV_C_K = 3
IN_WIDTH = 2 * QK_WIDTH + 2 * RET_WIDTH + 2 * CONV_A_WIDTH + 3 * CONV_C_WIDTH
D_FF = ((8 * D_MODEL // 3 + 255) // 256) * 256
RET_CHUNK = 128
ROPE_BASE = 10000.0
EPS = 1e-6

OFF_Q = 0
OFF_K = OFF_Q + QK_WIDTH
OFF_V = OFF_K + QK_WIDTH
OFF_G = OFF_V + RET_WIDTH
OFF_AV = OFF_G + RET_WIDTH
OFF_AG = OFF_AV + CONV_A_WIDTH
OFF_CB = OFF_AG + CONV_A_WIDTH
OFF_CC = OFF_CB + CONV_C_WIDTH
OFF_CX = OFF_CC + CONV_C_WIDTH
MIX_A = RET_WIDTH
MIX_C = RET_WIDTH + CONV_A_WIDTH

LANES = 128
SUBLANES = 8
V7X_VMEM_LIMIT_BYTES = 60 * 1024 * 1024

TOK_TILE = 512
ROW_BLK = 32
CONV_BLK = 64
CONV_STRIDE = 4
HIST_A = 32
HIST_C = 8
SEQ_BLK = 16
FF_BLK = 256
IN_BLK = 256
IN_ROW_PARTS = 1
PRE_BLK = 1408

F32 = jnp.float32
BF16 = jnp.bfloat16


def _sigmoid(x):
    return jax.nn.sigmoid(x)


def _silu(x):
    return x * _sigmoid(x)


def _rms(x):
    return x * lax.rsqrt(jnp.mean(x * x, axis=-1, keepdims=True) + EPS)


def _layernorm(x, g, b=None):
    mu = jnp.mean(x, axis=-1, keepdims=True)
    d = x - mu
    var = jnp.mean(d * d, axis=-1, keepdims=True)
    y = d * lax.rsqrt(var + EPS) * g
    return y if b is None else y + b


def _rope_half(x, cos, sin, first_half):
    partner = jnp.where(first_half, pltpu.roll(x, 96, 1), pltpu.roll(x, 32, 1))
    return x * cos + partner * sin


def _first_half_mask(rows):
    lane = lax.broadcasted_iota(jnp.int32, (rows, LANES), 1)
    return (lane & (RET_DK - 1)) < (RET_DK // 2), lane < RET_DK


def _swiglu_block(h, w1_ref, w3_ref, col0):
    cols = pl.ds(col0, FF_BLK)
    a = jnp.dot(h, w1_ref[:, cols], preferred_element_type=F32)
    b = jnp.dot(h, w3_ref[:, cols], preferred_element_type=F32)
    return (_silu(a) * b).astype(BF16)


def _emit(items):
    for item in items:
        item()


def _emit_interleaved(primary, secondary):
    n, m = len(primary), len(secondary)
    done = 0
    for i, item in enumerate(primary):
        item()
        upto = ((i + 1) * m) // n
        _emit(secondary[done:upto])
        done = upto


def _rope_tables(pos, k_scale):
    half = RET_DK // 2
    inv = ROPE_BASE ** (-np.arange(half, dtype=np.float64) / half)
    ang = np.asarray(pos, np.float64)[:, None] * inv[None, :]
    cos = np.tile(np.cos(ang), (1, 4))
    sin = np.tile(np.concatenate([-np.sin(ang), np.sin(ang)], axis=1), (1, 2))
    return (cos * k_scale).astype(np.float32), (sin * k_scale).astype(np.float32)


def _decay_tables(chunk):
    log_g = np.log(1.0 - np.exp2(-5.0 - np.arange(RET_HEADS, dtype=np.float64)))
    idx = np.arange(chunk, dtype=np.float64)
    diff = idx[:, None] - idx[None, :]
    dmat = np.where(diff[None] >= 0, np.exp(np.maximum(diff, 0.0)[None] * log_g[:, None, None]), 0.0)
    read_dec = np.exp((idx + 1.0)[:, None] * log_g[None, :])
    upd_dec = np.exp((chunk - 1.0 - idx)[:, None] * log_g[None, :])
    chunk_dec = np.exp(chunk * log_g)
    pairs = RET_HEADS // 2
    side_by_side = lambda a: a.reshape(pairs, 2, *a.shape[1:]).transpose(0, 2, 1, 3).reshape(pairs, a.shape[1], -1)
    dmat2 = side_by_side(dmat)
    rd2 = side_by_side(np.broadcast_to(read_dec.T[:, :, None], (RET_HEADS, chunk, RET_DV)))
    updt = np.repeat(upd_dec.T, RET_DK, axis=0).reshape(pairs, 2 * RET_DK, chunk)
    cd = np.broadcast_to(chunk_dec[:, None, None], (RET_HEADS, 1, RET_DV))
    cd2 = side_by_side(cd)
    f = lambda a: np.ascontiguousarray(a, dtype=np.float32)
    return f(dmat2), f(rd2), f(updt), f(cd2), f(cd)


def _const_spec(shape, grid_rank, single_buffer=True):
    zeros = (0,) * len(shape)
    idx = (lambda b, c: zeros) if grid_rank == 2 else (lambda j: zeros)
    if single_buffer:
        return pl.BlockSpec(shape, idx, pipeline_mode=pl.Buffered(1))
    return pl.BlockSpec(shape, idx)


def _layer_spec(shape, layer, single_buffer=True):
    zeros = (0,) * len(shape)
    mode = dict(pipeline_mode=pl.Buffered(1)) if single_buffer else {}
    return pl.BlockSpec((None,) + tuple(shape), lambda j: (layer,) + zeros, **mode)


def _params(n_axes):
    return pltpu.CompilerParams(dimension_semantics=("arbitrary",) * n_axes,
                                vmem_limit_bytes=V7X_VMEM_LIMIT_BYTES)


def _stacked_call(kern, *, name, grid, in_specs, args, out_specs, out_shape, layer, stacked, prev, scratch_shapes):
    def spec(n_layers, block, index_fn):
        return pl.BlockSpec((n_layers,) + tuple(block), lambda *g: (0,) + tuple(index_fn(*g)))

    in_specs, out_specs, out_shape, args = list(in_specs), list(out_specs), list(out_shape), list(args)
    for dims, block, index_fn in stacked:
        out_specs.append(spec(layer + 1, block, index_fn))
        out_shape.append(jax.ShapeDtypeStruct((layer + 1,) + tuple(dims), F32))
        if layer > 0:
            in_specs.append(spec(layer, block, index_fn))
    if layer > 0:
        args += list(prev)
    return pl.pallas_call(
        kern, grid=grid, in_specs=in_specs, out_specs=out_specs, out_shape=out_shape,
        scratch_shapes=scratch_shapes, compiler_params=_params(len(grid)), name=name)(*args)


def _ada_kernel(c_ref, w_ref, b_ref, o_ref):
    s = _silu(c_ref[...]).astype(BF16)
    o_ref[...] = jnp.dot(s, w_ref[...].astype(BF16), preferred_element_type=F32) + b_ref[...]


def _ada_modulation(c_all, ada_w, ada_b):
    rows = c_all.shape[0]
    ncol = 6 * D_MODEL
    blk = 2 * D_MODEL
    return pl.pallas_call(
        _ada_kernel,
        grid=(DEPTH, ncol // blk),
        in_specs=[
            pl.BlockSpec((rows, D_MODEL), lambda l, j: (0, 0)),
            pl.BlockSpec((None, D_MODEL, blk), lambda l, j: (l, 0, j)),
            pl.BlockSpec((None, 1, blk), lambda l, j: (l, 0, j)),
        ],
        out_specs=pl.BlockSpec((None, rows, blk), lambda l, j: (l, 0, j)),
        out_shape=jax.ShapeDtypeStruct((DEPTH, rows, ncol), F32),
        compiler_params=_params(2),
        name="ada_mod",
    )(c_all, ada_w, ada_b.reshape(DEPTH, 1, ncol))


def _prompt_layer_kernel(*refs, tile, n_tiles, tiles_per_seq, layer):
    (xc_ref, xp_ref, modc_ref, modp_ref, gpm_ref, gqm_ref, gpf_ref, gqf_ref,
     win_ref, wout_ref, w1_ref, w3_ref, w2_ref,
     cq_ref, sq_ref, ck_ref, sk_ref, dmat_ref, rd_ref, upd_ref, cd_ref,
     gn_ref, caw_ref, cab_ref, lng_ref, lnb_ref, ccw_ref) = refs[:27]
    n_prev = 3 if layer > 0 else 0
    prev_states = refs[27:27 + n_prev]
    (y_ref, sret_all, sca_all, scc_all,
     h_scr, proj_scr, mixin_scr, mix_scr, hf_scr, p_scr, ubuf, ua_scr, zbuf,
     sc_scr, qs_scr, sbd_scr) = refs[27 + n_prev:]
    sret_ref, sca_ref, scc_ref = sret_all.at[layer], sca_all.at[layer], scc_all.at[layer]
    step = pl.program_id(0)
    row_blocks = [pl.ds(r * ROW_BLK, ROW_BLK) for r in range(tile // ROW_BLK)]
    first_tap = HIST_A - (CONV_A_K - 1)
    first_tap_c = HIST_C - (CONV_C_K - 1)
    span = SUBLANES * CONV_STRIDE

    @pl.when((step < n_tiles) & (lax.rem(step, tiles_per_seq) == 0))
    def _():
        for prev_ref, all_ref in zip(prev_states, (sret_all, sca_all, scc_all)):
            all_ref[0:layer] = prev_ref[...]
        sret_ref[...] = jnp.zeros_like(sret_ref)
        sbd_scr[...] = jnp.zeros_like(sbd_scr)
        ubuf[:, 0:HIST_A, :] = jnp.zeros((2, HIST_A, LANES), F32)
        zbuf[0:HIST_C, :] = jnp.zeros((HIST_C, CONV_C_WIDTH), F32)

    def mix_prenorm():
        sh = modc_ref[0:1, :]
        pre_scale = gpm_ref[...] * (1.0 + modc_ref[1:2, :])
        for rows in row_blocks:
            h_scr[rows, :] = (_rms(xc_ref[rows, :]) * pre_scale + sh).astype(BF16)

    def in_proj(part, n):
        rows = pl.ds(part * (tile // IN_ROW_PARTS), tile // IN_ROW_PARTS)
        cols = pl.ds(n * IN_BLK, IN_BLK)
        proj_scr[rows, cols] = jnp.dot(h_scr[rows, :], win_ref[:, cols], preferred_element_type=F32)

    def conv_groups():
        taps = [[jnp.broadcast_to(caw_ref[j:j + 1, pl.ds(half * LANES, LANES)], (SUBLANES, LANES))
                 for j in range(CONV_A_K)] for half in range(2)]
        for r in range(tile // CONV_BLK):
            rows = pl.ds(r * CONV_BLK, CONV_BLK)
            dst = pl.ds(HIST_A + r * CONV_BLK, CONV_BLK)
            for half in range(2):
                lo = half * LANES
                u = (proj_scr[rows, pl.ds(OFF_AV + lo, LANES)]
                     * _sigmoid(proj_scr[rows, pl.ds(OFF_AG + lo, LANES)]))
                ubuf[half, dst, :] = u
        for half in range(2):
            for m in range(tile // span):
                base = m * span
                acc = [None] * CONV_STRIDE
                for k in range(CONV_STRIDE + CONV_A_K - 1):
                    win = ubuf[half, pl.ds(base + first_tap + k, SUBLANES, stride=CONV_STRIDE), :]
                    for t in range(CONV_STRIDE):
                        j = k - t
                        if 0 <= j < CONV_A_K:
                            term = taps[half][j] * win
                            acc[t] = term if acc[t] is None else acc[t] + term
                for t in range(CONV_STRIDE):
                    ua_scr[half, pl.ds(base + t, SUBLANES, stride=CONV_STRIDE), :] = acc[t]
        for r in range(tile // CONV_BLK):
            rows = pl.ds(r * CONV_BLK, CONV_BLK)
            ua = jnp.concatenate([ua_scr[0, rows, :], ua_scr[1, rows, :]], axis=-1) + cab_ref[...]
            o_a = _silu(_layernorm(ua, lng_ref[...], lnb_ref[...]))
            mixin_scr[rows, MIX_A:MIX_A + CONV_A_WIDTH] = o_a.astype(BF16)
        for r in range(tile // CONV_BLK):
            rows = pl.ds(r * CONV_BLK, CONV_BLK)
            z = proj_scr[rows, OFF_CC:OFF_CC + CONV_C_WIDTH] * proj_scr[rows, OFF_CX:OFF_CX + CONV_C_WIDTH]
            zbuf[pl.ds(HIST_C + r * CONV_BLK, CONV_BLK), :] = z
        for r in range(tile // CONV_BLK):
            rows = pl.ds(r * CONV_BLK, CONV_BLK)
            zc = ccw_ref[0:1, :] * zbuf[pl.ds(r * CONV_BLK + first_tap_c, CONV_BLK), :]
            for j in range(1, CONV_C_K):
                zc = zc + ccw_ref[j:j + 1, :] * zbuf[pl.ds(r * CONV_BLK + first_tap_c + j, CONV_BLK), :]
            o_c = proj_scr[rows, OFF_CB:OFF_CB + CONV_C_WIDTH] * zc
            mixin_scr[rows, MIX_C:MIX_C + CONV_C_WIDTH] = o_c.astype(BF16)
        hist_a = pl.ds(tile + first_tap, CONV_A_K - 1)
        sca_ref[...] = jnp.concatenate([ubuf[0, hist_a, :], ubuf[1, hist_a, :]], axis=-1)
        scc_ref[...] = zbuf[pl.ds(tile + first_tap_c, CONV_C_K - 1), :]
        ubuf[:, 0:HIST_A, :] = ubuf[:, pl.ds(tile, HIST_A), :]
        zbuf[0:HIST_C, :] = zbuf[pl.ds(tile, HIST_C), :]

    def retention_scores(i, pair):
        first_half, _ = _first_half_mask(RET_CHUNK)
        rows = pl.ds(i * RET_CHUNK, RET_CHUNK)
        cq, sq, ck, sk = cq_ref[rows, :], sq_ref[rows, :], ck_ref[rows, :], sk_ref[rows, :]
        qr = _rope_half(proj_scr[rows, pl.ds(OFF_Q + pair * LANES, LANES)], cq, sq, first_half)
        kr = _rope_half(proj_scr[rows, pl.ds(OFF_K + pair * LANES, LANES)], ck, sk, first_half)
        qr_b = qr.astype(BF16)
        k_t = kr.T
        head0_rows = lax.broadcasted_iota(jnp.int32, (2 * RET_DK, RET_CHUNK), 0) < RET_DK
        k_bd = jnp.concatenate([jnp.where(head0_rows, k_t, 0.0), jnp.where(head0_rows, 0.0, k_t)], axis=1)
        scores = jnp.dot(qr_b, k_bd.astype(BF16), preferred_element_type=F32) * dmat_ref[pair]
        sc_scr[...] = scores.astype(BF16)
        s_bd = sbd_scr[pair]
        qs_scr[...] = jnp.dot(qr_b, s_bd.astype(BF16), preferred_element_type=F32) * rd_ref[pair]
        v_pair = proj_scr[rows, pl.ds(OFF_V + 2 * pair * RET_DV, 2 * RET_DV)].astype(BF16)
        kv = jnp.dot((k_t * upd_ref[pair]).astype(BF16), v_pair, preferred_element_type=F32)
        r0 = lax.broadcasted_iota(jnp.int32, (2 * RET_DK, 2 * RET_DV), 0) < RET_DK
        c0 = lax.broadcasted_iota(jnp.int32, (2 * RET_DK, 2 * RET_DV), 1) < RET_DV
        s_new = s_bd * cd_ref[pair] + jnp.where(r0 == c0, kv, 0.0)
        sbd_scr[pair] = s_new
        sret_ref[2 * pair] = s_new[0:RET_DK, 0:RET_DV]
        sret_ref[2 * pair + 1] = s_new[RET_DK:2 * RET_DK, RET_DV:2 * RET_DV]

    def retention_values(i, pair):
        rows = pl.ds(i * RET_CHUNK, RET_CHUNK)
        for hl in range(2):
            h = 2 * pair + hl
            vh = proj_scr[rows, pl.ds(OFF_V + h * RET_DV, RET_DV)].astype(BF16)
            o = (jnp.dot(sc_scr[:, pl.ds(hl * RET_CHUNK, RET_CHUNK)], vh, preferred_element_type=F32)
                 + qs_scr[:, pl.ds(hl * RET_DV, RET_DV)])
            gate = proj_scr[rows, pl.ds(OFF_G + h * RET_DV, RET_DV)]
            o = _layernorm(o, gn_ref[:, pl.ds(h * RET_DV, RET_DV)]) * _silu(gate)
            mixin_scr[rows, pl.ds(h * RET_DV, RET_DV)] = o.astype(BF16)

    def out_proj():
        mix_scr[...] = jnp.dot(mixin_scr[...], wout_ref[...], preferred_element_type=F32)

    in_tiles = [functools.partial(in_proj, part, n)
                for part in range(IN_ROW_PARTS) for n in range(IN_WIDTH // IN_BLK)]
    ret_items = [functools.partial(phase, i, pair)
                 for i in range(tile // RET_CHUNK) for pair in range(RET_HEADS // 2)
                 for phase in (retention_scores, retention_values)]

    def ffn_norms():
        post_scale = gqm_ref[...] * modp_ref[2:3, :]
        sh = modp_ref[3:4, :]
        pre_scale = gpf_ref[...] * (1.0 + modp_ref[4:5, :])
        for rows in row_blocks:
            x1 = xp_ref[rows, :] + _rms(mix_scr[rows, :]) * post_scale
            y_ref[rows, :] = x1
            hf_scr[rows, :] = (_rms(x1) * pre_scale + sh).astype(BF16)

    def hidden(j):
        p_scr[:, pl.ds(j * FF_BLK, FF_BLK)] = _swiglu_block(hf_scr[...], w1_ref, w3_ref, j * FF_BLK)

    def down(n):
        cols = pl.ds(n * FF_BLK, FF_BLK)
        proj_scr[:, cols] = jnp.dot(p_scr[...], w2_ref[:, cols], preferred_element_type=F32)

    def ffn_finish():
        post_scale = gqf_ref[...] * modp_ref[5:6, :]
        for rows in row_blocks:
            y_ref[rows, :] = y_ref[rows, :] + _rms(proj_scr[rows, 0:D_MODEL]) * post_scale

    hidden_blocks = [functools.partial(hidden, j) for j in range(D_FF // FF_BLK)]
    down_tiles = [functools.partial(down, n) for n in range(D_MODEL // FF_BLK)]

    @pl.when(step == 0)
    def _():
        mix_prenorm()
        _emit(in_tiles)
        _emit(ret_items)
        conv_groups()
        out_proj()

    @pl.when((step > 0) & (step < n_tiles))
    def _():
        mix_prenorm()
        ffn_norms()
        _emit(in_tiles)
        _emit_interleaved(hidden_blocks, ret_items)
        conv_groups()
        _emit(down_tiles)
        out_proj()
        ffn_finish()

    @pl.when(step == n_tiles)
    def _():
        ffn_norms()
        _emit(hidden_blocks)
        _emit(down_tiles)
        ffn_finish()


def _prompt_layer(layer, prev, x, mod_p, g_pm, g_qm, g_pf, g_qf, w_in, w_out, w1, w3, w2, tables,
                  gn, caw, cab, lng, lnb, ccw):
    bsz, seq, _ = x.shape
    tile = TOK_TILE
    tps = seq // tile
    n_tiles = bsz * tps
    cq, sq, ck, sk, dmat, rd, upd, cd, _ = tables
    cur = lambda i: jnp.minimum(i, n_tiles - 1)
    prv = lambda i: jnp.maximum(i - 1, 0)
    tok_cur = pl.BlockSpec((None, tile, D_MODEL), lambda i: (cur(i) // tps, cur(i) % tps, 0))
    tok_prev = pl.BlockSpec((None, tile, D_MODEL), lambda i: (prv(i) // tps, prv(i) % tps, 0))
    mod_cur = pl.BlockSpec((None, 6, D_MODEL), lambda i: (cur(i) // tps, 0, 0))
    mod_prev = pl.BlockSpec((None, 6, D_MODEL), lambda i: (prv(i) // tps, 0, 0))
    rope = pl.BlockSpec((tile, LANES), lambda i: (cur(i) % tps, 0))
    cs = functools.partial(_const_spec, grid_rank=1)
    ls = functools.partial(_layer_spec, layer=layer)
    args = (x, x, mod_p, mod_p, g_pm, g_qm, g_pf, g_qf, w_in, w_out, w1, w3, w2,
            cq, sq, ck, sk, dmat, rd, upd, cd, gn, caw, cab, lng, lnb, ccw)
    seq_block = lambda i: (cur(i) // tps,)
    return _stacked_call(
        functools.partial(_prompt_layer_kernel, tile=tile, n_tiles=n_tiles, tiles_per_seq=tps, layer=layer),
        name="prompt_layer", grid=(n_tiles + 1,), args=args, layer=layer, prev=prev,
        stacked=[
            ((bsz, RET_HEADS, RET_DK, RET_DV), (None, RET_HEADS, RET_DK, RET_DV),
             lambda i: seq_block(i) + (0, 0, 0)),
            ((bsz, CONV_A_K - 1, CONV_A_WIDTH), (None, CONV_A_K - 1, CONV_A_WIDTH),
             lambda i: seq_block(i) + (0, 0)),
            ((bsz, CONV_C_K - 1, CONV_C_WIDTH), (None, CONV_C_K - 1, CONV_C_WIDTH),
             lambda i: seq_block(i) + (0, 0)),
        ],
        in_specs=[
            tok_cur, tok_prev, mod_cur, mod_prev,
            ls((1, D_MODEL)), ls((1, D_MODEL)), ls((1, D_MODEL)), ls((1, D_MODEL)),
            cs((D_MODEL, IN_WIDTH)), cs((D_MODEL, D_MODEL)),
            cs((D_MODEL, D_FF)), cs((D_MODEL, D_FF)), cs((D_FF, D_MODEL)),
            rope, rope, rope, rope,
            cs(dmat.shape), cs(rd.shape), cs(upd.shape), cs(cd.shape),
            ls((1, RET_WIDTH)), ls((CONV_A_K, CONV_A_WIDTH)), ls((1, CONV_A_WIDTH)),
            ls((1, CONV_A_WIDTH)), ls((1, CONV_A_WIDTH)), ls((CONV_C_K, CONV_C_WIDTH)),
        ],
        out_specs=[tok_prev],
        out_shape=[jax.ShapeDtypeStruct(x.shape, F32)],
        scratch_shapes=[
            pltpu.VMEM((tile, D_MODEL), BF16),
            pltpu.VMEM((tile, IN_WIDTH), F32),
            pltpu.VMEM((tile, D_MODEL), BF16),
            pltpu.VMEM((tile, D_MODEL), F32),
            pltpu.VMEM((tile, D_MODEL), BF16),
            pltpu.VMEM((tile, D_FF), BF16),
            pltpu.VMEM((2, tile + HIST_A, LANES), F32),
            pltpu.VMEM((2, tile, LANES), F32),
            pltpu.VMEM((tile + HIST_C, CONV_C_WIDTH), F32),
            pltpu.VMEM((RET_CHUNK, 2 * RET_CHUNK), BF16),
            pltpu.VMEM((RET_CHUNK, 2 * RET_DV), F32),
            pltpu.VMEM((RET_HEADS // 2, 2 * RET_DK, 2 * RET_DV), F32),
        ])


def _sample_pre_kernel(x_ref, mod_ref, gpre_ref, win_ref, rope_ref, proj_ref, qt_ref, kt_ref, winb_ref, h_scr):
    j = pl.program_id(0)
    n = x_ref.shape[0]

    @pl.when(j == 0)
    def _():
        sh = mod_ref[:, 0:D_MODEL]
        sc1 = 1.0 + mod_ref[:, D_MODEL:2 * D_MODEL]
        h_scr[...] = (_rms(x_ref[...]) * (gpre_ref[...] * sc1) + sh).astype(BF16)

    w_blk = win_ref[...].astype(BF16)
    winb_ref[...] = w_blk
    proj_ref[...] = jnp.dot(h_scr[...], w_blk, preferred_element_type=F32)

    def rope_transposed(off, cos, sin, dst_ref):
        first_half, _ = _first_half_mask(n)
        for pair in range(RET_HEADS // 2):
            src = proj_ref[:, pl.ds(off + pair * LANES, LANES)]
            dst_ref[pl.ds(pair * LANES, LANES), :] = _rope_half(src, cos, sin, first_half).T

    @pl.when(j == 0)
    def _():
        rope_transposed(OFF_Q, rope_ref[0:1, :], rope_ref[1:2, :], qt_ref)
        rope_transposed(OFF_K, rope_ref[2:3, :], rope_ref[3:4, :], kt_ref)


def _sample_pre(layer, xs, mod, gpre, w_in, rope_s):
    n = xs.shape[0]
    cs = functools.partial(_const_spec, grid_rank=1, single_buffer=False)
    ls = functools.partial(_layer_spec, layer=layer, single_buffer=False)
    return pl.pallas_call(
        _sample_pre_kernel,
        grid=(IN_WIDTH // PRE_BLK,),
        in_specs=[cs((n, D_MODEL)), ls((n, 6 * D_MODEL)), ls((1, D_MODEL)),
                  pl.BlockSpec((None, D_MODEL, PRE_BLK), lambda j: (layer, 0, j)), cs(rope_s.shape)],
        out_specs=[pl.BlockSpec((n, PRE_BLK), lambda j: (0, j)), cs((QK_WIDTH, n)), cs((QK_WIDTH, n)),
                   pl.BlockSpec((D_MODEL, PRE_BLK), lambda j: (0, j))],
        out_shape=[jax.ShapeDtypeStruct((n, IN_WIDTH), F32),
                   jax.ShapeDtypeStruct((QK_WIDTH, n), F32),
                   jax.ShapeDtypeStruct((QK_WIDTH, n), F32),
                   jax.ShapeDtypeStruct((D_MODEL, IN_WIDTH), BF16)],
        scratch_shapes=[pltpu.VMEM((n, D_MODEL), BF16)],
        compiler_params=_params(1),
        name="sample_pre",
    )(xs, mod, gpre, w_in, rope_s)


def _sample_state_kernel(*refs, layer):
    (proj_ref, qt_ref, kt_ref, sin_ref, bufa_ref, bufc_ref, cd_ref,
     gn_ref, caw_ref, cab_ref, lng_ref, lnb_ref, ccw_ref) = refs[:13]
    n_prev = 3 if layer > 0 else 0
    prev_states = refs[13:13 + n_prev]
    mix_ref, sout_all, outa_all, outc_all, o_scr = refs[13 + n_prev:]
    for prev_ref, all_ref in zip(prev_states, (sout_all, outa_all, outc_all)):
        all_ref[0:layer] = prev_ref[...]
    sout_ref, outa_ref, outc_ref = sout_all.at[layer], outa_all.at[layer], outc_all.at[layer]
    blk = pl.program_id(0)
    n = qt_ref.shape[1]
    lane = lax.broadcasted_iota(jnp.int32, (QK_WIDTH, n), 1)

    for bl in range(SEQ_BLK):
        onehot = lane == (blk * SEQ_BLK + bl)
        qcol = jnp.sum(jnp.where(onehot, qt_ref[...], 0.0), axis=1, keepdims=True)
        kcol = jnp.sum(jnp.where(onehot, kt_ref[...], 0.0), axis=1, keepdims=True)
        row = pl.ds(bl, 1)
        for h in range(RET_HEADS):
            vrow = proj_ref[row, pl.ds(OFF_V + h * RET_DV, RET_DV)]
            s_new = (sin_ref[bl, h] * cd_ref[h]
                     + kcol[h * RET_DK:(h + 1) * RET_DK, :] * vrow)
            sout_ref[bl, h] = s_new
            o_scr[row, pl.ds(h * RET_DV, RET_DV)] = jnp.sum(
                qcol[h * RET_DK:(h + 1) * RET_DK, :] * s_new, axis=0, keepdims=True)
    for h in range(RET_HEADS):
        cols = pl.ds(h * RET_DV, RET_DV)
        o = _layernorm(o_scr[:, cols], gn_ref[:, cols]) * _silu(proj_ref[:, pl.ds(OFF_G + h * RET_DV, RET_DV)])
        mix_ref[:, cols] = o

    u = proj_ref[:, OFF_AV:OFF_AV + CONV_A_WIDTH] * _sigmoid(proj_ref[:, OFF_AG:OFF_AG + CONV_A_WIDTH])
    ua = caw_ref[CONV_A_K - 1:CONV_A_K, :] * u
    for j in range(CONV_A_K - 1):
        ua = ua + caw_ref[j:j + 1, :] * bufa_ref[j]
    outa_ref[0:CONV_A_K - 2] = bufa_ref[1:CONV_A_K - 1]
    outa_ref[CONV_A_K - 2] = u
    mix_ref[:, MIX_A:MIX_A + CONV_A_WIDTH] = _silu(_layernorm(ua + cab_ref[...], lng_ref[...], lnb_ref[...]))

    z = proj_ref[:, OFF_CC:OFF_CC + CONV_C_WIDTH] * proj_ref[:, OFF_CX:OFF_CX + CONV_C_WIDTH]
    zc = ccw_ref[0:1, :] * bufc_ref[0] + ccw_ref[1:2, :] * bufc_ref[1] + ccw_ref[2:3, :] * z
    outc_ref[0] = bufc_ref[1]
    outc_ref[1] = z
    mix_ref[:, MIX_C:MIX_C + CONV_C_WIDTH] = proj_ref[:, OFF_CB:OFF_CB + CONV_C_WIDTH] * zc


def _sample_state(layer, prev, proj, qt, kt, s_ret, buf_a_t, buf_c_t, cd, gn, caw, cab, lng, lnb, ccw):
    n = proj.shape[0]
    cs = functools.partial(_const_spec, grid_rank=1, single_buffer=False)
    ls = functools.partial(_layer_spec, layer=layer, single_buffer=False)
    s_spec = pl.BlockSpec((None, SEQ_BLK, RET_HEADS, RET_DK, RET_DV), lambda j: (layer, j, 0, 0, 0))
    a_spec = pl.BlockSpec((None, CONV_A_K - 1, SEQ_BLK, CONV_A_WIDTH), lambda j: (layer, 0, j, 0))
    c_spec = pl.BlockSpec((None, CONV_C_K - 1, SEQ_BLK, CONV_C_WIDTH), lambda j: (layer, 0, j, 0))
    args = (proj, qt, kt, s_ret, buf_a_t, buf_c_t, cd, gn, caw, cab, lng, lnb, ccw)
    return _stacked_call(
        functools.partial(_sample_state_kernel, layer=layer),
        name="sample_state", grid=(n // SEQ_BLK,), args=args, layer=layer, prev=prev,
        stacked=[
            ((n, RET_HEADS, RET_DK, RET_DV), (SEQ_BLK, RET_HEADS, RET_DK, RET_DV), lambda j: (j, 0, 0, 0)),
            ((CONV_A_K - 1, n, CONV_A_WIDTH), (CONV_A_K - 1, SEQ_BLK, CONV_A_WIDTH), lambda j: (0, j, 0)),
            ((CONV_C_K - 1, n, CONV_C_WIDTH), (CONV_C_K - 1, SEQ_BLK, CONV_C_WIDTH), lambda j: (0, j, 0)),
        ],
        in_specs=[
            pl.BlockSpec((SEQ_BLK, IN_WIDTH), lambda j: (j, 0)),
            cs((QK_WIDTH, n)), cs((QK_WIDTH, n)),
            s_spec, a_spec, c_spec,
            cs(cd.shape), ls((1, RET_WIDTH)), ls((CONV_A_K, CONV_A_WIDTH)), ls((1, CONV_A_WIDTH)),
            ls((1, CONV_A_WIDTH)), ls((1, CONV_A_WIDTH)), ls((CONV_C_K, CONV_C_WIDTH)),
        ],
        out_specs=[pl.BlockSpec((SEQ_BLK, D_MODEL), lambda j: (j, 0))],
        out_shape=[jax.ShapeDtypeStruct((n, D_MODEL), F32)],
        scratch_shapes=[pltpu.VMEM((SEQ_BLK, RET_WIDTH), F32)])


def _sample_post_kernel(x_ref, mod_ref, mix_ref, wout_ref, gpm_ref, gpf_ref, gqf_ref, w1_ref, w3_ref, w2_ref,
                        y_ref, woutb_ref, w1b_ref, w3b_ref, w2b_ref, x1_scr, h_scr, f_scr, *, n_steps):
    j = pl.program_id(0)

    @pl.when(j == 0)
    def _():
        gt_m = mod_ref[:, 2 * D_MODEL:3 * D_MODEL]
        wout_b = wout_ref[...].astype(BF16)
        woutb_ref[...] = wout_b
        mix = jnp.dot(mix_ref[...].astype(BF16), wout_b, preferred_element_type=F32)
        x1 = x_ref[...] + _rms(mix) * (gpm_ref[...] * gt_m)
        x1_scr[...] = x1
        sh = mod_ref[:, 3 * D_MODEL:4 * D_MODEL]
        sc1 = 1.0 + mod_ref[:, 4 * D_MODEL:5 * D_MODEL]
        h_scr[...] = (_rms(x1) * (gpf_ref[...] * sc1) + sh).astype(BF16)
        f_scr[...] = jnp.zeros_like(f_scr)

    w1_b, w3_b, w2_b = w1_ref[...].astype(BF16), w3_ref[...].astype(BF16), w2_ref[...].astype(BF16)
    w1b_ref[...] = w1_b
    w3b_ref[...] = w3_b
    w2b_ref[...] = w2_b
    h = h_scr[...]
    a = jnp.dot(h, w1_b, preferred_element_type=F32)
    b = jnp.dot(h, w3_b, preferred_element_type=F32)
    p = (_silu(a) * b).astype(BF16)
    f_scr[...] += jnp.dot(p, w2_b, preferred_element_type=F32)

    @pl.when(j == n_steps - 1)
    def _():
        gt_f = mod_ref[:, 5 * D_MODEL:6 * D_MODEL]
        y_ref[...] = x1_scr[...] + _rms(f_scr[...]) * (gqf_ref[...] * gt_f)


def _sample_post(layer, xs, mod, mix, w_out, gpost_m, gpre_f, gpost_f, w1, w3, w2):
    n = xs.shape[0]
    n_steps = D_FF // FF_BLK
    cs = functools.partial(_const_spec, grid_rank=1, single_buffer=False)
    ls = functools.partial(_layer_spec, layer=layer, single_buffer=False)
    return pl.pallas_call(
        functools.partial(_sample_post_kernel, n_steps=n_steps),
        grid=(n_steps,),
        in_specs=[
            cs((n, D_MODEL)), ls((n, 6 * D_MODEL)), cs((n, D_MODEL)), ls((D_MODEL, D_MODEL)),
            ls((1, D_MODEL)), ls((1, D_MODEL)), ls((1, D_MODEL)),
            pl.BlockSpec((None, D_MODEL, FF_BLK), lambda j: (layer, 0, j)),
            pl.BlockSpec((None, D_MODEL, FF_BLK), lambda j: (layer, 0, j)),
            pl.BlockSpec((None, FF_BLK, D_MODEL), lambda j: (layer, j, 0)),
        ],
        out_specs=[
            cs((n, D_MODEL)), cs((D_MODEL, D_MODEL)),
            pl.BlockSpec((D_MODEL, FF_BLK), lambda j: (0, j)),
            pl.BlockSpec((D_MODEL, FF_BLK), lambda j: (0, j)),
            pl.BlockSpec((FF_BLK, D_MODEL), lambda j: (j, 0)),
        ],
        out_shape=[
            jax.ShapeDtypeStruct((n, D_MODEL), F32),
            jax.ShapeDtypeStruct((D_MODEL, D_MODEL), BF16),
            jax.ShapeDtypeStruct((D_MODEL, D_FF), BF16),
            jax.ShapeDtypeStruct((D_MODEL, D_FF), BF16),
            jax.ShapeDtypeStruct((D_FF, D_MODEL), BF16),
        ],
        scratch_shapes=[
            pltpu.VMEM((n, D_MODEL), F32),
            pltpu.VMEM((n, D_MODEL), BF16),
            pltpu.VMEM((n, D_MODEL), F32),
        ],
        compiler_params=_params(1),
        name="sample_post",
    )(xs, mod, mix, w_out, gpost_m, gpre_f, gpost_f, w1, w3, w2)


def kernel(x_prompt, x_sample, c_prompt, c_sample, state_ret, state_conv_a, state_conv_c, ada_w, ada_b, norm_pre_mix, norm_post_mix, norm_pre_ffn, norm_post_ffn, w_in, w_out, ret_gn_g, conv_a_w, conv_a_b, conv_a_ln_g, conv_a_ln_b, conv_c_w, ffn_w1, ffn_w3, ffn_w2):
    bp, lp, _ = x_prompt.shape
    ns = x_sample.shape[0]
    assert x_sample.shape[1] == 1 and ns % SEQ_BLK == 0
    assert OFF_K + QK_WIDTH <= PRE_BLK and IN_WIDTH % PRE_BLK == 0
    assert TOK_TILE % (IN_ROW_PARTS * 2 * LANES) == 0
    assert lp % TOK_TILE == 0 and TOK_TILE % RET_CHUNK == 0
    assert TOK_TILE % CONV_BLK == 0 and CONV_BLK % (SUBLANES * CONV_STRIDE) == 0

    k_scale = RET_DK ** -0.5
    cq, sq = _rope_tables(np.arange(lp), 1.0)
    ck, sk = _rope_tables(np.arange(lp), k_scale)
    tables_p = (cq, sq, ck, sk) + _decay_tables(RET_CHUNK)
    cqs, sqs = _rope_tables([PAST_LEN], 1.0)
    cks, sks = _rope_tables([PAST_LEN], k_scale)
    rope_s = np.concatenate([cqs, sqs, cks, sks, np.zeros((4, LANES), np.float32)], axis=0)
    cd_s = _decay_tables(1)[4]

    mod = _ada_modulation(jnp.concatenate([c_sample, c_prompt], axis=0), ada_w, ada_b)

    rows = lambda a: a.reshape(DEPTH, 1, -1)
    gn, cab, lng, lnb = rows(ret_gn_g), rows(conv_a_b), rows(conv_a_ln_g), rows(conv_a_ln_b)
    g_pm, g_qm, g_pf, g_qf = rows(norm_pre_mix), rows(norm_post_mix), rows(norm_pre_ffn), rows(norm_post_ffn)
    conv_a_t = jnp.transpose(state_conv_a, (0, 2, 1, 3))
    conv_c_t = jnp.transpose(state_conv_c, (0, 2, 1, 3))

    yp = x_prompt
    ys = x_sample.reshape(ns, D_MODEL)
    st_p = st_s = ()
    for l in range(DEPTH):
        proj, qt, kt, w_in_b = _sample_pre(l, ys, mod, g_pm, w_in, rope_s)
        mix, *st_s = _sample_state(l, st_s, proj, qt, kt, state_ret, conv_a_t, conv_c_t, cd_s,
                                   gn, conv_a_w, cab, lng, lnb, conv_c_w)
        ys, w_out_b, w1_b, w3_b, w2_b = _sample_post(l, ys, mod, mix, w_out, g_qm, g_pf, g_qf,
                                                    ffn_w1, ffn_w3, ffn_w2)
        mod_p = mod[l, ns:].reshape(bp, 6, D_MODEL)
        yp, *st_p = _prompt_layer(l, st_p, yp, mod_p, g_pm, g_qm, g_pf, g_qf, w_in_b, w_out_b, w1_b, w3_b, w2_b,
                                  tables_p, gn, conv_a_w, cab, lng, lnb, conv_c_w)

    sret_s, sca_s, scc_s = st_s
    sca_s = jnp.transpose(sca_s, (0, 2, 1, 3))
    scc_s = jnp.transpose(scc_s, (0, 2, 1, 3))
    return (yp, ys.reshape(ns, 1, D_MODEL)) + tuple(st_p) + (sret_s, sca_s, scc_s)
```

```python
import functools

import numpy as np
import jax
import jax.numpy as jnp
from jax import lax
from jax.experimental import pallas as pl
from jax.experimental.pallas import tpu as pltpu

D_MODEL = 1024
DEPTH = 2
PAST_LEN = 16384
RET_HEADS = 4
RET_WIDTH = D_MODEL // 2
RET_DV = RET_WIDTH // RET_HEADS
RET_DK = RET_DV // 2
QK_WIDTH = RET_HEADS * RET_DK
CONV_A_WIDTH = D_MODEL // 4
CONV_A_K = 31
CONV_C_WIDTH = D_MODEL - RET_WIDTH - CONV_A_WIDTH
CONV_C_K = 3
IN_WIDTH = 2 * QK_WIDTH + 2 * RET_WIDTH + 2 * CONV_A_WIDTH + 3 * CONV_C_WIDTH
D_FF = ((8 * D_MODEL // 3 + 255) // 256) * 256
RET_CHUNK = 128
ROPE_BASE = 10000.0
EPS = 1e-6

OFF_Q = 0
OFF_K = OFF_Q + QK_WIDTH
OFF_V = OFF_K + QK_WIDTH
OFF_G = OFF_V + RET_WIDTH
OFF_AV = OFF_G + RET_WIDTH
OFF_AG = OFF_AV + CONV_A_WIDTH
OFF_CB = OFF_AG + CONV_A_WIDTH
OFF_CC = OFF_CB + CONV_C_WIDTH
OFF_CX = OFF_CC + CONV_C_WIDTH
MIX_A = RET_WIDTH
MIX_C = RET_WIDTH + CONV_A_WIDTH

LANES = 128
SUBLANES = 8
V7X_VMEM_LIMIT_BYTES = 60 * 1024 * 1024

TOK_TILE = 512
ROW_BLK = 32
CONV_BLK = 64
CONV_STRIDE = 4
HIST_A = 32
HIST_C = 8
SEQ_BLK = 16
FF_BLK = 256
IN_BLK = 256
PRE_BLK = 1408

F32 = jnp.float32
BF16 = jnp.bfloat16


def _sigmoid(x):
    return jax.nn.sigmoid(x)


def _silu(x):
    return x * _sigmoid(x)


def _rms(x):
    return x * lax.rsqrt(jnp.mean(x * x, axis=-1, keepdims=True) + EPS)


def _layernorm(x, g, b=None):
    mu = jnp.mean(x, axis=-1, keepdims=True)
    d = x - mu
    var = jnp.mean(d * d, axis=-1, keepdims=True)
    y = d * lax.rsqrt(var + EPS) * g
    return y if b is None else y + b


def _rope_half(x, cos, sin, first_half):
    partner = jnp.where(first_half, pltpu.roll(x, 96, 1), pltpu.roll(x, 32, 1))
    return x * cos + partner * sin


def _first_half_mask(rows):
    lane = lax.broadcasted_iota(jnp.int32, (rows, LANES), 1)
    return (lane & (RET_DK - 1)) < (RET_DK // 2), lane < RET_DK


def _swiglu_block(h, w1_ref, w3_ref, col0):
    cols = pl.ds(col0, FF_BLK)
    a = jnp.dot(h, w1_ref[:, cols], preferred_element_type=F32)
    b = jnp.dot(h, w3_ref[:, cols], preferred_element_type=F32)
    return (_silu(a) * b).astype(BF16)


def _emit(items):
    for item in items:
        item()


def _emit_interleaved(primary, secondary):
    n, m = len(primary), len(secondary)
    done = 0
    for i, item in enumerate(primary):
        item()
        upto = ((i + 1) * m) // n
        _emit(secondary[done:upto])
        done = upto


def _rope_tables(pos, k_scale):
    half = RET_DK // 2
    inv = ROPE_BASE ** (-np.arange(half, dtype=np.float64) / half)
    ang = np.asarray(pos, np.float64)[:, None] * inv[None, :]
    cos = np.tile(np.cos(ang), (1, 4))
    sin = np.tile(np.concatenate([-np.sin(ang), np.sin(ang)], axis=1), (1, 2))
    return (cos * k_scale).astype(np.float32), (sin * k_scale).astype(np.float32)


def _decay_tables(chunk):
    log_g = np.log(1.0 - np.exp2(-5.0 - np.arange(RET_HEADS, dtype=np.float64)))
    idx = np.arange(chunk, dtype=np.float64)
    diff = idx[:, None] - idx[None, :]
    dmat = np.where(diff[None] >= 0, np.exp(np.maximum(diff, 0.0)[None] * log_g[:, None, None]), 0.0)
    read_dec = np.exp((idx + 1.0)[:, None] * log_g[None, :])
    upd_dec = np.exp((chunk - 1.0 - idx)[:, None] * log_g[None, :])
    chunk_dec = np.exp(chunk * log_g)
    pairs = RET_HEADS // 2
    side_by_side = lambda a: a.reshape(pairs, 2, *a.shape[1:]).transpose(0, 2, 1, 3).reshape(pairs, a.shape[1], -1)
    dmat2 = side_by_side(dmat)
    rd2 = side_by_side(np.broadcast_to(read_dec.T[:, :, None], (RET_HEADS, chunk, RET_DV)))
    updt = np.repeat(upd_dec.T, RET_DK, axis=0).reshape(pairs, 2 * RET_DK, chunk)
    cd = np.broadcast_to(chunk_dec[:, None, None], (RET_HEADS, 1, RET_DV))
    cd2 = side_by_side(cd)
    f = lambda a: np.ascontiguousarray(a, dtype=np.float32)
    return f(dmat2), f(rd2), f(updt), f(cd2), f(cd)


def _const_spec(shape, grid_rank, single_buffer=True):
    zeros = (0,) * len(shape)
    idx = (lambda b, c: zeros) if grid_rank == 2 else (lambda j: zeros)
    if single_buffer:
        return pl.BlockSpec(shape, idx, pipeline_mode=pl.Buffered(1))
    return pl.BlockSpec(shape, idx)


def _layer_spec(shape, layer, single_buffer=True):
    zeros = (0,) * len(shape)
    mode = dict(pipeline_mode=pl.Buffered(1)) if single_buffer else {}
    return pl.BlockSpec((None,) + tuple(shape), lambda j: (layer,) + zeros, **mode)


def _params(n_axes):
    return pltpu.CompilerParams(dimension_semantics=("arbitrary",) * n_axes,
                                vmem_limit_bytes=V7X_VMEM_LIMIT_BYTES)


def _stacked_call(kern, *, name, grid, in_specs, args, out_specs, out_shape, layer, stacked, prev, scratch_shapes):
    def spec(n_layers, block, index_fn):
        return pl.BlockSpec((n_layers,) + tuple(block), lambda *g: (0,) + tuple(index_fn(*g)))

    in_specs, out_specs, out_shape, args = list(in_specs), list(out_specs), list(out_shape), list(args)
    for dims, block, index_fn in stacked:
        out_specs.append(spec(layer + 1, block, index_fn))
        out_shape.append(jax.ShapeDtypeStruct((layer + 1,) + tuple(dims), F32))
        if layer > 0:
            in_specs.append(spec(layer, block, index_fn))
    if layer > 0:
        args += list(prev)
    return pl.pallas_call(
        kern, grid=grid, in_specs=in_specs, out_specs=out_specs, out_shape=out_shape,
        scratch_shapes=scratch_shapes, compiler_params=_params(len(grid)), name=name)(*args)


def _ada_kernel(cs_ref, cp_ref, w_ref, b_ref, os_ref, op_ref):
    layer = pl.program_id(0)
    bias = b_ref[0:1, :]
    for d in range(1, DEPTH):
        bias = jnp.where(layer == d, b_ref[d:d + 1, :], bias)
    w = w_ref[...].astype(BF16)
    os_ref[...] = jnp.dot(_silu(cs_ref[...]).astype(BF16), w, preferred_element_type=F32) + bias
    mod_p = jnp.dot(_silu(cp_ref[...]).astype(BF16), w, preferred_element_type=F32) + bias
    for b in range(cp_ref.shape[0]):
        op_ref[b] = mod_p[b:b + 1, :]


def _ada_modulation(c_sample, c_prompt, ada_w, ada_b):
    ns, bp = c_sample.shape[0], c_prompt.shape[0]
    ncol = 6 * D_MODEL
    blk = 2 * D_MODEL
    return pl.pallas_call(
        _ada_kernel,
        grid=(DEPTH, ncol // blk),
        in_specs=[
            pl.BlockSpec((ns, D_MODEL), lambda l, j: (0, 0)),
            pl.BlockSpec((bp, D_MODEL), lambda l, j: (0, 0)),
            pl.BlockSpec((None, D_MODEL, blk), lambda l, j: (l, 0, j)),
            pl.BlockSpec((DEPTH, blk), lambda l, j: (0, j)),
        ],
        out_specs=[pl.BlockSpec((None, ns, blk), lambda l, j: (l, 0, j)),
                   pl.BlockSpec((None, bp, 1, blk), lambda l, j: (l, 0, 0, j))],
        out_shape=[jax.ShapeDtypeStruct((DEPTH, ns, ncol), F32),
                   jax.ShapeDtypeStruct((DEPTH, bp, 1, ncol), F32)],
        compiler_params=_params(2),
        name="ada_mod",
    )(c_sample, c_prompt, ada_w, ada_b)


def _prompt_layer_kernel(*refs, tile, n_tiles, tiles_per_seq, layer):
    (xc_ref, xp_ref, modc_ref, modp_ref, gpm_ref, gqm_ref, gpf_ref, gqf_ref,
     win_ref, wout_ref, w1_ref, w3_ref, w2_ref,
     cq_ref, sq_ref, ck_ref, sk_ref, dmat_ref, rd_ref, upd_ref, cd_ref,
     gn_ref, caw_ref, cab_ref, lng_ref, lnb_ref, ccw_ref) = refs[:27]
    n_prev = 3 if layer > 0 else 0
    prev_states = refs[27:27 + n_prev]
    (y_ref, sret_all, sca_all, scc_all,
     h_scr, proj_scr, mixin_scr, mix_scr, hf_scr, p_scr, ubuf, ua_scr, zbuf,
     sc_scr, qs_scr, sbd_scr) = refs[27 + n_prev:]
    sret_ref, sca_ref, scc_ref = sret_all.at[layer], sca_all.at[layer], scc_all.at[layer]
    gpm_ref, gqm_ref, gpf_ref, gqf_ref, gn_ref, cab_ref, lng_ref, lnb_ref = (
        ref.at[pl.ds(layer, 1)] for ref in (gpm_ref, gqm_ref, gpf_ref, gqf_ref, gn_ref, cab_ref, lng_ref, lnb_ref))
    mod_chunk = lambda ref, k: ref[:, pl.ds(k * D_MODEL, D_MODEL)]
    step = pl.program_id(0)
    row_blocks = [pl.ds(r * ROW_BLK, ROW_BLK) for r in range(tile // ROW_BLK)]
    first_tap = HIST_A - (CONV_A_K - 1)
    first_tap_c = HIST_C - (CONV_C_K - 1)
    span = SUBLANES * CONV_STRIDE

    @pl.when((step < n_tiles) & (lax.rem(step, tiles_per_seq) == 0))
    def _():
        for prev_ref, all_ref in zip(prev_states, (sret_all, sca_all, scc_all)):
            all_ref[0:layer] = prev_ref[...]
        sret_ref[...] = jnp.zeros_like(sret_ref)
        sbd_scr[...] = jnp.zeros_like(sbd_scr)
        ubuf[:, 0:HIST_A, :] = jnp.zeros((2, HIST_A, LANES), F32)
        zbuf[0:HIST_C, :] = jnp.zeros((HIST_C, CONV_C_WIDTH), F32)

    def mix_prenorm():
        sh = mod_chunk(modc_ref, 0)
        pre_scale = gpm_ref[...] * (1.0 + mod_chunk(modc_ref, 1))
        for rows in row_blocks:
            h_scr[rows, :] = (_rms(xc_ref[rows, :]) * pre_scale + sh).astype(BF16)

    def in_proj(n):
        cols = pl.ds(n * IN_BLK, IN_BLK)
        proj_scr[:, cols] = jnp.dot(h_scr[...], win_ref[:, cols], preferred_element_type=F32)

    def conv_groups():
        taps = [[jnp.broadcast_to(caw_ref[j:j + 1, pl.ds(half * LANES, LANES)], (SUBLANES, LANES))
                 for j in range(CONV_A_K)] for half in range(2)]
        for r in range(tile // CONV_BLK):
            rows = pl.ds(r * CONV_BLK, CONV_BLK)
            dst = pl.ds(HIST_A + r * CONV_BLK, CONV_BLK)
            for half in range(2):
                lo = half * LANES
                u = (proj_scr[rows, pl.ds(OFF_AV + lo, LANES)]
                     * _sigmoid(proj_scr[rows, pl.ds(OFF_AG + lo, LANES)]))
                ubuf[half, dst, :] = u
        for half in range(2):
            for m in range(tile // span):
                base = m * span
                acc = [None] * CONV_STRIDE
                for k in range(CONV_STRIDE + CONV_A_K - 1):
                    win = ubuf[half, pl.ds(base + first_tap + k, SUBLANES, stride=CONV_STRIDE), :]
                    for t in range(CONV_STRIDE):
                        j = k - t
                        if 0 <= j < CONV_A_K:
                            term = taps[half][j] * win
                            acc[t] = term if acc[t] is None else acc[t] + term
                for t in range(CONV_STRIDE):
                    ua_scr[half, pl.ds(base + t, SUBLANES, stride=CONV_STRIDE), :] = acc[t]
        for r in range(tile // CONV_BLK):
            rows = pl.ds(r * CONV_BLK, CONV_BLK)
            ua = jnp.concatenate([ua_scr[0, rows, :], ua_scr[1, rows, :]], axis=-1) + cab_ref[...]
            o_a = _silu(_layernorm(ua, lng_ref[...], lnb_ref[...]))
            mixin_scr[rows, MIX_A:MIX_A + CONV_A_WIDTH] = o_a.astype(BF16)
        for r in range(tile // CONV_BLK):
            rows = pl.ds(r * CONV_BLK, CONV_BLK)
            z = proj_scr[rows, OFF_CC:OFF_CC + CONV_C_WIDTH] * proj_scr[rows, OFF_CX:OFF_CX + CONV_C_WIDTH]
            zbuf[pl.ds(HIST_C + r * CONV_BLK, CONV_BLK), :] = z
        for r in range(tile // CONV_BLK):
            rows = pl.ds(r * CONV_BLK, CONV_BLK)
            zc = ccw_ref[0:1, :] * zbuf[pl.ds(r * CONV_BLK + first_tap_c, CONV_BLK), :]
            for j in range(1, CONV_C_K):
                zc = zc + ccw_ref[j:j + 1, :] * zbuf[pl.ds(r * CONV_BLK + first_tap_c + j, CONV_BLK), :]
            o_c = proj_scr[rows, OFF_CB:OFF_CB + CONV_C_WIDTH] * zc
            mixin_scr[rows, MIX_C:MIX_C + CONV_C_WIDTH] = o_c.astype(BF16)
        hist_a = pl.ds(tile + first_tap, CONV_A_K - 1)
        sca_ref[...] = jnp.concatenate([ubuf[0, hist_a, :], ubuf[1, hist_a, :]], axis=-1)
        scc_ref[...] = zbuf[pl.ds(tile + first_tap_c, CONV_C_K - 1), :]
        ubuf[:, 0:HIST_A, :] = ubuf[:, pl.ds(tile, HIST_A), :]
        zbuf[0:HIST_C, :] = zbuf[pl.ds(tile, HIST_C), :]

    def retention_scores(i, pair):
        first_half, _ = _first_half_mask(RET_CHUNK)
        rows = pl.ds(i * RET_CHUNK, RET_CHUNK)
        cq, sq, ck, sk = cq_ref[rows, :], sq_ref[rows, :], ck_ref[rows, :], sk_ref[rows, :]
        qr = _rope_half(proj_scr[rows, pl.ds(OFF_Q + pair * LANES, LANES)], cq, sq, first_half)
        kr = _rope_half(proj_scr[rows, pl.ds(OFF_K + pair * LANES, LANES)], ck, sk, first_half)
        qr_b = qr.astype(BF16)
        k_t = kr.T
        head0_rows = lax.broadcasted_iota(jnp.int32, (2 * RET_DK, RET_CHUNK), 0) < RET_DK
        k_bd = jnp.concatenate([jnp.where(head0_rows, k_t, 0.0), jnp.where(head0_rows, 0.0, k_t)], axis=1)
        scores = jnp.dot(qr_b, k_bd.astype(BF16), preferred_element_type=F32) * dmat_ref[pair]
        sc_scr[...] = scores.astype(BF16)
        s_bd = sbd_scr[pair]
        qs_scr[...] = jnp.dot(qr_b, s_bd.astype(BF16), preferred_element_type=F32) * rd_ref[pair]
        v_pair = proj_scr[rows, pl.ds(OFF_V + 2 * pair * RET_DV, 2 * RET_DV)].astype(BF16)
        kv = jnp.dot((k_t * upd_ref[pair]).astype(BF16), v_pair, preferred_element_type=F32)
        r0 = lax.broadcasted_iota(jnp.int32, (2 * RET_DK, 2 * RET_DV), 0) < RET_DK
        c0 = lax.broadcasted_iota(jnp.int32, (2 * RET_DK, 2 * RET_DV), 1) < RET_DV
        s_new = s_bd * cd_ref[pair] + jnp.where(r0 == c0, kv, 0.0)
        sbd_scr[pair] = s_new
        sret_ref[2 * pair] = s_new[0:RET_DK, 0:RET_DV]
        sret_ref[2 * pair + 1] = s_new[RET_DK:2 * RET_DK, RET_DV:2 * RET_DV]

    def retention_values(i, pair):
        rows = pl.ds(i * RET_CHUNK, RET_CHUNK)
        for hl in range(2):
            h = 2 * pair + hl
            vh = proj_scr[rows, pl.ds(OFF_V + h * RET_DV, RET_DV)].astype(BF16)
            o = (jnp.dot(sc_scr[:, pl.ds(hl * RET_CHUNK, RET_CHUNK)], vh, preferred_element_type=F32)
                 + qs_scr[:, pl.ds(hl * RET_DV, RET_DV)])
            gate = proj_scr[rows, pl.ds(OFF_G + h * RET_DV, RET_DV)]
            o = _layernorm(o, gn_ref[:, pl.ds(h * RET_DV, RET_DV)]) * _silu(gate)
            mixin_scr[rows, pl.ds(h * RET_DV, RET_DV)] = o.astype(BF16)

    def out_proj():
        mix_scr[...] = jnp.dot(mixin_scr[...], wout_ref[...], preferred_element_type=F32)

    in_tiles = [functools.partial(in_proj, n) for n in range(IN_WIDTH // IN_BLK)]
    ret_items = [functools.partial(phase, i, pair)
                 for i in range(tile // RET_CHUNK) for pair in range(RET_HEADS // 2)
                 for phase in (retention_scores, retention_values)]

    def ffn_norms():
        post_scale = gqm_ref[...] * mod_chunk(modp_ref, 2)
        sh = mod_chunk(modp_ref, 3)
        pre_scale = gpf_ref[...] * (1.0 + mod_chunk(modp_ref, 4))
        for rows in row_blocks:
            x1 = xp_ref[rows, :] + _rms(mix_scr[rows, :]) * post_scale
            y_ref[rows, :] = x1
            hf_scr[rows, :] = (_rms(x1) * pre_scale + sh).astype(BF16)

    def hidden(j):
        p_scr[:, pl.ds(j * FF_BLK, FF_BLK)] = _swiglu_block(hf_scr[...], w1_ref, w3_ref, j * FF_BLK)

    def down(n):
        cols = pl.ds(n * FF_BLK, FF_BLK)
        proj_scr[:, cols] = jnp.dot(p_scr[...], w2_ref[:, cols], preferred_element_type=F32)

    def ffn_finish():
        post_scale = gqf_ref[...] * mod_chunk(modp_ref, 5)
        for rows in row_blocks:
            y_ref[rows, :] = y_ref[rows, :] + _rms(proj_scr[rows, 0:D_MODEL]) * post_scale

    hidden_blocks = [functools.partial(hidden, j) for j in range(D_FF // FF_BLK)]
    down_tiles = [functools.partial(down, n) for n in range(D_MODEL // FF_BLK)]

    @pl.when(step == 0)
    def _():
        mix_prenorm()
        _emit(in_tiles)
        _emit(ret_items)
        conv_groups()
        out_proj()

    @pl.when((step > 0) & (step < n_tiles))
    def _():
        mix_prenorm()
        ffn_norms()
        _emit(in_tiles)
        _emit_interleaved(hidden_blocks, ret_items)
        conv_groups()
        _emit(down_tiles)
        out_proj()
        ffn_finish()

    @pl.when(step == n_tiles)
    def _():
        ffn_norms()
        _emit(hidden_blocks)
        _emit(down_tiles)
        ffn_finish()


def _prompt_layer(layer, prev, x, mod_p, g_pm, g_qm, g_pf, g_qf, w_in, w_out, w1, w3, w2, tables,
                  gn, caw, cab, lng, lnb, ccw):
    bsz, seq, _ = x.shape
    tile = TOK_TILE
    tps = seq // tile
    n_tiles = bsz * tps
    cq, sq, ck, sk, dmat, rd, upd, cd, _ = tables
    cur = lambda i: jnp.minimum(i, n_tiles - 1)
    prv = lambda i: jnp.maximum(i - 1, 0)
    tok_cur = pl.BlockSpec((None, tile, D_MODEL), lambda i: (cur(i) // tps, cur(i) % tps, 0))
    tok_prev = pl.BlockSpec((None, tile, D_MODEL), lambda i: (prv(i) // tps, prv(i) % tps, 0))
    mod_cur = pl.BlockSpec((None, None, 1, 6 * D_MODEL), lambda i: (layer, cur(i) // tps, 0, 0))
    mod_prev = pl.BlockSpec((None, None, 1, 6 * D_MODEL), lambda i: (layer, prv(i) // tps, 0, 0))
    rope = pl.BlockSpec((tile, LANES), lambda i: (cur(i) % tps, 0))
    cs = functools.partial(_const_spec, grid_rank=1)
    ls = functools.partial(_layer_spec, layer=layer)
    args = (x, x, mod_p, mod_p, g_pm, g_qm, g_pf, g_qf, w_in, w_out, w1, w3, w2,
            cq, sq, ck, sk, dmat, rd, upd, cd, gn, caw, cab, lng, lnb, ccw)
    seq_block = lambda i: (cur(i) // tps,)
    return _stacked_call(
        functools.partial(_prompt_layer_kernel, tile=tile, n_tiles=n_tiles, tiles_per_seq=tps, layer=layer),
        name="prompt_layer", grid=(n_tiles + 1,), args=args, layer=layer, prev=prev,
        stacked=[
            ((bsz, RET_HEADS, RET_DK, RET_DV), (None, RET_HEADS, RET_DK, RET_DV),
             lambda i: seq_block(i) + (0, 0, 0)),
            ((bsz, CONV_A_K - 1, CONV_A_WIDTH), (None, CONV_A_K - 1, CONV_A_WIDTH),
             lambda i: seq_block(i) + (0, 0)),
            ((bsz, CONV_C_K - 1, CONV_C_WIDTH), (None, CONV_C_K - 1, CONV_C_WIDTH),
             lambda i: seq_block(i) + (0, 0)),
        ],
        in_specs=[
            tok_cur, tok_prev, mod_cur, mod_prev,
            cs((DEPTH, D_MODEL)), cs((DEPTH, D_MODEL)), cs((DEPTH, D_MODEL)), cs((DEPTH, D_MODEL)),
            cs((D_MODEL, IN_WIDTH)), cs((D_MODEL, D_MODEL)),
            cs((D_MODEL, D_FF)), cs((D_MODEL, D_FF)), cs((D_FF, D_MODEL)),
            rope, rope, rope, rope,
            cs(dmat.shape), cs(rd.shape), cs(upd.shape), cs(cd.shape),
            cs((DEPTH, RET_WIDTH)), ls((CONV_A_K, CONV_A_WIDTH)), cs((DEPTH, CONV_A_WIDTH)),
            cs((DEPTH, CONV_A_WIDTH)), cs((DEPTH, CONV_A_WIDTH)), ls((CONV_C_K, CONV_C_WIDTH)),
        ],
        out_specs=[tok_prev],
        out_shape=[jax.ShapeDtypeStruct(x.shape, F32)],
        scratch_shapes=[
            pltpu.VMEM((tile, D_MODEL), BF16),
            pltpu.VMEM((tile, IN_WIDTH), F32),
            pltpu.VMEM((tile, D_MODEL), BF16),
            pltpu.VMEM((tile, D_MODEL), F32),
            pltpu.VMEM((tile, D_MODEL), BF16),
            pltpu.VMEM((tile, D_FF), BF16),
            pltpu.VMEM((2, tile + HIST_A, LANES), F32),
            pltpu.VMEM((2, tile, LANES), F32),
            pltpu.VMEM((tile + HIST_C, CONV_C_WIDTH), F32),
            pltpu.VMEM((RET_CHUNK, 2 * RET_CHUNK), BF16),
            pltpu.VMEM((RET_CHUNK, 2 * RET_DV), F32),
            pltpu.VMEM((RET_HEADS // 2, 2 * RET_DK, 2 * RET_DV), F32),
        ])


def _sample_pre_kernel(x_ref, mod_ref, gpre_ref, win_ref, rope_ref, after_ref, proj_ref, qt_ref, kt_ref, winb_ref,
                       h_scr, *, layer):
    j = pl.program_id(0)
    n = x_ref.shape[0]

    @pl.when(j == 0)
    def _():
        sh = mod_ref[:, 0:D_MODEL]
        sc1 = 1.0 + mod_ref[:, D_MODEL:2 * D_MODEL]
        h_scr[...] = (_rms(x_ref[...]) * (gpre_ref[pl.ds(layer, 1), :] * sc1) + sh).astype(BF16)

    w_blk = win_ref[...].astype(BF16)
    winb_ref[...] = w_blk
    proj_ref[...] = jnp.dot(h_scr[...], w_blk, preferred_element_type=F32)

    def rope_transposed(off, cos, sin, dst_ref):
        first_half, _ = _first_half_mask(n)
        for pair in range(RET_HEADS // 2):
            src = proj_ref[:, pl.ds(off + pair * LANES, LANES)]
            dst_ref[pl.ds(pair * LANES, LANES), :] = _rope_half(src, cos, sin, first_half).T

    @pl.when(j == 0)
    def _():
        rope_transposed(OFF_Q, rope_ref[0:1, :], rope_ref[1:2, :], qt_ref)
        rope_transposed(OFF_K, rope_ref[2:3, :], rope_ref[3:4, :], kt_ref)


def _sample_pre(layer, xs, mod, gpre, w_in, rope_s, after):
    n = xs.shape[0]
    cs = functools.partial(_const_spec, grid_rank=1, single_buffer=False)
    ls = functools.partial(_layer_spec, layer=layer, single_buffer=False)
    return pl.pallas_call(
        functools.partial(_sample_pre_kernel, layer=layer),
        grid=(IN_WIDTH // PRE_BLK,),
        in_specs=[cs((n, D_MODEL)), ls((n, 6 * D_MODEL)), cs((DEPTH, D_MODEL)),
                  pl.BlockSpec((None, D_MODEL, PRE_BLK), lambda j: (layer, 0, j)), cs(rope_s.shape),
                  pl.BlockSpec(memory_space=pl.ANY)],
        out_specs=[pl.BlockSpec((n, PRE_BLK), lambda j: (0, j)), cs((QK_WIDTH, n)), cs((QK_WIDTH, n)),
                   pl.BlockSpec((D_MODEL, PRE_BLK), lambda j: (0, j))],
        out_shape=[jax.ShapeDtypeStruct((n, IN_WIDTH), F32),
                   jax.ShapeDtypeStruct((QK_WIDTH, n), F32),
                   jax.ShapeDtypeStruct((QK_WIDTH, n), F32),
                   jax.ShapeDtypeStruct((D_MODEL, IN_WIDTH), BF16)],
        scratch_shapes=[pltpu.VMEM((n, D_MODEL), BF16)],
        compiler_params=_params(1),
        name="sample_pre",
    )(xs, mod, gpre, w_in, rope_s, after)


def _sample_state_kernel(*refs, layer):
    (proj_ref, qt_ref, kt_ref, sin_ref, bufa_ref, bufc_ref, cd_ref,
     gn_ref, caw_ref, cab_ref, lng_ref, lnb_ref, ccw_ref) = refs[:13]
    n_prev = 3 if layer > 0 else 0
    prev_states = refs[13:13 + n_prev]
    mix_ref, sout_all, outa_all, outc_all, o_scr = refs[13 + n_prev:]
    for prev_ref, all_ref in zip(prev_states, (sout_all, outa_all, outc_all)):
        all_ref[0:layer] = prev_ref[...]
    sout_ref, outa_ref, outc_ref = sout_all.at[layer], outa_all.at[layer], outc_all.at[layer]
    gn_ref, cab_ref, lng_ref, lnb_ref = (ref.at[pl.ds(layer, 1)] for ref in (gn_ref, cab_ref, lng_ref, lnb_ref))
    blk = pl.program_id(0)
    n = qt_ref.shape[1]
    lane = lax.broadcasted_iota(jnp.int32, (QK_WIDTH, n), 1)

    for bl in range(SEQ_BLK):
        onehot = lane == (blk * SEQ_BLK + bl)
        qcol = jnp.sum(jnp.where(onehot, qt_ref[...], 0.0), axis=1, keepdims=True)
        kcol = jnp.sum(jnp.where(onehot, kt_ref[...], 0.0), axis=1, keepdims=True)
        row = pl.ds(bl, 1)
        for h in range(RET_HEADS):
            vrow = proj_ref[row, pl.ds(OFF_V + h * RET_DV, RET_DV)]
            s_new = (sin_ref[bl, h] * cd_ref[h]
                     + kcol[h * RET_DK:(h + 1) * RET_DK, :] * vrow)
            sout_ref[bl, h] = s_new
            o_scr[row, pl.ds(h * RET_DV, RET_DV)] = jnp.sum(
                qcol[h * RET_DK:(h + 1) * RET_DK, :] * s_new, axis=0, keepdims=True)
    for h in range(RET_HEADS):
        cols = pl.ds(h * RET_DV, RET_DV)
        o = _layernorm(o_scr[:, cols], gn_ref[:, cols]) * _silu(proj_ref[:, pl.ds(OFF_G + h * RET_DV, RET_DV)])
        mix_ref[:, cols] = o

    u = proj_ref[:, OFF_AV:OFF_AV + CONV_A_WIDTH] * _sigmoid(proj_ref[:, OFF_AG:OFF_AG + CONV_A_WIDTH])
    ua = caw_ref[CONV_A_K - 1:CONV_A_K, :] * u
    for j in range(CONV_A_K - 1):
        ua = ua + caw_ref[j:j + 1, :] * bufa_ref[j]
    outa_ref[0:CONV_A_K - 2] = bufa_ref[1:CONV_A_K - 1]
    outa_ref[CONV_A_K - 2] = u
    mix_ref[:, MIX_A:MIX_A + CONV_A_WIDTH] = _silu(_layernorm(ua + cab_ref[...], lng_ref[...], lnb_ref[...]))

    z = proj_ref[:, OFF_CC:OFF_CC + CONV_C_WIDTH] * proj_ref[:, OFF_CX:OFF_CX + CONV_C_WIDTH]
    zc = ccw_ref[0:1, :] * bufc_ref[0] + ccw_ref[1:2, :] * bufc_ref[1] + ccw_ref[2:3, :] * z
    outc_ref[0] = bufc_ref[1]
    outc_ref[1] = z
    mix_ref[:, MIX_C:MIX_C + CONV_C_WIDTH] = proj_ref[:, OFF_CB:OFF_CB + CONV_C_WIDTH] * zc


def _sample_state(layer, prev, proj, qt, kt, s_ret, buf_a_t, buf_c_t, cd, gn, caw, cab, lng, lnb, ccw):
    n = proj.shape[0]
    cs = functools.partial(_const_spec, grid_rank=1, single_buffer=False)
    ls = functools.partial(_layer_spec, layer=layer, single_buffer=False)
    s_spec = pl.BlockSpec((None, SEQ_BLK, RET_HEADS, RET_DK, RET_DV), lambda j: (layer, j, 0, 0, 0))
    a_spec = pl.BlockSpec((None, CONV_A_K - 1, SEQ_BLK, CONV_A_WIDTH), lambda j: (layer, 0, j, 0))
    c_spec = pl.BlockSpec((None, CONV_C_K - 1, SEQ_BLK, CONV_C_WIDTH), lambda j: (layer, 0, j, 0))
    args = (proj, qt, kt, s_ret, buf_a_t, buf_c_t, cd, gn, caw, cab, lng, lnb, ccw)
    return _stacked_call(
        functools.partial(_sample_state_kernel, layer=layer),
        name="sample_state", grid=(n // SEQ_BLK,), args=args, layer=layer, prev=prev,
        stacked=[
            ((n, RET_HEADS, RET_DK, RET_DV), (SEQ_BLK, RET_HEADS, RET_DK, RET_DV), lambda j: (j, 0, 0, 0)),
            ((CONV_A_K - 1, n, CONV_A_WIDTH), (CONV_A_K - 1, SEQ_BLK, CONV_A_WIDTH), lambda j: (0, j, 0)),
            ((CONV_C_K - 1, n, CONV_C_WIDTH), (CONV_C_K - 1, SEQ_BLK, CONV_C_WIDTH), lambda j: (0, j, 0)),
        ],
        in_specs=[
            pl.BlockSpec((SEQ_BLK, IN_WIDTH), lambda j: (j, 0)),
            cs((QK_WIDTH, n)), cs((QK_WIDTH, n)),
            s_spec, a_spec, c_spec,
            cs(cd.shape), cs((DEPTH, RET_WIDTH)), ls((CONV_A_K, CONV_A_WIDTH)), cs((DEPTH, CONV_A_WIDTH)),
            cs((DEPTH, CONV_A_WIDTH)), cs((DEPTH, CONV_A_WIDTH)), ls((CONV_C_K, CONV_C_WIDTH)),
        ],
        out_specs=[pl.BlockSpec((SEQ_BLK, D_MODEL), lambda j: (j, 0))],
        out_shape=[jax.ShapeDtypeStruct((n, D_MODEL), F32)],
        scratch_shapes=[pltpu.VMEM((SEQ_BLK, RET_WIDTH), F32)])


def _sample_post_kernel(x_ref, mod_ref, mix_ref, wout_ref, gpm_ref, gpf_ref, gqf_ref, w1_ref, w3_ref, w2_ref,
                        y_ref, woutb_ref, w1b_ref, w3b_ref, w2b_ref, x1_scr, h_scr, f_scr, *, n_steps, layer):
    j = pl.program_id(0)
    gpm_ref, gpf_ref, gqf_ref = (ref.at[pl.ds(layer, 1)] for ref in (gpm_ref, gpf_ref, gqf_ref))

    @pl.when(j == 0)
    def _():
        gt_m = mod_ref[:, 2 * D_MODEL:3 * D_MODEL]
        wout_b = wout_ref[...].astype(BF16)
        woutb_ref[...] = wout_b
        mix = jnp.dot(mix_ref[...].astype(BF16), wout_b, preferred_element_type=F32)
        x1 = x_ref[...] + _rms(mix) * (gpm_ref[...] * gt_m)
        x1_scr[...] = x1
        sh = mod_ref[:, 3 * D_MODEL:4 * D_MODEL]
        sc1 = 1.0 + mod_ref[:, 4 * D_MODEL:5 * D_MODEL]
        h_scr[...] = (_rms(x1) * (gpf_ref[...] * sc1) + sh).astype(BF16)
        f_scr[...] = jnp.zeros_like(f_scr)

    w1_b, w3_b, w2_b = w1_ref[...].astype(BF16), w3_ref[...].astype(BF16), w2_ref[...].astype(BF16)
    w1b_ref[...] = w1_b
    w3b_ref[...] = w3_b
    w2b_ref[...] = w2_b
    h = h_scr[...]
    a = jnp.dot(h, w1_b, preferred_element_type=F32)
    b = jnp.dot(h, w3_b, preferred_element_type=F32)
    p = (_silu(a) * b).astype(BF16)
    f_scr[...] += jnp.dot(p, w2_b, preferred_element_type=F32)

    @pl.when(j == n_steps - 1)
    def _():
        gt_f = mod_ref[:, 5 * D_MODEL:6 * D_MODEL]
        y_ref[...] = x1_scr[...] + _rms(f_scr[...]) * (gqf_ref[...] * gt_f)


def _sample_post(layer, xs, mod, mix, w_out, gpost_m, gpre_f, gpost_f, w1, w3, w2):
    n = xs.shape[0]
    n_steps = D_FF // FF_BLK
    cs = functools.partial(_const_spec, grid_rank=1, single_buffer=False)
    ls = functools.partial(_layer_spec, layer=layer, single_buffer=False)
    return pl.pallas_call(
        functools.partial(_sample_post_kernel, n_steps=n_steps, layer=layer),
        grid=(n_steps,),
        in_specs=[
            cs((n, D_MODEL)), ls((n, 6 * D_MODEL)), cs((n, D_MODEL)), ls((D_MODEL, D_MODEL)),
            cs((DEPTH, D_MODEL)), cs((DEPTH, D_MODEL)), cs((DEPTH, D_MODEL)),
            pl.BlockSpec((None, D_MODEL, FF_BLK), lambda j: (layer, 0, j)),
            pl.BlockSpec((None, D_MODEL, FF_BLK), lambda j: (layer, 0, j)),
            pl.BlockSpec((None, FF_BLK, D_MODEL), lambda j: (layer, j, 0)),
        ],
        out_specs=[
            cs((n, D_MODEL)), cs((D_MODEL, D_MODEL)),
            pl.BlockSpec((D_MODEL, FF_BLK), lambda j: (0, j)),
            pl.BlockSpec((D_MODEL, FF_BLK), lambda j: (0, j)),
            pl.BlockSpec((FF_BLK, D_MODEL), lambda j: (j, 0)),
        ],
        out_shape=[
            jax.ShapeDtypeStruct((n, D_MODEL), F32),
            jax.ShapeDtypeStruct((D_MODEL, D_MODEL), BF16),
            jax.ShapeDtypeStruct((D_MODEL, D_FF), BF16),
            jax.ShapeDtypeStruct((D_MODEL, D_FF), BF16),
            jax.ShapeDtypeStruct((D_FF, D_MODEL), BF16),
        ],
        scratch_shapes=[
            pltpu.VMEM((n, D_MODEL), F32),
            pltpu.VMEM((n, D_MODEL), BF16),
            pltpu.VMEM((n, D_MODEL), F32),
        ],
        compiler_params=_params(1),
        name="sample_post",
    )(xs, mod, mix, w_out, gpost_m, gpre_f, gpost_f, w1, w3, w2)


def kernel(x_prompt, x_sample, c_prompt, c_sample, state_ret, state_conv_a, state_conv_c, ada_w, ada_b, norm_pre_mix, norm_post_mix, norm_pre_ffn, norm_post_ffn, w_in, w_out, ret_gn_g, conv_a_w, conv_a_b, conv_a_ln_g, conv_a_ln_b, conv_c_w, ffn_w1, ffn_w3, ffn_w2):
    bp, lp, _ = x_prompt.shape
    ns = x_sample.shape[0]
    assert x_sample.shape[1] == 1 and ns % SEQ_BLK == 0
    assert OFF_K + QK_WIDTH <= PRE_BLK and IN_WIDTH % PRE_BLK == 0
    assert lp % TOK_TILE == 0 and TOK_TILE % RET_CHUNK == 0
    assert TOK_TILE % CONV_BLK == 0 and CONV_BLK % (SUBLANES * CONV_STRIDE) == 0

    k_scale = RET_DK ** -0.5
    cq, sq = _rope_tables(np.arange(lp), 1.0)
    ck, sk = _rope_tables(np.arange(lp), k_scale)
    tables_p = (cq, sq, ck, sk) + _decay_tables(RET_CHUNK)
    cqs, sqs = _rope_tables([PAST_LEN], 1.0)
    cks, sks = _rope_tables([PAST_LEN], k_scale)
    rope_s = np.concatenate([cqs, sqs, cks, sks, np.zeros((4, LANES), np.float32)], axis=0)
    cd_s = _decay_tables(1)[4]

    mod_s, mod_p = _ada_modulation(c_sample, c_prompt, ada_w, ada_b)

    gn, cab, lng, lnb = ret_gn_g, conv_a_b, conv_a_ln_g, conv_a_ln_b
    g_pm, g_qm, g_pf, g_qf = norm_pre_mix, norm_post_mix, norm_pre_ffn, norm_post_ffn
    conv_a_t = jnp.transpose(state_conv_a, (0, 2, 1, 3))
    conv_c_t = jnp.transpose(state_conv_c, (0, 2, 1, 3))

    yp = x_prompt
    ys = x_sample.reshape(ns, D_MODEL)
    st_p = st_s = ()
    for l in range(DEPTH):
        proj, qt, kt, w_in_b = _sample_pre(l, ys, mod_s, g_pm, w_in, rope_s, yp)
        mix, *st_s = _sample_state(l, st_s, proj, qt, kt, state_ret, conv_a_t, conv_c_t, cd_s,
                                   gn, conv_a_w, cab, lng, lnb, conv_c_w)
        ys, w_out_b, w1_b, w3_b, w2_b = _sample_post(l, ys, mod_s, mix, w_out, g_qm, g_pf, g_qf,
                                                    ffn_w1, ffn_w3, ffn_w2)
        yp, *st_p = _prompt_layer(l, st_p, yp, mod_p, g_pm, g_qm, g_pf, g_qf, w_in_b, w_out_b, w1_b, w3_b, w2_b,
                                  tables_p, gn, conv_a_w, cab, lng, lnb, conv_c_w)

    sret_s, sca_s, scc_s = st_s
    sca_s = jnp.transpose(sca_s, (0, 2, 1, 3))
    scc_s = jnp.transpose(scc_s, (0, 2, 1, 3))
    return (yp, ys.reshape(ns, 1, D_MODEL)) + tuple(st_p) + (sret_s, sca_s, scc_s)
```

```python
import functools

import numpy as np
import jax
import jax.numpy as jnp
from jax import lax
from jax.experimental import pallas as pl
from jax.experimental.pallas import tpu as pltpu

D_MODEL = 1024
DEPTH = 2
PAST_LEN = 16384
RET_HEADS = 4
RET_WIDTH = D_MODEL // 2
RET_DV = RET_WIDTH // RET_HEADS
RET_DK = RET_DV // 2
QK_WIDTH = RET_HEADS * RET_DK
CONV_A_WIDTH = D_MODEL // 4
CONV_A_K = 31
CONV_C_WIDTH = D_MODEL - RET_WIDTH - CONV_A_WIDTH
CONV_C_K = 3
IN_WIDTH = 2 * QK_WIDTH + 2 * RET_WIDTH + 2 * CONV_A_WIDTH + 3 * CONV_C_WIDTH
D_FF = ((8 * D_MODEL // 3 + 255) // 256) * 256
RET_CHUNK = 128
ROPE_BASE = 10000.0
EPS = 1e-6

OFF_Q = 0
OFF_K = OFF_Q + QK_WIDTH
OFF_V = OFF_K + QK_WIDTH
OFF_G = OFF_V + RET_WIDTH
OFF_AV = OFF_G + RET_WIDTH
OFF_AG = OFF_AV + CONV_A_WIDTH
OFF_CB = OFF_AG + CONV_A_WIDTH
OFF_CC = OFF_CB + CONV_C_WIDTH
OFF_CX = OFF_CC + CONV_C_WIDTH
MIX_A = RET_WIDTH
MIX_C = RET_WIDTH + CONV_A_WIDTH

LANES = 128
SUBLANES = 8
V7X_VMEM_LIMIT_BYTES = 60 * 1024 * 1024

TOK_TILE = 512
ROW_BLK = 32
CONV_BLK = 64
CONV_STRIDE = 4
HIST_A = 32
HIST_C = 8
SEQ_BLK = 16
FF_BLK = 256
IN_BLK = 256
PRE_BLK = 1408

F32 = jnp.float32
BF16 = jnp.bfloat16


def _sigmoid(x):
    return jax.nn.sigmoid(x)


def _silu(x):
    return x * _sigmoid(x)


def _rms(x):
    return x * lax.rsqrt(jnp.mean(x * x, axis=-1, keepdims=True) + EPS)


def _layernorm(x, g, b=None):
    mu = jnp.mean(x, axis=-1, keepdims=True)
    d = x - mu
    var = jnp.mean(d * d, axis=-1, keepdims=True)
    y = d * lax.rsqrt(var + EPS) * g
    return y if b is None else y + b


def _rope_half(x, cos, sin, first_half):
    partner = jnp.where(first_half, pltpu.roll(x, 96, 1), pltpu.roll(x, 32, 1))
    return x * cos + partner * sin


def _first_half_mask(rows):
    lane = lax.broadcasted_iota(jnp.int32, (rows, LANES), 1)
    return (lane & (RET_DK - 1)) < (RET_DK // 2), lane < RET_DK


def _swiglu_block(h, w1_ref, w3_ref, col0):
    cols = pl.ds(col0, FF_BLK)
    a = jnp.dot(h, w1_ref[:, cols], preferred_element_type=F32)
    b = jnp.dot(h, w3_ref[:, cols], preferred_element_type=F32)
    return (_silu(a) * b).astype(BF16)


def _emit(items):
    for item in items:
        item()


def _emit_interleaved(primary, secondary):
    n, m = len(primary), len(secondary)
    done = 0
    for i, item in enumerate(primary):
        item()
        upto = ((i + 1) * m) // n
        _emit(secondary[done:upto])
        done = upto


def _rope_tables(pos, k_scale):
    half = RET_DK // 2
    inv = ROPE_BASE ** (-np.arange(half, dtype=np.float64) / half)
    ang = np.asarray(pos, np.float64)[:, None] * inv[None, :]
    cos = np.tile(np.cos(ang), (1, 4))
    sin = np.tile(np.concatenate([-np.sin(ang), np.sin(ang)], axis=1), (1, 2))
    return (cos * k_scale).astype(np.float32), (sin * k_scale).astype(np.float32)


def _decay_tables(chunk):
    log_g = np.log(1.0 - np.exp2(-5.0 - np.arange(RET_HEADS, dtype=np.float64)))
    idx = np.arange(chunk, dtype=np.float64)
    diff = idx[:, None] - idx[None, :]
    dmat = np.where(diff[None] >= 0, np.exp(np.maximum(diff, 0.0)[None] * log_g[:, None, None]), 0.0)
    read_dec = np.exp((idx + 1.0)[:, None] * log_g[None, :])
    upd_dec = np.exp((chunk - 1.0 - idx)[:, None] * log_g[None, :])
    chunk_dec = np.exp(chunk * log_g)
    pairs = RET_HEADS // 2
    side_by_side = lambda a: a.reshape(pairs, 2, *a.shape[1:]).transpose(0, 2, 1, 3).reshape(pairs, a.shape[1], -1)
    dmat2 = side_by_side(dmat)
    rd2 = side_by_side(np.broadcast_to(read_dec.T[:, :, None], (RET_HEADS, chunk, RET_DV)))
    updt = np.repeat(upd_dec.T, RET_DK, axis=0).reshape(pairs, 2 * RET_DK, chunk)
    cd = np.broadcast_to(chunk_dec[:, None, None], (RET_HEADS, 1, RET_DV))
    cd2 = side_by_side(cd)
    f = lambda a: np.ascontiguousarray(a, dtype=np.float32)
    return f(dmat2), f(rd2), f(updt), f(cd2), f(cd)


def _const_spec(shape, grid_rank, single_buffer=True):
    zeros = (0,) * len(shape)
    idx = (lambda b, c: zeros) if grid_rank == 2 else (lambda j: zeros)
    if single_buffer:
        return pl.BlockSpec(shape, idx, pipeline_mode=pl.Buffered(1))
    return pl.BlockSpec(shape, idx)


def _layer_spec(shape, layer, single_buffer=True):
    zeros = (0,) * len(shape)
    mode = dict(pipeline_mode=pl.Buffered(1)) if single_buffer else {}
    return pl.BlockSpec((None,) + tuple(shape), lambda j: (layer,) + zeros, **mode)


def _params(n_axes):
    return pltpu.CompilerParams(dimension_semantics=("arbitrary",) * n_axes,
                                vmem_limit_bytes=V7X_VMEM_LIMIT_BYTES)


def _stacked_call(kern, *, name, grid, in_specs, args, out_specs, out_shape, layer, stacked, prev, scratch_shapes):
    def spec(n_layers, block, index_fn):
        return pl.BlockSpec((n_layers,) + tuple(block), lambda *g: (0,) + tuple(index_fn(*g)))

    in_specs, out_specs, out_shape, args = list(in_specs), list(out_specs), list(out_shape), list(args)
    for dims, block, index_fn in stacked:
        out_specs.append(spec(layer + 1, block, index_fn))
        out_shape.append(jax.ShapeDtypeStruct((layer + 1,) + tuple(dims), F32))
        if layer > 0:
            in_specs.append(spec(layer, block, index_fn))
    if layer > 0:
        args += list(prev)
    return pl.pallas_call(
        kern, grid=grid, in_specs=in_specs, out_specs=out_specs, out_shape=out_shape,
        scratch_shapes=scratch_shapes, compiler_params=_params(len(grid)), name=name)(*args)


def _ada_kernel(cs_ref, cp_ref, w_ref, b_ref, os_ref, op_ref):
    layer = pl.program_id(0)
    bias = b_ref[0:1, :]
    for d in range(1, DEPTH):
        bias = jnp.where(layer == d, b_ref[d:d + 1, :], bias)
    w = w_ref[...].astype(BF16)
    os_ref[...] = jnp.dot(_silu(cs_ref[...]).astype(BF16), w, preferred_element_type=F32) + bias
    mod_p = jnp.dot(_silu(cp_ref[...]).astype(BF16), w, preferred_element_type=F32) + bias
    for b in range(cp_ref.shape[0]):
        op_ref[b] = mod_p[b:b + 1, :]


def _ada_modulation(c_sample, c_prompt, ada_w, ada_b):
    ns, bp = c_sample.shape[0], c_prompt.shape[0]
    ncol = 6 * D_MODEL
    blk = 3 * D_MODEL
    return pl.pallas_call(
        _ada_kernel,
        grid=(DEPTH, ncol // blk),
        in_specs=[
            pl.BlockSpec((ns, D_MODEL), lambda l, j: (0, 0)),
            pl.BlockSpec((bp, D_MODEL), lambda l, j: (0, 0)),
            pl.BlockSpec((None, D_MODEL, blk), lambda l, j: (l, 0, j)),
            pl.BlockSpec((DEPTH, blk), lambda l, j: (0, j)),
        ],
        out_specs=[pl.BlockSpec((None, ns, blk), lambda l, j: (l, 0, j)),
                   pl.BlockSpec((None, bp, 1, blk), lambda l, j: (l, 0, 0, j))],
        out_shape=[jax.ShapeDtypeStruct((DEPTH, ns, ncol), F32),
                   jax.ShapeDtypeStruct((DEPTH, bp, 1, ncol), F32)],
        compiler_params=_params(2),
        name="ada_mod",
    )(c_sample, c_prompt, ada_w, ada_b)


def _prompt_layer_kernel(*refs, tile, n_tiles, tiles_per_seq, layer):
    (xc_ref, xp_ref, modc_ref, modp_ref, gpm_ref, gqm_ref, gpf_ref, gqf_ref,
     win_ref, wout_ref, w1_ref, w3_ref, w2_ref,
     cq_ref, sq_ref, ck_ref, sk_ref, dmat_ref, rd_ref, upd_ref, cd_ref,
     gn_ref, caw_ref, cab_ref, lng_ref, lnb_ref, ccw_ref) = refs[:27]
    n_prev = 3 if layer > 0 else 0
    prev_states = refs[27:27 + n_prev]
    (y_ref, sret_all, sca_all, scc_all,
     h_scr, proj_scr, mixin_scr, mix_scr, hf_scr, p_scr, ubuf, ua_scr, zbuf,
     sc_scr, qs_scr, sbd_scr) = refs[27 + n_prev:]
    sret_ref, sca_ref, scc_ref = sret_all.at[layer], sca_all.at[layer], scc_all.at[layer]
    gpm_ref, gqm_ref, gpf_ref, gqf_ref, gn_ref, cab_ref, lng_ref, lnb_ref = (
        ref.at[pl.ds(layer, 1)] for ref in (gpm_ref, gqm_ref, gpf_ref, gqf_ref, gn_ref, cab_ref, lng_ref, lnb_ref))
    mod_chunk = lambda ref, k: ref[:, pl.ds(k * D_MODEL, D_MODEL)]
    step = pl.program_id(0)
    row_blocks = [pl.ds(r * ROW_BLK, ROW_BLK) for r in range(tile // ROW_BLK)]
    first_tap = HIST_A - (CONV_A_K - 1)
    first_tap_c = HIST_C - (CONV_C_K - 1)
    span = SUBLANES * CONV_STRIDE

    @pl.when((step < n_tiles) & (lax.rem(step, tiles_per_seq) == 0))
    def _():
        for prev_ref, all_ref in zip(prev_states, (sret_all, sca_all, scc_all)):
            all_ref[0:layer] = prev_ref[...]
        sret_ref[...] = jnp.zeros_like(sret_ref)
        sbd_scr[...] = jnp.zeros_like(sbd_scr)
        ubuf[:, 0:HIST_A, :] = jnp.zeros((2, HIST_A, LANES), F32)
        zbuf[0:HIST_C, :] = jnp.zeros((HIST_C, CONV_C_WIDTH), F32)

    def mix_prenorm():
        sh = mod_chunk(modc_ref, 0)
        pre_scale = gpm_ref[...] * (1.0 + mod_chunk(modc_ref, 1))
        for rows in row_blocks:
            h_scr[rows, :] = (_rms(xc_ref[rows, :]) * pre_scale + sh).astype(BF16)

    def in_proj(n):
        cols = pl.ds(n * IN_BLK, IN_BLK)
        proj_scr[:, cols] = jnp.dot(h_scr[...], win_ref[:, cols], preferred_element_type=F32)

    def conv_groups():
        taps = [[jnp.broadcast_to(caw_ref[j:j + 1, pl.ds(half * LANES, LANES)], (SUBLANES, LANES))
                 for j in range(CONV_A_K)] for half in range(2)]
        for r in range(tile // CONV_BLK):
            rows = pl.ds(r * CONV_BLK, CONV_BLK)
            dst = pl.ds(HIST_A + r * CONV_BLK, CONV_BLK)
            for half in range(2):
                lo = half * LANES
                u = (proj_scr[rows, pl.ds(OFF_AV + lo, LANES)]
                     * _sigmoid(proj_scr[rows, pl.ds(OFF_AG + lo, LANES)]))
                ubuf[half, dst, :] = u
        for half in range(2):
            for m in range(tile // span):
                base = m * span
                acc = [None] * CONV_STRIDE
                for k in range(CONV_STRIDE + CONV_A_K - 1):
                    win = ubuf[half, pl.ds(base + first_tap + k, SUBLANES, stride=CONV_STRIDE), :]
                    for t in range(CONV_STRIDE):
                        j = k - t
                        if 0 <= j < CONV_A_K:
                            term = taps[half][j] * win
                            acc[t] = term if acc[t] is None else acc[t] + term
                for t in range(CONV_STRIDE):
                    ua_scr[half, pl.ds(base + t, SUBLANES, stride=CONV_STRIDE), :] = acc[t]
        for r in range(tile // CONV_BLK):
            rows = pl.ds(r * CONV_BLK, CONV_BLK)
            ua = jnp.concatenate([ua_scr[0, rows, :], ua_scr[1, rows, :]], axis=-1) + cab_ref[...]
            o_a = _silu(_layernorm(ua, lng_ref[...], lnb_ref[...]))
            mixin_scr[rows, MIX_A:MIX_A + CONV_A_WIDTH] = o_a.astype(BF16)
        for r in range(tile // CONV_BLK):
            rows = pl.ds(r * CONV_BLK, CONV_BLK)
            z = proj_scr[rows, OFF_CC:OFF_CC + CONV_C_WIDTH] * proj_scr[rows, OFF_CX:OFF_CX + CONV_C_WIDTH]
            zbuf[pl.ds(HIST_C + r * CONV_BLK, CONV_BLK), :] = z
        for r in range(tile // CONV_BLK):
            rows = pl.ds(r * CONV_BLK, CONV_BLK)
            zc = ccw_ref[0:1, :] * zbuf[pl.ds(r * CONV_BLK + first_tap_c, CONV_BLK), :]
            for j in range(1, CONV_C_K):
                zc = zc + ccw_ref[j:j + 1, :] * zbuf[pl.ds(r * CONV_BLK + first_tap_c + j, CONV_BLK), :]
            o_c = proj_scr[rows, OFF_CB:OFF_CB + CONV_C_WIDTH] * zc
            mixin_scr[rows, MIX_C:MIX_C + CONV_C_WIDTH] = o_c.astype(BF16)
        hist_a = pl.ds(tile + first_tap, CONV_A_K - 1)
        sca_ref[...] = jnp.concatenate([ubuf[0, hist_a, :], ubuf[1, hist_a, :]], axis=-1)
        scc_ref[...] = zbuf[pl.ds(tile + first_tap_c, CONV_C_K - 1), :]
        ubuf[:, 0:HIST_A, :] = ubuf[:, pl.ds(tile, HIST_A), :]
        zbuf[0:HIST_C, :] = zbuf[pl.ds(tile, HIST_C), :]

    def retention_scores(i, pair):
        first_half, _ = _first_half_mask(RET_CHUNK)
        rows = pl.ds(i * RET_CHUNK, RET_CHUNK)
        cq, sq, ck, sk = cq_ref[rows, :], sq_ref[rows, :], ck_ref[rows, :], sk_ref[rows, :]
        qr = _rope_half(proj_scr[rows, pl.ds(OFF_Q + pair * LANES, LANES)], cq, sq, first_half)
        kr = _rope_half(proj_scr[rows, pl.ds(OFF_K + pair * LANES, LANES)], ck, sk, first_half)
        qr_b = qr.astype(BF16)
        k_t = kr.T
        head0_rows = lax.broadcasted_iota(jnp.int32, (2 * RET_DK, RET_CHUNK), 0) < RET_DK
        k_bd = jnp.concatenate([jnp.where(head0_rows, k_t, 0.0), jnp.where(head0_rows, 0.0, k_t)], axis=1)
        scores = jnp.dot(qr_b, k_bd.astype(BF16), preferred_element_type=F32) * dmat_ref[pair]
        sc_scr[...] = scores.astype(BF16)
        s_bd = sbd_scr[pair]
        qs_scr[...] = jnp.dot(qr_b, s_bd.astype(BF16), preferred_element_type=F32) * rd_ref[pair]
        v_pair = proj_scr[rows, pl.ds(OFF_V + 2 * pair * RET_DV, 2 * RET_DV)].astype(BF16)
        kv = jnp.dot((k_t * upd_ref[pair]).astype(BF16), v_pair, preferred_element_type=F32)
        r0 = lax.broadcasted_iota(jnp.int32, (2 * RET_DK, 2 * RET_DV), 0) < RET_DK
        c0 = lax.broadcasted_iota(jnp.int32, (2 * RET_DK, 2 * RET_DV), 1) < RET_DV
        s_new = s_bd * cd_ref[pair] + jnp.where(r0 == c0, kv, 0.0)
        sbd_scr[pair] = s_new
        sret_ref[2 * pair] = s_new[0:RET_DK, 0:RET_DV]
        sret_ref[2 * pair + 1] = s_new[RET_DK:2 * RET_DK, RET_DV:2 * RET_DV]

    def retention_values(i, pair):
        rows = pl.ds(i * RET_CHUNK, RET_CHUNK)
        for hl in range(2):
            h = 2 * pair + hl
            vh = proj_scr[rows, pl.ds(OFF_V + h * RET_DV, RET_DV)].astype(BF16)
            o = (jnp.dot(sc_scr[:, pl.ds(hl * RET_CHUNK, RET_CHUNK)], vh, preferred_element_type=F32)
                 + qs_scr[:, pl.ds(hl * RET_DV, RET_DV)])
            gate = proj_scr[rows, pl.ds(OFF_G + h * RET_DV, RET_DV)]
            o = _layernorm(o, gn_ref[:, pl.ds(h * RET_DV, RET_DV)]) * _silu(gate)
            mixin_scr[rows, pl.ds(h * RET_DV, RET_DV)] = o.astype(BF16)

    def out_proj():
        mix_scr[...] = jnp.dot(mixin_scr[...], wout_ref[...], preferred_element_type=F32)

    in_tiles = [functools.partial(in_proj, n) for n in range(IN_WIDTH // IN_BLK)]
    ret_items = [functools.partial(phase, i, pair)
                 for i in range(tile // RET_CHUNK) for pair in range(RET_HEADS // 2)
                 for phase in (retention_scores, retention_values)]

    def ffn_norms():
        post_scale = gqm_ref[...] * mod_chunk(modp_ref, 2)
        sh = mod_chunk(modp_ref, 3)
        pre_scale = gpf_ref[...] * (1.0 + mod_chunk(modp_ref, 4))
        for rows in row_blocks:
            x1 = xp_ref[rows, :] + _rms(mix_scr[rows, :]) * post_scale
            y_ref[rows, :] = x1
            hf_scr[rows, :] = (_rms(x1) * pre_scale + sh).astype(BF16)

    def hidden(j):
        p_scr[:, pl.ds(j * FF_BLK, FF_BLK)] = _swiglu_block(hf_scr[...], w1_ref, w3_ref, j * FF_BLK)

    def down(n):
        cols = pl.ds(n * FF_BLK, FF_BLK)
        proj_scr[:, cols] = jnp.dot(p_scr[...], w2_ref[:, cols], preferred_element_type=F32)

    def ffn_finish():
        post_scale = gqf_ref[...] * mod_chunk(modp_ref, 5)
        for rows in row_blocks:
            y_ref[rows, :] = y_ref[rows, :] + _rms(proj_scr[rows, 0:D_MODEL]) * post_scale

    hidden_blocks = [functools.partial(hidden, j) for j in range(D_FF // FF_BLK)]
    down_tiles = [functools.partial(down, n) for n in range(D_MODEL // FF_BLK)]

    @pl.when(step == 0)
    def _():
        mix_prenorm()
        _emit(in_tiles)
        _emit(ret_items)
        conv_groups()
        out_proj()

    @pl.when((step > 0) & (step < n_tiles))
    def _():
        mix_prenorm()
        ffn_norms()
        _emit(in_tiles)
        _emit_interleaved(hidden_blocks, ret_items)
        conv_groups()
        _emit(down_tiles)
        out_proj()
        ffn_finish()

    @pl.when(step == n_tiles)
    def _():
        ffn_norms()
        _emit(hidden_blocks)
        _emit(down_tiles)
        ffn_finish()


def _prompt_layer(layer, prev, x, mod_p, g_pm, g_qm, g_pf, g_qf, w_in, w_out, w1, w3, w2, tables,
                  gn, caw, cab, lng, lnb, ccw):
    bsz, seq, _ = x.shape
    tile = TOK_TILE
    tps = seq // tile
    n_tiles = bsz * tps
    cq, sq, ck, sk, dmat, rd, upd, cd, _ = tables
    cur = lambda i: jnp.minimum(i, n_tiles - 1)
    prv = lambda i: jnp.maximum(i - 1, 0)
    tok_cur = pl.BlockSpec((None, tile, D_MODEL), lambda i: (cur(i) // tps, cur(i) % tps, 0))
    tok_prev = pl.BlockSpec((None, tile, D_MODEL), lambda i: (prv(i) // tps, prv(i) % tps, 0))
    mod_cur = pl.BlockSpec((None, None, 1, 6 * D_MODEL), lambda i: (layer, cur(i) // tps, 0, 0))
    mod_prev = pl.BlockSpec((None, None, 1, 6 * D_MODEL), lambda i: (layer, prv(i) // tps, 0, 0))
    rope = pl.BlockSpec((tile, LANES), lambda i: (cur(i) % tps, 0))
    cs = functools.partial(_const_spec, grid_rank=1)
    ls = functools.partial(_layer_spec, layer=layer)
    args = (x, x, mod_p, mod_p, g_pm, g_qm, g_pf, g_qf, w_in, w_out, w1, w3, w2,
            cq, sq, ck, sk, dmat, rd, upd, cd, gn, caw, cab, lng, lnb, ccw)
    seq_block = lambda i: (cur(i) // tps,)
    return _stacked_call(
        functools.partial(_prompt_layer_kernel, tile=tile, n_tiles=n_tiles, tiles_per_seq=tps, layer=layer),
        name="prompt_layer", grid=(n_tiles + 1,), args=args, layer=layer, prev=prev,
        stacked=[
            ((bsz, RET_HEADS, RET_DK, RET_DV), (None, RET_HEADS, RET_DK, RET_DV),
             lambda i: seq_block(i) + (0, 0, 0)),
            ((bsz, CONV_A_K - 1, CONV_A_WIDTH), (None, CONV_A_K - 1, CONV_A_WIDTH),
             lambda i: seq_block(i) + (0, 0)),
            ((bsz, CONV_C_K - 1, CONV_C_WIDTH), (None, CONV_C_K - 1, CONV_C_WIDTH),
             lambda i: seq_block(i) + (0, 0)),
        ],
        in_specs=[
            tok_cur, tok_prev, mod_cur, mod_prev,
            cs((DEPTH, D_MODEL)), cs((DEPTH, D_MODEL)), cs((DEPTH, D_MODEL)), cs((DEPTH, D_MODEL)),
            cs((D_MODEL, IN_WIDTH)), cs((D_MODEL, D_MODEL)),
            cs((D_MODEL, D_FF)), cs((D_MODEL, D_FF)), cs((D_FF, D_MODEL)),
            rope, rope, rope, rope,
            cs(dmat.shape), cs(rd.shape), cs(upd.shape), cs(cd.shape),
            cs((DEPTH, RET_WIDTH)), ls((CONV_A_K, CONV_A_WIDTH)), cs((DEPTH, CONV_A_WIDTH)),
            cs((DEPTH, CONV_A_WIDTH)), cs((DEPTH, CONV_A_WIDTH)), ls((CONV_C_K, CONV_C_WIDTH)),
        ],
        out_specs=[tok_prev],
        out_shape=[jax.ShapeDtypeStruct(x.shape, F32)],
        scratch_shapes=[
            pltpu.VMEM((tile, D_MODEL), BF16),
            pltpu.VMEM((tile, IN_WIDTH), F32),
            pltpu.VMEM((tile, D_MODEL), BF16),
            pltpu.VMEM((tile, D_MODEL), F32),
            pltpu.VMEM((tile, D_MODEL), BF16),
            pltpu.VMEM((tile, D_FF), BF16),
            pltpu.VMEM((2, tile + HIST_A, LANES), F32),
            pltpu.VMEM((2, tile, LANES), F32),
            pltpu.VMEM((tile + HIST_C, CONV_C_WIDTH), F32),
            pltpu.VMEM((RET_CHUNK, 2 * RET_CHUNK), BF16),
            pltpu.VMEM((RET_CHUNK, 2 * RET_DV), F32),
            pltpu.VMEM((RET_HEADS // 2, 2 * RET_DK, 2 * RET_DV), F32),
        ])


def _sample_pre_kernel(x_ref, mod_ref, gpre_ref, win_ref, rope_ref, after_ref, proj_ref, qt_ref, kt_ref, winb_ref,
                       h_scr, *, layer):
    j = pl.program_id(0)
    n = x_ref.shape[0]

    @pl.when(j == 0)
    def _():
        sh = mod_ref[:, 0:D_MODEL]
        sc1 = 1.0 + mod_ref[:, D_MODEL:2 * D_MODEL]
        h_scr[...] = (_rms(x_ref[...]) * (gpre_ref[pl.ds(layer, 1), :] * sc1) + sh).astype(BF16)

    w_blk = win_ref[...].astype(BF16)
    winb_ref[...] = w_blk
    proj_ref[...] = jnp.dot(h_scr[...], w_blk, preferred_element_type=F32)

    def rope_transposed(off, cos, sin, dst_ref):
        first_half, _ = _first_half_mask(n)
        for pair in range(RET_HEADS // 2):
            src = proj_ref[:, pl.ds(off + pair * LANES, LANES)]
            dst_ref[pl.ds(pair * LANES, LANES), :] = _rope_half(src, cos, sin, first_half).T

    @pl.when(j == 0)
    def _():
        rope_transposed(OFF_Q, rope_ref[0:1, :], rope_ref[1:2, :], qt_ref)
        rope_transposed(OFF_K, rope_ref[2:3, :], rope_ref[3:4, :], kt_ref)


def _sample_pre(layer, xs, mod, gpre, w_in, rope_s, after):
    n = xs.shape[0]
    cs = functools.partial(_const_spec, grid_rank=1, single_buffer=False)
    ls = functools.partial(_layer_spec, layer=layer, single_buffer=False)
    tok = pl.BlockSpec((n, None, D_MODEL), lambda j: (0, 0, 0))
    return pl.pallas_call(
        functools.partial(_sample_pre_kernel, layer=layer),
        grid=(IN_WIDTH // PRE_BLK,),
        in_specs=[tok, ls((n, 6 * D_MODEL)), cs((DEPTH, D_MODEL)),
                  pl.BlockSpec((None, D_MODEL, PRE_BLK), lambda j: (layer, 0, j)), cs(rope_s.shape),
                  pl.BlockSpec(memory_space=pl.ANY)],
        out_specs=[pl.BlockSpec((n, PRE_BLK), lambda j: (0, j)), cs((QK_WIDTH, n)), cs((QK_WIDTH, n)),
                   pl.BlockSpec((D_MODEL, PRE_BLK), lambda j: (0, j))],
        out_shape=[jax.ShapeDtypeStruct((n, IN_WIDTH), F32),
                   jax.ShapeDtypeStruct((QK_WIDTH, n), F32),
                   jax.ShapeDtypeStruct((QK_WIDTH, n), F32),
                   jax.ShapeDtypeStruct((D_MODEL, IN_WIDTH), BF16)],
        scratch_shapes=[pltpu.VMEM((n, D_MODEL), BF16)],
        compiler_params=_params(1),
        name="sample_pre",
    )(xs, mod, gpre, w_in, rope_s, after)


def _sample_state_kernel(*refs, layer):
    (proj_ref, qt_ref, kt_ref, sin_ref, bufa_ref, bufc_ref, cd_ref,
     gn_ref, caw_ref, cab_ref, lng_ref, lnb_ref, ccw_ref) = refs[:13]
    n_prev = 3 if layer > 0 else 0
    prev_states = refs[13:13 + n_prev]
    mix_ref, sout_all, outa_all, outc_all, o_scr = refs[13 + n_prev:]
    for prev_ref, all_ref in zip(prev_states, (sout_all, outa_all, outc_all)):
        all_ref[0:layer] = prev_ref[...]
    sout_ref, outa_ref, outc_ref = sout_all.at[layer], outa_all.at[layer], outc_all.at[layer]
    gn_ref, cab_ref, lng_ref, lnb_ref = (ref.at[pl.ds(layer, 1)] for ref in (gn_ref, cab_ref, lng_ref, lnb_ref))
    blk = pl.program_id(0)
    n = qt_ref.shape[1]
    lane = lax.broadcasted_iota(jnp.int32, (QK_WIDTH, n), 1)

    for bl in range(SEQ_BLK):
        onehot = lane == (blk * SEQ_BLK + bl)
        qcol = jnp.sum(jnp.where(onehot, qt_ref[...], 0.0), axis=1, keepdims=True)
        kcol = jnp.sum(jnp.where(onehot, kt_ref[...], 0.0), axis=1, keepdims=True)
        row = pl.ds(bl, 1)
        for h in range(RET_HEADS):
            vrow = proj_ref[row, pl.ds(OFF_V + h * RET_DV, RET_DV)]
            s_new = (sin_ref[bl, h] * cd_ref[h]
                     + kcol[h * RET_DK:(h + 1) * RET_DK, :] * vrow)
            sout_ref[bl, h] = s_new
            o_scr[row, pl.ds(h * RET_DV, RET_DV)] = jnp.sum(
                qcol[h * RET_DK:(h + 1) * RET_DK, :] * s_new, axis=0, keepdims=True)
    for h in range(RET_HEADS):
        cols = pl.ds(h * RET_DV, RET_DV)
        o = _layernorm(o_scr[:, cols], gn_ref[:, cols]) * _silu(proj_ref[:, pl.ds(OFF_G + h * RET_DV, RET_DV)])
        mix_ref[:, cols] = o

    u = proj_ref[:, OFF_AV:OFF_AV + CONV_A_WIDTH] * _sigmoid(proj_ref[:, OFF_AG:OFF_AG + CONV_A_WIDTH])
    ua = caw_ref[CONV_A_K - 1:CONV_A_K, :] * u
    for j in range(CONV_A_K - 1):
        ua = ua + caw_ref[j:j + 1, :] * bufa_ref[j]
    outa_ref[0:CONV_A_K - 2] = bufa_ref[1:CONV_A_K - 1]
    outa_ref[CONV_A_K - 2] = u
    mix_ref[:, MIX_A:MIX_A + CONV_A_WIDTH] = _silu(_layernorm(ua + cab_ref[...], lng_ref[...], lnb_ref[...]))

    z = proj_ref[:, OFF_CC:OFF_CC + CONV_C_WIDTH] * proj_ref[:, OFF_CX:OFF_CX + CONV_C_WIDTH]
    zc = ccw_ref[0:1, :] * bufc_ref[0] + ccw_ref[1:2, :] * bufc_ref[1] + ccw_ref[2:3, :] * z
    outc_ref[0] = bufc_ref[1]
    outc_ref[1] = z
    mix_ref[:, MIX_C:MIX_C + CONV_C_WIDTH] = proj_ref[:, OFF_CB:OFF_CB + CONV_C_WIDTH] * zc


def _sample_state(layer, prev, proj, qt, kt, s_ret, buf_a_t, buf_c_t, cd, gn, caw, cab, lng, lnb, ccw):
    n = proj.shape[0]
    cs = functools.partial(_const_spec, grid_rank=1, single_buffer=False)
    ls = functools.partial(_layer_spec, layer=layer, single_buffer=False)
    s_spec = pl.BlockSpec((None, SEQ_BLK, RET_HEADS, RET_DK, RET_DV), lambda j: (layer, j, 0, 0, 0))
    a_spec = pl.BlockSpec((None, CONV_A_K - 1, SEQ_BLK, CONV_A_WIDTH), lambda j: (layer, 0, j, 0))
    c_spec = pl.BlockSpec((None, CONV_C_K - 1, SEQ_BLK, CONV_C_WIDTH), lambda j: (layer, 0, j, 0))
    args = (proj, qt, kt, s_ret, buf_a_t, buf_c_t, cd, gn, caw, cab, lng, lnb, ccw)
    return _stacked_call(
        functools.partial(_sample_state_kernel, layer=layer),
        name="sample_state", grid=(n // SEQ_BLK,), args=args, layer=layer, prev=prev,
        stacked=[
            ((n, RET_HEADS, RET_DK, RET_DV), (SEQ_BLK, RET_HEADS, RET_DK, RET_DV), lambda j: (j, 0, 0, 0)),
            ((CONV_A_K - 1, n, CONV_A_WIDTH), (CONV_A_K - 1, SEQ_BLK, CONV_A_WIDTH), lambda j: (0, j, 0)),
            ((CONV_C_K - 1, n, CONV_C_WIDTH), (CONV_C_K - 1, SEQ_BLK, CONV_C_WIDTH), lambda j: (0, j, 0)),
        ],
        in_specs=[
            pl.BlockSpec((SEQ_BLK, IN_WIDTH), lambda j: (j, 0)),
            cs((QK_WIDTH, n)), cs((QK_WIDTH, n)),
            s_spec, a_spec, c_spec,
            cs(cd.shape), cs((DEPTH, RET_WIDTH)), ls((CONV_A_K, CONV_A_WIDTH)), cs((DEPTH, CONV_A_WIDTH)),
            cs((DEPTH, CONV_A_WIDTH)), cs((DEPTH, CONV_A_WIDTH)), ls((CONV_C_K, CONV_C_WIDTH)),
        ],
        out_specs=[pl.BlockSpec((SEQ_BLK, D_MODEL), lambda j: (j, 0))],
        out_shape=[jax.ShapeDtypeStruct((n, D_MODEL), F32)],
        scratch_shapes=[pltpu.VMEM((SEQ_BLK, RET_WIDTH), F32)])


def _sample_post_kernel(x_ref, mod_ref, mix_ref, wout_ref, gpm_ref, gpf_ref, gqf_ref, w1_ref, w3_ref, w2_ref,
                        y_ref, woutb_ref, w1b_ref, w3b_ref, w2b_ref, x1_scr, h_scr, f_scr, *, n_steps, layer):
    j = pl.program_id(0)
    gpm_ref, gpf_ref, gqf_ref = (ref.at[pl.ds(layer, 1)] for ref in (gpm_ref, gpf_ref, gqf_ref))

    @pl.when(j == 0)
    def _():
        gt_m = mod_ref[:, 2 * D_MODEL:3 * D_MODEL]
        wout_b = wout_ref[...].astype(BF16)
        woutb_ref[...] = wout_b
        mix = jnp.dot(mix_ref[...].astype(BF16), wout_b, preferred_element_type=F32)
        x1 = x_ref[...] + _rms(mix) * (gpm_ref[...] * gt_m)
        x1_scr[...] = x1
        sh = mod_ref[:, 3 * D_MODEL:4 * D_MODEL]
        sc1 = 1.0 + mod_ref[:, 4 * D_MODEL:5 * D_MODEL]
        h_scr[...] = (_rms(x1) * (gpf_ref[...] * sc1) + sh).astype(BF16)
        f_scr[...] = jnp.zeros_like(f_scr)

    w1_b, w3_b, w2_b = w1_ref[...].astype(BF16), w3_ref[...].astype(BF16), w2_ref[...].astype(BF16)
    w1b_ref[...] = w1_b
    w3b_ref[...] = w3_b
    w2b_ref[...] = w2_b
    h = h_scr[...]
    a = jnp.dot(h, w1_b, preferred_element_type=F32)
    b = jnp.dot(h, w3_b, preferred_element_type=F32)
    p = (_silu(a) * b).astype(BF16)
    f_scr[...] += jnp.dot(p, w2_b, preferred_element_type=F32)

    @pl.when(j == n_steps - 1)
    def _():
        gt_f = mod_ref[:, 5 * D_MODEL:6 * D_MODEL]
        y_ref[...] = x1_scr[...] + _rms(f_scr[...]) * (gqf_ref[...] * gt_f)


def _sample_post(layer, xs, mod, mix, w_out, gpost_m, gpre_f, gpost_f, w1, w3, w2):
    n = xs.shape[0]
    n_steps = D_FF // FF_BLK
    cs = functools.partial(_const_spec, grid_rank=1, single_buffer=False)
    ls = functools.partial(_layer_spec, layer=layer, single_buffer=False)
    tok = pl.BlockSpec((n, None, D_MODEL), lambda j: (0, 0, 0))
    return pl.pallas_call(
        functools.partial(_sample_post_kernel, n_steps=n_steps, layer=layer),
        grid=(n_steps,),
        in_specs=[
            tok, ls((n, 6 * D_MODEL)), cs((n, D_MODEL)), ls((D_MODEL, D_MODEL)),
            cs((DEPTH, D_MODEL)), cs((DEPTH, D_MODEL)), cs((DEPTH, D_MODEL)),
            pl.BlockSpec((None, D_MODEL, FF_BLK), lambda j: (layer, 0, j)),
            pl.BlockSpec((None, D_MODEL, FF_BLK), lambda j: (layer, 0, j)),
            pl.BlockSpec((None, FF_BLK, D_MODEL), lambda j: (layer, j, 0)),
        ],
        out_specs=[
            tok, cs((D_MODEL, D_MODEL)),
            pl.BlockSpec((D_MODEL, FF_BLK), lambda j: (0, j)),
            pl.BlockSpec((D_MODEL, FF_BLK), lambda j: (0, j)),
            pl.BlockSpec((FF_BLK, D_MODEL), lambda j: (j, 0)),
        ],
        out_shape=[
            jax.ShapeDtypeStruct((n, 1, D_MODEL), F32),
            jax.ShapeDtypeStruct((D_MODEL, D_MODEL), BF16),
            jax.ShapeDtypeStruct((D_MODEL, D_FF), BF16),
            jax.ShapeDtypeStruct((D_MODEL, D_FF), BF16),
            jax.ShapeDtypeStruct((D_FF, D_MODEL), BF16),
        ],
        scratch_shapes=[
            pltpu.VMEM((n, D_MODEL), F32),
            pltpu.VMEM((n, D_MODEL), BF16),
            pltpu.VMEM((n, D_MODEL), F32),
        ],
        compiler_params=_params(1),
        name="sample_post",
    )(xs, mod, mix, w_out, gpost_m, gpre_f, gpost_f, w1, w3, w2)


def kernel(x_prompt, x_sample, c_prompt, c_sample, state_ret, state_conv_a, state_conv_c, ada_w, ada_b, norm_pre_mix, norm_post_mix, norm_pre_ffn, norm_post_ffn, w_in, w_out, ret_gn_g, conv_a_w, conv_a_b, conv_a_ln_g, conv_a_ln_b, conv_c_w, ffn_w1, ffn_w3, ffn_w2):
    bp, lp, _ = x_prompt.shape
    ns = x_sample.shape[0]
    assert x_sample.shape[1] == 1 and ns % SEQ_BLK == 0
    assert OFF_K + QK_WIDTH <= PRE_BLK and IN_WIDTH % PRE_BLK == 0
    assert lp % TOK_TILE == 0 and TOK_TILE % RET_CHUNK == 0
    assert TOK_TILE % CONV_BLK == 0 and CONV_BLK % (SUBLANES * CONV_STRIDE) == 0

    k_scale = RET_DK ** -0.5
    cq, sq = _rope_tables(np.arange(lp), 1.0)
    ck, sk = _rope_tables(np.arange(lp), k_scale)
    tables_p = (cq, sq, ck, sk) + _decay_tables(RET_CHUNK)
    cqs, sqs = _rope_tables([PAST_LEN], 1.0)
    cks, sks = _rope_tables([PAST_LEN], k_scale)
    rope_s = np.concatenate([cqs, sqs, cks, sks, np.zeros((4, LANES), np.float32)], axis=0)
    cd_s = _decay_tables(1)[4]

    mod_s, mod_p = _ada_modulation(c_sample, c_prompt, ada_w, ada_b)

    gn, cab, lng, lnb = ret_gn_g, conv_a_b, conv_a_ln_g, conv_a_ln_b
    g_pm, g_qm, g_pf, g_qf = norm_pre_mix, norm_post_mix, norm_pre_ffn, norm_post_ffn
    conv_a_t = jnp.transpose(state_conv_a, (0, 2, 1, 3))
    conv_c_t = jnp.transpose(state_conv_c, (0, 2, 1, 3))

    yp = x_prompt
    ys = x_sample
    st_p = st_s = ()
    for l in range(DEPTH):
        proj, qt, kt, w_in_b = _sample_pre(l, ys, mod_s, g_pm, w_in, rope_s, yp)
        mix, *st_s = _sample_state(l, st_s, proj, qt, kt, state_ret, conv_a_t, conv_c_t, cd_s,
                                   gn, conv_a_w, cab, lng, lnb, conv_c_w)
        ys, w_out_b, w1_b, w3_b, w2_b = _sample_post(l, ys, mod_s, mix, w_out, g_qm, g_pf, g_qf,
                                                    ffn_w1, ffn_w3, ffn_w2)
        yp, *st_p = _prompt_layer(l, st_p, yp, mod_p, g_pm, g_qm, g_pf, g_qf, w_in_b, w_out_b, w1_b, w3_b, w2_b,
                                  tables_p, gn, conv_a_w, cab, lng, lnb, conv_c_w)

    sret_s, sca_s, scc_s = st_s
    sca_s = jnp.transpose(sca_s, (0, 2, 1, 3))
    scc_s = jnp.transpose(scc_s, (0, 2, 1, 3))
    return (yp, ys) + tuple(st_p) + (sret_s, sca_s, scc_s)
```

```python
import functools

import numpy as np
import jax
import jax.numpy as jnp
from jax import lax
from jax.experimental import pallas as pl
from jax.experimental.pallas import tpu as pltpu

D_MODEL = 1024
DEPTH = 2
PAST_LEN = 16384
RET_HEADS = 4
RET_WIDTH = D_MODEL // 2
RET_DV = RET_WIDTH // RET_HEADS
RET_DK = RET_DV // 2
QK_WIDTH = RET_HEADS * RET_DK
CONV_A_WIDTH = D_MODEL // 4
CONV_A_K = 31
CONV_C_WIDTH = D_MODEL - RET_WIDTH - CONV_A_WIDTH
CONV_C_K = 3
IN_WIDTH = 2 * QK_WIDTH + 2 * RET_WIDTH + 2 * CONV_A_WIDTH + 3 * CONV_C_WIDTH
D_FF = ((8 * D_MODEL // 3 + 255) // 256) * 256
RET_CHUNK = 128
ROPE_BASE = 10000.0
EPS = 1e-6

OFF_Q = 0
OFF_K = OFF_Q + QK_WIDTH
OFF_V = OFF_K + QK_WIDTH
OFF_G = OFF_V + RET_WIDTH
OFF_AV = OFF_G + RET_WIDTH
OFF_AG = OFF_AV + CONV_A_WIDTH
OFF_CB = OFF_AG + CONV_A_WIDTH
OFF_CC = OFF_CB + CONV_C_WIDTH
OFF_CX = OFF_CC + CONV_C_WIDTH
MIX_A = RET_WIDTH
MIX_C = RET_WIDTH + CONV_A_WIDTH

LANES = 128
SUBLANES = 8
V7X_VMEM_LIMIT_BYTES = 60 * 1024 * 1024

TOK_TILE = 512
ROW_BLK = 32
CONV_BLK = 64
CONV_STRIDE = 4
HIST_A = 32
HIST_C = 8
SEQ_BLK = 16
FF_BLK = 256
IN_BLK = 256
PRE_BLK = 1408

F32 = jnp.float32
BF16 = jnp.bfloat16


def _sigmoid(x):
    return jax.nn.sigmoid(x)


def _silu(x):
    return x * _sigmoid(x)


def _rms(x):
    return x * lax.rsqrt(jnp.mean(x * x, axis=-1, keepdims=True) + EPS)


def _layernorm(x, g, b=None):
    mu = jnp.mean(x, axis=-1, keepdims=True)
    d = x - mu
    var = jnp.mean(d * d, axis=-1, keepdims=True)
    y = d * lax.rsqrt(var + EPS) * g
    return y if b is None else y + b


def _rope_half(x, cos, sin, first_half):
    partner = jnp.where(first_half, pltpu.roll(x, 96, 1), pltpu.roll(x, 32, 1))
    return x * cos + partner * sin


def _first_half_mask(rows):
    lane = lax.broadcasted_iota(jnp.int32, (rows, LANES), 1)
    return (lane & (RET_DK - 1)) < (RET_DK // 2), lane < RET_DK


def _swiglu_block(h, w1_ref, w3_ref, col0):
    cols = pl.ds(col0, FF_BLK)
    a = jnp.dot(h, w1_ref[:, cols], preferred_element_type=F32).astype(BF16)
    b = jnp.dot(h, w3_ref[:, cols], preferred_element_type=F32).astype(BF16)
    return _silu(a) * b


def _emit(items):
    for item in items:
        item()


def _emit_interleaved(primary, secondary):
    n, m = len(primary), len(secondary)
    done = 0
    for i, item in enumerate(primary):
        item()
        upto = ((i + 1) * m) // n
        _emit(secondary[done:upto])
        done = upto


def _rope_tables(pos, k_scale):
    half = RET_DK // 2
    inv = ROPE_BASE ** (-np.arange(half, dtype=np.float64) / half)
    ang = np.asarray(pos, np.float64)[:, None] * inv[None, :]
    cos = np.tile(np.cos(ang), (1, 4))
    sin = np.tile(np.concatenate([-np.sin(ang), np.sin(ang)], axis=1), (1, 2))
    return (cos * k_scale).astype(np.float32), (sin * k_scale).astype(np.float32)


def _decay_tables(chunk):
    log_g = np.log(1.0 - np.exp2(-5.0 - np.arange(RET_HEADS, dtype=np.float64)))
    idx = np.arange(chunk, dtype=np.float64)
    diff = idx[:, None] - idx[None, :]
    dmat = np.where(diff[None] >= 0, np.exp(np.maximum(diff, 0.0)[None] * log_g[:, None, None]), 0.0)
    read_dec = np.exp((idx + 1.0)[:, None] * log_g[None, :])
    upd_dec = np.exp((chunk - 1.0 - idx)[:, None] * log_g[None, :])
    chunk_dec = np.exp(chunk * log_g)
    pairs = RET_HEADS // 2
    side_by_side = lambda a: a.reshape(pairs, 2, *a.shape[1:]).transpose(0, 2, 1, 3).reshape(pairs, a.shape[1], -1)
    dmat2 = side_by_side(dmat)
    rd2 = side_by_side(np.broadcast_to(read_dec.T[:, :, None], (RET_HEADS, chunk, RET_DV)))
    updt = np.repeat(upd_dec.T, RET_DK, axis=0).reshape(pairs, 2 * RET_DK, chunk)
    cd = np.broadcast_to(chunk_dec[:, None, None], (RET_HEADS, 1, RET_DV))
    cd2 = side_by_side(cd)
    f = lambda a: np.ascontiguousarray(a, dtype=np.float32)
    return f(dmat2), f(rd2), f(updt), f(cd2), f(cd)


def _const_spec(shape, grid_rank, single_buffer=True):
    zeros = (0,) * len(shape)
    idx = (lambda b, c: zeros) if grid_rank == 2 else (lambda j: zeros)
    if single_buffer:
        return pl.BlockSpec(shape, idx, pipeline_mode=pl.Buffered(1))
    return pl.BlockSpec(shape, idx)


def _layer_spec(shape, layer, single_buffer=True):
    zeros = (0,) * len(shape)
    mode = dict(pipeline_mode=pl.Buffered(1)) if single_buffer else {}
    return pl.BlockSpec((None,) + tuple(shape), lambda j: (layer,) + zeros, **mode)


def _params(n_axes):
    return pltpu.CompilerParams(dimension_semantics=("arbitrary",) * n_axes,
                                vmem_limit_bytes=V7X_VMEM_LIMIT_BYTES)


def _stacked_call(kern, *, name, grid, in_specs, args, out_specs, out_shape, layer, stacked, prev, scratch_shapes):
    def spec(n_layers, block, index_fn):
        return pl.BlockSpec((n_layers,) + tuple(block), lambda *g: (0,) + tuple(index_fn(*g)))

    in_specs, out_specs, out_shape, args = list(in_specs), list(out_specs), list(out_shape), list(args)
    for dims, block, index_fn in stacked:
        out_specs.append(spec(layer + 1, block, index_fn))
        out_shape.append(jax.ShapeDtypeStruct((layer + 1,) + tuple(dims), F32))
        if layer > 0:
            in_specs.append(spec(layer, block, index_fn))
    if layer > 0:
        args += list(prev)
    return pl.pallas_call(
        kern, grid=grid, in_specs=in_specs, out_specs=out_specs, out_shape=out_shape,
        scratch_shapes=scratch_shapes, compiler_params=_params(len(grid)), name=name)(*args)


def _ada_kernel(cs_ref, cp_ref, w_ref, b_ref, os_ref, op_ref):
    layer = pl.program_id(0)
    bias = b_ref[0:1, :]
    for d in range(1, DEPTH):
        bias = jnp.where(layer == d, b_ref[d:d + 1, :], bias)
    w = w_ref[...].astype(BF16)
    os_ref[...] = jnp.dot(_silu(cs_ref[...]).astype(BF16), w, preferred_element_type=F32) + bias
    mod_p = jnp.dot(_silu(cp_ref[...]).astype(BF16), w, preferred_element_type=F32) + bias
    for b in range(cp_ref.shape[0]):
        op_ref[b] = mod_p[b:b + 1, :]


def _ada_modulation(c_sample, c_prompt, ada_w, ada_b):
    ns, bp = c_sample.shape[0], c_prompt.shape[0]
    ncol = 6 * D_MODEL
    blk = 2 * D_MODEL
    return pl.pallas_call(
        _ada_kernel,
        grid=(DEPTH, ncol // blk),
        in_specs=[
            pl.BlockSpec((ns, D_MODEL), lambda l, j: (0, 0)),
            pl.BlockSpec((bp, D_MODEL), lambda l, j: (0, 0)),
            pl.BlockSpec((None, D_MODEL, blk), lambda l, j: (l, 0, j)),
            pl.BlockSpec((DEPTH, blk), lambda l, j: (0, j)),
        ],
        out_specs=[pl.BlockSpec((None, ns, blk), lambda l, j: (l, 0, j)),
                   pl.BlockSpec((None, bp, 1, blk), lambda l, j: (l, 0, 0, j))],
        out_shape=[jax.ShapeDtypeStruct((DEPTH, ns, ncol), F32),
                   jax.ShapeDtypeStruct((DEPTH, bp, 1, ncol), F32)],
        compiler_params=_params(2),
        name="ada_mod",
    )(c_sample, c_prompt, ada_w, ada_b)


def _prompt_layer_kernel(*refs, tile, n_tiles, tiles_per_seq, layer):
    (xc_ref, xp_ref, modc_ref, modp_ref, gpm_ref, gqm_ref, gpf_ref, gqf_ref,
     win_ref, wout_ref, w1_ref, w3_ref, w2_ref,
     cq_ref, sq_ref, ck_ref, sk_ref, dmat_ref, rd_ref, upd_ref, cd_ref,
     gn_ref, caw_ref, cab_ref, lng_ref, lnb_ref, ccw_ref) = refs[:27]
    n_prev = 3 if layer > 0 else 0
    prev_states = refs[27:27 + n_prev]
    (y_ref, sret_all, sca_all, scc_all,
     h_scr, proj_scr, mixin_scr, mix_scr, hf_scr, p_scr, ubuf, ua_scr, zbuf,
     sc_scr, qs_scr, sbd_scr) = refs[27 + n_prev:]
    sret_ref, sca_ref, scc_ref = sret_all.at[layer], sca_all.at[layer], scc_all.at[layer]
    gpm_ref, gqm_ref, gpf_ref, gqf_ref, gn_ref, cab_ref, lng_ref, lnb_ref = (
        ref.at[pl.ds(layer, 1)] for ref in (gpm_ref, gqm_ref, gpf_ref, gqf_ref, gn_ref, cab_ref, lng_ref, lnb_ref))
    mod_chunk = lambda ref, k: ref[:, pl.ds(k * D_MODEL, D_MODEL)]
    step = pl.program_id(0)
    row_blocks = [pl.ds(r * ROW_BLK, ROW_BLK) for r in range(tile // ROW_BLK)]
    first_tap = HIST_A - (CONV_A_K - 1)
    first_tap_c = HIST_C - (CONV_C_K - 1)
    span = SUBLANES * CONV_STRIDE

    @pl.when((step < n_tiles) & (lax.rem(step, tiles_per_seq) == 0))
    def _():
        for prev_ref, all_ref in zip(prev_states, (sret_all, sca_all, scc_all)):
            all_ref[0:layer] = prev_ref[...]
        sret_ref[...] = jnp.zeros_like(sret_ref)
        sbd_scr[...] = jnp.zeros_like(sbd_scr)
        ubuf[:, 0:HIST_A, :] = jnp.zeros((2, HIST_A, LANES), F32)
        zbuf[0:HIST_C, :] = jnp.zeros((HIST_C, CONV_C_WIDTH), F32)

    def mix_prenorm():
        sh = mod_chunk(modc_ref, 0)
        pre_scale = gpm_ref[...] * (1.0 + mod_chunk(modc_ref, 1))
        for rows in row_blocks:
            h_scr[rows, :] = (_rms(xc_ref[rows, :]) * pre_scale + sh).astype(BF16)

    def in_proj(n):
        cols = pl.ds(n * IN_BLK, IN_BLK)
        proj_scr[:, cols] = jnp.dot(h_scr[...], win_ref[:, cols], preferred_element_type=F32)

    def conv_groups():
        taps = [[jnp.broadcast_to(caw_ref[j:j + 1, pl.ds(half * LANES, LANES)], (SUBLANES, LANES))
                 for j in range(CONV_A_K)] for half in range(2)]
        for r in range(tile // CONV_BLK):
            rows = pl.ds(r * CONV_BLK, CONV_BLK)
            dst = pl.ds(HIST_A + r * CONV_BLK, CONV_BLK)
            for half in range(2):
                lo = half * LANES
                u = (proj_scr[rows, pl.ds(OFF_AV + lo, LANES)]
                     * _sigmoid(proj_scr[rows, pl.ds(OFF_AG + lo, LANES)]))
                ubuf[half, dst, :] = u
        for half in range(2):
            for m in range(tile // span):
                base = m * span
                acc = [None] * CONV_STRIDE
                for k in range(CONV_STRIDE + CONV_A_K - 1):
                    win = ubuf[half, pl.ds(base + first_tap + k, SUBLANES, stride=CONV_STRIDE), :]
                    for t in range(CONV_STRIDE):
                        j = k - t
                        if 0 <= j < CONV_A_K:
                            term = taps[half][j] * win
                            acc[t] = term if acc[t] is None else acc[t] + term
                for t in range(CONV_STRIDE):
                    ua_scr[half, pl.ds(base + t, SUBLANES, stride=CONV_STRIDE), :] = acc[t]
        for r in range(tile // CONV_BLK):
            rows = pl.ds(r * CONV_BLK, CONV_BLK)
            ua = jnp.concatenate([ua_scr[0, rows, :], ua_scr[1, rows, :]], axis=-1) + cab_ref[...]
            o_a = _silu(_layernorm(ua, lng_ref[...], lnb_ref[...]))
            mixin_scr[rows, MIX_A:MIX_A + CONV_A_WIDTH] = o_a.astype(BF16)
        for r in range(tile // CONV_BLK):
            rows = pl.ds(r * CONV_BLK, CONV_BLK)
            z = proj_scr[rows, OFF_CC:OFF_CC + CONV_C_WIDTH] * proj_scr[rows, OFF_CX:OFF_CX + CONV_C_WIDTH]
            zbuf[pl.ds(HIST_C + r * CONV_BLK, CONV_BLK), :] = z
        for r in range(tile // CONV_BLK):
            rows = pl.ds(r * CONV_BLK, CONV_BLK)
            zc = ccw_ref[0:1, :] * zbuf[pl.ds(r * CONV_BLK + first_tap_c, CONV_BLK), :]
            for j in range(1, CONV_C_K):
                zc = zc + ccw_ref[j:j + 1, :] * zbuf[pl.ds(r * CONV_BLK + first_tap_c + j, CONV_BLK), :]
            o_c = proj_scr[rows, OFF_CB:OFF_CB + CONV_C_WIDTH] * zc
            mixin_scr[rows, MIX_C:MIX_C + CONV_C_WIDTH] = o_c.astype(BF16)
        hist_a = pl.ds(tile + first_tap, CONV_A_K - 1)
        sca_ref[...] = jnp.concatenate([ubuf[0, hist_a, :], ubuf[1, hist_a, :]], axis=-1)
        scc_ref[...] = zbuf[pl.ds(tile + first_tap_c, CONV_C_K - 1), :]
        ubuf[:, 0:HIST_A, :] = ubuf[:, pl.ds(tile, HIST_A), :]
        zbuf[0:HIST_C, :] = zbuf[pl.ds(tile, HIST_C), :]

    def retention_scores(i, pair):
        first_half, _ = _first_half_mask(RET_CHUNK)
        rows = pl.ds(i * RET_CHUNK, RET_CHUNK)
        cq, sq, ck, sk = cq_ref[rows, :], sq_ref[rows, :], ck_ref[rows, :], sk_ref[rows, :]
        qr = _rope_half(proj_scr[rows, pl.ds(OFF_Q + pair * LANES, LANES)], cq, sq, first_half)
        kr = _rope_half(proj_scr[rows, pl.ds(OFF_K + pair * LANES, LANES)], ck, sk, first_half)
        qr_b = qr.astype(BF16)
        k_t = kr.T
        head0_rows = lax.broadcasted_iota(jnp.int32, (2 * RET_DK, RET_CHUNK), 0) < RET_DK
        k_bd = jnp.concatenate([jnp.where(head0_rows, k_t, 0.0), jnp.where(head0_rows, 0.0, k_t)], axis=1)
        scores = jnp.dot(qr_b, k_bd.astype(BF16), preferred_element_type=F32) * dmat_ref[pair]
        sc_scr[...] = scores.astype(BF16)
        s_bd = sbd_scr[pair]
        qs_scr[...] = jnp.dot(qr_b, s_bd.astype(BF16), preferred_element_type=F32) * rd_ref[pair]
        v_pair = proj_scr[rows, pl.ds(OFF_V + 2 * pair * RET_DV, 2 * RET_DV)].astype(BF16)
        kv = jnp.dot((k_t * upd_ref[pair]).astype(BF16), v_pair, preferred_element_type=F32)
        r0 = lax.broadcasted_iota(jnp.int32, (2 * RET_DK, 2 * RET_DV), 0) < RET_DK
        c0 = lax.broadcasted_iota(jnp.int32, (2 * RET_DK, 2 * RET_DV), 1) < RET_DV
        s_new = s_bd * cd_ref[pair] + jnp.where(r0 == c0, kv, 0.0)
        sbd_scr[pair] = s_new
        sret_ref[2 * pair] = s_new[0:RET_DK, 0:RET_DV]
        sret_ref[2 * pair + 1] = s_new[RET_DK:2 * RET_DK, RET_DV:2 * RET_DV]

    def retention_values(i, pair):
        rows = pl.ds(i * RET_CHUNK, RET_CHUNK)
        for hl in range(2):
            h = 2 * pair + hl
            vh = proj_scr[rows, pl.ds(OFF_V + h * RET_DV, RET_DV)].astype(BF16)
            o = (jnp.dot(sc_scr[:, pl.ds(hl * RET_CHUNK, RET_CHUNK)], vh, preferred_element_type=F32)
                 + qs_scr[:, pl.ds(hl * RET_DV, RET_DV)])
            gate = proj_scr[rows, pl.ds(OFF_G + h * RET_DV, RET_DV)]
            o = _layernorm(o, gn_ref[:, pl.ds(h * RET_DV, RET_DV)]) * _silu(gate)
            mixin_scr[rows, pl.ds(h * RET_DV, RET_DV)] = o.astype(BF16)

    def out_proj():
        mix_scr[...] = jnp.dot(mixin_scr[...], wout_ref[...], preferred_element_type=F32)

    in_tiles = [functools.partial(in_proj, n) for n in range(IN_WIDTH // IN_BLK)]
    ret_items = [functools.partial(phase, i, pair)
                 for i in range(tile // RET_CHUNK) for pair in range(RET_HEADS // 2)
                 for phase in (retention_scores, retention_values)]

    def ffn_norms():
        post_scale = gqm_ref[...] * mod_chunk(modp_ref, 2)
        sh = mod_chunk(modp_ref, 3)
        pre_scale = gpf_ref[...] * (1.0 + mod_chunk(modp_ref, 4))
        for rows in row_blocks:
            x1 = xp_ref[rows, :] + _rms(mix_scr[rows, :]) * post_scale
            y_ref[rows, :] = x1
            hf_scr[rows, :] = (_rms(x1) * pre_scale + sh).astype(BF16)

    def hidden(j):
        p_scr[:, pl.ds(j * FF_BLK, FF_BLK)] = _swiglu_block(hf_scr[...], w1_ref, w3_ref, j * FF_BLK)

    def down(n):
        cols = pl.ds(n * FF_BLK, FF_BLK)
        proj_scr[:, cols] = jnp.dot(p_scr[...], w2_ref[:, cols], preferred_element_type=F32)

    def ffn_finish():
        post_scale = gqf_ref[...] * mod_chunk(modp_ref, 5)
        for rows in row_blocks:
            y_ref[rows, :] = y_ref[rows, :] + _rms(proj_scr[rows, 0:D_MODEL]) * post_scale

    hidden_blocks = [functools.partial(hidden, j) for j in range(D_FF // FF_BLK)]
    down_tiles = [functools.partial(down, n) for n in range(D_MODEL // FF_BLK)]

    @pl.when(step == 0)
    def _():
        mix_prenorm()
        _emit(in_tiles)
        _emit(ret_items)
        conv_groups()
        out_proj()

    @pl.when((step > 0) & (step < n_tiles))
    def _():
        mix_prenorm()
        ffn_norms()
        _emit(in_tiles)
        _emit_interleaved(hidden_blocks, ret_items)
        conv_groups()
        _emit(down_tiles)
        out_proj()
        ffn_finish()

    @pl.when(step == n_tiles)
    def _():
        ffn_norms()
        _emit(hidden_blocks)
        _emit(down_tiles)
        ffn_finish()


def _prompt_layer(layer, prev, x, mod_p, g_pm, g_qm, g_pf, g_qf, w_in, w_out, w1, w3, w2, tables,
                  gn, caw, cab, lng, lnb, ccw):
    bsz, seq, _ = x.shape
    tile = TOK_TILE
    tps = seq // tile
    n_tiles = bsz * tps
    cq, sq, ck, sk, dmat, rd, upd, cd, _ = tables
    cur = lambda i: jnp.minimum(i, n_tiles - 1)
    prv = lambda i: jnp.maximum(i - 1, 0)
    tok_cur = pl.BlockSpec((None, tile, D_MODEL), lambda i: (cur(i) // tps, cur(i) % tps, 0))
    tok_prev = pl.BlockSpec((None, tile, D_MODEL), lambda i: (prv(i) // tps, prv(i) % tps, 0))
    mod_cur = pl.BlockSpec((None, None, 1, 6 * D_MODEL), lambda i: (layer, cur(i) // tps, 0, 0))
    mod_prev = pl.BlockSpec((None, None, 1, 6 * D_MODEL), lambda i: (layer, prv(i) // tps, 0, 0))
    rope = pl.BlockSpec((tile, LANES), lambda i: (cur(i) % tps, 0))
    cs = functools.partial(_const_spec, grid_rank=1)
    ls = functools.partial(_layer_spec, layer=layer)
    args = (x, x, mod_p, mod_p, g_pm, g_qm, g_pf, g_qf, w_in, w_out, w1, w3, w2,
            cq, sq, ck, sk, dmat, rd, upd, cd, gn, caw, cab, lng, lnb, ccw)
    seq_block = lambda i: (cur(i) // tps,)
    return _stacked_call(
        functools.partial(_prompt_layer_kernel, tile=tile, n_tiles=n_tiles, tiles_per_seq=tps, layer=layer),
        name="prompt_layer", grid=(n_tiles + 1,), args=args, layer=layer, prev=prev,
        stacked=[
            ((bsz, RET_HEADS, RET_DK, RET_DV), (None, RET_HEADS, RET_DK, RET_DV),
             lambda i: seq_block(i) + (0, 0, 0)),
            ((bsz, CONV_A_K - 1, CONV_A_WIDTH), (None, CONV_A_K - 1, CONV_A_WIDTH),
             lambda i: seq_block(i) + (0, 0)),
            ((bsz, CONV_C_K - 1, CONV_C_WIDTH), (None, CONV_C_K - 1, CONV_C_WIDTH),
             lambda i: seq_block(i) + (0, 0)),
        ],
        in_specs=[
            tok_cur, tok_prev, mod_cur, mod_prev,
            cs((DEPTH, D_MODEL)), cs((DEPTH, D_MODEL)), cs((DEPTH, D_MODEL)), cs((DEPTH, D_MODEL)),
            cs((D_MODEL, IN_WIDTH)), cs((D_MODEL, D_MODEL)),
            cs((D_MODEL, D_FF)), cs((D_MODEL, D_FF)), cs((D_FF, D_MODEL)),
            rope, rope, rope, rope,
            cs(dmat.shape), cs(rd.shape), cs(upd.shape), cs(cd.shape),
            cs((DEPTH, RET_WIDTH)), ls((CONV_A_K, CONV_A_WIDTH)), cs((DEPTH, CONV_A_WIDTH)),
            cs((DEPTH, CONV_A_WIDTH)), cs((DEPTH, CONV_A_WIDTH)), ls((CONV_C_K, CONV_C_WIDTH)),
        ],
        out_specs=[tok_prev],
        out_shape=[jax.ShapeDtypeStruct(x.shape, F32)],
        scratch_shapes=[
            pltpu.VMEM((tile, D_MODEL), BF16),
            pltpu.VMEM((tile, IN_WIDTH), F32),
            pltpu.VMEM((tile, D_MODEL), BF16),
            pltpu.VMEM((tile, D_MODEL), F32),
            pltpu.VMEM((tile, D_MODEL), BF16),
            pltpu.VMEM((tile, D_FF), BF16),
            pltpu.VMEM((2, tile + HIST_A, LANES), F32),
            pltpu.VMEM((2, tile, LANES), F32),
            pltpu.VMEM((tile + HIST_C, CONV_C_WIDTH), F32),
            pltpu.VMEM((RET_CHUNK, 2 * RET_CHUNK), BF16),
            pltpu.VMEM((RET_CHUNK, 2 * RET_DV), F32),
            pltpu.VMEM((RET_HEADS // 2, 2 * RET_DK, 2 * RET_DV), F32),
        ])


def _sample_pre_kernel(x_ref, mod_ref, gpre_ref, win_ref, rope_ref, after_ref, proj_ref, qt_ref, kt_ref, winb_ref,
                       h_scr, *, layer):
    j = pl.program_id(0)
    n = x_ref.shape[0]

    @pl.when(j == 0)
    def _():
        sh = mod_ref[:, 0:D_MODEL]
        sc1 = 1.0 + mod_ref[:, D_MODEL:2 * D_MODEL]
        h_scr[...] = (_rms(x_ref[...]) * (gpre_ref[pl.ds(layer, 1), :] * sc1) + sh).astype(BF16)

    w_blk = win_ref[...].astype(BF16)
    winb_ref[...] = w_blk
    proj_ref[...] = jnp.dot(h_scr[...], w_blk, preferred_element_type=F32)

    def rope_transposed(off, cos, sin, dst_ref):
        first_half, _ = _first_half_mask(n)
        for pair in range(RET_HEADS // 2):
            src = proj_ref[:, pl.ds(off + pair * LANES, LANES)]
            dst_ref[pl.ds(pair * LANES, LANES), :] = _rope_half(src, cos, sin, first_half).T

    @pl.when(j == 0)
    def _():
        rope_transposed(OFF_Q, rope_ref[0:1, :], rope_ref[1:2, :], qt_ref)
        rope_transposed(OFF_K, rope_ref[2:3, :], rope_ref[3:4, :], kt_ref)


def _sample_pre(layer, xs, mod, gpre, w_in, rope_s, after):
    n = xs.shape[0]
    cs = functools.partial(_const_spec, grid_rank=1, single_buffer=False)
    ls = functools.partial(_layer_spec, layer=layer, single_buffer=False)
    return pl.pallas_call(
        functools.partial(_sample_pre_kernel, layer=layer),
        grid=(IN_WIDTH // PRE_BLK,),
        in_specs=[cs((n, D_MODEL)), ls((n, 6 * D_MODEL)), cs((DEPTH, D_MODEL)),
                  pl.BlockSpec((None, D_MODEL, PRE_BLK), lambda j: (layer, 0, j)), cs(rope_s.shape),
                  pl.BlockSpec(memory_space=pl.ANY)],
        out_specs=[pl.BlockSpec((n, PRE_BLK), lambda j: (0, j)), cs((QK_WIDTH, n)), cs((QK_WIDTH, n)),
                   pl.BlockSpec((D_MODEL, PRE_BLK), lambda j: (0, j))],
        out_shape=[jax.ShapeDtypeStruct((n, IN_WIDTH), F32),
                   jax.ShapeDtypeStruct((QK_WIDTH, n), F32),
                   jax.ShapeDtypeStruct((QK_WIDTH, n), F32),
                   jax.ShapeDtypeStruct((D_MODEL, IN_WIDTH), BF16)],
        scratch_shapes=[pltpu.VMEM((n, D_MODEL), BF16)],
        compiler_params=_params(1),
        name="sample_pre",
    )(xs, mod, gpre, w_in, rope_s, after)


def _sample_state_kernel(*refs, layer):
    (proj_ref, qt_ref, kt_ref, sin_ref, bufa_ref, bufc_ref, cd_ref,
     gn_ref, caw_ref, cab_ref, lng_ref, lnb_ref, ccw_ref) = refs[:13]
    n_prev = 3 if layer > 0 else 0
    prev_states = refs[13:13 + n_prev]
    mix_ref, sout_all, outa_all, outc_all, o_scr = refs[13 + n_prev:]
    for prev_ref, all_ref in zip(prev_states, (sout_all, outa_all, outc_all)):
        all_ref[0:layer] = prev_ref[...]
    sout_ref, outa_ref, outc_ref = sout_all.at[layer], outa_all.at[layer], outc_all.at[layer]
    gn_ref, cab_ref, lng_ref, lnb_ref = (ref.at[pl.ds(layer, 1)] for ref in (gn_ref, cab_ref, lng_ref, lnb_ref))
    blk = pl.program_id(0)
    n = qt_ref.shape[1]
    lane = lax.broadcasted_iota(jnp.int32, (QK_WIDTH, n), 1)

    for bl in range(SEQ_BLK):
        onehot = lane == (blk * SEQ_BLK + bl)
        qcol = jnp.sum(jnp.where(onehot, qt_ref[...], 0.0), axis=1, keepdims=True)
        kcol = jnp.sum(jnp.where(onehot, kt_ref[...], 0.0), axis=1, keepdims=True)
        row = pl.ds(bl, 1)
        for h in range(RET_HEADS):
            vrow = proj_ref[row, pl.ds(OFF_V + h * RET_DV, RET_DV)]
            s_new = (sin_ref[bl, h] * cd_ref[h]
                     + kcol[h * RET_DK:(h + 1) * RET_DK, :] * vrow)
            sout_ref[bl, h] = s_new
            o_scr[row, pl.ds(h * RET_DV, RET_DV)] = jnp.sum(
                qcol[h * RET_DK:(h + 1) * RET_DK, :] * s_new, axis=0, keepdims=True)
    for h in range(RET_HEADS):
        cols = pl.ds(h * RET_DV, RET_DV)
        o = _layernorm(o_scr[:, cols], gn_ref[:, cols]) * _silu(proj_ref[:, pl.ds(OFF_G + h * RET_DV, RET_DV)])
        mix_ref[:, cols] = o

    u = proj_ref[:, OFF_AV:OFF_AV + CONV_A_WIDTH] * _sigmoid(proj_ref[:, OFF_AG:OFF_AG + CONV_A_WIDTH])
    ua = caw_ref[CONV_A_K - 1:CONV_A_K, :] * u
    for j in range(CONV_A_K - 1):
        ua = ua + caw_ref[j:j + 1, :] * bufa_ref[j]
    outa_ref[0:CONV_A_K - 2] = bufa_ref[1:CONV_A_K - 1]
    outa_ref[CONV_A_K - 2] = u
    mix_ref[:, MIX_A:MIX_A + CONV_A_WIDTH] = _silu(_layernorm(ua + cab_ref[...], lng_ref[...], lnb_ref[...]))

    z = proj_ref[:, OFF_CC:OFF_CC + CONV_C_WIDTH] * proj_ref[:, OFF_CX:OFF_CX + CONV_C_WIDTH]
    zc = ccw_ref[0:1, :] * bufc_ref[0] + ccw_ref[1:2, :] * bufc_ref[1] + ccw_ref[2:3, :] * z
    outc_ref[0] = bufc_ref[1]
    outc_ref[1] = z
    mix_ref[:, MIX_C:MIX_C + CONV_C_WIDTH] = proj_ref[:, OFF_CB:OFF_CB + CONV_C_WIDTH] * zc


def _sample_state(layer, prev, proj, qt, kt, s_ret, buf_a_t, buf_c_t, cd, gn, caw, cab, lng, lnb, ccw):
    n = proj.shape[0]
    cs = functools.partial(_const_spec, grid_rank=1, single_buffer=False)
    ls = functools.partial(_layer_spec, layer=layer, single_buffer=False)
    s_spec = pl.BlockSpec((None, SEQ_BLK, RET_HEADS, RET_DK, RET_DV), lambda j: (layer, j, 0, 0, 0))
    a_spec = pl.BlockSpec((None, CONV_A_K - 1, SEQ_BLK, CONV_A_WIDTH), lambda j: (layer, 0, j, 0))
    c_spec = pl.BlockSpec((None, CONV_C_K - 1, SEQ_BLK, CONV_C_WIDTH), lambda j: (layer, 0, j, 0))
    args = (proj, qt, kt, s_ret, buf_a_t, buf_c_t, cd, gn, caw, cab, lng, lnb, ccw)
    return _stacked_call(
        functools.partial(_sample_state_kernel, layer=layer),
        name="sample_state", grid=(n // SEQ_BLK,), args=args, layer=layer, prev=prev,
        stacked=[
            ((n, RET_HEADS, RET_DK, RET_DV), (SEQ_BLK, RET_HEADS, RET_DK, RET_DV), lambda j: (j, 0, 0, 0)),
            ((CONV_A_K - 1, n, CONV_A_WIDTH), (CONV_A_K - 1, SEQ_BLK, CONV_A_WIDTH), lambda j: (0, j, 0)),
            ((CONV_C_K - 1, n, CONV_C_WIDTH), (CONV_C_K - 1, SEQ_BLK, CONV_C_WIDTH), lambda j: (0, j, 0)),
        ],
        in_specs=[
            pl.BlockSpec((SEQ_BLK, IN_WIDTH), lambda j: (j, 0)),
            cs((QK_WIDTH, n)), cs((QK_WIDTH, n)),
            s_spec, a_spec, c_spec,
            cs(cd.shape), cs((DEPTH, RET_WIDTH)), ls((CONV_A_K, CONV_A_WIDTH)), cs((DEPTH, CONV_A_WIDTH)),
            cs((DEPTH, CONV_A_WIDTH)), cs((DEPTH, CONV_A_WIDTH)), ls((CONV_C_K, CONV_C_WIDTH)),
        ],
        out_specs=[pl.BlockSpec((SEQ_BLK, D_MODEL), lambda j: (j, 0))],
        out_shape=[jax.ShapeDtypeStruct((n, D_MODEL), F32)],
        scratch_shapes=[pltpu.VMEM((SEQ_BLK, RET_WIDTH), F32)])


def _sample_post_kernel(x_ref, mod_ref, mix_ref, wout_ref, gpm_ref, gpf_ref, gqf_ref, w1_ref, w3_ref, w2_ref,
                        y_ref, woutb_ref, w1b_ref, w3b_ref, w2b_ref, x1_scr, h_scr, f_scr, *, n_steps, layer):
    j = pl.program_id(0)
    gpm_ref, gpf_ref, gqf_ref = (ref.at[pl.ds(layer, 1)] for ref in (gpm_ref, gpf_ref, gqf_ref))

    @pl.when(j == 0)
    def _():
        gt_m = mod_ref[:, 2 * D_MODEL:3 * D_MODEL]
        wout_b = wout_ref[...].astype(BF16)
        woutb_ref[...] = wout_b
        mix = jnp.dot(mix_ref[...].astype(BF16), wout_b, preferred_element_type=F32)
        x1 = x_ref[...] + _rms(mix) * (gpm_ref[...] * gt_m)
        x1_scr[...] = x1
        sh = mod_ref[:, 3 * D_MODEL:4 * D_MODEL]
        sc1 = 1.0 + mod_ref[:, 4 * D_MODEL:5 * D_MODEL]
        h_scr[...] = (_rms(x1) * (gpf_ref[...] * sc1) + sh).astype(BF16)
        f_scr[...] = jnp.zeros_like(f_scr)

    w1_b, w3_b, w2_b = w1_ref[...].astype(BF16), w3_ref[...].astype(BF16), w2_ref[...].astype(BF16)
    w1b_ref[...] = w1_b
    w3b_ref[...] = w3_b
    w2b_ref[...] = w2_b
    h = h_scr[...]
    a = jnp.dot(h, w1_b, preferred_element_type=F32)
    b = jnp.dot(h, w3_b, preferred_element_type=F32)
    p = (_silu(a) * b).astype(BF16)
    f_scr[...] += jnp.dot(p, w2_b, preferred_element_type=F32)

    @pl.when(j == n_steps - 1)
    def _():
        gt_f = mod_ref[:, 5 * D_MODEL:6 * D_MODEL]
        y_ref[...] = x1_scr[...] + _rms(f_scr[...]) * (gqf_ref[...] * gt_f)


def _sample_post(layer, xs, mod, mix, w_out, gpost_m, gpre_f, gpost_f, w1, w3, w2):
    n = xs.shape[0]
    n_steps = D_FF // FF_BLK
    cs = functools.partial(_const_spec, grid_rank=1, single_buffer=False)
    ls = functools.partial(_layer_spec, layer=layer, single_buffer=False)
    return pl.pallas_call(
        functools.partial(_sample_post_kernel, n_steps=n_steps, layer=layer),
        grid=(n_steps,),
        in_specs=[
            cs((n, D_MODEL)), ls((n, 6 * D_MODEL)), cs((n, D_MODEL)), ls((D_MODEL, D_MODEL)),
            cs((DEPTH, D_MODEL)), cs((DEPTH, D_MODEL)), cs((DEPTH, D_MODEL)),
            pl.BlockSpec((None, D_MODEL, FF_BLK), lambda j: (layer, 0, j)),
            pl.BlockSpec((None, D_MODEL, FF_BLK), lambda j: (layer, 0, j)),
            pl.BlockSpec((None, FF_BLK, D_MODEL), lambda j: (layer, j, 0)),
        ],
        out_specs=[
            cs((n, D_MODEL)), cs((D_MODEL, D_MODEL)),
            pl.BlockSpec((D_MODEL, FF_BLK), lambda j: (0, j)),
            pl.BlockSpec((D_MODEL, FF_BLK), lambda j: (0, j)),
            pl.BlockSpec((FF_BLK, D_MODEL), lambda j: (j, 0)),
        ],
        out_shape=[
            jax.ShapeDtypeStruct((n, D_MODEL), F32),
            jax.ShapeDtypeStruct((D_MODEL, D_MODEL), BF16),
            jax.ShapeDtypeStruct((D_MODEL, D_FF), BF16),
            jax.ShapeDtypeStruct((D_MODEL, D_FF), BF16),
            jax.ShapeDtypeStruct((D_FF, D_MODEL), BF16),
        ],
        scratch_shapes=[
            pltpu.VMEM((n, D_MODEL), F32),
            pltpu.VMEM((n, D_MODEL), BF16),
            pltpu.VMEM((n, D_MODEL), F32),
        ],
        compiler_params=_params(1),
        name="sample_post",
    )(xs, mod, mix, w_out, gpost_m, gpre_f, gpost_f, w1, w3, w2)


def kernel(x_prompt, x_sample, c_prompt, c_sample, state_ret, state_conv_a, state_conv_c, ada_w, ada_b, norm_pre_mix, norm_post_mix, norm_pre_ffn, norm_post_ffn, w_in, w_out, ret_gn_g, conv_a_w, conv_a_b, conv_a_ln_g, conv_a_ln_b, conv_c_w, ffn_w1, ffn_w3, ffn_w2):
    bp, lp, _ = x_prompt.shape
    ns = x_sample.shape[0]
    assert x_sample.shape[1] == 1 and ns % SEQ_BLK == 0
    assert OFF_K + QK_WIDTH <= PRE_BLK and IN_WIDTH % PRE_BLK == 0
    assert lp % TOK_TILE == 0 and TOK_TILE % RET_CHUNK == 0
    assert TOK_TILE % CONV_BLK == 0 and CONV_BLK % (SUBLANES * CONV_STRIDE) == 0

    k_scale = RET_DK ** -0.5
    cq, sq = _rope_tables(np.arange(lp), 1.0)
    ck, sk = _rope_tables(np.arange(lp), k_scale)
    tables_p = (cq, sq, ck, sk) + _decay_tables(RET_CHUNK)
    cqs, sqs = _rope_tables([PAST_LEN], 1.0)
    cks, sks = _rope_tables([PAST_LEN], k_scale)
    rope_s = np.concatenate([cqs, sqs, cks, sks, np.zeros((4, LANES), np.float32)], axis=0)
    cd_s = _decay_tables(1)[4]

    mod_s, mod_p = _ada_modulation(c_sample, c_prompt, ada_w, ada_b)

    gn, cab, lng, lnb = ret_gn_g, conv_a_b, conv_a_ln_g, conv_a_ln_b
    g_pm, g_qm, g_pf, g_qf = norm_pre_mix, norm_post_mix, norm_pre_ffn, norm_post_ffn
    conv_a_t = jnp.transpose(state_conv_a, (0, 2, 1, 3))
    conv_c_t = jnp.transpose(state_conv_c, (0, 2, 1, 3))

    yp = x_prompt
    ys = x_sample.reshape(ns, D_MODEL)
    st_p = st_s = ()
    for l in range(DEPTH):
        proj, qt, kt, w_in_b = _sample_pre(l, ys, mod_s, g_pm, w_in, rope_s, yp)
        mix, *st_s = _sample_state(l, st_s, proj, qt, kt, state_ret, conv_a_t, conv_c_t, cd_s,
                                   gn, conv_a_w, cab, lng, lnb, conv_c_w)
        ys, w_out_b, w1_b, w3_b, w2_b = _sample_post(l, ys, mod_s, mix, w_out, g_qm, g_pf, g_qf,
                                                    ffn_w1, ffn_w3, ffn_w2)
        yp, *st_p = _prompt_layer(l, st_p, yp, mod_p, g_pm, g_qm, g_pf, g_qf, w_in_b, w_out_b, w1_b, w3_b, w2_b,
                                  tables_p, gn, conv_a_w, cab, lng, lnb, conv_c_w)

    sret_s, sca_s, scc_s = st_s
    sca_s = jnp.transpose(sca_s, (0, 2, 1, 3))
    scc_s = jnp.transpose(scc_s, (0, 2, 1, 3))
    return (yp, ys.reshape(ns, 1, D_MODEL)) + tuple(st_p) + (sret_s, sca_s, scc_s)
```

```python
import functools

import numpy as np
import jax
import jax.numpy as jnp
from jax import lax
from jax.experimental import pallas as pl
from jax.experimental.pallas import tpu as pltpu

D_MODEL = 1024
DEPTH = 2
PAST_LEN = 16384
RET_HEADS = 4
RET_WIDTH = D_MODEL // 2
RET_DV = RET_WIDTH // RET_HEADS
RET_DK = RET_DV // 2
QK_WIDTH = RET_HEADS * RET_DK
CONV_A_WIDTH = D_MODEL // 4
CONV_A_K = 31
CONV_C_WIDTH = D_MODEL - RET_WIDTH - CONV_A_WIDTH
CONV_C_K = 3
IN_WIDTH = 2 * QK_WIDTH + 2 * RET_WIDTH + 2 * CONV_A_WIDTH + 3 * CONV_C_WIDTH
D_FF = ((8 * D_MODEL // 3 + 255) // 256) * 256
RET_CHUNK = 128
ROPE_BASE = 10000.0
EPS = 1e-6

OFF_Q = 0
OFF_K = OFF_Q + QK_WIDTH
OFF_V = OFF_K + QK_WIDTH
OFF_G = OFF_V + RET_WIDTH
OFF_AV = OFF_G + RET_WIDTH
OFF_AG = OFF_AV + CONV_A_WIDTH
OFF_CB = OFF_AG + CONV_A_WIDTH
OFF_CC = OFF_CB + CONV_C_WIDTH
OFF_CX = OFF_CC + CONV_C_WIDTH
MIX_A = RET_WIDTH
MIX_C = RET_WIDTH + CONV_A_WIDTH

LANES = 128
SUBLANES = 8
V7X_VMEM_LIMIT_BYTES = 60 * 1024 * 1024

TOK_TILE = 512
ROW_BLK = 32
CONV_BLK = 64
CONV_STRIDE = 4
HIST_A = 32
HIST_C = 8
SEQ_BLK = 16
FF_BLK = 256
IN_BLK = 256
PRE_BLK = 1408

F32 = jnp.float32
BF16 = jnp.bfloat16


def _sigmoid(x):
    return jax.nn.sigmoid(x)


def _silu(x):
    return x * _sigmoid(x)


def _rms(x):
    return x * lax.rsqrt(jnp.mean(x * x, axis=-1, keepdims=True) + EPS)


def _layernorm(x, g, b=None):
    mu = jnp.mean(x, axis=-1, keepdims=True)
    d = x - mu
    var = jnp.mean(d * d, axis=-1, keepdims=True)
    y = d * lax.rsqrt(var + EPS) * g
    return y if b is None else y + b


def _rope_half(x, cos, sin, first_half):
    partner = jnp.where(first_half, pltpu.roll(x, 96, 1), pltpu.roll(x, 32, 1))
    return x * cos + partner * sin


def _first_half_mask(rows):
    lane = lax.broadcasted_iota(jnp.int32, (rows, LANES), 1)
    return (lane & (RET_DK - 1)) < (RET_DK // 2), lane < RET_DK


def _swiglu_block(h, w1_ref, w3_ref, col0):
    cols = pl.ds(col0, FF_BLK)
    a = jnp.dot(h, w1_ref[:, cols], preferred_element_type=F32)
    b = jnp.dot(h, w3_ref[:, cols], preferred_element_type=F32)
    return (_silu(a) * b).astype(BF16)


def _emit(items):
    for item in items:
        item()


def _emit_interleaved(primary, secondary):
    n, m = len(primary), len(secondary)
    done = 0
    for i, item in enumerate(primary):
        item()
        upto = ((i + 1) * m) // n
        _emit(secondary[done:upto])
        done = upto


def _rope_tables(pos, k_scale):
    half = RET_DK // 2
    inv = ROPE_BASE ** (-np.arange(half, dtype=np.float64) / half)
    ang = np.asarray(pos, np.float64)[:, None] * inv[None, :]
    cos = np.tile(np.cos(ang), (1, 4))
    sin = np.tile(np.concatenate([-np.sin(ang), np.sin(ang)], axis=1), (1, 2))
    return (cos * k_scale).astype(np.float32), (sin * k_scale).astype(np.float32)


def _decay_tables(chunk):
    log_g = np.log(1.0 - np.exp2(-5.0 - np.arange(RET_HEADS, dtype=np.float64)))
    idx = np.arange(chunk, dtype=np.float64)
    diff = idx[:, None] - idx[None, :]
    dmat = np.where(diff[None] >= 0, np.exp(np.maximum(diff, 0.0)[None] * log_g[:, None, None]), 0.0)
    read_dec = np.exp((idx + 1.0)[:, None] * log_g[None, :])
    upd_dec = np.exp((chunk - 1.0 - idx)[:, None] * log_g[None, :])
    chunk_dec = np.exp(chunk * log_g)
    pairs = RET_HEADS // 2
    side_by_side = lambda a: a.reshape(pairs, 2, *a.shape[1:]).transpose(0, 2, 1, 3).reshape(pairs, a.shape[1], -1)
    dmat2 = side_by_side(dmat)
    rd2 = side_by_side(np.broadcast_to(read_dec.T[:, :, None], (RET_HEADS, chunk, RET_DV)))
    updt = np.repeat(upd_dec.T, RET_DK, axis=0).reshape(pairs, 2 * RET_DK, chunk)
    cd = np.broadcast_to(chunk_dec[:, None, None], (RET_HEADS, 1, RET_DV))
    cd2 = side_by_side(cd)
    f = lambda a: np.ascontiguousarray(a, dtype=np.float32)
    return f(dmat2), f(rd2), f(updt), f(cd2), f(cd)


def _const_spec(shape, grid_rank, single_buffer=True):
    zeros = (0,) * len(shape)
    idx = (lambda b, c: zeros) if grid_rank == 2 else (lambda j: zeros)
    if single_buffer:
        return pl.BlockSpec(shape, idx, pipeline_mode=pl.Buffered(1))
    return pl.BlockSpec(shape, idx)


def _layer_spec(shape, layer, single_buffer=True):
    zeros = (0,) * len(shape)
    mode = dict(pipeline_mode=pl.Buffered(1)) if single_buffer else {}
    return pl.BlockSpec((None,) + tuple(shape), lambda j: (layer,) + zeros, **mode)


def _params(n_axes):
    return pltpu.CompilerParams(dimension_semantics=("arbitrary",) * n_axes,
                                vmem_limit_bytes=V7X_VMEM_LIMIT_BYTES)


def _stacked_call(kern, *, name, grid, in_specs, args, out_specs, out_shape, layer, stacked, prev, scratch_shapes):
    def spec(n_layers, block, index_fn):
        return pl.BlockSpec((n_layers,) + tuple(block), lambda *g: (0,) + tuple(index_fn(*g)))

    in_specs, out_specs, out_shape, args = list(in_specs), list(out_specs), list(out_shape), list(args)
    for dims, block, index_fn in stacked:
        out_specs.append(spec(layer + 1, block, index_fn))
        out_shape.append(jax.ShapeDtypeStruct((layer + 1,) + tuple(dims), F32))
        if layer > 0:
            in_specs.append(spec(layer, block, index_fn))
    if layer > 0:
        args += list(prev)
    return pl.pallas_call(
        kern, grid=grid, in_specs=in_specs, out_specs=out_specs, out_shape=out_shape,
        scratch_shapes=scratch_shapes, compiler_params=_params(len(grid)), name=name)(*args)


def _ada_kernel(cs_ref, cp_ref, w_ref, b_ref, os_ref, op_ref):
    layer = pl.program_id(0)
    bias = b_ref[0:1, :]
    for d in range(1, DEPTH):
        bias = jnp.where(layer == d, b_ref[d:d + 1, :], bias)
    w = w_ref[...].astype(BF16)
    os_ref[...] = jnp.dot(_silu(cs_ref[...]).astype(BF16), w, preferred_element_type=F32) + bias
    mod_p = jnp.dot(_silu(cp_ref[...]).astype(BF16), w, preferred_element_type=F32) + bias
    for b in range(cp_ref.shape[0]):
        op_ref[b] = mod_p[b:b + 1, :]


def _ada_modulation(c_sample, c_prompt, ada_w, ada_b):
    ns, bp = c_sample.shape[0], c_prompt.shape[0]
    ncol = 6 * D_MODEL
    blk = 2 * D_MODEL
    return pl.pallas_call(
        _ada_kernel,
        grid=(DEPTH, ncol // blk),
        in_specs=[
            pl.BlockSpec((ns, D_MODEL), lambda l, j: (0, 0)),
            pl.BlockSpec((bp, D_MODEL), lambda l, j: (0, 0)),
            pl.BlockSpec((None, D_MODEL, blk), lambda l, j: (l, 0, j)),
            pl.BlockSpec((DEPTH, blk), lambda l, j: (0, j)),
        ],
        out_specs=[pl.BlockSpec((None, ns, blk), lambda l, j: (l, 0, j)),
                   pl.BlockSpec((None, bp, 1, blk), lambda l, j: (l, 0, 0, j))],
        out_shape=[jax.ShapeDtypeStruct((DEPTH, ns, ncol), F32),
                   jax.ShapeDtypeStruct((DEPTH, bp, 1, ncol), F32)],
        compiler_params=_params(2),
        name="ada_mod",
    )(c_sample, c_prompt, ada_w, ada_b)


def _prompt_layer_kernel(*refs, tile, n_tiles, tiles_per_seq, layer):
    (xc_ref, xp_ref, modc_ref, modp_ref, gpm_ref, gqm_ref, gpf_ref, gqf_ref,
     win_ref, wout_ref, w1_ref, w3_ref, w2_ref,
     cq_ref, sq_ref, ck_ref, sk_ref, dmat_ref, rd_ref, upd_ref, cd_ref,
     gn_ref, caw_ref, cab_ref, lng_ref, lnb_ref, ccw_ref) = refs[:27]
    n_prev = 3 if layer > 0 else 0
    prev_states = refs[27:27 + n_prev]
    (y_ref, sret_all, sca_all, scc_all,
     h_scr, proj_scr, mixin_scr, mix_scr, hf_scr, p_scr, ubuf, ua_scr, zbuf,
     sc_scr, qs_scr, sbd_scr) = refs[27 + n_prev:]
    sret_ref, sca_ref, scc_ref = sret_all.at[layer], sca_all.at[layer], scc_all.at[layer]
    gpm_ref, gqm_ref, gpf_ref, gqf_ref, gn_ref, cab_ref, lng_ref, lnb_ref = (
        ref.at[pl.ds(layer, 1)] for ref in (gpm_ref, gqm_ref, gpf_ref, gqf_ref, gn_ref, cab_ref, lng_ref, lnb_ref))
    mod_chunk = lambda ref, k: ref[:, pl.ds(k * D_MODEL, D_MODEL)]
    step = pl.program_id(0)
    row_blocks = [pl.ds(r * ROW_BLK, ROW_BLK) for r in range(tile // ROW_BLK)]
    first_tap = HIST_A - (CONV_A_K - 1)
    first_tap_c = HIST_C - (CONV_C_K - 1)
    span = SUBLANES * CONV_STRIDE

    @pl.when((step < n_tiles) & (lax.rem(step, tiles_per_seq) == 0))
    def _():
        for prev_ref, all_ref in zip(prev_states, (sret_all, sca_all, scc_all)):
            all_ref[0:layer] = prev_ref[...]
        sret_ref[...] = jnp.zeros_like(sret_ref)
        sbd_scr[...] = jnp.zeros_like(sbd_scr)
        ubuf[:, 0:HIST_A, :] = jnp.zeros((2, HIST_A, LANES), F32)
        zbuf[0:HIST_C, :] = jnp.zeros((HIST_C, CONV_C_WIDTH), F32)

    def mix_prenorm():
        sh = mod_chunk(modc_ref, 0)
        pre_scale = gpm_ref[...] * (1.0 + mod_chunk(modc_ref, 1))
        for rows in row_blocks:
            h_scr[rows, :] = (_rms(xc_ref[rows, :]) * pre_scale + sh).astype(BF16)

    def in_proj(n):
        cols = pl.ds(n * IN_BLK, IN_BLK)
        proj_scr[:, cols] = jnp.dot(h_scr[...], win_ref[:, cols], preferred_element_type=F32)

    def conv_groups():
        taps = [[jnp.broadcast_to(caw_ref[j:j + 1, pl.ds(half * LANES, LANES)], (SUBLANES, LANES))
                 for j in range(CONV_A_K)] for half in range(2)]
        for r in range(tile // CONV_BLK):
            rows = pl.ds(r * CONV_BLK, CONV_BLK)
            dst = pl.ds(HIST_A + r * CONV_BLK, CONV_BLK)
            for half in range(2):
                lo = half * LANES
                u = (proj_scr[rows, pl.ds(OFF_AV + lo, LANES)]
                     * _sigmoid(proj_scr[rows, pl.ds(OFF_AG + lo, LANES)]))
                ubuf[half, dst, :] = u
        for half in range(2):
            for m in range(tile // span):
                base = m * span
                acc = [None] * CONV_STRIDE
                for k in range(CONV_STRIDE + CONV_A_K - 1):
                    win = ubuf[half, pl.ds(base + first_tap + k, SUBLANES, stride=CONV_STRIDE), :]
                    for t in range(CONV_STRIDE):
                        j = k - t
                        if 0 <= j < CONV_A_K:
                            term = taps[half][j] * win
                            acc[t] = term if acc[t] is None else acc[t] + term
                for t in range(CONV_STRIDE):
                    ua_scr[half, pl.ds(base + t, SUBLANES, stride=CONV_STRIDE), :] = acc[t]
        for r in range(tile // CONV_BLK):
            rows = pl.ds(r * CONV_BLK, CONV_BLK)
            ua = jnp.concatenate([ua_scr[0, rows, :], ua_scr[1, rows, :]], axis=-1) + cab_ref[...]
            o_a = _silu(_layernorm(ua, lng_ref[...], lnb_ref[...]))
            mixin_scr[rows, MIX_A:MIX_A + CONV_A_WIDTH] = o_a.astype(BF16)
        for r in range(tile // CONV_BLK):
            rows = pl.ds(r * CONV_BLK, CONV_BLK)
            z = proj_scr[rows, OFF_CC:OFF_CC + CONV_C_WIDTH] * proj_scr[rows, OFF_CX:OFF_CX + CONV_C_WIDTH]
            zbuf[pl.ds(HIST_C + r * CONV_BLK, CONV_BLK), :] = z
        for r in range(tile // CONV_BLK):
            rows = pl.ds(r * CONV_BLK, CONV_BLK)
            zc = ccw_ref[0:1, :] * zbuf[pl.ds(r * CONV_BLK + first_tap_c, CONV_BLK), :]
            for j in range(1, CONV_C_K):
                zc = zc + ccw_ref[j:j + 1, :] * zbuf[pl.ds(r * CONV_BLK + first_tap_c + j, CONV_BLK), :]
            o_c = proj_scr[rows, OFF_CB:OFF_CB + CONV_C_WIDTH] * zc
            mixin_scr[rows, MIX_C:MIX_C + CONV_C_WIDTH] = o_c.astype(BF16)
        hist_a = pl.ds(tile + first_tap, CONV_A_K - 1)
        sca_ref[...] = jnp.concatenate([ubuf[0, hist_a, :], ubuf[1, hist_a, :]], axis=-1)
        scc_ref[...] = zbuf[pl.ds(tile + first_tap_c, CONV_C_K - 1), :]
        ubuf[:, 0:HIST_A, :] = ubuf[:, pl.ds(tile, HIST_A), :]
        zbuf[0:HIST_C, :] = zbuf[pl.ds(tile, HIST_C), :]

    def retention_scores(i, pair, buf):
        first_half, _ = _first_half_mask(RET_CHUNK)
        rows = pl.ds(i * RET_CHUNK, RET_CHUNK)
        cq, sq, ck, sk = cq_ref[rows, :], sq_ref[rows, :], ck_ref[rows, :], sk_ref[rows, :]
        qr = _rope_half(proj_scr[rows, pl.ds(OFF_Q + pair * LANES, LANES)], cq, sq, first_half)
        kr = _rope_half(proj_scr[rows, pl.ds(OFF_K + pair * LANES, LANES)], ck, sk, first_half)
        qr_b = qr.astype(BF16)
        k_t = kr.T
        head0_rows = lax.broadcasted_iota(jnp.int32, (2 * RET_DK, RET_CHUNK), 0) < RET_DK
        k_bd = jnp.concatenate([jnp.where(head0_rows, k_t, 0.0), jnp.where(head0_rows, 0.0, k_t)], axis=1)
        scores = jnp.dot(qr_b, k_bd.astype(BF16), preferred_element_type=F32) * dmat_ref[pair]
        sc_scr[buf] = scores.astype(BF16)
        s_bd = sbd_scr[pair]
        qs_scr[buf] = jnp.dot(qr_b, s_bd.astype(BF16), preferred_element_type=F32) * rd_ref[pair]
        v_pair = proj_scr[rows, pl.ds(OFF_V + 2 * pair * RET_DV, 2 * RET_DV)].astype(BF16)
        kv = jnp.dot((k_t * upd_ref[pair]).astype(BF16), v_pair, preferred_element_type=F32)
        r0 = lax.broadcasted_iota(jnp.int32, (2 * RET_DK, 2 * RET_DV), 0) < RET_DK
        c0 = lax.broadcasted_iota(jnp.int32, (2 * RET_DK, 2 * RET_DV), 1) < RET_DV
        s_new = s_bd * cd_ref[pair] + jnp.where(r0 == c0, kv, 0.0)
        sbd_scr[pair] = s_new
        sret_ref[2 * pair] = s_new[0:RET_DK, 0:RET_DV]
        sret_ref[2 * pair + 1] = s_new[RET_DK:2 * RET_DK, RET_DV:2 * RET_DV]

    def retention_values(i, pair, buf):
        rows = pl.ds(i * RET_CHUNK, RET_CHUNK)
        for hl in range(2):
            h = 2 * pair + hl
            vh = proj_scr[rows, pl.ds(OFF_V + h * RET_DV, RET_DV)].astype(BF16)
            o = (jnp.dot(sc_scr[buf, :, pl.ds(hl * RET_CHUNK, RET_CHUNK)], vh, preferred_element_type=F32)
                 + qs_scr[buf, :, pl.ds(hl * RET_DV, RET_DV)])
            gate = proj_scr[rows, pl.ds(OFF_G + h * RET_DV, RET_DV)]
            o = _layernorm(o, gn_ref[:, pl.ds(h * RET_DV, RET_DV)]) * _silu(gate)
            mixin_scr[rows, pl.ds(h * RET_DV, RET_DV)] = o.astype(BF16)

    def out_proj():
        mix_scr[...] = jnp.dot(mixin_scr[...], wout_ref[...], preferred_element_type=F32)

    in_tiles = [functools.partial(in_proj, n) for n in range(IN_WIDTH // IN_BLK)]
    items = [(i, pair) for i in range(tile // RET_CHUNK) for pair in range(RET_HEADS // 2)]
    ret_items = []
    for k, (i, pair) in enumerate(items):
        ret_items.append(functools.partial(retention_scores, i, pair, k % 2))
        if k > 0:
            ret_items.append(functools.partial(retention_values, *items[k - 1], (k - 1) % 2))
    ret_items.append(functools.partial(retention_values, *items[-1], (len(items) - 1) % 2))
    n_qkvg = OFF_AV // IN_BLK

    def ffn_norms():
        post_scale = gqm_ref[...] * mod_chunk(modp_ref, 2)
        sh = mod_chunk(modp_ref, 3)
        pre_scale = gpf_ref[...] * (1.0 + mod_chunk(modp_ref, 4))
        for rows in row_blocks:
            x1 = xp_ref[rows, :] + _rms(mix_scr[rows, :]) * post_scale
            y_ref[rows, :] = x1
            hf_scr[rows, :] = (_rms(x1) * pre_scale + sh).astype(BF16)

    def hidden(j):
        p_scr[:, pl.ds(j * FF_BLK, FF_BLK)] = _swiglu_block(hf_scr[...], w1_ref, w3_ref, j * FF_BLK)

    def down(n):
        cols = pl.ds(n * FF_BLK, FF_BLK)
        proj_scr[:, cols] = jnp.dot(p_scr[...], w2_ref[:, cols], preferred_element_type=F32)

    def ffn_finish():
        post_scale = gqf_ref[...] * mod_chunk(modp_ref, 5)
        for rows in row_blocks:
            y_ref[rows, :] = y_ref[rows, :] + _rms(proj_scr[rows, 0:D_MODEL]) * post_scale

    hidden_blocks = [functools.partial(hidden, j) for j in range(D_FF // FF_BLK)]
    down_tiles = [functools.partial(down, n) for n in range(D_MODEL // FF_BLK)]

    @pl.when(step == 0)
    def _():
        mix_prenorm()
        _emit(in_tiles)
        _emit(ret_items)
        conv_groups()
        out_proj()

    @pl.when((step > 0) & (step < n_tiles))
    def _():
        mix_prenorm()
        ffn_norms()
        _emit(in_tiles[:n_qkvg])
        _emit_interleaved(in_tiles[n_qkvg:] + hidden_blocks, ret_items)
        conv_groups()
        _emit(down_tiles)
        out_proj()
        ffn_finish()

    @pl.when(step == n_tiles)
    def _():
        ffn_norms()
        _emit(hidden_blocks)
        _emit(down_tiles)
        ffn_finish()


def _prompt_layer(layer, prev, x, mod_p, g_pm, g_qm, g_pf, g_qf, w_in, w_out, w1, w3, w2, tables,
                  gn, caw, cab, lng, lnb, ccw):
    bsz, seq, _ = x.shape
    tile = TOK_TILE
    tps = seq // tile
    n_tiles = bsz * tps
    cq, sq, ck, sk, dmat, rd, upd, cd, _ = tables
    cur = lambda i: jnp.minimum(i, n_tiles - 1)
    prv = lambda i: jnp.maximum(i - 1, 0)
    tok_cur = pl.BlockSpec((None, tile, D_MODEL), lambda i: (cur(i) // tps, cur(i) % tps, 0))
    tok_prev = pl.BlockSpec((None, tile, D_MODEL), lambda i: (prv(i) // tps, prv(i) % tps, 0))
    mod_cur = pl.BlockSpec((None, None, 1, 6 * D_MODEL), lambda i: (layer, cur(i) // tps, 0, 0))
    mod_prev = pl.BlockSpec((None, None, 1, 6 * D_MODEL), lambda i: (layer, prv(i) // tps, 0, 0))
    rope = pl.BlockSpec((tile, LANES), lambda i: (cur(i) % tps, 0))
    cs = functools.partial(_const_spec, grid_rank=1)
    ls = functools.partial(_layer_spec, layer=layer)
    args = (x, x, mod_p, mod_p, g_pm, g_qm, g_pf, g_qf, w_in, w_out, w1, w3, w2,
            cq, sq, ck, sk, dmat, rd, upd, cd, gn, caw, cab, lng, lnb, ccw)
    seq_block = lambda i: (cur(i) // tps,)
    return _stacked_call(
        functools.partial(_prompt_layer_kernel, tile=tile, n_tiles=n_tiles, tiles_per_seq=tps, layer=layer),
        name="prompt_layer", grid=(n_tiles + 1,), args=args, layer=layer, prev=prev,
        stacked=[
            ((bsz, RET_HEADS, RET_DK, RET_DV), (None, RET_HEADS, RET_DK, RET_DV),
             lambda i: seq_block(i) + (0, 0, 0)),
            ((bsz, CONV_A_K - 1, CONV_A_WIDTH), (None, CONV_A_K - 1, CONV_A_WIDTH),
             lambda i: seq_block(i) + (0, 0)),
            ((bsz, CONV_C_K - 1, CONV_C_WIDTH), (None, CONV_C_K - 1, CONV_C_WIDTH),
             lambda i: seq_block(i) + (0, 0)),
        ],
        in_specs=[
            tok_cur, tok_prev, mod_cur, mod_prev,
            cs((DEPTH, D_MODEL)), cs((DEPTH, D_MODEL)), cs((DEPTH, D_MODEL)), cs((DEPTH, D_MODEL)),
            cs((D_MODEL, IN_WIDTH)), cs((D_MODEL, D_MODEL)),
            cs((D_MODEL, D_FF)), cs((D_MODEL, D_FF)), cs((D_FF, D_MODEL)),
            rope, rope, rope, rope,
            cs(dmat.shape), cs(rd.shape), cs(upd.shape), cs(cd.shape),
            cs((DEPTH, RET_WIDTH)), ls((CONV_A_K, CONV_A_WIDTH)), cs((DEPTH, CONV_A_WIDTH)),
            cs((DEPTH, CONV_A_WIDTH)), cs((DEPTH, CONV_A_WIDTH)), ls((CONV_C_K, CONV_C_WIDTH)),
        ],
        out_specs=[tok_prev],
        out_shape=[jax.ShapeDtypeStruct(x.shape, F32)],
        scratch_shapes=[
            pltpu.VMEM((tile, D_MODEL), BF16),
            pltpu.VMEM((tile, IN_WIDTH), F32),
            pltpu.VMEM((tile, D_MODEL), BF16),
            pltpu.VMEM((tile, D_MODEL), F32),
            pltpu.VMEM((tile, D_MODEL), BF16),
            pltpu.VMEM((tile, D_FF), BF16),
            pltpu.VMEM((2, tile + HIST_A, LANES), F32),
            pltpu.VMEM((2, tile, LANES), F32),
            pltpu.VMEM((tile + HIST_C, CONV_C_WIDTH), F32),
            pltpu.VMEM((2, RET_CHUNK, 2 * RET_CHUNK), BF16),
            pltpu.VMEM((2, RET_CHUNK, 2 * RET_DV), F32),
            pltpu.VMEM((RET_HEADS // 2, 2 * RET_DK, 2 * RET_DV), F32),
        ])


def _sample_pre_kernel(x_ref, mod_ref, gpre_ref, win_ref, rope_ref, after_ref, proj_ref, qt_ref, kt_ref, winb_ref,
                       h_scr, *, layer):
    j = pl.program_id(0)
    n = x_ref.shape[0]

    @pl.when(j == 0)
    def _():
        sh = mod_ref[:, 0:D_MODEL]
        sc1 = 1.0 + mod_ref[:, D_MODEL:2 * D_MODEL]
        h_scr[...] = (_rms(x_ref[...]) * (gpre_ref[pl.ds(layer, 1), :] * sc1) + sh).astype(BF16)

    w_blk = win_ref[...].astype(BF16)
    winb_ref[...] = w_blk
    proj_ref[...] = jnp.dot(h_scr[...], w_blk, preferred_element_type=F32)

    def rope_transposed(off, cos, sin, dst_ref):
        first_half, _ = _first_half_mask(n)
        for pair in range(RET_HEADS // 2):
            src = proj_ref[:, pl.ds(off + pair * LANES, LANES)]
            dst_ref[pl.ds(pair * LANES, LANES), :] = _rope_half(src, cos, sin, first_half).T

    @pl.when(j == 0)
    def _():
        rope_transposed(OFF_Q, rope_ref[0:1, :], rope_ref[1:2, :], qt_ref)
        rope_transposed(OFF_K, rope_ref[2:3, :], rope_ref[3:4, :], kt_ref)


def _sample_pre(layer, xs, mod, gpre, w_in, rope_s, after):
    n = xs.shape[0]
    cs = functools.partial(_const_spec, grid_rank=1, single_buffer=False)
    ls = functools.partial(_layer_spec, layer=layer, single_buffer=False)
    return pl.pallas_call(
        functools.partial(_sample_pre_kernel, layer=layer),
        grid=(IN_WIDTH // PRE_BLK,),
        in_specs=[cs((n, D_MODEL)), ls((n, 6 * D_MODEL)), cs((DEPTH, D_MODEL)),
                  pl.BlockSpec((None, D_MODEL, PRE_BLK), lambda j: (layer, 0, j)), cs(rope_s.shape),
                  pl.BlockSpec(memory_space=pl.ANY)],
        out_specs=[pl.BlockSpec((n, PRE_BLK), lambda j: (0, j)), cs((QK_WIDTH, n)), cs((QK_WIDTH, n)),
                   pl.BlockSpec((D_MODEL, PRE_BLK), lambda j: (0, j))],
        out_shape=[jax.ShapeDtypeStruct((n, IN_WIDTH), F32),
                   jax.ShapeDtypeStruct((QK_WIDTH, n), F32),
                   jax.ShapeDtypeStruct((QK_WIDTH, n), F32),
                   jax.ShapeDtypeStruct((D_MODEL, IN_WIDTH), BF16)],
        scratch_shapes=[pltpu.VMEM((n, D_MODEL), BF16)],
        compiler_params=_params(1),
        name="sample_pre",
    )(xs, mod, gpre, w_in, rope_s, after)


def _sample_state_kernel(*refs, layer):
    (proj_ref, qt_ref, kt_ref, sin_ref, bufa_ref, bufc_ref, cd_ref,
     gn_ref, caw_ref, cab_ref, lng_ref, lnb_ref, ccw_ref) = refs[:13]
    n_prev = 3 if layer > 0 else 0
    prev_states = refs[13:13 + n_prev]
    mix_ref, sout_all, outa_all, outc_all, o_scr = refs[13 + n_prev:]
    for prev_ref, all_ref in zip(prev_states, (sout_all, outa_all, outc_all)):
        all_ref[0:layer] = prev_ref[...]
    sout_ref, outa_ref, outc_ref = sout_all.at[layer], outa_all.at[layer], outc_all.at[layer]
    gn_ref, cab_ref, lng_ref, lnb_ref = (ref.at[pl.ds(layer, 1)] for ref in (gn_ref, cab_ref, lng_ref, lnb_ref))
    blk = pl.program_id(0)
    n = qt_ref.shape[1]
    lane = lax.broadcasted_iota(jnp.int32, (QK_WIDTH, n), 1)

    for bl in range(SEQ_BLK):
        onehot = lane == (blk * SEQ_BLK + bl)
        qcol = jnp.sum(jnp.where(onehot, qt_ref[...], 0.0), axis=1, keepdims=True)
        kcol = jnp.sum(jnp.where(onehot, kt_ref[...], 0.0), axis=1, keepdims=True)
        row = pl.ds(bl, 1)
        for h in range(RET_HEADS):
            vrow = proj_ref[row, pl.ds(OFF_V + h * RET_DV, RET_DV)]
            s_new = (sin_ref[bl, h] * cd_ref[h]
                     + kcol[h * RET_DK:(h + 1) * RET_DK, :] * vrow)
            sout_ref[bl, h] = s_new
            o_scr[row, pl.ds(h * RET_DV, RET_DV)] = jnp.sum(
                qcol[h * RET_DK:(h + 1) * RET_DK, :] * s_new, axis=0, keepdims=True)
    for h in range(RET_HEADS):
        cols = pl.ds(h * RET_DV, RET_DV)
        o = _layernorm(o_scr[:, cols], gn_ref[:, cols]) * _silu(proj_ref[:, pl.ds(OFF_G + h * RET_DV, RET_DV)])
        mix_ref[:, cols] = o

    u = proj_ref[:, OFF_AV:OFF_AV + CONV_A_WIDTH] * _sigmoid(proj_ref[:, OFF_AG:OFF_AG + CONV_A_WIDTH])
    ua = caw_ref[CONV_A_K - 1:CONV_A_K, :] * u
    for j in range(CONV_A_K - 1):
        ua = ua + caw_ref[j:j + 1, :] * bufa_ref[j]
    outa_ref[0:CONV_A_K - 2] = bufa_ref[1:CONV_A_K - 1]
    outa_ref[CONV_A_K - 2] = u
    mix_ref[:, MIX_A:MIX_A + CONV_A_WIDTH] = _silu(_layernorm(ua + cab_ref[...], lng_ref[...], lnb_ref[...]))

    z = proj_ref[:, OFF_CC:OFF_CC + CONV_C_WIDTH] * proj_ref[:, OFF_CX:OFF_CX + CONV_C_WIDTH]
    zc = ccw_ref[0:1, :] * bufc_ref[0] + ccw_ref[1:2, :] * bufc_ref[1] + ccw_ref[2:3, :] * z
    outc_ref[0] = bufc_ref[1]
    outc_ref[1] = z
    mix_ref[:, MIX_C:MIX_C + CONV_C_WIDTH] = proj_ref[:, OFF_CB:OFF_CB + CONV_C_WIDTH] * zc


def _sample_state(layer, prev, proj, qt, kt, s_ret, buf_a_t, buf_c_t, cd, gn, caw, cab, lng, lnb, ccw):
    n = proj.shape[0]
    cs = functools.partial(_const_spec, grid_rank=1, single_buffer=False)
    ls = functools.partial(_layer_spec, layer=layer, single_buffer=False)
    s_spec = pl.BlockSpec((None, SEQ_BLK, RET_HEADS, RET_DK, RET_DV), lambda j: (layer, j, 0, 0, 0))
    a_spec = pl.BlockSpec((None, CONV_A_K - 1, SEQ_BLK, CONV_A_WIDTH), lambda j: (layer, 0, j, 0))
    c_spec = pl.BlockSpec((None, CONV_C_K - 1, SEQ_BLK, CONV_C_WIDTH), lambda j: (layer, 0, j, 0))
    args = (proj, qt, kt, s_ret, buf_a_t, buf_c_t, cd, gn, caw, cab, lng, lnb, ccw)
    return _stacked_call(
        functools.partial(_sample_state_kernel, layer=layer),
        name="sample_state", grid=(n // SEQ_BLK,), args=args, layer=layer, prev=prev,
        stacked=[
            ((n, RET_HEADS, RET_DK, RET_DV), (SEQ_BLK, RET_HEADS, RET_DK, RET_DV), lambda j: (j, 0, 0, 0)),
            ((CONV_A_K - 1, n, CONV_A_WIDTH), (CONV_A_K - 1, SEQ_BLK, CONV_A_WIDTH), lambda j: (0, j, 0)),
            ((CONV_C_K - 1, n, CONV_C_WIDTH), (CONV_C_K - 1, SEQ_BLK, CONV_C_WIDTH), lambda j: (0, j, 0)),
        ],
        in_specs=[
            pl.BlockSpec((SEQ_BLK, IN_WIDTH), lambda j: (j, 0)),
            cs((QK_WIDTH, n)), cs((QK_WIDTH, n)),
            s_spec, a_spec, c_spec,
            cs(cd.shape), cs((DEPTH, RET_WIDTH)), ls((CONV_A_K, CONV_A_WIDTH)), cs((DEPTH, CONV_A_WIDTH)),
            cs((DEPTH, CONV_A_WIDTH)), cs((DEPTH, CONV_A_WIDTH)), ls((CONV_C_K, CONV_C_WIDTH)),
        ],
        out_specs=[pl.BlockSpec((SEQ_BLK, D_MODEL), lambda j: (j, 0))],
        out_shape=[jax.ShapeDtypeStruct((n, D_MODEL), F32)],
        scratch_shapes=[pltpu.VMEM((SEQ_BLK, RET_WIDTH), F32)])


def _sample_post_kernel(x_ref, mod_ref, mix_ref, wout_ref, gpm_ref, gpf_ref, gqf_ref, w1_ref, w3_ref, w2_ref,
                        y_ref, woutb_ref, w1b_ref, w3b_ref, w2b_ref, x1_scr, h_scr, f_scr, *, n_steps, layer):
    j = pl.program_id(0)
    gpm_ref, gpf_ref, gqf_ref = (ref.at[pl.ds(layer, 1)] for ref in (gpm_ref, gpf_ref, gqf_ref))

    @pl.when(j == 0)
    def _():
        gt_m = mod_ref[:, 2 * D_MODEL:3 * D_MODEL]
        wout_b = wout_ref[...].astype(BF16)
        woutb_ref[...] = wout_b
        mix = jnp.dot(mix_ref[...].astype(BF16), wout_b, preferred_element_type=F32)
        x1 = x_ref[...] + _rms(mix) * (gpm_ref[...] * gt_m)
        x1_scr[...] = x1
        sh = mod_ref[:, 3 * D_MODEL:4 * D_MODEL]
        sc1 = 1.0 + mod_ref[:, 4 * D_MODEL:5 * D_MODEL]
        h_scr[...] = (_rms(x1) * (gpf_ref[...] * sc1) + sh).astype(BF16)
        f_scr[...] = jnp.zeros_like(f_scr)

    w1_b, w3_b, w2_b = w1_ref[...].astype(BF16), w3_ref[...].astype(BF16), w2_ref[...].astype(BF16)
    w1b_ref[...] = w1_b
    w3b_ref[...] = w3_b
    w2b_ref[...] = w2_b
    h = h_scr[...]
    a = jnp.dot(h, w1_b, preferred_element_type=F32)
    b = jnp.dot(h, w3_b, preferred_element_type=F32)
    p = (_silu(a) * b).astype(BF16)
    f_scr[...] += jnp.dot(p, w2_b, preferred_element_type=F32)

    @pl.when(j == n_steps - 1)
    def _():
        gt_f = mod_ref[:, 5 * D_MODEL:6 * D_MODEL]
        y_ref[...] = x1_scr[...] + _rms(f_scr[...]) * (gqf_ref[...] * gt_f)


def _sample_post(layer, xs, mod, mix, w_out, gpost_m, gpre_f, gpost_f, w1, w3, w2):
    n = xs.shape[0]
    n_steps = D_FF // FF_BLK
    cs = functools.partial(_const_spec, grid_rank=1, single_buffer=False)
    ls = functools.partial(_layer_spec, layer=layer, single_buffer=False)
    return pl.pallas_call(
        functools.partial(_sample_post_kernel, n_steps=n_steps, layer=layer),
        grid=(n_steps,),
        in_specs=[
            cs((n, D_MODEL)), ls((n, 6 * D_MODEL)), cs((n, D_MODEL)), ls((D_MODEL, D_MODEL)),
            cs((DEPTH, D_MODEL)), cs((DEPTH, D_MODEL)), cs((DEPTH, D_MODEL)),
            pl.BlockSpec((None, D_MODEL, FF_BLK), lambda j: (layer, 0, j)),
            pl.BlockSpec((None, D_MODEL, FF_BLK), lambda j: (layer, 0, j)),
            pl.BlockSpec((None, FF_BLK, D_MODEL), lambda j: (layer, j, 0)),
        ],
        out_specs=[
            cs((n, D_MODEL)), cs((D_MODEL, D_MODEL)),
            pl.BlockSpec((D_MODEL, FF_BLK), lambda j: (0, j)),
            pl.BlockSpec((D_MODEL, FF_BLK), lambda j: (0, j)),
            pl.BlockSpec((FF_BLK, D_MODEL), lambda j: (j, 0)),
        ],
        out_shape=[
            jax.ShapeDtypeStruct((n, D_MODEL), F32),
            jax.ShapeDtypeStruct((D_MODEL, D_MODEL), BF16),
            jax.ShapeDtypeStruct((D_MODEL, D_FF), BF16),
            jax.ShapeDtypeStruct((D_MODEL, D_FF), BF16),
            jax.ShapeDtypeStruct((D_FF, D_MODEL), BF16),
        ],
        scratch_shapes=[
            pltpu.VMEM((n, D_MODEL), F32),
            pltpu.VMEM((n, D_MODEL), BF16),
            pltpu.VMEM((n, D_MODEL), F32),
        ],
        compiler_params=_params(1),
        name="sample_post",
    )(xs, mod, mix, w_out, gpost_m, gpre_f, gpost_f, w1, w3, w2)


def kernel(x_prompt, x_sample, c_prompt, c_sample, state_ret, state_conv_a, state_conv_c, ada_w, ada_b, norm_pre_mix, norm_post_mix, norm_pre_ffn, norm_post_ffn, w_in, w_out, ret_gn_g, conv_a_w, conv_a_b, conv_a_ln_g, conv_a_ln_b, conv_c_w, ffn_w1, ffn_w3, ffn_w2):
    bp, lp, _ = x_prompt.shape
    ns = x_sample.shape[0]
    assert x_sample.shape[1] == 1 and ns % SEQ_BLK == 0
    assert OFF_K + QK_WIDTH <= PRE_BLK and IN_WIDTH % PRE_BLK == 0
    assert lp % TOK_TILE == 0 and TOK_TILE % RET_CHUNK == 0
    assert TOK_TILE % CONV_BLK == 0 and CONV_BLK % (SUBLANES * CONV_STRIDE) == 0

    k_scale = RET_DK ** -0.5
    cq, sq = _rope_tables(np.arange(lp), 1.0)
    ck, sk = _rope_tables(np.arange(lp), k_scale)
    tables_p = (cq, sq, ck, sk) + _decay_tables(RET_CHUNK)
    cqs, sqs = _rope_tables([PAST_LEN], 1.0)
    cks, sks = _rope_tables([PAST_LEN], k_scale)
    rope_s = np.concatenate([cqs, sqs, cks, sks, np.zeros((4, LANES), np.float32)], axis=0)
    cd_s = _decay_tables(1)[4]

    mod_s, mod_p = _ada_modulation(c_sample, c_prompt, ada_w, ada_b)

    gn, cab, lng, lnb = ret_gn_g, conv_a_b, conv_a_ln_g, conv_a_ln_b
    g_pm, g_qm, g_pf, g_qf = norm_pre_mix, norm_post_mix, norm_pre_ffn, norm_post_ffn
    conv_a_t = jnp.transpose(state_conv_a, (0, 2, 1, 3))
    conv_c_t = jnp.transpose(state_conv_c, (0, 2, 1, 3))

    yp = x_prompt
    ys = x_sample.reshape(ns, D_MODEL)
    st_p = st_s = ()
    for l in range(DEPTH):
        proj, qt, kt, w_in_b = _sample_pre(l, ys, mod_s, g_pm, w_in, rope_s, yp)
        mix, *st_s = _sample_state(l, st_s, proj, qt, kt, state_ret, conv_a_t, conv_c_t, cd_s,
                                   gn, conv_a_w, cab, lng, lnb, conv_c_w)
        ys, w_out_b, w1_b, w3_b, w2_b = _sample_post(l, ys, mod_s, mix, w_out, g_qm, g_pf, g_qf,
                                                    ffn_w1, ffn_w3, ffn_w2)
        yp, *st_p = _prompt_layer(l, st_p, yp, mod_p, g_pm, g_qm, g_pf, g_qf, w_in_b, w_out_b, w1_b, w3_b, w2_b,
                                  tables_p, gn, conv_a_w, cab, lng, lnb, conv_c_w)

    sret_s, sca_s, scc_s = st_s
    sca_s = jnp.transpose(sca_s, (0, 2, 1, 3))
    scc_s = jnp.transpose(scc_s, (0, 2, 1, 3))
    return (yp, ys.reshape(ns, 1, D_MODEL)) + tuple(st_p) + (sret_s, sca_s, scc_s)
```

```python
import functools

import numpy as np
import jax
import jax.numpy as jnp
from jax import lax
from jax.experimental import pallas as pl
from jax.experimental.pallas import tpu as pltpu

D_MODEL = 1024
DEPTH = 2
PAST_LEN = 16384
RET_HEADS = 4
RET_WIDTH = D_MODEL // 2
RET_DV = RET_WIDTH // RET_HEADS
RET_DK = RET_DV // 2
QK_WIDTH = RET_HEADS * RET_DK
CONV_A_WIDTH = D_MODEL // 4
CONV_A_K = 31
CONV_C_WIDTH = D_MODEL - RET_WIDTH - CONV_A_WIDTH
CONV_C_K = 3
IN_WIDTH = 2 * QK_WIDTH + 2 * RET_WIDTH + 2 * CONV_A_WIDTH + 3 * CONV_C_WIDTH
D_FF = ((8 * D_MODEL // 3 + 255) // 256) * 256
RET_CHUNK = 128
ROPE_BASE = 10000.0
EPS = 1e-6

OFF_Q = 0
OFF_K = OFF_Q + QK_WIDTH
OFF_V = OFF_K + QK_WIDTH
OFF_G = OFF_V + RET_WIDTH
OFF_AV = OFF_G + RET_WIDTH
OFF_AG = OFF_AV + CONV_A_WIDTH
OFF_CB = OFF_AG + CONV_A_WIDTH
OFF_CC = OFF_CB + CONV_C_WIDTH
OFF_CX = OFF_CC + CONV_C_WIDTH
MIX_A = RET_WIDTH
MIX_C = RET_WIDTH + CONV_A_WIDTH

LANES = 128
SUBLANES = 8
V7X_VMEM_LIMIT_BYTES = 60 * 1024 * 1024

TOK_TILE = 512
ROW_BLK = 32
CONV_BLK = 64
CONV_STRIDE = 4
HIST_A = 32
HIST_C = 8
SEQ_BLK = 16
FF_BLK = 256
IN_BLK = 256
PRE_BLK = 1408

F32 = jnp.float32
BF16 = jnp.bfloat16


def _sigmoid(x):
    return jax.nn.sigmoid(x)


def _silu(x):
    return x * _sigmoid(x)


def _rms(x):
    return x * lax.rsqrt(jnp.mean(x * x, axis=-1, keepdims=True) + EPS)


def _layernorm(x, g, b=None):
    mu = jnp.mean(x, axis=-1, keepdims=True)
    d = x - mu
    var = jnp.mean(d * d, axis=-1, keepdims=True)
    y = d * lax.rsqrt(var + EPS) * g
    return y if b is None else y + b


def _rope_half(x, cos, sin, first_half):
    partner = jnp.where(first_half, pltpu.roll(x, 96, 1), pltpu.roll(x, 32, 1))
    return x * cos + partner * sin


def _first_half_mask(rows):
    lane = lax.broadcasted_iota(jnp.int32, (rows, LANES), 1)
    return (lane & (RET_DK - 1)) < (RET_DK // 2), lane < RET_DK


def _swiglu_block(h, w1_ref, w3_ref, col0):
    cols = pl.ds(col0, FF_BLK)
    a = jnp.dot(h, w1_ref[:, cols], preferred_element_type=F32)
    b = jnp.dot(h, w3_ref[:, cols], preferred_element_type=F32)
    return (_silu(a) * b).astype(BF16)


def _emit(items):
    for item in items:
        item()


def _emit_interleaved(primary, secondary):
    n, m = len(primary), len(secondary)
    done = 0
    for i, item in enumerate(primary):
        item()
        upto = ((i + 1) * m) // n
        _emit(secondary[done:upto])
        done = upto


def _rope_tables(pos, k_scale):
    half = RET_DK // 2
    inv = ROPE_BASE ** (-np.arange(half, dtype=np.float64) / half)
    ang = np.asarray(pos, np.float64)[:, None] * inv[None, :]
    cos = np.tile(np.cos(ang), (1, 4))
    sin = np.tile(np.concatenate([-np.sin(ang), np.sin(ang)], axis=1), (1, 2))
    return (cos * k_scale).astype(np.float32), (sin * k_scale).astype(np.float32)


def _decay_tables(chunk):
    log_g = np.log(1.0 - np.exp2(-5.0 - np.arange(RET_HEADS, dtype=np.float64)))
    idx = np.arange(chunk, dtype=np.float64)
    diff = idx[:, None] - idx[None, :]
    dmat = np.where(diff[None] >= 0, np.exp(np.maximum(diff, 0.0)[None] * log_g[:, None, None]), 0.0)
    read_dec = np.exp((idx + 1.0)[:, None] * log_g[None, :])
    upd_dec = np.exp((chunk - 1.0 - idx)[:, None] * log_g[None, :])
    chunk_dec = np.exp(chunk * log_g)
    pairs = RET_HEADS // 2
    side_by_side = lambda a: a.reshape(pairs, 2, *a.shape[1:]).transpose(0, 2, 1, 3).reshape(pairs, a.shape[1], -1)
    dmat2 = side_by_side(dmat)
    rd2 = side_by_side(np.broadcast_to(read_dec.T[:, :, None], (RET_HEADS, chunk, RET_DV)))
    updt = np.repeat(upd_dec.T, RET_DK, axis=0).reshape(pairs, 2 * RET_DK, chunk)
    cd = np.broadcast_to(chunk_dec[:, None, None], (RET_HEADS, 1, RET_DV))
    cd2 = side_by_side(cd)
    f = lambda a: np.ascontiguousarray(a, dtype=np.float32)
    return f(dmat2), f(rd2), f(updt), f(cd2), f(cd)


def _const_spec(shape, single_buffer=True):
    zeros = (0,) * len(shape)
    mode = dict(pipeline_mode=pl.Buffered(1)) if single_buffer else {}
    return pl.BlockSpec(shape, lambda j: zeros, **mode)


def _layer_spec(shape, layer, single_buffer=True):
    zeros = (0,) * len(shape)
    mode = dict(pipeline_mode=pl.Buffered(1)) if single_buffer else {}
    return pl.BlockSpec((None,) + tuple(shape), lambda j: (layer,) + zeros, **mode)


def _params(n_axes):
    return pltpu.CompilerParams(dimension_semantics=("arbitrary",) * n_axes,
                                vmem_limit_bytes=V7X_VMEM_LIMIT_BYTES)


def _stacked_call(kern, *, name, grid, in_specs, args, out_specs, out_shape, layer, stacked, prev, scratch_shapes):
    def spec(n_layers, block, index_fn):
        return pl.BlockSpec((n_layers,) + tuple(block), lambda *g: (0,) + tuple(index_fn(*g)))

    in_specs, out_specs, out_shape, args = list(in_specs), list(out_specs), list(out_shape), list(args)
    for dims, block, index_fn in stacked:
        out_specs.append(spec(layer + 1, block, index_fn))
        out_shape.append(jax.ShapeDtypeStruct((layer + 1,) + tuple(dims), F32))
        if layer > 0:
            in_specs.append(spec(layer, block, index_fn))
    if layer > 0:
        args += list(prev)
    return pl.pallas_call(
        kern, grid=grid, in_specs=in_specs, out_specs=out_specs, out_shape=out_shape,
        scratch_shapes=scratch_shapes, compiler_params=_params(len(grid)), name=name)(*args)


def _ada_kernel(cs_ref, cp_ref, w_ref, b_ref, os_ref, op_ref):
    layer = pl.program_id(0)
    bias = b_ref[0:1, :]
    for d in range(1, DEPTH):
        bias = jnp.where(layer == d, b_ref[d:d + 1, :], bias)
    w = w_ref[...].astype(BF16)
    os_ref[...] = jnp.dot(_silu(cs_ref[...]).astype(BF16), w, preferred_element_type=F32) + bias
    mod_p = jnp.dot(_silu(cp_ref[...]).astype(BF16), w, preferred_element_type=F32) + bias
    for b in range(cp_ref.shape[0]):
        op_ref[b] = mod_p[b:b + 1, :]


def _ada_modulation(c_sample, c_prompt, ada_w, ada_b):
    ns, bp = c_sample.shape[0], c_prompt.shape[0]
    ncol = 6 * D_MODEL
    blk = 2 * D_MODEL
    return pl.pallas_call(
        _ada_kernel,
        grid=(DEPTH, ncol // blk),
        in_specs=[
            pl.BlockSpec((ns, D_MODEL), lambda l, j: (0, 0)),
            pl.BlockSpec((bp, D_MODEL), lambda l, j: (0, 0)),
            pl.BlockSpec((None, D_MODEL, blk), lambda l, j: (l, 0, j)),
            pl.BlockSpec((DEPTH, blk), lambda l, j: (0, j)),
        ],
        out_specs=[pl.BlockSpec((None, ns, blk), lambda l, j: (l, 0, j)),
                   pl.BlockSpec((None, bp, 1, blk), lambda l, j: (l, 0, 0, j))],
        out_shape=[jax.ShapeDtypeStruct((DEPTH, ns, ncol), F32),
                   jax.ShapeDtypeStruct((DEPTH, bp, 1, ncol), F32)],
        compiler_params=_params(2),
        name="ada_mod",
    )(c_sample, c_prompt, ada_w, ada_b)


def _prompt_layer_kernel(*refs, tile, n_tiles, tiles_per_seq, layer):
    (xc_ref, xp_ref, modc_ref, modp_ref, gpm_ref, gqm_ref, gpf_ref, gqf_ref,
     win_ref, wout_ref, w1_ref, w3_ref, w2_ref,
     cq_ref, sq_ref, ck_ref, sk_ref, dmat_ref, rd_ref, upd_ref, cd_ref,
     gn_ref, caw_ref, cab_ref, lng_ref, lnb_ref, ccw_ref) = refs[:27]
    n_prev = 3 if layer > 0 else 0
    prev_states = refs[27:27 + n_prev]
    (y_ref, sret_all, sca_all, scc_all,
     h_scr, proj_scr, mixin_scr, mix_scr, hf_scr, p_scr, ubuf, ua_scr, zbuf,
     sc_scr, qs_scr, sbd_scr) = refs[27 + n_prev:]
    sret_ref, sca_ref, scc_ref = sret_all.at[layer], sca_all.at[layer], scc_all.at[layer]
    gpm_ref, gqm_ref, gpf_ref, gqf_ref, gn_ref, cab_ref, lng_ref, lnb_ref = (
        ref.at[pl.ds(layer, 1)] for ref in (gpm_ref, gqm_ref, gpf_ref, gqf_ref, gn_ref, cab_ref, lng_ref, lnb_ref))
    mod_chunk = lambda ref, k: ref[:, pl.ds(k * D_MODEL, D_MODEL)]
    step = pl.program_id(0)
    row_blocks = [pl.ds(r * ROW_BLK, ROW_BLK) for r in range(tile // ROW_BLK)]
    first_tap = HIST_A - (CONV_A_K - 1)
    first_tap_c = HIST_C - (CONV_C_K - 1)
    span = SUBLANES * CONV_STRIDE

    @pl.when((step < n_tiles) & (lax.rem(step, tiles_per_seq) == 0))
    def _():
        for prev_ref, all_ref in zip(prev_states, (sret_all, sca_all, scc_all)):
            all_ref[0:layer] = prev_ref[...]
        sret_ref[...] = jnp.zeros_like(sret_ref)
        sbd_scr[...] = jnp.zeros_like(sbd_scr)
        ubuf[:, 0:HIST_A, :] = jnp.zeros((2, HIST_A, LANES), F32)
        zbuf[0:HIST_C, :] = jnp.zeros((HIST_C, CONV_C_WIDTH), F32)

    def mix_prenorm():
        sh = mod_chunk(modc_ref, 0)
        pre_scale = gpm_ref[...] * (1.0 + mod_chunk(modc_ref, 1))
        for rows in row_blocks:
            h_scr[rows, :] = (_rms(xc_ref[rows, :]) * pre_scale + sh).astype(BF16)

    def in_proj(n):
        cols = pl.ds(n * IN_BLK, IN_BLK)
        proj_scr[:, cols] = jnp.dot(h_scr[...], win_ref[:, cols], preferred_element_type=F32)

    def conv_groups():
        taps = [[jnp.broadcast_to(caw_ref[j:j + 1, pl.ds(half * LANES, LANES)], (SUBLANES, LANES))
                 for j in range(CONV_A_K)] for half in range(2)]
        for r in range(tile // CONV_BLK):
            rows = pl.ds(r * CONV_BLK, CONV_BLK)
            dst = pl.ds(HIST_A + r * CONV_BLK, CONV_BLK)
            for half in range(2):
                lo = half * LANES
                u = (proj_scr[rows, pl.ds(OFF_AV + lo, LANES)]
                     * _sigmoid(proj_scr[rows, pl.ds(OFF_AG + lo, LANES)]))
                ubuf[half, dst, :] = u
        for half in range(2):
            for m in range(tile // span):
                base = m * span
                acc = [None] * CONV_STRIDE
                for k in range(CONV_STRIDE + CONV_A_K - 1):
                    win = ubuf[half, pl.ds(base + first_tap + k, SUBLANES, stride=CONV_STRIDE), :]
                    for t in range(CONV_STRIDE):
                        j = k - t
                        if 0 <= j < CONV_A_K:
                            term = taps[half][j] * win
                            acc[t] = term if acc[t] is None else acc[t] + term
                for t in range(CONV_STRIDE):
                    ua_scr[half, pl.ds(base + t, SUBLANES, stride=CONV_STRIDE), :] = acc[t]
        for r in range(tile // CONV_BLK):
            rows = pl.ds(r * CONV_BLK, CONV_BLK)
            ua = jnp.concatenate([ua_scr[0, rows, :], ua_scr[1, rows, :]], axis=-1) + cab_ref[...]
            o_a = _silu(_layernorm(ua, lng_ref[...], lnb_ref[...]))
            mixin_scr[rows, MIX_A:MIX_A + CONV_A_WIDTH] = o_a.astype(BF16)
        for r in range(tile // CONV_BLK):
            rows = pl.ds(r * CONV_BLK, CONV_BLK)
            z = proj_scr[rows, OFF_CC:OFF_CC + CONV_C_WIDTH] * proj_scr[rows, OFF_CX:OFF_CX + CONV_C_WIDTH]
            zbuf[pl.ds(HIST_C + r * CONV_BLK, CONV_BLK), :] = z
        for r in range(tile // CONV_BLK):
            rows = pl.ds(r * CONV_BLK, CONV_BLK)
            zc = ccw_ref[0:1, :] * zbuf[pl.ds(r * CONV_BLK + first_tap_c, CONV_BLK), :]
            for j in range(1, CONV_C_K):
                zc = zc + ccw_ref[j:j + 1, :] * zbuf[pl.ds(r * CONV_BLK + first_tap_c + j, CONV_BLK), :]
            o_c = proj_scr[rows, OFF_CB:OFF_CB + CONV_C_WIDTH] * zc
            mixin_scr[rows, MIX_C:MIX_C + CONV_C_WIDTH] = o_c.astype(BF16)
        hist_a = pl.ds(tile + first_tap, CONV_A_K - 1)
        sca_ref[...] = jnp.concatenate([ubuf[0, hist_a, :], ubuf[1, hist_a, :]], axis=-1)
        scc_ref[...] = zbuf[pl.ds(tile + first_tap_c, CONV_C_K - 1), :]
        ubuf[:, 0:HIST_A, :] = ubuf[:, pl.ds(tile, HIST_A), :]
        zbuf[0:HIST_C, :] = zbuf[pl.ds(tile, HIST_C), :]

    def retention_scores(i, pair, buf):
        first_half, _ = _first_half_mask(RET_CHUNK)
        rows = pl.ds(i * RET_CHUNK, RET_CHUNK)
        cq, sq, ck, sk = cq_ref[rows, :], sq_ref[rows, :], ck_ref[rows, :], sk_ref[rows, :]
        qr = _rope_half(proj_scr[rows, pl.ds(OFF_Q + pair * LANES, LANES)], cq, sq, first_half)
        kr = _rope_half(proj_scr[rows, pl.ds(OFF_K + pair * LANES, LANES)], ck, sk, first_half)
        qr_b = qr.astype(BF16)
        k_t = kr.T
        head0_rows = lax.broadcasted_iota(jnp.int32, (2 * RET_DK, RET_CHUNK), 0) < RET_DK
        k_bd = jnp.concatenate([jnp.where(head0_rows, k_t, 0.0), jnp.where(head0_rows, 0.0, k_t)], axis=1)
        scores = jnp.dot(qr_b, k_bd.astype(BF16), preferred_element_type=F32) * dmat_ref[pair]
        sc_scr[buf] = scores.astype(BF16)
        s_bd = sbd_scr[pair]
        qs_scr[buf] = jnp.dot(qr_b, s_bd.astype(BF16), preferred_element_type=F32) * rd_ref[pair]
        v_pair = proj_scr[rows, pl.ds(OFF_V + 2 * pair * RET_DV, 2 * RET_DV)].astype(BF16)
        kv = jnp.dot((k_t * upd_ref[pair]).astype(BF16), v_pair, preferred_element_type=F32)
        r0 = lax.broadcasted_iota(jnp.int32, (2 * RET_DK, 2 * RET_DV), 0) < RET_DK
        c0 = lax.broadcasted_iota(jnp.int32, (2 * RET_DK, 2 * RET_DV), 1) < RET_DV
        s_new = s_bd * cd_ref[pair] + jnp.where(r0 == c0, kv, 0.0)
        sbd_scr[pair] = s_new
        sret_ref[2 * pair] = s_new[0:RET_DK, 0:RET_DV]
        sret_ref[2 * pair + 1] = s_new[RET_DK:2 * RET_DK, RET_DV:2 * RET_DV]

    def retention_values(i, pair, buf):
        rows = pl.ds(i * RET_CHUNK, RET_CHUNK)
        for hl in range(2):
            h = 2 * pair + hl
            vh = proj_scr[rows, pl.ds(OFF_V + h * RET_DV, RET_DV)].astype(BF16)
            o = (jnp.dot(sc_scr[buf, :, pl.ds(hl * RET_CHUNK, RET_CHUNK)], vh, preferred_element_type=F32)
                 + qs_scr[buf, :, pl.ds(hl * RET_DV, RET_DV)])
            gate = proj_scr[rows, pl.ds(OFF_G + h * RET_DV, RET_DV)]
            o = _layernorm(o, gn_ref[:, pl.ds(h * RET_DV, RET_DV)]) * _silu(gate)
            mixin_scr[rows, pl.ds(h * RET_DV, RET_DV)] = o.astype(BF16)

    def out_proj():
        mix_scr[...] = jnp.dot(mixin_scr[...], wout_ref[...], preferred_element_type=F32)

    in_tiles = [functools.partial(in_proj, n) for n in range(IN_WIDTH // IN_BLK)]
    items = [(i, pair) for i in range(tile // RET_CHUNK) for pair in range(RET_HEADS // 2)]
    ret_items = []
    for k, (i, pair) in enumerate(items):
        ret_items.append(functools.partial(retention_scores, i, pair, k % 2))
        if k > 0:
            ret_items.append(functools.partial(retention_values, *items[k - 1], (k - 1) % 2))
    ret_items.append(functools.partial(retention_values, *items[-1], (len(items) - 1) % 2))
    n_qkvg = OFF_AV // IN_BLK

    def ffn_norms():
        post_scale = gqm_ref[...] * mod_chunk(modp_ref, 2)
        sh = mod_chunk(modp_ref, 3)
        pre_scale = gpf_ref[...] * (1.0 + mod_chunk(modp_ref, 4))
        for rows in row_blocks:
            x1 = xp_ref[rows, :] + _rms(mix_scr[rows, :]) * post_scale
            y_ref[rows, :] = x1
            hf_scr[rows, :] = (_rms(x1) * pre_scale + sh).astype(BF16)

    def hidden(j):
        p_scr[:, pl.ds(j * FF_BLK, FF_BLK)] = _swiglu_block(hf_scr[...], w1_ref, w3_ref, j * FF_BLK)

    def down(n):
        cols = pl.ds(n * FF_BLK, FF_BLK)
        proj_scr[:, cols] = jnp.dot(p_scr[...], w2_ref[:, cols], preferred_element_type=F32)

    def ffn_finish():
        post_scale = gqf_ref[...] * mod_chunk(modp_ref, 5)
        for rows in row_blocks:
            y_ref[rows, :] = y_ref[rows, :] + _rms(proj_scr[rows, 0:D_MODEL]) * post_scale

    hidden_blocks = [functools.partial(hidden, j) for j in range(D_FF // FF_BLK)]
    down_tiles = [functools.partial(down, n) for n in range(D_MODEL // FF_BLK)]

    @pl.when(step == 0)
    def _():
        mix_prenorm()
        _emit(in_tiles)
        _emit(ret_items)
        conv_groups()
        out_proj()

    @pl.when((step > 0) & (step < n_tiles))
    def _():
        mix_prenorm()
        ffn_norms()
        _emit(in_tiles[:n_qkvg])
        _emit_interleaved(in_tiles[n_qkvg:] + hidden_blocks, ret_items)
        conv_groups()
        _emit(down_tiles)
        out_proj()
        ffn_finish()

    @pl.when(step == n_tiles)
    def _():
        ffn_norms()
        _emit(hidden_blocks)
        _emit(down_tiles)
        ffn_finish()


def _prompt_layer(layer, prev, x, mod_p, g_pm, g_qm, g_pf, g_qf, w_in, w_out, w1, w3, w2, tables,
                  gn, caw, cab, lng, lnb, ccw):
    bsz, seq, _ = x.shape
    tile = TOK_TILE
    tps = seq // tile
    n_tiles = bsz * tps
    cq, sq, ck, sk, dmat, rd, upd, cd, _ = tables
    cur = lambda i: jnp.minimum(i, n_tiles - 1)
    prv = lambda i: jnp.maximum(i - 1, 0)
    tok_cur = pl.BlockSpec((None, tile, D_MODEL), lambda i: (cur(i) // tps, cur(i) % tps, 0))
    tok_prev = pl.BlockSpec((None, tile, D_MODEL), lambda i: (prv(i) // tps, prv(i) % tps, 0))
    mod_cur = pl.BlockSpec((None, None, 1, 6 * D_MODEL), lambda i: (layer, cur(i) // tps, 0, 0))
    mod_prev = pl.BlockSpec((None, None, 1, 6 * D_MODEL), lambda i: (layer, prv(i) // tps, 0, 0))
    rope = pl.BlockSpec((tile, LANES), lambda i: (cur(i) % tps, 0))
    cs = _const_spec
    ls = functools.partial(_layer_spec, layer=layer)
    args = (x, x, mod_p, mod_p, g_pm, g_qm, g_pf, g_qf, w_in, w_out, w1, w3, w2,
            cq, sq, ck, sk, dmat, rd, upd, cd, gn, caw, cab, lng, lnb, ccw)
    seq_block = lambda i: (cur(i) // tps,)
    return _stacked_call(
        functools.partial(_prompt_layer_kernel, tile=tile, n_tiles=n_tiles, tiles_per_seq=tps, layer=layer),
        name="prompt_layer", grid=(n_tiles + 1,), args=args, layer=layer, prev=prev,
        stacked=[
            ((bsz, RET_HEADS, RET_DK, RET_DV), (None, RET_HEADS, RET_DK, RET_DV),
             lambda i: seq_block(i) + (0, 0, 0)),
            ((bsz, CONV_A_K - 1, CONV_A_WIDTH), (None, CONV_A_K - 1, CONV_A_WIDTH),
             lambda i: seq_block(i) + (0, 0)),
            ((bsz, CONV_C_K - 1, CONV_C_WIDTH), (None, CONV_C_K - 1, CONV_C_WIDTH),
             lambda i: seq_block(i) + (0, 0)),
        ],
        in_specs=[
            tok_cur, tok_prev, mod_cur, mod_prev,
            cs((DEPTH, D_MODEL)), cs((DEPTH, D_MODEL)), cs((DEPTH, D_MODEL)), cs((DEPTH, D_MODEL)),
            cs((D_MODEL, IN_WIDTH)), cs((D_MODEL, D_MODEL)),
            cs((D_MODEL, D_FF)), cs((D_MODEL, D_FF)), cs((D_FF, D_MODEL)),
            rope, rope, rope, rope,
            cs(dmat.shape), cs(rd.shape), cs(upd.shape), cs(cd.shape),
            cs((DEPTH, RET_WIDTH)), ls((CONV_A_K, CONV_A_WIDTH)), cs((DEPTH, CONV_A_WIDTH)),
            cs((DEPTH, CONV_A_WIDTH)), cs((DEPTH, CONV_A_WIDTH)), ls((CONV_C_K, CONV_C_WIDTH)),
        ],
        out_specs=[tok_prev],
        out_shape=[jax.ShapeDtypeStruct(x.shape, F32)],
        scratch_shapes=[
            pltpu.VMEM((tile, D_MODEL), BF16),
            pltpu.VMEM((tile, IN_WIDTH), F32),
            pltpu.VMEM((tile, D_MODEL), BF16),
            pltpu.VMEM((tile, D_MODEL), F32),
            pltpu.VMEM((tile, D_MODEL), BF16),
            pltpu.VMEM((tile, D_FF), BF16),
            pltpu.VMEM((2, tile + HIST_A, LANES), F32),
            pltpu.VMEM((2, tile, LANES), F32),
            pltpu.VMEM((tile + HIST_C, CONV_C_WIDTH), F32),
            pltpu.VMEM((2, RET_CHUNK, 2 * RET_CHUNK), BF16),
            pltpu.VMEM((2, RET_CHUNK, 2 * RET_DV), F32),
            pltpu.VMEM((RET_HEADS // 2, 2 * RET_DK, 2 * RET_DV), F32),
        ])


def _sample_pre_kernel(x_ref, mod_ref, gpre_ref, win_ref, rope_ref, after_ref, proj_ref, qt_ref, kt_ref, winb_ref,
                       h_scr, *, layer):
    j = pl.program_id(0)
    n = x_ref.shape[0]

    @pl.when(j == 0)
    def _():
        sh = mod_ref[:, 0:D_MODEL]
        sc1 = 1.0 + mod_ref[:, D_MODEL:2 * D_MODEL]
        h_scr[...] = (_rms(x_ref[...]) * (gpre_ref[pl.ds(layer, 1), :] * sc1) + sh).astype(BF16)

    w_blk = win_ref[...].astype(BF16)
    winb_ref[...] = w_blk
    proj_ref[...] = jnp.dot(h_scr[...], w_blk, preferred_element_type=F32)

    def rope_transposed(off, cos, sin, dst_ref):
        first_half, _ = _first_half_mask(n)
        for pair in range(RET_HEADS // 2):
            src = proj_ref[:, pl.ds(off + pair * LANES, LANES)]
            dst_ref[pl.ds(pair * LANES, LANES), :] = _rope_half(src, cos, sin, first_half).T

    @pl.when(j == 0)
    def _():
        rope_transposed(OFF_Q, rope_ref[0:1, :], rope_ref[1:2, :], qt_ref)
        rope_transposed(OFF_K, rope_ref[2:3, :], rope_ref[3:4, :], kt_ref)


def _sample_pre(layer, xs, mod, gpre, w_in, rope_s, after):
    n = xs.shape[0]
    cs = functools.partial(_const_spec, single_buffer=False)
    ls = functools.partial(_layer_spec, layer=layer, single_buffer=False)
    return pl.pallas_call(
        functools.partial(_sample_pre_kernel, layer=layer),
        grid=(IN_WIDTH // PRE_BLK,),
        in_specs=[cs((n, D_MODEL)), ls((n, 6 * D_MODEL)), cs((DEPTH, D_MODEL)),
                  pl.BlockSpec((None, D_MODEL, PRE_BLK), lambda j: (layer, 0, j)), cs(rope_s.shape),
                  pl.BlockSpec(memory_space=pl.ANY)],
        out_specs=[pl.BlockSpec((n, PRE_BLK), lambda j: (0, j)), cs((QK_WIDTH, n)), cs((QK_WIDTH, n)),
                   pl.BlockSpec((D_MODEL, PRE_BLK), lambda j: (0, j))],
        out_shape=[jax.ShapeDtypeStruct((n, IN_WIDTH), F32),
                   jax.ShapeDtypeStruct((QK_WIDTH, n), F32),
                   jax.ShapeDtypeStruct((QK_WIDTH, n), F32),
                   jax.ShapeDtypeStruct((D_MODEL, IN_WIDTH), BF16)],
        scratch_shapes=[pltpu.VMEM((n, D_MODEL), BF16)],
        compiler_params=_params(1),
        name="sample_pre",
    )(xs, mod, gpre, w_in, rope_s, after)


def _sample_state_kernel(*refs, layer):
    (proj_ref, qt_ref, kt_ref, sin_ref, bufa_ref, bufc_ref, cd_ref,
     gn_ref, caw_ref, cab_ref, lng_ref, lnb_ref, ccw_ref) = refs[:13]
    n_prev = 3 if layer > 0 else 0
    prev_states = refs[13:13 + n_prev]
    mix_ref, sout_all, outa_all, outc_all, o_scr = refs[13 + n_prev:]
    for prev_ref, all_ref in zip(prev_states, (sout_all, outa_all, outc_all)):
        all_ref[0:layer] = prev_ref[...]
    sout_ref, outa_ref, outc_ref = sout_all.at[layer], outa_all.at[layer], outc_all.at[layer]
    gn_ref, cab_ref, lng_ref, lnb_ref = (ref.at[pl.ds(layer, 1)] for ref in (gn_ref, cab_ref, lng_ref, lnb_ref))
    blk = pl.program_id(0)
    n = qt_ref.shape[1]
    lane = lax.broadcasted_iota(jnp.int32, (QK_WIDTH, n), 1)

    for bl in range(SEQ_BLK):
        onehot = lane == (blk * SEQ_BLK + bl)
        qcol = jnp.sum(jnp.where(onehot, qt_ref[...], 0.0), axis=1, keepdims=True)
        kcol = jnp.sum(jnp.where(onehot, kt_ref[...], 0.0), axis=1, keepdims=True)
        row = pl.ds(bl, 1)
        for h in range(RET_HEADS):
            vrow = proj_ref[row, pl.ds(OFF_V + h * RET_DV, RET_DV)]
            s_new = (sin_ref[bl, h] * cd_ref[h]
                     + kcol[h * RET_DK:(h + 1) * RET_DK, :] * vrow)
            sout_ref[bl, h] = s_new
            o_scr[row, pl.ds(h * RET_DV, RET_DV)] = jnp.sum(
                qcol[h * RET_DK:(h + 1) * RET_DK, :] * s_new, axis=0, keepdims=True)
    for h in range(RET_HEADS):
        cols = pl.ds(h * RET_DV, RET_DV)
        o = _layernorm(o_scr[:, cols], gn_ref[:, cols]) * _silu(proj_ref[:, pl.ds(OFF_G + h * RET_DV, RET_DV)])
        mix_ref[:, cols] = o

    u = proj_ref[:, OFF_AV:OFF_AV + CONV_A_WIDTH] * _sigmoid(proj_ref[:, OFF_AG:OFF_AG + CONV_A_WIDTH])
    ua = caw_ref[CONV_A_K - 1:CONV_A_K, :] * u
    for j in range(CONV_A_K - 1):
        ua = ua + caw_ref[j:j + 1, :] * bufa_ref[j]
    outa_ref[0:CONV_A_K - 2] = bufa_ref[1:CONV_A_K - 1]
    outa_ref[CONV_A_K - 2] = u
    mix_ref[:, MIX_A:MIX_A + CONV_A_WIDTH] = _silu(_layernorm(ua + cab_ref[...], lng_ref[...], lnb_ref[...]))

    z = proj_ref[:, OFF_CC:OFF_CC + CONV_C_WIDTH] * proj_ref[:, OFF_CX:OFF_CX + CONV_C_WIDTH]
    zc = ccw_ref[0:1, :] * bufc_ref[0] + ccw_ref[1:2, :] * bufc_ref[1] + ccw_ref[2:3, :] * z
    outc_ref[0] = bufc_ref[1]
    outc_ref[1] = z
    mix_ref[:, MIX_C:MIX_C + CONV_C_WIDTH] = proj_ref[:, OFF_CB:OFF_CB + CONV_C_WIDTH] * zc


def _sample_state(layer, prev, proj, qt, kt, s_ret, buf_a_t, buf_c_t, cd, gn, caw, cab, lng, lnb, ccw):
    n = proj.shape[0]
    cs = functools.partial(_const_spec, single_buffer=False)
    ls = functools.partial(_layer_spec, layer=layer, single_buffer=False)
    s_spec = pl.BlockSpec((None, SEQ_BLK, RET_HEADS, RET_DK, RET_DV), lambda j: (layer, j, 0, 0, 0))
    a_spec = pl.BlockSpec((None, CONV_A_K - 1, SEQ_BLK, CONV_A_WIDTH), lambda j: (layer, 0, j, 0))
    c_spec = pl.BlockSpec((None, CONV_C_K - 1, SEQ_BLK, CONV_C_WIDTH), lambda j: (layer, 0, j, 0))
    args = (proj, qt, kt, s_ret, buf_a_t, buf_c_t, cd, gn, caw, cab, lng, lnb, ccw)
    return _stacked_call(
        functools.partial(_sample_state_kernel, layer=layer),
        name="sample_state", grid=(n // SEQ_BLK,), args=args, layer=layer, prev=prev,
        stacked=[
            ((n, RET_HEADS, RET_DK, RET_DV), (SEQ_BLK, RET_HEADS, RET_DK, RET_DV), lambda j: (j, 0, 0, 0)),
            ((CONV_A_K - 1, n, CONV_A_WIDTH), (CONV_A_K - 1, SEQ_BLK, CONV_A_WIDTH), lambda j: (0, j, 0)),
            ((CONV_C_K - 1, n, CONV_C_WIDTH), (CONV_C_K - 1, SEQ_BLK, CONV_C_WIDTH), lambda j: (0, j, 0)),
        ],
        in_specs=[
            pl.BlockSpec((SEQ_BLK, IN_WIDTH), lambda j: (j, 0)),
            cs((QK_WIDTH, n)), cs((QK_WIDTH, n)),
            s_spec, a_spec, c_spec,
            cs(cd.shape), cs((DEPTH, RET_WIDTH)), ls((CONV_A_K, CONV_A_WIDTH)), cs((DEPTH, CONV_A_WIDTH)),
            cs((DEPTH, CONV_A_WIDTH)), cs((DEPTH, CONV_A_WIDTH)), ls((CONV_C_K, CONV_C_WIDTH)),
        ],
        out_specs=[pl.BlockSpec((SEQ_BLK, D_MODEL), lambda j: (j, 0))],
        out_shape=[jax.ShapeDtypeStruct((n, D_MODEL), F32)],
        scratch_shapes=[pltpu.VMEM((SEQ_BLK, RET_WIDTH), F32)])


def _sample_post_kernel(x_ref, mod_ref, mix_ref, wout_ref, gpm_ref, gpf_ref, gqf_ref, w1_ref, w3_ref, w2_ref,
                        y_ref, woutb_ref, w1b_ref, w3b_ref, w2b_ref, x1_scr, h_scr, f_scr, *, n_steps, layer):
    j = pl.program_id(0)
    gpm_ref, gpf_ref, gqf_ref = (ref.at[pl.ds(layer, 1)] for ref in (gpm_ref, gpf_ref, gqf_ref))

    @pl.when(j == 0)
    def _():
        gt_m = mod_ref[:, 2 * D_MODEL:3 * D_MODEL]
        wout_b = wout_ref[...].astype(BF16)
        woutb_ref[...] = wout_b
        mix = jnp.dot(mix_ref[...].astype(BF16), wout_b, preferred_element_type=F32)
        x1 = x_ref[...] + _rms(mix) * (gpm_ref[...] * gt_m)
        x1_scr[...] = x1
        sh = mod_ref[:, 3 * D_MODEL:4 * D_MODEL]
        sc1 = 1.0 + mod_ref[:, 4 * D_MODEL:5 * D_MODEL]
        h_scr[...] = (_rms(x1) * (gpf_ref[...] * sc1) + sh).astype(BF16)
        f_scr[...] = jnp.zeros_like(f_scr)

    w1_b, w3_b, w2_b = w1_ref[...].astype(BF16), w3_ref[...].astype(BF16), w2_ref[...].astype(BF16)
    w1b_ref[...] = w1_b
    w3b_ref[...] = w3_b
    w2b_ref[...] = w2_b
    h = h_scr[...]
    a = jnp.dot(h, w1_b, preferred_element_type=F32)
    b = jnp.dot(h, w3_b, preferred_element_type=F32)
    p = (_silu(a) * b).astype(BF16)
    f_scr[...] += jnp.dot(p, w2_b, preferred_element_type=F32)

    @pl.when(j == n_steps - 1)
    def _():
        gt_f = mod_ref[:, 5 * D_MODEL:6 * D_MODEL]
        y_ref[...] = x1_scr[...] + _rms(f_scr[...]) * (gqf_ref[...] * gt_f)


def _sample_post(layer, xs, mod, mix, w_out, gpost_m, gpre_f, gpost_f, w1, w3, w2):
    n = xs.shape[0]
    n_steps = D_FF // FF_BLK
    cs = functools.partial(_const_spec, single_buffer=False)
    ls = functools.partial(_layer_spec, layer=layer, single_buffer=False)
    return pl.pallas_call(
        functools.partial(_sample_post_kernel, n_steps=n_steps, layer=layer),
        grid=(n_steps,),
        in_specs=[
            cs((n, D_MODEL)), ls((n, 6 * D_MODEL)), cs((n, D_MODEL)), ls((D_MODEL, D_MODEL)),
            cs((DEPTH, D_MODEL)), cs((DEPTH, D_MODEL)), cs((DEPTH, D_MODEL)),
            pl.BlockSpec((None, D_MODEL, FF_BLK), lambda j: (layer, 0, j)),
            pl.BlockSpec((None, D_MODEL, FF_BLK), lambda j: (layer, 0, j)),
            pl.BlockSpec((None, FF_BLK, D_MODEL), lambda j: (layer, j, 0)),
        ],
        out_specs=[
            cs((n, D_MODEL)), cs((D_MODEL, D_MODEL)),
            pl.BlockSpec((D_MODEL, FF_BLK), lambda j: (0, j)),
            pl.BlockSpec((D_MODEL, FF_BLK), lambda j: (0, j)),
            pl.BlockSpec((FF_BLK, D_MODEL), lambda j: (j, 0)),
        ],
        out_shape=[
            jax.ShapeDtypeStruct((n, D_MODEL), F32),
            jax.ShapeDtypeStruct((D_MODEL, D_MODEL), BF16),
            jax.ShapeDtypeStruct((D_MODEL, D_FF), BF16),
            jax.ShapeDtypeStruct((D_MODEL, D_FF), BF16),
            jax.ShapeDtypeStruct((D_FF, D_MODEL), BF16),
        ],
        scratch_shapes=[
            pltpu.VMEM((n, D_MODEL), F32),
            pltpu.VMEM((n, D_MODEL), BF16),
            pltpu.VMEM((n, D_MODEL), F32),
        ],
        compiler_params=_params(1),
        name="sample_post",
    )(xs, mod, mix, w_out, gpost_m, gpre_f, gpost_f, w1, w3, w2)


def kernel(x_prompt, x_sample, c_prompt, c_sample, state_ret, state_conv_a, state_conv_c, ada_w, ada_b, norm_pre_mix, norm_post_mix, norm_pre_ffn, norm_post_ffn, w_in, w_out, ret_gn_g, conv_a_w, conv_a_b, conv_a_ln_g, conv_a_ln_b, conv_c_w, ffn_w1, ffn_w3, ffn_w2):
    bp, lp, _ = x_prompt.shape
    ns = x_sample.shape[0]
    assert x_sample.shape[1] == 1 and ns % SEQ_BLK == 0
    assert OFF_K + QK_WIDTH <= PRE_BLK and IN_WIDTH % PRE_BLK == 0
    assert lp % TOK_TILE == 0 and TOK_TILE % RET_CHUNK == 0
    assert TOK_TILE % CONV_BLK == 0 and CONV_BLK % (SUBLANES * CONV_STRIDE) == 0

    k_scale = RET_DK ** -0.5
    cq, sq = _rope_tables(np.arange(lp), 1.0)
    ck, sk = _rope_tables(np.arange(lp), k_scale)
    tables_p = (cq, sq, ck, sk) + _decay_tables(RET_CHUNK)
    cqs, sqs = _rope_tables([PAST_LEN], 1.0)
    cks, sks = _rope_tables([PAST_LEN], k_scale)
    rope_s = np.concatenate([cqs, sqs, cks, sks, np.zeros((4, LANES), np.float32)], axis=0)
    cd_s = _decay_tables(1)[4]

    mod_s, mod_p = _ada_modulation(c_sample, c_prompt, ada_w, ada_b)

    gn, cab, lng, lnb = ret_gn_g, conv_a_b, conv_a_ln_g, conv_a_ln_b
    g_pm, g_qm, g_pf, g_qf = norm_pre_mix, norm_post_mix, norm_pre_ffn, norm_post_ffn
    conv_a_t = jnp.transpose(state_conv_a, (0, 2, 1, 3))
    conv_c_t = jnp.transpose(state_conv_c, (0, 2, 1, 3))

    yp = x_prompt
    ys = x_sample.reshape(ns, D_MODEL)
    st_p = st_s = ()
    for l in range(DEPTH):
        proj, qt, kt, w_in_b = _sample_pre(l, ys, mod_s, g_pm, w_in, rope_s, yp)
        mix, *st_s = _sample_state(l, st_s, proj, qt, kt, state_ret, conv_a_t, conv_c_t, cd_s,
                                   gn, conv_a_w, cab, lng, lnb, conv_c_w)
        ys, w_out_b, w1_b, w3_b, w2_b = _sample_post(l, ys, mod_s, mix, w_out, g_qm, g_pf, g_qf,
                                                    ffn_w1, ffn_w3, ffn_w2)
        yp, *st_p = _prompt_layer(l, st_p, yp, mod_p, g_pm, g_qm, g_pf, g_qf, w_in_b, w_out_b, w1_b, w3_b, w2_b,
                                  tables_p, gn, conv_a_w, cab, lng, lnb, conv_c_w)

    sret_s, sca_s, scc_s = st_s
    sca_s = jnp.transpose(sca_s, (0, 2, 1, 3))
    scc_s = jnp.transpose(scc_s, (0, 2, 1, 3))
    return (yp, ys.reshape(ns, 1, D_MODEL)) + tuple(st_p) + (sret_s, sca_s, scc_s)
```

```python
import functools

import numpy as np
import jax
import jax.numpy as jnp
from jax import lax
from jax.experimental import pallas as pl
from jax.experimental.pallas import tpu as pltpu

D_MODEL = 1024
DEPTH = 2
PAST_LEN = 16384
RET_HEADS = 4
RET_WIDTH = D_MODEL // 2
RET_DV = RET_WIDTH // RET_HEADS
RET_DK = RET_DV // 2
QK_WIDTH = RET_HEADS * RET_DK
CONV_A_WIDTH = D_MODEL // 4
CONV_A_K = 31
CONV_C_WIDTH = D_MODEL - RET_WIDTH - CONV_A_WIDTH
CONV_C_K = 3
IN_WIDTH = 2 * QK_WIDTH + 2 * RET_WIDTH + 2 * CONV_A_WIDTH + 3 * CONV_C_WIDTH
D_FF = ((8 * D_MODEL // 3 + 255) // 256) * 256
RET_CHUNK = 128
ROPE_BASE = 10000.0
EPS = 1e-6

OFF_Q = 0
OFF_K = OFF_Q + QK_WIDTH
OFF_V = OFF_K + QK_WIDTH
OFF_G = OFF_V + RET_WIDTH
OFF_AV = OFF_G + RET_WIDTH
OFF_AG = OFF_AV + CONV_A_WIDTH
OFF_CB = OFF_AG + CONV_A_WIDTH
OFF_CC = OFF_CB + CONV_C_WIDTH
OFF_CX = OFF_CC + CONV_C_WIDTH
MIX_A = RET_WIDTH
MIX_C = RET_WIDTH + CONV_A_WIDTH

LANES = 128
SUBLANES = 8
V7X_VMEM_LIMIT_BYTES = 60 * 1024 * 1024

TOK_TILE = 512
ROW_BLK = 32
CONV_BLK = 64
CONV_STRIDE = 4
HIST_A = 32
HIST_C = 8
SEQ_BLK = 16
FF_BLK = 256
IN_BLK = 256
PRE_BLK = 1408

F32 = jnp.float32
BF16 = jnp.bfloat16


def _sigmoid(x):
    return jax.nn.sigmoid(x)


def _silu(x):
    return x * _sigmoid(x)


def _rms(x):
    return x * lax.rsqrt(jnp.mean(x * x, axis=-1, keepdims=True) + EPS)


def _layernorm(x, g, b=None):
    mu = jnp.mean(x, axis=-1, keepdims=True)
    d = x - mu
    var = jnp.mean(d * d, axis=-1, keepdims=True)
    y = d * lax.rsqrt(var + EPS) * g
    return y if b is None else y + b


def _rope_half(x, cos, sin, first_half):
    partner = jnp.where(first_half, pltpu.roll(x, 96, 1), pltpu.roll(x, 32, 1))
    return x * cos + partner * sin


def _first_half_mask(rows):
    lane = lax.broadcasted_iota(jnp.int32, (rows, LANES), 1)
    return (lane & (RET_DK - 1)) < (RET_DK // 2), lane < RET_DK


def _swiglu_block(h, w1_ref, w3_ref, col0):
    cols = pl.ds(col0, FF_BLK)
    a = jnp.dot(h, w1_ref[:, cols], preferred_element_type=F32)
    b = jnp.dot(h, w3_ref[:, cols], preferred_element_type=F32)
    return (_silu(a) * b).astype(BF16)


def _emit(items):
    for item in items:
        item()


def _emit_interleaved(primary, secondary):
    n, m = len(primary), len(secondary)
    done = 0
    for i, item in enumerate(primary):
        item()
        upto = ((i + 1) * m) // n
        _emit(secondary[done:upto])
        done = upto


def _rope_tables(pos, k_scale):
    half = RET_DK // 2
    inv = ROPE_BASE ** (-np.arange(half, dtype=np.float64) / half)
    ang = np.asarray(pos, np.float64)[:, None] * inv[None, :]
    cos = np.tile(np.cos(ang), (1, 4))
    sin = np.tile(np.concatenate([-np.sin(ang), np.sin(ang)], axis=1), (1, 2))
    return (cos * k_scale).astype(np.float32), (sin * k_scale).astype(np.float32)


def _decay_tables(chunk):
    log_g = np.log(1.0 - np.exp2(-5.0 - np.arange(RET_HEADS, dtype=np.float64)))
    idx = np.arange(chunk, dtype=np.float64)
    diff = idx[:, None] - idx[None, :]
    dmat = np.where(diff[None] >= 0, np.exp(np.maximum(diff, 0.0)[None] * log_g[:, None, None]), 0.0)
    read_dec = np.exp((idx + 1.0)[:, None] * log_g[None, :])
    upd_dec = np.exp((chunk - 1.0 - idx)[:, None] * log_g[None, :])
    chunk_dec = np.exp(chunk * log_g)
    pairs = RET_HEADS // 2
    side_by_side = lambda a: a.reshape(pairs, 2, *a.shape[1:]).transpose(0, 2, 1, 3).reshape(pairs, a.shape[1], -1)
    dmat2 = side_by_side(dmat)
    rd2 = side_by_side(np.broadcast_to(read_dec.T[:, :, None], (RET_HEADS, chunk, RET_DV)))
    updt = np.repeat(upd_dec.T, RET_DK, axis=0).reshape(pairs, 2 * RET_DK, chunk)
    cd = np.broadcast_to(chunk_dec[:, None, None], (RET_HEADS, 1, RET_DV))
    cd2 = side_by_side(cd)
    f = lambda a: np.ascontiguousarray(a, dtype=np.float32)
    return f(dmat2), f(rd2), f(updt), f(cd2), f(cd)


def _const_spec(shape, single_buffer=True):
    zeros = (0,) * len(shape)
    mode = dict(pipeline_mode=pl.Buffered(1)) if single_buffer else {}
    return pl.BlockSpec(shape, lambda j: zeros, **mode)


def _layer_spec(shape, layer, single_buffer=True):
    zeros = (0,) * len(shape)
    mode = dict(pipeline_mode=pl.Buffered(1)) if single_buffer else {}
    return pl.BlockSpec((None,) + tuple(shape), lambda j: (layer,) + zeros, **mode)


def _params(n_axes):
    return pltpu.CompilerParams(dimension_semantics=("arbitrary",) * n_axes,
                                vmem_limit_bytes=V7X_VMEM_LIMIT_BYTES)


def _stacked_call(kern, *, name, grid, in_specs, args, out_specs, out_shape, layer, stacked, prev, scratch_shapes):
    def spec(n_layers, block, index_fn):
        return pl.BlockSpec((n_layers,) + tuple(block), lambda *g: (0,) + tuple(index_fn(*g)))

    in_specs, out_specs, out_shape, args = list(in_specs), list(out_specs), list(out_shape), list(args)
    for dims, block, index_fn in stacked:
        out_specs.append(spec(layer + 1, block, index_fn))
        out_shape.append(jax.ShapeDtypeStruct((layer + 1,) + tuple(dims), F32))
        if layer > 0:
            in_specs.append(spec(layer, block, index_fn))
    if layer > 0:
        args += list(prev)
    return pl.pallas_call(
        kern, grid=grid, in_specs=in_specs, out_specs=out_specs, out_shape=out_shape,
        scratch_shapes=scratch_shapes, compiler_params=_params(len(grid)), name=name)(*args)


def _ada_kernel(cs_ref, cp_ref, w_ref, b_ref, os_ref, op_ref):
    layer = pl.program_id(0)
    bias = b_ref[0:1, :]
    for d in range(1, DEPTH):
        bias = jnp.where(layer == d, b_ref[d:d + 1, :], bias)
    w = w_ref[...].astype(BF16)
    os_ref[...] = jnp.dot(_silu(cs_ref[...]).astype(BF16), w, preferred_element_type=F32) + bias
    mod_p = jnp.dot(_silu(cp_ref[...]).astype(BF16), w, preferred_element_type=F32) + bias
    for b in range(cp_ref.shape[0]):
        op_ref[b] = mod_p[b:b + 1, :]


def _ada_modulation(c_sample, c_prompt, ada_w, ada_b):
    ns, bp = c_sample.shape[0], c_prompt.shape[0]
    ncol = 6 * D_MODEL
    blk = 2 * D_MODEL
    return pl.pallas_call(
        _ada_kernel,
        grid=(DEPTH, ncol // blk),
        in_specs=[
            pl.BlockSpec((ns, D_MODEL), lambda l, j: (0, 0)),
            pl.BlockSpec((bp, D_MODEL), lambda l, j: (0, 0)),
            pl.BlockSpec((None, D_MODEL, blk), lambda l, j: (l, 0, j)),
            pl.BlockSpec((DEPTH, blk), lambda l, j: (0, j)),
        ],
        out_specs=[pl.BlockSpec((None, ns, blk), lambda l, j: (l, 0, j)),
                   pl.BlockSpec((None, bp, 1, blk), lambda l, j: (l, 0, 0, j))],
        out_shape=[jax.ShapeDtypeStruct((DEPTH, ns, ncol), F32),
                   jax.ShapeDtypeStruct((DEPTH, bp, 1, ncol), F32)],
        compiler_params=_params(2),
        name="ada_mod",
    )(c_sample, c_prompt, ada_w, ada_b)


N_PROMPT_IN = 24

def _prompt_layer_kernel(*refs, tile, n_tiles, tiles_per_seq, layer):
    (xc_ref, xp_ref, modc_ref, modp_ref, gpm_ref, gqm_ref, gpf_ref, gqf_ref,
     win_ref, wout_ref, w1_ref, w3_ref, w2_ref,
     rope_ref, dmat_ref, rd_ref, upd_ref, cd_ref,
     gn_ref, caw_ref, cab_ref, lng_ref, lnb_ref, ccw_ref) = refs[:N_PROMPT_IN]
    n_prev = 3 if layer > 0 else 0
    prev_states = refs[N_PROMPT_IN:N_PROMPT_IN + n_prev]
    (y_ref, sret_all, sca_all, scc_all,
     h_scr, proj_scr, mixin_scr, mix_scr, hf_scr, p_scr, ubuf, ua_scr, zbuf,
     sc_scr, qs_scr, sbd_scr) = refs[N_PROMPT_IN + n_prev:]
    sret_ref, sca_ref, scc_ref = sret_all.at[layer], sca_all.at[layer], scc_all.at[layer]
    gpm_ref, gqm_ref, gpf_ref, gqf_ref, gn_ref, cab_ref, lng_ref, lnb_ref = (
        ref.at[pl.ds(layer, 1)] for ref in (gpm_ref, gqm_ref, gpf_ref, gqf_ref, gn_ref, cab_ref, lng_ref, lnb_ref))
    mod_chunk = lambda ref, k: ref[:, pl.ds(k * D_MODEL, D_MODEL)]
    step = pl.program_id(0)
    row_blocks = [pl.ds(r * ROW_BLK, ROW_BLK) for r in range(tile // ROW_BLK)]
    first_tap = HIST_A - (CONV_A_K - 1)
    first_tap_c = HIST_C - (CONV_C_K - 1)
    span = SUBLANES * CONV_STRIDE

    @pl.when((step < n_tiles) & (lax.rem(step, tiles_per_seq) == 0))
    def _():
        for prev_ref, all_ref in zip(prev_states, (sret_all, sca_all, scc_all)):
            all_ref[0:layer] = prev_ref[...]
        sret_ref[...] = jnp.zeros_like(sret_ref)
        sbd_scr[...] = jnp.zeros_like(sbd_scr)
        ubuf[:, 0:HIST_A, :] = jnp.zeros((2, HIST_A, LANES), F32)
        zbuf[0:HIST_C, :] = jnp.zeros((HIST_C, CONV_C_WIDTH), F32)

    def mix_prenorm():
        sh = mod_chunk(modc_ref, 0)
        pre_scale = gpm_ref[...] * (1.0 + mod_chunk(modc_ref, 1))
        for rows in row_blocks:
            h_scr[rows, :] = (_rms(xc_ref[rows, :]) * pre_scale + sh).astype(BF16)

    def in_proj(n):
        cols = pl.ds(n * IN_BLK, IN_BLK)
        proj_scr[:, cols] = jnp.dot(h_scr[...], win_ref[:, cols], preferred_element_type=F32)

    def conv_groups():
        taps = [[jnp.broadcast_to(caw_ref[j:j + 1, pl.ds(half * LANES, LANES)], (SUBLANES, LANES))
                 for j in range(CONV_A_K)] for half in range(2)]
        for r in range(tile // CONV_BLK):
            rows = pl.ds(r * CONV_BLK, CONV_BLK)
            dst = pl.ds(HIST_A + r * CONV_BLK, CONV_BLK)
            for half in range(2):
                lo = half * LANES
                u = (proj_scr[rows, pl.ds(OFF_AV + lo, LANES)]
                     * _sigmoid(proj_scr[rows, pl.ds(OFF_AG + lo, LANES)]))
                ubuf[half, dst, :] = u
        for half in range(2):
            for m in range(tile // span):
                base = m * span
                acc = [None] * CONV_STRIDE
                for k in range(CONV_STRIDE + CONV_A_K - 1):
                    win = ubuf[half, pl.ds(base + first_tap + k, SUBLANES, stride=CONV_STRIDE), :]
                    for t in range(CONV_STRIDE):
                        j = k - t
                        if 0 <= j < CONV_A_K:
                            term = taps[half][j] * win
                            acc[t] = term if acc[t] is None else acc[t] + term
                for t in range(CONV_STRIDE):
                    ua_scr[half, pl.ds(base + t, SUBLANES, stride=CONV_STRIDE), :] = acc[t]
        for r in range(tile // CONV_BLK):
            rows = pl.ds(r * CONV_BLK, CONV_BLK)
            ua = jnp.concatenate([ua_scr[0, rows, :], ua_scr[1, rows, :]], axis=-1) + cab_ref[...]
            o_a = _silu(_layernorm(ua, lng_ref[...], lnb_ref[...]))
            mixin_scr[rows, MIX_A:MIX_A + CONV_A_WIDTH] = o_a.astype(BF16)
        for r in range(tile // CONV_BLK):
            rows = pl.ds(r * CONV_BLK, CONV_BLK)
            z = proj_scr[rows, OFF_CC:OFF_CC + CONV_C_WIDTH] * proj_scr[rows, OFF_CX:OFF_CX + CONV_C_WIDTH]
            zbuf[pl.ds(HIST_C + r * CONV_BLK, CONV_BLK), :] = z
        for r in range(tile // CONV_BLK):
            rows = pl.ds(r * CONV_BLK, CONV_BLK)
            zc = ccw_ref[0:1, :] * zbuf[pl.ds(r * CONV_BLK + first_tap_c, CONV_BLK), :]
            for j in range(1, CONV_C_K):
                zc = zc + ccw_ref[j:j + 1, :] * zbuf[pl.ds(r * CONV_BLK + first_tap_c + j, CONV_BLK), :]
            o_c = proj_scr[rows, OFF_CB:OFF_CB + CONV_C_WIDTH] * zc
            mixin_scr[rows, MIX_C:MIX_C + CONV_C_WIDTH] = o_c.astype(BF16)
        hist_a = pl.ds(tile + first_tap, CONV_A_K - 1)
        sca_ref[...] = jnp.concatenate([ubuf[0, hist_a, :], ubuf[1, hist_a, :]], axis=-1)
        scc_ref[...] = zbuf[pl.ds(tile + first_tap_c, CONV_C_K - 1), :]
        ubuf[:, 0:HIST_A, :] = ubuf[:, pl.ds(tile, HIST_A), :]
        zbuf[0:HIST_C, :] = zbuf[pl.ds(tile, HIST_C), :]

    def retention_scores(i, pair, buf):
        first_half, _ = _first_half_mask(RET_CHUNK)
        rows = pl.ds(i * RET_CHUNK, RET_CHUNK)
        cq, sq, ck, sk = (rope_ref[rows, pl.ds(t * LANES, LANES)] for t in range(4))
        qr = _rope_half(proj_scr[rows, pl.ds(OFF_Q + pair * LANES, LANES)], cq, sq, first_half)
        kr = _rope_half(proj_scr[rows, pl.ds(OFF_K + pair * LANES, LANES)], ck, sk, first_half)
        qr_b = qr.astype(BF16)
        k_t = kr.T
        head0_rows = lax.broadcasted_iota(jnp.int32, (2 * RET_DK, RET_CHUNK), 0) < RET_DK
        k_bd = jnp.concatenate([jnp.where(head0_rows, k_t, 0.0), jnp.where(head0_rows, 0.0, k_t)], axis=1)
        scores = jnp.dot(qr_b, k_bd.astype(BF16), preferred_element_type=F32) * dmat_ref[pair]
        sc_scr[buf] = scores.astype(BF16)
        s_bd = sbd_scr[pair]
        qs_scr[buf] = jnp.dot(qr_b, s_bd.astype(BF16), preferred_element_type=F32) * rd_ref[pair]
        v_pair = proj_scr[rows, pl.ds(OFF_V + 2 * pair * RET_DV, 2 * RET_DV)].astype(BF16)
        kv = jnp.dot((k_t * upd_ref[pair]).astype(BF16), v_pair, preferred_element_type=F32)
        r0 = lax.broadcasted_iota(jnp.int32, (2 * RET_DK, 2 * RET_DV), 0) < RET_DK
        c0 = lax.broadcasted_iota(jnp.int32, (2 * RET_DK, 2 * RET_DV), 1) < RET_DV
        s_new = s_bd * cd_ref[pair] + jnp.where(r0 == c0, kv, 0.0)
        sbd_scr[pair] = s_new
        sret_ref[2 * pair] = s_new[0:RET_DK, 0:RET_DV]
        sret_ref[2 * pair + 1] = s_new[RET_DK:2 * RET_DK, RET_DV:2 * RET_DV]

    def retention_values(i, pair, buf):
        rows = pl.ds(i * RET_CHUNK, RET_CHUNK)
        for hl in range(2):
            h = 2 * pair + hl
            vh = proj_scr[rows, pl.ds(OFF_V + h * RET_DV, RET_DV)].astype(BF16)
            o = (jnp.dot(sc_scr[buf, :, pl.ds(hl * RET_CHUNK, RET_CHUNK)], vh, preferred_element_type=F32)
                 + qs_scr[buf, :, pl.ds(hl * RET_DV, RET_DV)])
            gate = proj_scr[rows, pl.ds(OFF_G + h * RET_DV, RET_DV)]
            o = _layernorm(o, gn_ref[:, pl.ds(h * RET_DV, RET_DV)]) * _silu(gate)
            mixin_scr[rows, pl.ds(h * RET_DV, RET_DV)] = o.astype(BF16)

    def out_proj():
        mix_scr[...] = jnp.dot(mixin_scr[...], wout_ref[...], preferred_element_type=F32)

    in_tiles = [functools.partial(in_proj, n) for n in range(IN_WIDTH // IN_BLK)]
    items = [(i, pair) for i in range(tile // RET_CHUNK) for pair in range(RET_HEADS // 2)]
    ret_items = []
    for k, (i, pair) in enumerate(items):
        ret_items.append(functools.partial(retention_scores, i, pair, k % 2))
        if k > 0:
            ret_items.append(functools.partial(retention_values, *items[k - 1], (k - 1) % 2))
    ret_items.append(functools.partial(retention_values, *items[-1], (len(items) - 1) % 2))
    n_qkvg = OFF_AV // IN_BLK

    def ffn_norms():
        post_scale = gqm_ref[...] * mod_chunk(modp_ref, 2)
        sh = mod_chunk(modp_ref, 3)
        pre_scale = gpf_ref[...] * (1.0 + mod_chunk(modp_ref, 4))
        for rows in row_blocks:
            x1 = xp_ref[rows, :] + _rms(mix_scr[rows, :]) * post_scale
            y_ref[rows, :] = x1
            hf_scr[rows, :] = (_rms(x1) * pre_scale + sh).astype(BF16)

    def hidden(j):
        p_scr[:, pl.ds(j * FF_BLK, FF_BLK)] = _swiglu_block(hf_scr[...], w1_ref, w3_ref, j * FF_BLK)

    def down(n):
        cols = pl.ds(n * FF_BLK, FF_BLK)
        proj_scr[:, cols] = jnp.dot(p_scr[...], w2_ref[:, cols], preferred_element_type=F32)

    def ffn_finish():
        post_scale = gqf_ref[...] * mod_chunk(modp_ref, 5)
        for rows in row_blocks:
            y_ref[rows, :] = y_ref[rows, :] + _rms(proj_scr[rows, 0:D_MODEL]) * post_scale

    hidden_blocks = [functools.partial(hidden, j) for j in range(D_FF // FF_BLK)]
    down_tiles = [functools.partial(down, n) for n in range(D_MODEL // FF_BLK)]

    @pl.when(step == 0)
    def _():
        mix_prenorm()
        _emit(in_tiles)
        _emit(ret_items)
        conv_groups()
        out_proj()

    @pl.when((step > 0) & (step < n_tiles))
    def _():
        mix_prenorm()
        ffn_norms()
        _emit(in_tiles[:n_qkvg])
        _emit_interleaved(in_tiles[n_qkvg:] + hidden_blocks, ret_items)
        conv_groups()
        _emit(down_tiles)
        out_proj()
        ffn_finish()

    @pl.when(step == n_tiles)
    def _():
        ffn_norms()
        _emit(hidden_blocks)
        _emit(down_tiles)
        ffn_finish()


def _prompt_layer(layer, prev, x, mod_p, g_pm, g_qm, g_pf, g_qf, w_in, w_out, w1, w3, w2, tables,
                  gn, caw, cab, lng, lnb, ccw):
    bsz, seq, _ = x.shape
    tile = TOK_TILE
    tps = seq // tile
    n_tiles = bsz * tps
    rope_tab, dmat, rd, upd, cd, _ = tables
    cur = lambda i: jnp.minimum(i, n_tiles - 1)
    prv = lambda i: jnp.maximum(i - 1, 0)
    tok_cur = pl.BlockSpec((None, tile, D_MODEL), lambda i: (cur(i) // tps, cur(i) % tps, 0))
    tok_prev = pl.BlockSpec((None, tile, D_MODEL), lambda i: (prv(i) // tps, prv(i) % tps, 0))
    mod_cur = pl.BlockSpec((None, None, 1, 6 * D_MODEL), lambda i: (layer, cur(i) // tps, 0, 0))
    mod_prev = pl.BlockSpec((None, None, 1, 6 * D_MODEL), lambda i: (layer, prv(i) // tps, 0, 0))
    rope = pl.BlockSpec((tile, 4 * LANES), lambda i: (cur(i) % tps, 0))
    cs = _const_spec
    ls = functools.partial(_layer_spec, layer=layer)
    args = (x, x, mod_p, mod_p, g_pm, g_qm, g_pf, g_qf, w_in, w_out, w1, w3, w2,
            rope_tab, dmat, rd, upd, cd, gn, caw, cab, lng, lnb, ccw)
    seq_block = lambda i: (cur(i) // tps,)
    return _stacked_call(
        functools.partial(_prompt_layer_kernel, tile=tile, n_tiles=n_tiles, tiles_per_seq=tps, layer=layer),
        name="prompt_layer", grid=(n_tiles + 1,), args=args, layer=layer, prev=prev,
        stacked=[
            ((bsz, RET_HEADS, RET_DK, RET_DV), (None, RET_HEADS, RET_DK, RET_DV),
             lambda i: seq_block(i) + (0, 0, 0)),
            ((bsz, CONV_A_K - 1, CONV_A_WIDTH), (None, CONV_A_K - 1, CONV_A_WIDTH),
             lambda i: seq_block(i) + (0, 0)),
            ((bsz, CONV_C_K - 1, CONV_C_WIDTH), (None, CONV_C_K - 1, CONV_C_WIDTH),
             lambda i: seq_block(i) + (0, 0)),
        ],
        in_specs=[
            tok_cur, tok_prev, mod_cur, mod_prev,
            cs((DEPTH, D_MODEL)), cs((DEPTH, D_MODEL)), cs((DEPTH, D_MODEL)), cs((DEPTH, D_MODEL)),
            cs((D_MODEL, IN_WIDTH)), cs((D_MODEL, D_MODEL)),
            cs((D_MODEL, D_FF)), cs((D_MODEL, D_FF)), cs((D_FF, D_MODEL)),
            rope,
            cs(dmat.shape), cs(rd.shape), cs(upd.shape), cs(cd.shape),
            cs((DEPTH, RET_WIDTH)), ls((CONV_A_K, CONV_A_WIDTH)), cs((DEPTH, CONV_A_WIDTH)),
            cs((DEPTH, CONV_A_WIDTH)), cs((DEPTH, CONV_A_WIDTH)), ls((CONV_C_K, CONV_C_WIDTH)),
        ],
        out_specs=[tok_prev],
        out_shape=[jax.ShapeDtypeStruct(x.shape, F32)],
        scratch_shapes=[
            pltpu.VMEM((tile, D_MODEL), BF16),
            pltpu.VMEM((tile, IN_WIDTH), F32),
            pltpu.VMEM((tile, D_MODEL), BF16),
            pltpu.VMEM((tile, D_MODEL), F32),
            pltpu.VMEM((tile, D_MODEL), BF16),
            pltpu.VMEM((tile, D_FF), BF16),
            pltpu.VMEM((2, tile + HIST_A, LANES), F32),
            pltpu.VMEM((2, tile, LANES), F32),
            pltpu.VMEM((tile + HIST_C, CONV_C_WIDTH), F32),
            pltpu.VMEM((2, RET_CHUNK, 2 * RET_CHUNK), BF16),
            pltpu.VMEM((2, RET_CHUNK, 2 * RET_DV), F32),
            pltpu.VMEM((RET_HEADS // 2, 2 * RET_DK, 2 * RET_DV), F32),
        ])


def _sample_pre_kernel(x_ref, mod_ref, gpre_ref, win_ref, rope_ref, after_ref, proj_ref, qt_ref, kt_ref, winb_ref,
                       h_scr, *, layer):
    j = pl.program_id(0)
    n = x_ref.shape[0]

    @pl.when(j == 0)
    def _():
        sh = mod_ref[:, 0:D_MODEL]
        sc1 = 1.0 + mod_ref[:, D_MODEL:2 * D_MODEL]
        h_scr[...] = (_rms(x_ref[...]) * (gpre_ref[pl.ds(layer, 1), :] * sc1) + sh).astype(BF16)

    w_blk = win_ref[...].astype(BF16)
    winb_ref[...] = w_blk
    proj_ref[...] = jnp.dot(h_scr[...], w_blk, preferred_element_type=F32)

    def rope_transposed(off, cos, sin, dst_ref):
        first_half, _ = _first_half_mask(n)
        for pair in range(RET_HEADS // 2):
            src = proj_ref[:, pl.ds(off + pair * LANES, LANES)]
            dst_ref[pl.ds(pair * LANES, LANES), :] = _rope_half(src, cos, sin, first_half).T

    @pl.when(j == 0)
    def _():
        rope_transposed(OFF_Q, rope_ref[0:1, :], rope_ref[1:2, :], qt_ref)
        rope_transposed(OFF_K, rope_ref[2:3, :], rope_ref[3:4, :], kt_ref)


def _sample_pre(layer, xs, mod, gpre, w_in, rope_s, after):
    n = xs.shape[0]
    cs = functools.partial(_const_spec, single_buffer=False)
    ls = functools.partial(_layer_spec, layer=layer, single_buffer=False)
    return pl.pallas_call(
        functools.partial(_sample_pre_kernel, layer=layer),
        grid=(IN_WIDTH // PRE_BLK,),
        in_specs=[cs((n, D_MODEL)), ls((n, 6 * D_MODEL)), cs((DEPTH, D_MODEL)),
                  pl.BlockSpec((None, D_MODEL, PRE_BLK), lambda j: (layer, 0, j)), cs(rope_s.shape),
                  pl.BlockSpec(memory_space=pl.ANY)],
        out_specs=[pl.BlockSpec((n, PRE_BLK), lambda j: (0, j)), cs((QK_WIDTH, n)), cs((QK_WIDTH, n)),
                   pl.BlockSpec((D_MODEL, PRE_BLK), lambda j: (0, j))],
        out_shape=[jax.ShapeDtypeStruct((n, IN_WIDTH), F32),
                   jax.ShapeDtypeStruct((QK_WIDTH, n), F32),
                   jax.ShapeDtypeStruct((QK_WIDTH, n), F32),
                   jax.ShapeDtypeStruct((D_MODEL, IN_WIDTH), BF16)],
        scratch_shapes=[pltpu.VMEM((n, D_MODEL), BF16)],
        compiler_params=_params(1),
        name="sample_pre",
    )(xs, mod, gpre, w_in, rope_s, after)


def _sample_state_kernel(*refs, layer):
    (proj_ref, qt_ref, kt_ref, sin_ref, bufa_ref, bufc_ref, cd_ref,
     gn_ref, caw_ref, cab_ref, lng_ref, lnb_ref, ccw_ref) = refs[:13]
    n_prev = 3 if layer > 0 else 0
    prev_states = refs[13:13 + n_prev]
    mix_ref, sout_all, outa_all, outc_all, o_scr = refs[13 + n_prev:]
    for prev_ref, all_ref in zip(prev_states, (sout_all, outa_all, outc_all)):
        all_ref[0:layer] = prev_ref[...]
    sout_ref, outa_ref, outc_ref = sout_all.at[layer], outa_all.at[layer], outc_all.at[layer]
    gn_ref, cab_ref, lng_ref, lnb_ref = (ref.at[pl.ds(layer, 1)] for ref in (gn_ref, cab_ref, lng_ref, lnb_ref))
    blk = pl.program_id(0)
    n = qt_ref.shape[1]
    lane = lax.broadcasted_iota(jnp.int32, (QK_WIDTH, n), 1)

    for bl in range(SEQ_BLK):
        onehot = lane == (blk * SEQ_BLK + bl)
        qcol = jnp.sum(jnp.where(onehot, qt_ref[...], 0.0), axis=1, keepdims=True)
        kcol = jnp.sum(jnp.where(onehot, kt_ref[...], 0.0), axis=1, keepdims=True)
        row = pl.ds(bl, 1)
        for h in range(RET_HEADS):
            vrow = proj_ref[row, pl.ds(OFF_V + h * RET_DV, RET_DV)]
            s_new = (sin_ref[bl, h] * cd_ref[h]
                     + kcol[h * RET_DK:(h + 1) * RET_DK, :] * vrow)
            sout_ref[bl, h] = s_new
            o_scr[row, pl.ds(h * RET_DV, RET_DV)] = jnp.sum(
                qcol[h * RET_DK:(h + 1) * RET_DK, :] * s_new, axis=0, keepdims=True)
    for h in range(RET_HEADS):
        cols = pl.ds(h * RET_DV, RET_DV)
        o = _layernorm(o_scr[:, cols], gn_ref[:, cols]) * _silu(proj_ref[:, pl.ds(OFF_G + h * RET_DV, RET_DV)])
        mix_ref[:, cols] = o

    u = proj_ref[:, OFF_AV:OFF_AV + CONV_A_WIDTH] * _sigmoid(proj_ref[:, OFF_AG:OFF_AG + CONV_A_WIDTH])
    ua = caw_ref[CONV_A_K - 1:CONV_A_K, :] * u
    for j in range(CONV_A_K - 1):
        ua = ua + caw_ref[j:j + 1, :] * bufa_ref[j]
    outa_ref[0:CONV_A_K - 2] = bufa_ref[1:CONV_A_K - 1]
    outa_ref[CONV_A_K - 2] = u
    mix_ref[:, MIX_A:MIX_A + CONV_A_WIDTH] = _silu(_layernorm(ua + cab_ref[...], lng_ref[...], lnb_ref[...]))

    z = proj_ref[:, OFF_CC:OFF_CC + CONV_C_WIDTH] * proj_ref[:, OFF_CX:OFF_CX + CONV_C_WIDTH]
    zc = ccw_ref[0:1, :] * bufc_ref[0] + ccw_ref[1:2, :] * bufc_ref[1] + ccw_ref[2:3, :] * z
    outc_ref[0] = bufc_ref[1]
    outc_ref[1] = z
    mix_ref[:, MIX_C:MIX_C + CONV_C_WIDTH] = proj_ref[:, OFF_CB:OFF_CB + CONV_C_WIDTH] * zc


def _sample_state(layer, prev, proj, qt, kt, s_ret, buf_a_t, buf_c_t, cd, gn, caw, cab, lng, lnb, ccw):
    n = proj.shape[0]
    cs = functools.partial(_const_spec, single_buffer=False)
    ls = functools.partial(_layer_spec, layer=layer, single_buffer=False)
    s_spec = pl.BlockSpec((None, SEQ_BLK, RET_HEADS, RET_DK, RET_DV), lambda j: (layer, j, 0, 0, 0))
    a_spec = pl.BlockSpec((None, CONV_A_K - 1, SEQ_BLK, CONV_A_WIDTH), lambda j: (layer, 0, j, 0))
    c_spec = pl.BlockSpec((None, CONV_C_K - 1, SEQ_BLK, CONV_C_WIDTH), lambda j: (layer, 0, j, 0))
    args = (proj, qt, kt, s_ret, buf_a_t, buf_c_t, cd, gn, caw, cab, lng, lnb, ccw)
    return _stacked_call(
        functools.partial(_sample_state_kernel, layer=layer),
        name="sample_state", grid=(n // SEQ_BLK,), args=args, layer=layer, prev=prev,
        stacked=[
            ((n, RET_HEADS, RET_DK, RET_DV), (SEQ_BLK, RET_HEADS, RET_DK, RET_DV), lambda j: (j, 0, 0, 0)),
            ((CONV_A_K - 1, n, CONV_A_WIDTH), (CONV_A_K - 1, SEQ_BLK, CONV_A_WIDTH), lambda j: (0, j, 0)),
            ((CONV_C_K - 1, n, CONV_C_WIDTH), (CONV_C_K - 1, SEQ_BLK, CONV_C_WIDTH), lambda j: (0, j, 0)),
        ],
        in_specs=[
            pl.BlockSpec((SEQ_BLK, IN_WIDTH), lambda j: (j, 0)),
            cs((QK_WIDTH, n)), cs((QK_WIDTH, n)),
            s_spec, a_spec, c_spec,
            cs(cd.shape), cs((DEPTH, RET_WIDTH)), ls((CONV_A_K, CONV_A_WIDTH)), cs((DEPTH, CONV_A_WIDTH)),
            cs((DEPTH, CONV_A_WIDTH)), cs((DEPTH, CONV_A_WIDTH)), ls((CONV_C_K, CONV_C_WIDTH)),
        ],
        out_specs=[pl.BlockSpec((SEQ_BLK, D_MODEL), lambda j: (j, 0))],
        out_shape=[jax.ShapeDtypeStruct((n, D_MODEL), F32)],
        scratch_shapes=[pltpu.VMEM((SEQ_BLK, RET_WIDTH), F32)])


def _sample_post_kernel(x_ref, mod_ref, mix_ref, wout_ref, gpm_ref, gpf_ref, gqf_ref, w1_ref, w3_ref, w2_ref,
                        y_ref, woutb_ref, w1b_ref, w3b_ref, w2b_ref, x1_scr, h_scr, f_scr, *, n_steps, layer):
    j = pl.program_id(0)
    gpm_ref, gpf_ref, gqf_ref = (ref.at[pl.ds(layer, 1)] for ref in (gpm_ref, gpf_ref, gqf_ref))

    @pl.when(j == 0)
    def _():
        gt_m = mod_ref[:, 2 * D_MODEL:3 * D_MODEL]
        wout_b = wout_ref[...].astype(BF16)
        woutb_ref[...] = wout_b
        mix = jnp.dot(mix_ref[...].astype(BF16), wout_b, preferred_element_type=F32)
        x1 = x_ref[...] + _rms(mix) * (gpm_ref[...] * gt_m)
        x1_scr[...] = x1
        sh = mod_ref[:, 3 * D_MODEL:4 * D_MODEL]
        sc1 = 1.0 + mod_ref[:, 4 * D_MODEL:5 * D_MODEL]
        h_scr[...] = (_rms(x1) * (gpf_ref[...] * sc1) + sh).astype(BF16)
        f_scr[...] = jnp.zeros_like(f_scr)

    w1_b, w3_b, w2_b = w1_ref[...].astype(BF16), w3_ref[...].astype(BF16), w2_ref[...].astype(BF16)
    w1b_ref[...] = w1_b
    w3b_ref[...] = w3_b
    w2b_ref[...] = w2_b
    h = h_scr[...]
    a = jnp.dot(h, w1_b, preferred_element_type=F32)
    b = jnp.dot(h, w3_b, preferred_element_type=F32)
    p = (_silu(a) * b).astype(BF16)
    f_scr[...] += jnp.dot(p, w2_b, preferred_element_type=F32)

    @pl.when(j == n_steps - 1)
    def _():
        gt_f = mod_ref[:, 5 * D_MODEL:6 * D_MODEL]
        y_ref[...] = x1_scr[...] + _rms(f_scr[...]) * (gqf_ref[...] * gt_f)


def _sample_post(layer, xs, mod, mix, w_out, gpost_m, gpre_f, gpost_f, w1, w3, w2):
    n = xs.shape[0]
    n_steps = D_FF // FF_BLK
    cs = functools.partial(_const_spec, single_buffer=False)
    ls = functools.partial(_layer_spec, layer=layer, single_buffer=False)
    return pl.pallas_call(
        functools.partial(_sample_post_kernel, n_steps=n_steps, layer=layer),
        grid=(n_steps,),
        in_specs=[
            cs((n, D_MODEL)), ls((n, 6 * D_MODEL)), cs((n, D_MODEL)), ls((D_MODEL, D_MODEL)),
            cs((DEPTH, D_MODEL)), cs((DEPTH, D_MODEL)), cs((DEPTH, D_MODEL)),
            pl.BlockSpec((None, D_MODEL, FF_BLK), lambda j: (layer, 0, j)),
            pl.BlockSpec((None, D_MODEL, FF_BLK), lambda j: (layer, 0, j)),
            pl.BlockSpec((None, FF_BLK, D_MODEL), lambda j: (layer, j, 0)),
        ],
        out_specs=[
            cs((n, D_MODEL)), cs((D_MODEL, D_MODEL)),
            pl.BlockSpec((D_MODEL, FF_BLK), lambda j: (0, j)),
            pl.BlockSpec((D_MODEL, FF_BLK), lambda j: (0, j)),
            pl.BlockSpec((FF_BLK, D_MODEL), lambda j: (j, 0)),
        ],
        out_shape=[
            jax.ShapeDtypeStruct((n, D_MODEL), F32),
            jax.ShapeDtypeStruct((D_MODEL, D_MODEL), BF16),
            jax.ShapeDtypeStruct((D_MODEL, D_FF), BF16),
            jax.ShapeDtypeStruct((D_MODEL, D_FF), BF16),
            jax.ShapeDtypeStruct((D_FF, D_MODEL), BF16),
        ],
        scratch_shapes=[
            pltpu.VMEM((n, D_MODEL), F32),
            pltpu.VMEM((n, D_MODEL), BF16),
            pltpu.VMEM((n, D_MODEL), F32),
        ],
        compiler_params=_params(1),
        name="sample_post",
    )(xs, mod, mix, w_out, gpost_m, gpre_f, gpost_f, w1, w3, w2)


def kernel(x_prompt, x_sample, c_prompt, c_sample, state_ret, state_conv_a, state_conv_c, ada_w, ada_b, norm_pre_mix, norm_post_mix, norm_pre_ffn, norm_post_ffn, w_in, w_out, ret_gn_g, conv_a_w, conv_a_b, conv_a_ln_g, conv_a_ln_b, conv_c_w, ffn_w1, ffn_w3, ffn_w2):
    bp, lp, _ = x_prompt.shape
    ns = x_sample.shape[0]
    assert x_sample.shape[1] == 1 and ns % SEQ_BLK == 0
    assert OFF_K + QK_WIDTH <= PRE_BLK and IN_WIDTH % PRE_BLK == 0
    assert lp % TOK_TILE == 0 and TOK_TILE % RET_CHUNK == 0
    assert TOK_TILE % CONV_BLK == 0 and CONV_BLK % (SUBLANES * CONV_STRIDE) == 0

    k_scale = RET_DK ** -0.5
    cq, sq = _rope_tables(np.arange(lp), 1.0)
    ck, sk = _rope_tables(np.arange(lp), k_scale)
    tables_p = (np.concatenate([cq, sq, ck, sk], axis=1),) + _decay_tables(RET_CHUNK)
    cqs, sqs = _rope_tables([PAST_LEN], 1.0)
    cks, sks = _rope_tables([PAST_LEN], k_scale)
    rope_s = np.concatenate([cqs, sqs, cks, sks, np.zeros((4, LANES), np.float32)], axis=0)
    cd_s = _decay_tables(1)[4]

    mod_s, mod_p = _ada_modulation(c_sample, c_prompt, ada_w, ada_b)

    gn, cab, lng, lnb = ret_gn_g, conv_a_b, conv_a_ln_g, conv_a_ln_b
    g_pm, g_qm, g_pf, g_qf = norm_pre_mix, norm_post_mix, norm_pre_ffn, norm_post_ffn
    conv_a_t = jnp.transpose(state_conv_a, (0, 2, 1, 3))
    conv_c_t = jnp.transpose(state_conv_c, (0, 2, 1, 3))

    yp = x_prompt
    ys = x_sample.reshape(ns, D_MODEL)
    st_p = st_s = ()
    for l in range(DEPTH):
        proj, qt, kt, w_in_b = _sample_pre(l, ys, mod_s, g_pm, w_in, rope_s, yp)
        mix, *st_s = _sample_state(l, st_s, proj, qt, kt, state_ret, conv_a_t, conv_c_t, cd_s,
                                   gn, conv_a_w, cab, lng, lnb, conv_c_w)
        ys, w_out_b, w1_b, w3_b, w2_b = _sample_post(l, ys, mod_s, mix, w_out, g_qm, g_pf, g_qf,
                                                    ffn_w1, ffn_w3, ffn_w2)
        yp, *st_p = _prompt_layer(l, st_p, yp, mod_p, g_pm, g_qm, g_pf, g_qf, w_in_b, w_out_b, w1_b, w3_b, w2_b,
                                  tables_p, gn, conv_a_w, cab, lng, lnb, conv_c_w)

    sret_s, sca_s, scc_s = st_s
    sca_s = jnp.transpose(sca_s, (0, 2, 1, 3))
    scc_s = jnp.transpose(scc_s, (0, 2, 1, 3))
    return (yp, ys.reshape(ns, 1, D_MODEL)) + tuple(st_p) + (sret_s, sca_s, scc_s)
```

```python
import functools

import numpy as np
import jax
import jax.numpy as jnp
from jax import lax
from jax.experimental import pallas as pl
from jax.experimental.pallas import tpu as pltpu

D_MODEL = 1024
DEPTH = 2
PAST_LEN = 16384
RET_HEADS = 4
RET_WIDTH = D_MODEL // 2
RET_DV = RET_WIDTH // RET_HEADS
RET_DK = RET_DV // 2
QK_WIDTH = RET_HEADS * RET_DK
CONV_A_WIDTH = D_MODEL // 4
CONV_A_K = 31
CONV_C_WIDTH = D_MODEL - RET_WIDTH - CONV_A_WIDTH
CONV_C_K = 3
IN_WIDTH = 2 * QK_WIDTH + 2 * RET_WIDTH + 2 * CONV_A_WIDTH + 3 * CONV_C_WIDTH
D_FF = ((8 * D_MODEL // 3 + 255) // 256) * 256
RET_CHUNK = 128
ROPE_BASE = 10000.0
EPS = 1e-6

OFF_Q = 0
OFF_K = OFF_Q + QK_WIDTH
OFF_V = OFF_K + QK_WIDTH
OFF_G = OFF_V + RET_WIDTH
OFF_AV = OFF_G + RET_WIDTH
OFF_AG = OFF_AV + CONV_A_WIDTH
OFF_CB = OFF_AG + CONV_A_WIDTH
OFF_CC = OFF_CB + CONV_C_WIDTH
OFF_CX = OFF_CC + CONV_C_WIDTH
MIX_A = RET_WIDTH
MIX_C = RET_WIDTH + CONV_A_WIDTH

LANES = 128
SUBLANES = 8
V7X_VMEM_LIMIT_BYTES = 60 * 1024 * 1024

TOK_TILE = 512
ROW_BLK = 32
CONV_BLK = 64
CONV_STRIDE = 4
HIST_A = 32
HIST_C = 8
SEQ_BLK = 16
FF_BLK = 256
IN_BLK = 256
PRE_BLK = 1408

F32 = jnp.float32
BF16 = jnp.bfloat16


def _sigmoid(x):
    return jax.nn.sigmoid(x)


def _silu(x):
    return x * _sigmoid(x)


def _rms(x):
    return x * lax.rsqrt(jnp.mean(x * x, axis=-1, keepdims=True) + EPS)


def _layernorm(x, g, b=None):
    mu = jnp.mean(x, axis=-1, keepdims=True)
    d = x - mu
    var = jnp.mean(d * d, axis=-1, keepdims=True)
    y = d * lax.rsqrt(var + EPS) * g
    return y if b is None else y + b


def _rope_half(x, cos, sin, first_half):
    partner = jnp.where(first_half, pltpu.roll(x, 96, 1), pltpu.roll(x, 32, 1))
    return x * cos + partner * sin


def _first_half_mask(rows):
    lane = lax.broadcasted_iota(jnp.int32, (rows, LANES), 1)
    return (lane & (RET_DK - 1)) < (RET_DK // 2), lane < RET_DK


def _swiglu_block(h, w1_ref, w3_ref, col0):
    cols = pl.ds(col0, FF_BLK)
    a = jnp.dot(h, w1_ref[:, cols], preferred_element_type=F32)
    b = jnp.dot(h, w3_ref[:, cols], preferred_element_type=F32)
    return (_silu(a) * b).astype(BF16)


def _emit(items):
    for item in items:
        item()


def _emit_interleaved(primary, secondary):
    n, m = len(primary), len(secondary)
    done = 0
    for i, item in enumerate(primary):
        item()
        upto = ((i + 1) * m) // n
        _emit(secondary[done:upto])
        done = upto


def _rope_tables(pos, k_scale):
    half = RET_DK // 2
    inv = ROPE_BASE ** (-np.arange(half, dtype=np.float64) / half)
    ang = np.asarray(pos, np.float64)[:, None] * inv[None, :]
    cos = np.tile(np.cos(ang), (1, 4))
    sin = np.tile(np.concatenate([-np.sin(ang), np.sin(ang)], axis=1), (1, 2))
    return (cos * k_scale).astype(np.float32), (sin * k_scale).astype(np.float32)


def _decay_tables(chunk):
    log_g = np.log(1.0 - np.exp2(-5.0 - np.arange(RET_HEADS, dtype=np.float64)))
    idx = np.arange(chunk, dtype=np.float64)
    diff = idx[:, None] - idx[None, :]
    dmat = np.where(diff[None] >= 0, np.exp(np.maximum(diff, 0.0)[None] * log_g[:, None, None]), 0.0)
    read_dec = np.exp((idx + 1.0)[:, None] * log_g[None, :])
    upd_dec = np.exp((chunk - 1.0 - idx)[:, None] * log_g[None, :])
    chunk_dec = np.exp(chunk * log_g)
    pairs = RET_HEADS // 2
    side_by_side = lambda a: a.reshape(pairs, 2, *a.shape[1:]).transpose(0, 2, 1, 3).reshape(pairs, a.shape[1], -1)
    dmat2 = side_by_side(dmat)
    rd2 = side_by_side(np.broadcast_to(read_dec.T[:, :, None], (RET_HEADS, chunk, RET_DV)))
    updt = np.repeat(upd_dec.T, RET_DK, axis=0).reshape(pairs, 2 * RET_DK, chunk)
    cd = np.broadcast_to(chunk_dec[:, None, None], (RET_HEADS, 1, RET_DV))
    cd2 = side_by_side(cd)
    f = lambda a: np.ascontiguousarray(a, dtype=np.float32)
    return f(dmat2), f(rd2), f(updt), f(cd2), f(cd)


def _const_spec(shape, single_buffer=True):
    zeros = (0,) * len(shape)
    mode = dict(pipeline_mode=pl.Buffered(1)) if single_buffer else {}
    return pl.BlockSpec(shape, lambda j: zeros, **mode)


def _layer_spec(shape, layer, single_buffer=True):
    zeros = (0,) * len(shape)
    mode = dict(pipeline_mode=pl.Buffered(1)) if single_buffer else {}
    return pl.BlockSpec((None,) + tuple(shape), lambda j: (layer,) + zeros, **mode)


def _params(n_axes):
    return pltpu.CompilerParams(dimension_semantics=("arbitrary",) * n_axes,
                                vmem_limit_bytes=V7X_VMEM_LIMIT_BYTES)


def _stacked_call(kern, *, name, grid, in_specs, args, out_specs, out_shape, layer, stacked, prev, scratch_shapes):
    def spec(n_layers, block, index_fn):
        return pl.BlockSpec((n_layers,) + tuple(block), lambda *g: (0,) + tuple(index_fn(*g)))

    in_specs, out_specs, out_shape, args = list(in_specs), list(out_specs), list(out_shape), list(args)
    for dims, block, index_fn in stacked:
        out_specs.append(spec(layer + 1, block, index_fn))
        out_shape.append(jax.ShapeDtypeStruct((layer + 1,) + tuple(dims), F32))
        if layer > 0:
            in_specs.append(spec(layer, block, index_fn))
    if layer > 0:
        args += list(prev)
    return pl.pallas_call(
        kern, grid=grid, in_specs=in_specs, out_specs=out_specs, out_shape=out_shape,
        scratch_shapes=scratch_shapes, compiler_params=_params(len(grid)), name=name)(*args)


def _ada_kernel(cs_ref, cp_ref, w_ref, b_ref, os_ref, op_ref):
    layer = pl.program_id(0)
    bias = b_ref[0:1, :]
    for d in range(1, DEPTH):
        bias = jnp.where(layer == d, b_ref[d:d + 1, :], bias)
    w = w_ref[...].astype(BF16)
    os_ref[...] = jnp.dot(_silu(cs_ref[...]).astype(BF16), w, preferred_element_type=F32) + bias
    mod_p = jnp.dot(_silu(cp_ref[...]).astype(BF16), w, preferred_element_type=F32) + bias
    for b in range(cp_ref.shape[0]):
        op_ref[b] = mod_p[b:b + 1, :]


def _ada_modulation(c_sample, c_prompt, ada_w, ada_b):
    ns, bp = c_sample.shape[0], c_prompt.shape[0]
    ncol = 6 * D_MODEL
    blk = 2 * D_MODEL
    return pl.pallas_call(
        _ada_kernel,
        grid=(DEPTH, ncol // blk),
        in_specs=[
            pl.BlockSpec((ns, D_MODEL), lambda l, j: (0, 0)),
            pl.BlockSpec((bp, D_MODEL), lambda l, j: (0, 0)),
            pl.BlockSpec((None, D_MODEL, blk), lambda l, j: (l, 0, j)),
            pl.BlockSpec((DEPTH, blk), lambda l, j: (0, j)),
        ],
        out_specs=[pl.BlockSpec((None, ns, blk), lambda l, j: (l, 0, j)),
                   pl.BlockSpec((None, bp, 1, blk), lambda l, j: (l, 0, 0, j))],
        out_shape=[jax.ShapeDtypeStruct((DEPTH, ns, ncol), F32),
                   jax.ShapeDtypeStruct((DEPTH, bp, 1, ncol), F32)],
        compiler_params=_params(2),
        name="ada_mod",
    )(c_sample, c_prompt, ada_w, ada_b)


N_PROMPT_IN = 24

def _prompt_layer_kernel(*refs, tile, n_tiles, tiles_per_seq, layer):
    (xc_ref, xp_ref, modc_ref, modp_ref, gpm_ref, gqm_ref, gpf_ref, gqf_ref,
     win_ref, wout_ref, w1_hbm, w3_hbm, w2_hbm,
     rope_ref, dmat_ref, rd_ref, upd_ref, cd_ref,
     gn_ref, caw_ref, cab_ref, lng_ref, lnb_ref, ccw_ref) = refs[:N_PROMPT_IN]
    n_prev = 3 if layer > 0 else 0
    prev_states = refs[N_PROMPT_IN:N_PROMPT_IN + n_prev]
    (y_ref, sret_all, sca_all, scc_all,
     h_scr, proj_scr, mixin_scr, mix_scr, hf_scr, p_scr, ubuf, ua_scr, zbuf,
     sc_scr, qs_scr, sbd_scr, w1_ref, w3_ref, w2_ref, w_sem) = refs[N_PROMPT_IN + n_prev:]
    sret_ref, sca_ref, scc_ref = sret_all.at[layer], sca_all.at[layer], scc_all.at[layer]
    gpm_ref, gqm_ref, gpf_ref, gqf_ref, gn_ref, cab_ref, lng_ref, lnb_ref = (
        ref.at[pl.ds(layer, 1)] for ref in (gpm_ref, gqm_ref, gpf_ref, gqf_ref, gn_ref, cab_ref, lng_ref, lnb_ref))
    mod_chunk = lambda ref, k: ref[:, pl.ds(k * D_MODEL, D_MODEL)]
    step = pl.program_id(0)
    row_blocks = [pl.ds(r * ROW_BLK, ROW_BLK) for r in range(tile // ROW_BLK)]
    first_tap = HIST_A - (CONV_A_K - 1)
    first_tap_c = HIST_C - (CONV_C_K - 1)
    span = SUBLANES * CONV_STRIDE

    ffn_weight_copies = [pltpu.make_async_copy(src, dst, w_sem.at[k])
                         for k, (src, dst) in enumerate(((w1_hbm, w1_ref), (w3_hbm, w3_ref), (w2_hbm, w2_ref)))]

    @pl.when(step == 0)
    def _():
        for copy in ffn_weight_copies:
            copy.start()

    @pl.when(step == 1)
    def _():
        for copy in ffn_weight_copies:
            copy.wait()

    @pl.when((step < n_tiles) & (lax.rem(step, tiles_per_seq) == 0))
    def _():
        for prev_ref, all_ref in zip(prev_states, (sret_all, sca_all, scc_all)):
            all_ref[0:layer] = prev_ref[...]
        sret_ref[...] = jnp.zeros_like(sret_ref)
        sbd_scr[...] = jnp.zeros_like(sbd_scr)
        ubuf[:, 0:HIST_A, :] = jnp.zeros((2, HIST_A, LANES), F32)
        zbuf[0:HIST_C, :] = jnp.zeros((HIST_C, CONV_C_WIDTH), F32)

    def mix_prenorm():
        sh = mod_chunk(modc_ref, 0)
        pre_scale = gpm_ref[...] * (1.0 + mod_chunk(modc_ref, 1))
        for rows in row_blocks:
            h_scr[rows, :] = (_rms(xc_ref[rows, :]) * pre_scale + sh).astype(BF16)

    def in_proj(n):
        cols = pl.ds(n * IN_BLK, IN_BLK)
        proj_scr[:, cols] = jnp.dot(h_scr[...], win_ref[:, cols], preferred_element_type=F32)

    def conv_groups():
        taps = [[jnp.broadcast_to(caw_ref[j:j + 1, pl.ds(half * LANES, LANES)], (SUBLANES, LANES))
                 for j in range(CONV_A_K)] for half in range(2)]
        for r in range(tile // CONV_BLK):
            rows = pl.ds(r * CONV_BLK, CONV_BLK)
            dst = pl.ds(HIST_A + r * CONV_BLK, CONV_BLK)
            for half in range(2):
                lo = half * LANES
                u = (proj_scr[rows, pl.ds(OFF_AV + lo, LANES)]
                     * _sigmoid(proj_scr[rows, pl.ds(OFF_AG + lo, LANES)]))
                ubuf[half, dst, :] = u
        for half in range(2):
            for m in range(tile // span):
                base = m * span
                acc = [None] * CONV_STRIDE
                for k in range(CONV_STRIDE + CONV_A_K - 1):
                    win = ubuf[half, pl.ds(base + first_tap + k, SUBLANES, stride=CONV_STRIDE), :]
                    for t in range(CONV_STRIDE):
                        j = k - t
                        if 0 <= j < CONV_A_K:
                            term = taps[half][j] * win
                            acc[t] = term if acc[t] is None else acc[t] + term
                for t in range(CONV_STRIDE):
                    ua_scr[half, pl.ds(base + t, SUBLANES, stride=CONV_STRIDE), :] = acc[t]
        for r in range(tile // CONV_BLK):
            rows = pl.ds(r * CONV_BLK, CONV_BLK)
            ua = jnp.concatenate([ua_scr[0, rows, :], ua_scr[1, rows, :]], axis=-1) + cab_ref[...]
            o_a = _silu(_layernorm(ua, lng_ref[...], lnb_ref[...]))
            mixin_scr[rows, MIX_A:MIX_A + CONV_A_WIDTH] = o_a.astype(BF16)
        for r in range(tile // CONV_BLK):
            rows = pl.ds(r * CONV_BLK, CONV_BLK)
            z = proj_scr[rows, OFF_CC:OFF_CC + CONV_C_WIDTH] * proj_scr[rows, OFF_CX:OFF_CX + CONV_C_WIDTH]
            zbuf[pl.ds(HIST_C + r * CONV_BLK, CONV_BLK), :] = z
        for r in range(tile // CONV_BLK):
            rows = pl.ds(r * CONV_BLK, CONV_BLK)
            zc = ccw_ref[0:1, :] * zbuf[pl.ds(r * CONV_BLK + first_tap_c, CONV_BLK), :]
            for j in range(1, CONV_C_K):
                zc = zc + ccw_ref[j:j + 1, :] * zbuf[pl.ds(r * CONV_BLK + first_tap_c + j, CONV_BLK), :]
            o_c = proj_scr[rows, OFF_CB:OFF_CB + CONV_C_WIDTH] * zc
            mixin_scr[rows, MIX_C:MIX_C + CONV_C_WIDTH] = o_c.astype(BF16)
        hist_a = pl.ds(tile + first_tap, CONV_A_K - 1)
        sca_ref[...] = jnp.concatenate([ubuf[0, hist_a, :], ubuf[1, hist_a, :]], axis=-1)
        scc_ref[...] = zbuf[pl.ds(tile + first_tap_c, CONV_C_K - 1), :]
        ubuf[:, 0:HIST_A, :] = ubuf[:, pl.ds(tile, HIST_A), :]
        zbuf[0:HIST_C, :] = zbuf[pl.ds(tile, HIST_C), :]

    def retention_scores(i, pair, buf):
        first_half, _ = _first_half_mask(RET_CHUNK)
        rows = pl.ds(i * RET_CHUNK, RET_CHUNK)
        cq, sq, ck, sk = (rope_ref[rows, pl.ds(t * LANES, LANES)] for t in range(4))
        qr = _rope_half(proj_scr[rows, pl.ds(OFF_Q + pair * LANES, LANES)], cq, sq, first_half)
        kr = _rope_half(proj_scr[rows, pl.ds(OFF_K + pair * LANES, LANES)], ck, sk, first_half)
        qr_b = qr.astype(BF16)
        k_t = kr.T
        head0_rows = lax.broadcasted_iota(jnp.int32, (2 * RET_DK, RET_CHUNK), 0) < RET_DK
        k_bd = jnp.concatenate([jnp.where(head0_rows, k_t, 0.0), jnp.where(head0_rows, 0.0, k_t)], axis=1)
        scores = jnp.dot(qr_b, k_bd.astype(BF16), preferred_element_type=F32) * dmat_ref[pair]
        sc_scr[buf] = scores.astype(BF16)
        s_bd = sbd_scr[pair]
        qs_scr[buf] = jnp.dot(qr_b, s_bd.astype(BF16), preferred_element_type=F32) * rd_ref[pair]
        v_pair = proj_scr[rows, pl.ds(OFF_V + 2 * pair * RET_DV, 2 * RET_DV)].astype(BF16)
        kv = jnp.dot((k_t * upd_ref[pair]).astype(BF16), v_pair, preferred_element_type=F32)
        r0 = lax.broadcasted_iota(jnp.int32, (2 * RET_DK, 2 * RET_DV), 0) < RET_DK
        c0 = lax.broadcasted_iota(jnp.int32, (2 * RET_DK, 2 * RET_DV), 1) < RET_DV
        s_new = s_bd * cd_ref[pair] + jnp.where(r0 == c0, kv, 0.0)
        sbd_scr[pair] = s_new
        sret_ref[2 * pair] = s_new[0:RET_DK, 0:RET_DV]
        sret_ref[2 * pair + 1] = s_new[RET_DK:2 * RET_DK, RET_DV:2 * RET_DV]

    def retention_values(i, pair, buf):
        rows = pl.ds(i * RET_CHUNK, RET_CHUNK)
        for hl in range(2):
            h = 2 * pair + hl
            vh = proj_scr[rows, pl.ds(OFF_V + h * RET_DV, RET_DV)].astype(BF16)
            o = (jnp.dot(sc_scr[buf, :, pl.ds(hl * RET_CHUNK, RET_CHUNK)], vh, preferred_element_type=F32)
                 + qs_scr[buf, :, pl.ds(hl * RET_DV, RET_DV)])
            gate = proj_scr[rows, pl.ds(OFF_G + h * RET_DV, RET_DV)]
            o = _layernorm(o, gn_ref[:, pl.ds(h * RET_DV, RET_DV)]) * _silu(gate)
            mixin_scr[rows, pl.ds(h * RET_DV, RET_DV)] = o.astype(BF16)

    def out_proj():
        mix_scr[...] = jnp.dot(mixin_scr[...], wout_ref[...], preferred_element_type=F32)

    in_tiles = [functools.partial(in_proj, n) for n in range(IN_WIDTH // IN_BLK)]
    items = [(i, pair) for i in range(tile // RET_CHUNK) for pair in range(RET_HEADS // 2)]
    ret_items = []
    for k, (i, pair) in enumerate(items):
        ret_items.append(functools.partial(retention_scores, i, pair, k % 2))
        if k > 0:
            ret_items.append(functools.partial(retention_values, *items[k - 1], (k - 1) % 2))
    ret_items.append(functools.partial(retention_values, *items[-1], (len(items) - 1) % 2))
    n_qkvg = OFF_AV // IN_BLK

    def ffn_norms():
        post_scale = gqm_ref[...] * mod_chunk(modp_ref, 2)
        sh = mod_chunk(modp_ref, 3)
        pre_scale = gpf_ref[...] * (1.0 + mod_chunk(modp_ref, 4))
        for rows in row_blocks:
            x1 = xp_ref[rows, :] + _rms(mix_scr[rows, :]) * post_scale
            y_ref[rows, :] = x1
            hf_scr[rows, :] = (_rms(x1) * pre_scale + sh).astype(BF16)

    def hidden(j):
        p_scr[:, pl.ds(j * FF_BLK, FF_BLK)] = _swiglu_block(hf_scr[...], w1_ref, w3_ref, j * FF_BLK)

    def down(n):
        cols = pl.ds(n * FF_BLK, FF_BLK)
        proj_scr[:, cols] = jnp.dot(p_scr[...], w2_ref[:, cols], preferred_element_type=F32)

    def ffn_finish():
        post_scale = gqf_ref[...] * mod_chunk(modp_ref, 5)
        for rows in row_blocks:
            y_ref[rows, :] = y_ref[rows, :] + _rms(proj_scr[rows, 0:D_MODEL]) * post_scale

    hidden_blocks = [functools.partial(hidden, j) for j in range(D_FF // FF_BLK)]
    down_tiles = [functools.partial(down, n) for n in range(D_MODEL // FF_BLK)]

    @pl.when(step == 0)
    def _():
        mix_prenorm()
        _emit(in_tiles)
        _emit(ret_items)
        conv_groups()
        out_proj()

    @pl.when((step > 0) & (step < n_tiles))
    def _():
        mix_prenorm()
        ffn_norms()
        _emit(in_tiles[:n_qkvg])
        _emit_interleaved(in_tiles[n_qkvg:] + hidden_blocks, ret_items)
        conv_groups()
        _emit(down_tiles)
        out_proj()
        ffn_finish()

    @pl.when(step == n_tiles)
    def _():
        ffn_norms()
        _emit(hidden_blocks)
        _emit(down_tiles)
        ffn_finish()


def _prompt_layer(layer, prev, x, mod_p, g_pm, g_qm, g_pf, g_qf, w_in, w_out, w1, w3, w2, tables,
                  gn, caw, cab, lng, lnb, ccw):
    bsz, seq, _ = x.shape
    tile = TOK_TILE
    tps = seq // tile
    n_tiles = bsz * tps
    rope_tab, dmat, rd, upd, cd, _ = tables
    cur = lambda i: jnp.minimum(i, n_tiles - 1)
    prv = lambda i: jnp.maximum(i - 1, 0)
    tok_cur = pl.BlockSpec((None, tile, D_MODEL), lambda i: (cur(i) // tps, cur(i) % tps, 0))
    tok_prev = pl.BlockSpec((None, tile, D_MODEL), lambda i: (prv(i) // tps, prv(i) % tps, 0))
    mod_cur = pl.BlockSpec((None, None, 1, 6 * D_MODEL), lambda i: (layer, cur(i) // tps, 0, 0))
    mod_prev = pl.BlockSpec((None, None, 1, 6 * D_MODEL), lambda i: (layer, prv(i) // tps, 0, 0))
    rope = pl.BlockSpec((tile, 4 * LANES), lambda i: (cur(i) % tps, 0))
    cs = _const_spec
    ls = functools.partial(_layer_spec, layer=layer)
    hbm = pl.BlockSpec(memory_space=pl.ANY)
    assert n_tiles >= 2
    args = (x, x, mod_p, mod_p, g_pm, g_qm, g_pf, g_qf, w_in, w_out, w1, w3, w2,
            rope_tab, dmat, rd, upd, cd, gn, caw, cab, lng, lnb, ccw)
    seq_block = lambda i: (cur(i) // tps,)
    return _stacked_call(
        functools.partial(_prompt_layer_kernel, tile=tile, n_tiles=n_tiles, tiles_per_seq=tps, layer=layer),
        name="prompt_layer", grid=(n_tiles + 1,), args=args, layer=layer, prev=prev,
        stacked=[
            ((bsz, RET_HEADS, RET_DK, RET_DV), (None, RET_HEADS, RET_DK, RET_DV),
             lambda i: seq_block(i) + (0, 0, 0)),
            ((bsz, CONV_A_K - 1, CONV_A_WIDTH), (None, CONV_A_K - 1, CONV_A_WIDTH),
             lambda i: seq_block(i) + (0, 0)),
            ((bsz, CONV_C_K - 1, CONV_C_WIDTH), (None, CONV_C_K - 1, CONV_C_WIDTH),
             lambda i: seq_block(i) + (0, 0)),
        ],
        in_specs=[
            tok_cur, tok_prev, mod_cur, mod_prev,
            cs((DEPTH, D_MODEL)), cs((DEPTH, D_MODEL)), cs((DEPTH, D_MODEL)), cs((DEPTH, D_MODEL)),
            cs((D_MODEL, IN_WIDTH)), cs((D_MODEL, D_MODEL)),
            hbm, hbm, hbm,
            rope,
            cs(dmat.shape), cs(rd.shape), cs(upd.shape), cs(cd.shape),
            cs((DEPTH, RET_WIDTH)), ls((CONV_A_K, CONV_A_WIDTH)), cs((DEPTH, CONV_A_WIDTH)),
            cs((DEPTH, CONV_A_WIDTH)), cs((DEPTH, CONV_A_WIDTH)), ls((CONV_C_K, CONV_C_WIDTH)),
        ],
        out_specs=[tok_prev],
        out_shape=[jax.ShapeDtypeStruct(x.shape, F32)],
        scratch_shapes=[
            pltpu.VMEM((tile, D_MODEL), BF16),
            pltpu.VMEM((tile, IN_WIDTH), F32),
            pltpu.VMEM((tile, D_MODEL), BF16),
            pltpu.VMEM((tile, D_MODEL), F32),
            pltpu.VMEM((tile, D_MODEL), BF16),
            pltpu.VMEM((tile, D_FF), BF16),
            pltpu.VMEM((2, tile + HIST_A, LANES), F32),
            pltpu.VMEM((2, tile, LANES), F32),
            pltpu.VMEM((tile + HIST_C, CONV_C_WIDTH), F32),
            pltpu.VMEM((2, RET_CHUNK, 2 * RET_CHUNK), BF16),
            pltpu.VMEM((2, RET_CHUNK, 2 * RET_DV), F32),
            pltpu.VMEM((RET_HEADS // 2, 2 * RET_DK, 2 * RET_DV), F32),
            pltpu.VMEM((D_MODEL, D_FF), BF16),
            pltpu.VMEM((D_MODEL, D_FF), BF16),
            pltpu.VMEM((D_FF, D_MODEL), BF16),
            pltpu.SemaphoreType.DMA((3,)),
        ])


def _sample_pre_kernel(x_ref, mod_ref, gpre_ref, win_ref, rope_ref, after_ref, proj_ref, qt_ref, kt_ref, winb_ref,
                       h_scr, *, layer):
    j = pl.program_id(0)
    n = x_ref.shape[0]

    @pl.when(j == 0)
    def _():
        sh = mod_ref[:, 0:D_MODEL]
        sc1 = 1.0 + mod_ref[:, D_MODEL:2 * D_MODEL]
        h_scr[...] = (_rms(x_ref[...]) * (gpre_ref[pl.ds(layer, 1), :] * sc1) + sh).astype(BF16)

    w_blk = win_ref[...].astype(BF16)
    winb_ref[...] = w_blk
    proj_ref[...] = jnp.dot(h_scr[...], w_blk, preferred_element_type=F32)

    def rope_transposed(off, cos, sin, dst_ref):
        first_half, _ = _first_half_mask(n)
        for pair in range(RET_HEADS // 2):
            src = proj_ref[:, pl.ds(off + pair * LANES, LANES)]
            dst_ref[pl.ds(pair * LANES, LANES), :] = _rope_half(src, cos, sin, first_half).T

    @pl.when(j == 0)
    def _():
        rope_transposed(OFF_Q, rope_ref[0:1, :], rope_ref[1:2, :], qt_ref)
        rope_transposed(OFF_K, rope_ref[2:3, :], rope_ref[3:4, :], kt_ref)


def _sample_pre(layer, xs, mod, gpre, w_in, rope_s, after):
    n = xs.shape[0]
    cs = functools.partial(_const_spec, single_buffer=False)
    ls = functools.partial(_layer_spec, layer=layer, single_buffer=False)
    return pl.pallas_call(
        functools.partial(_sample_pre_kernel, layer=layer),
        grid=(IN_WIDTH // PRE_BLK,),
        in_specs=[cs((n, D_MODEL)), ls((n, 6 * D_MODEL)), cs((DEPTH, D_MODEL)),
                  pl.BlockSpec((None, D_MODEL, PRE_BLK), lambda j: (layer, 0, j)), cs(rope_s.shape),
                  pl.BlockSpec(memory_space=pl.ANY)],
        out_specs=[pl.BlockSpec((n, PRE_BLK), lambda j: (0, j)), cs((QK_WIDTH, n)), cs((QK_WIDTH, n)),
                   pl.BlockSpec((D_MODEL, PRE_BLK), lambda j: (0, j))],
        out_shape=[jax.ShapeDtypeStruct((n, IN_WIDTH), F32),
                   jax.ShapeDtypeStruct((QK_WIDTH, n), F32),
                   jax.ShapeDtypeStruct((QK_WIDTH, n), F32),
                   jax.ShapeDtypeStruct((D_MODEL, IN_WIDTH), BF16)],
        scratch_shapes=[pltpu.VMEM((n, D_MODEL), BF16)],
        compiler_params=_params(1),
        name="sample_pre",
    )(xs, mod, gpre, w_in, rope_s, after)


def _sample_state_kernel(*refs, layer):
    (proj_ref, qt_ref, kt_ref, sin_ref, bufa_ref, bufc_ref, cd_ref,
     gn_ref, caw_ref, cab_ref, lng_ref, lnb_ref, ccw_ref) = refs[:13]
    n_prev = 3 if layer > 0 else 0
    prev_states = refs[13:13 + n_prev]
    mix_ref, sout_all, outa_all, outc_all, o_scr = refs[13 + n_prev:]
    for prev_ref, all_ref in zip(prev_states, (sout_all, outa_all, outc_all)):
        all_ref[0:layer] = prev_ref[...]
    sout_ref, outa_ref, outc_ref = sout_all.at[layer], outa_all.at[layer], outc_all.at[layer]
    gn_ref, cab_ref, lng_ref, lnb_ref = (ref.at[pl.ds(layer, 1)] for ref in (gn_ref, cab_ref, lng_ref, lnb_ref))
    blk = pl.program_id(0)
    n = qt_ref.shape[1]
    lane = lax.broadcasted_iota(jnp.int32, (QK_WIDTH, n), 1)

    for bl in range(SEQ_BLK):
        onehot = lane == (blk * SEQ_BLK + bl)
        qcol = jnp.sum(jnp.where(onehot, qt_ref[...], 0.0), axis=1, keepdims=True)
        kcol = jnp.sum(jnp.where(onehot, kt_ref[...], 0.0), axis=1, keepdims=True)
        row = pl.ds(bl, 1)
        for h in range(RET_HEADS):
            vrow = proj_ref[row, pl.ds(OFF_V + h * RET_DV, RET_DV)]
            s_new = (sin_ref[bl, h] * cd_ref[h]
                     + kcol[h * RET_DK:(h + 1) * RET_DK, :] * vrow)
            sout_ref[bl, h] = s_new
            o_scr[row, pl.ds(h * RET_DV, RET_DV)] = jnp.sum(
                qcol[h * RET_DK:(h + 1) * RET_DK, :] * s_new, axis=0, keepdims=True)
    for h in range(RET_HEADS):
        cols = pl.ds(h * RET_DV, RET_DV)
        o = _layernorm(o_scr[:, cols], gn_ref[:, cols]) * _silu(proj_ref[:, pl.ds(OFF_G + h * RET_DV, RET_DV)])
        mix_ref[:, cols] = o

    u = proj_ref[:, OFF_AV:OFF_AV + CONV_A_WIDTH] * _sigmoid(proj_ref[:, OFF_AG:OFF_AG + CONV_A_WIDTH])
    ua = caw_ref[CONV_A_K - 1:CONV_A_K, :] * u
    for j in range(CONV_A_K - 1):
        ua = ua + caw_ref[j:j + 1, :] * bufa_ref[j]
    outa_ref[0:CONV_A_K - 2] = bufa_ref[1:CONV_A_K - 1]
    outa_ref[CONV_A_K - 2] = u
    mix_ref[:, MIX_A:MIX_A + CONV_A_WIDTH] = _silu(_layernorm(ua + cab_ref[...], lng_ref[...], lnb_ref[...]))

    z = proj_ref[:, OFF_CC:OFF_CC + CONV_C_WIDTH] * proj_ref[:, OFF_CX:OFF_CX + CONV_C_WIDTH]
    zc = ccw_ref[0:1, :] * bufc_ref[0] + ccw_ref[1:2, :] * bufc_ref[1] + ccw_ref[2:3, :] * z
    outc_ref[0] = bufc_ref[1]
    outc_ref[1] = z
    mix_ref[:, MIX_C:MIX_C + CONV_C_WIDTH] = proj_ref[:, OFF_CB:OFF_CB + CONV_C_WIDTH] * zc


def _sample_state(layer, prev, proj, qt, kt, s_ret, buf_a_t, buf_c_t, cd, gn, caw, cab, lng, lnb, ccw):
    n = proj.shape[0]
    cs = functools.partial(_const_spec, single_buffer=False)
    ls = functools.partial(_layer_spec, layer=layer, single_buffer=False)
    s_spec = pl.BlockSpec((None, SEQ_BLK, RET_HEADS, RET_DK, RET_DV), lambda j: (layer, j, 0, 0, 0))
    a_spec = pl.BlockSpec((None, CONV_A_K - 1, SEQ_BLK, CONV_A_WIDTH), lambda j: (layer, 0, j, 0))
    c_spec = pl.BlockSpec((None, CONV_C_K - 1, SEQ_BLK, CONV_C_WIDTH), lambda j: (layer, 0, j, 0))
    args = (proj, qt, kt, s_ret, buf_a_t, buf_c_t, cd, gn, caw, cab, lng, lnb, ccw)
    return _stacked_call(
        functools.partial(_sample_state_kernel, layer=layer),
        name="sample_state", grid=(n // SEQ_BLK,), args=args, layer=layer, prev=prev,
        stacked=[
            ((n, RET_HEADS, RET_DK, RET_DV), (SEQ_BLK, RET_HEADS, RET_DK, RET_DV), lambda j: (j, 0, 0, 0)),
            ((CONV_A_K - 1, n, CONV_A_WIDTH), (CONV_A_K - 1, SEQ_BLK, CONV_A_WIDTH), lambda j: (0, j, 0)),
            ((CONV_C_K - 1, n, CONV_C_WIDTH), (CONV_C_K - 1, SEQ_BLK, CONV_C_WIDTH), lambda j: (0, j, 0)),
        ],
        in_specs=[
            pl.BlockSpec((SEQ_BLK, IN_WIDTH), lambda j: (j, 0)),
            cs((QK_WIDTH, n)), cs((QK_WIDTH, n)),
            s_spec, a_spec, c_spec,
            cs(cd.shape), cs((DEPTH, RET_WIDTH)), ls((CONV_A_K, CONV_A_WIDTH)), cs((DEPTH, CONV_A_WIDTH)),
            cs((DEPTH, CONV_A_WIDTH)), cs((DEPTH, CONV_A_WIDTH)), ls((CONV_C_K, CONV_C_WIDTH)),
        ],
        out_specs=[pl.BlockSpec((SEQ_BLK, D_MODEL), lambda j: (j, 0))],
        out_shape=[jax.ShapeDtypeStruct((n, D_MODEL), F32)],
        scratch_shapes=[pltpu.VMEM((SEQ_BLK, RET_WIDTH), F32)])


def _sample_post_kernel(x_ref, mod_ref, mix_ref, wout_ref, gpm_ref, gpf_ref, gqf_ref, w1_ref, w3_ref, w2_ref,
                        y_ref, woutb_ref, w1b_ref, w3b_ref, w2b_ref, x1_scr, h_scr, f_scr, *, n_steps, layer):
    j = pl.program_id(0)
    gpm_ref, gpf_ref, gqf_ref = (ref.at[pl.ds(layer, 1)] for ref in (gpm_ref, gpf_ref, gqf_ref))

    @pl.when(j == 0)
    def _():
        gt_m = mod_ref[:, 2 * D_MODEL:3 * D_MODEL]
        wout_b = wout_ref[...].astype(BF16)
        woutb_ref[...] = wout_b
        mix = jnp.dot(mix_ref[...].astype(BF16), wout_b, preferred_element_type=F32)
        x1 = x_ref[...] + _rms(mix) * (gpm_ref[...] * gt_m)
        x1_scr[...] = x1
        sh = mod_ref[:, 3 * D_MODEL:4 * D_MODEL]
        sc1 = 1.0 + mod_ref[:, 4 * D_MODEL:5 * D_MODEL]
        h_scr[...] = (_rms(x1) * (gpf_ref[...] * sc1) + sh).astype(BF16)
        f_scr[...] = jnp.zeros_like(f_scr)

    w1_b, w3_b, w2_b = w1_ref[...].astype(BF16), w3_ref[...].astype(BF16), w2_ref[...].astype(BF16)
    w1b_ref[...] = w1_b
    w3b_ref[...] = w3_b
    w2b_ref[...] = w2_b
    h = h_scr[...]
    a = jnp.dot(h, w1_b, preferred_element_type=F32)
    b = jnp.dot(h, w3_b, preferred_element_type=F32)
    p = (_silu(a) * b).astype(BF16)
    f_scr[...] += jnp.dot(p, w2_b, preferred_element_type=F32)

    @pl.when(j == n_steps - 1)
    def _():
        gt_f = mod_ref[:, 5 * D_MODEL:6 * D_MODEL]
        y_ref[...] = x1_scr[...] + _rms(f_scr[...]) * (gqf_ref[...] * gt_f)


def _sample_post(layer, xs, mod, mix, w_out, gpost_m, gpre_f, gpost_f, w1, w3, w2):
    n = xs.shape[0]
    n_steps = D_FF // FF_BLK
    cs = functools.partial(_const_spec, single_buffer=False)
    ls = functools.partial(_layer_spec, layer=layer, single_buffer=False)
    return pl.pallas_call(
        functools.partial(_sample_post_kernel, n_steps=n_steps, layer=layer),
        grid=(n_steps,),
        in_specs=[
            cs((n, D_MODEL)), ls((n, 6 * D_MODEL)), cs((n, D_MODEL)), ls((D_MODEL, D_MODEL)),
            cs((DEPTH, D_MODEL)), cs((DEPTH, D_MODEL)), cs((DEPTH, D_MODEL)),
            pl.BlockSpec((None, D_MODEL, FF_BLK), lambda j: (layer, 0, j)),
            pl.BlockSpec((None, D_MODEL, FF_BLK), lambda j: (layer, 0, j)),
            pl.BlockSpec((None, FF_BLK, D_MODEL), lambda j: (layer, j, 0)),
        ],
        out_specs=[
            cs((n, D_MODEL)), cs((D_MODEL, D_MODEL)),
            pl.BlockSpec((D_MODEL, FF_BLK), lambda j: (0, j)),
            pl.BlockSpec((D_MODEL, FF_BLK), lambda j: (0, j)),
            pl.BlockSpec((FF_BLK, D_MODEL), lambda j: (j, 0)),
        ],
        out_shape=[
            jax.ShapeDtypeStruct((n, D_MODEL), F32),
            jax.ShapeDtypeStruct((D_MODEL, D_MODEL), BF16),
            jax.ShapeDtypeStruct((D_MODEL, D_FF), BF16),
            jax.ShapeDtypeStruct((D_MODEL, D_FF), BF16),
            jax.ShapeDtypeStruct((D_FF, D_MODEL), BF16),
        ],
        scratch_shapes=[
            pltpu.VMEM((n, D_MODEL), F32),
            pltpu.VMEM((n, D_MODEL), BF16),
            pltpu.VMEM((n, D_MODEL), F32),
        ],
        compiler_params=_params(1),
        name="sample_post",
    )(xs, mod, mix, w_out, gpost_m, gpre_f, gpost_f, w1, w3, w2)


def kernel(x_prompt, x_sample, c_prompt, c_sample, state_ret, state_conv_a, state_conv_c, ada_w, ada_b, norm_pre_mix, norm_post_mix, norm_pre_ffn, norm_post_ffn, w_in, w_out, ret_gn_g, conv_a_w, conv_a_b, conv_a_ln_g, conv_a_ln_b, conv_c_w, ffn_w1, ffn_w3, ffn_w2):
    bp, lp, _ = x_prompt.shape
    ns = x_sample.shape[0]
    assert x_sample.shape[1] == 1 and ns % SEQ_BLK == 0
    assert OFF_K + QK_WIDTH <= PRE_BLK and IN_WIDTH % PRE_BLK == 0
    assert lp % TOK_TILE == 0 and TOK_TILE % RET_CHUNK == 0
    assert TOK_TILE % CONV_BLK == 0 and CONV_BLK % (SUBLANES * CONV_STRIDE) == 0

    k_scale = RET_DK ** -0.5
    cq, sq = _rope_tables(np.arange(lp), 1.0)
    ck, sk = _rope_tables(np.arange(lp), k_scale)
    tables_p = (np.concatenate([cq, sq, ck, sk], axis=1),) + _decay_tables(RET_CHUNK)
    cqs, sqs = _rope_tables([PAST_LEN], 1.0)
    cks, sks = _rope_tables([PAST_LEN], k_scale)
    rope_s = np.concatenate([cqs, sqs, cks, sks, np.zeros((4, LANES), np.float32)], axis=0)
    cd_s = _decay_tables(1)[4]

    mod_s, mod_p = _ada_modulation(c_sample, c_prompt, ada_w, ada_b)

    gn, cab, lng, lnb = ret_gn_g, conv_a_b, conv_a_ln_g, conv_a_ln_b
    g_pm, g_qm, g_pf, g_qf = norm_pre_mix, norm_post_mix, norm_pre_ffn, norm_post_ffn
    conv_a_t = jnp.transpose(state_conv_a, (0, 2, 1, 3))
    conv_c_t = jnp.transpose(state_conv_c, (0, 2, 1, 3))

    yp = x_prompt
    ys = x_sample.reshape(ns, D_MODEL)
    st_p = st_s = ()
    for l in range(DEPTH):
        proj, qt, kt, w_in_b = _sample_pre(l, ys, mod_s, g_pm, w_in, rope_s, yp)
        mix, *st_s = _sample_state(l, st_s, proj, qt, kt, state_ret, conv_a_t, conv_c_t, cd_s,
                                   gn, conv_a_w, cab, lng, lnb, conv_c_w)
        ys, w_out_b, w1_b, w3_b, w2_b = _sample_post(l, ys, mod_s, mix, w_out, g_qm, g_pf, g_qf,
                                                    ffn_w1, ffn_w3, ffn_w2)
        yp, *st_p = _prompt_layer(l, st_p, yp, mod_p, g_pm, g_qm, g_pf, g_qf, w_in_b, w_out_b, w1_b, w3_b, w2_b,
                                  tables_p, gn, conv_a_w, cab, lng, lnb, conv_c_w)

    sret_s, sca_s, scc_s = st_s
    sca_s = jnp.transpose(sca_s, (0, 2, 1, 3))
    scc_s = jnp.transpose(scc_s, (0, 2, 1, 3))
    return (yp, ys.reshape(ns, 1, D_MODEL)) + tuple(st_p) + (sret_s, sca_s, scc_s)
```

```python
import functools

import numpy as np
import jax
import jax.numpy as jnp
from jax import lax
from jax.experimental import pallas as pl
from jax.experimental.pallas import tpu as pltpu

D_MODEL = 1024
DEPTH = 2
PAST_LEN = 16384
RET_HEADS = 4
RET_WIDTH = D_MODEL // 2
RET_DV = RET_WIDTH // RET_HEADS
RET_DK = RET_DV // 2
QK_WIDTH = RET_HEADS * RET_DK
CONV_A_WIDTH = D_MODEL // 4
CONV_A_K = 31
CONV_C_WIDTH = D_MODEL - RET_WIDTH - CONV_A_WIDTH
CONV_C_K = 3
IN_WIDTH = 2 * QK_WIDTH + 2 * RET_WIDTH + 2 * CONV_A_WIDTH + 3 * CONV_C_WIDTH
D_FF = ((8 * D_MODEL // 3 + 255) // 256) * 256
RET_CHUNK = 128
ROPE_BASE = 10000.0
EPS = 1e-6

OFF_Q = 0
OFF_K = OFF_Q + QK_WIDTH
OFF_V = OFF_K + QK_WIDTH
OFF_G = OFF_V + RET_WIDTH
OFF_AV = OFF_G + RET_WIDTH
OFF_AG = OFF_AV + CONV_A_WIDTH
OFF_CB = OFF_AG + CONV_A_WIDTH
OFF_CC = OFF_CB + CONV_C_WIDTH
OFF_CX = OFF_CC + CONV_C_WIDTH
MIX_A = RET_WIDTH
MIX_C = RET_WIDTH + CONV_A_WIDTH

LANES = 128
SUBLANES = 8
V7X_VMEM_LIMIT_BYTES = 60 * 1024 * 1024

TOK_TILE = 512
ROW_BLK = 32
CONV_BLK = 64
CONV_STRIDE = 4
HIST_A = 32
HIST_C = 16
SEQ_BLK = 16
FF_BLK = 256
IN_BLK = 256
PRE_BLK = 1408

F32 = jnp.float32
BF16 = jnp.bfloat16


def _sigmoid(x):
    return jax.nn.sigmoid(x)


def _silu(x):
    return x * _sigmoid(x)


def _rms(x):
    return x * lax.rsqrt(jnp.mean(x * x, axis=-1, keepdims=True) + EPS)


def _layernorm(x, g, b=None):
    mu = jnp.mean(x, axis=-1, keepdims=True)
    d = x - mu
    var = jnp.mean(d * d, axis=-1, keepdims=True)
    y = d * lax.rsqrt(var + EPS) * g
    return y if b is None else y + b


def _rope_half(x, cos, sin, first_half):
    partner = jnp.where(first_half, pltpu.roll(x, 96, 1), pltpu.roll(x, 32, 1))
    return x * cos + partner * sin


def _first_half_mask(rows):
    lane = lax.broadcasted_iota(jnp.int32, (rows, LANES), 1)
    return (lane & (RET_DK - 1)) < (RET_DK // 2), lane < RET_DK


def _swiglu_block(h, w1_ref, w3_ref, col0):
    cols = pl.ds(col0, FF_BLK)
    a = jnp.dot(h, w1_ref[:, cols], preferred_element_type=F32)
    b = jnp.dot(h, w3_ref[:, cols], preferred_element_type=F32)
    return (_silu(a) * b).astype(BF16)


def _emit(items):
    for item in items:
        item()


def _emit_interleaved(primary, secondary):
    n, m = len(primary), len(secondary)
    done = 0
    for i, item in enumerate(primary):
        item()
        upto = ((i + 1) * m) // n
        _emit(secondary[done:upto])
        done = upto


def _rope_tables(pos, k_scale):
    half = RET_DK // 2
    inv = ROPE_BASE ** (-np.arange(half, dtype=np.float64) / half)
    ang = np.asarray(pos, np.float64)[:, None] * inv[None, :]
    cos = np.tile(np.cos(ang), (1, 4))
    sin = np.tile(np.concatenate([-np.sin(ang), np.sin(ang)], axis=1), (1, 2))
    return (cos * k_scale).astype(np.float32), (sin * k_scale).astype(np.float32)


def _decay_tables(chunk):
    log_g = np.log(1.0 - np.exp2(-5.0 - np.arange(RET_HEADS, dtype=np.float64)))
    idx = np.arange(chunk, dtype=np.float64)
    diff = idx[:, None] - idx[None, :]
    dmat = np.where(diff[None] >= 0, np.exp(np.maximum(diff, 0.0)[None] * log_g[:, None, None]), 0.0)
    read_dec = np.exp((idx + 1.0)[:, None] * log_g[None, :])
    upd_dec = np.exp((chunk - 1.0 - idx)[:, None] * log_g[None, :])
    chunk_dec = np.exp(chunk * log_g)
    pairs = RET_HEADS // 2
    side_by_side = lambda a: a.reshape(pairs, 2, *a.shape[1:]).transpose(0, 2, 1, 3).reshape(pairs, a.shape[1], -1)
    dmat2 = side_by_side(dmat)
    rd2 = side_by_side(np.broadcast_to(read_dec.T[:, :, None], (RET_HEADS, chunk, RET_DV)))
    updt = np.repeat(upd_dec.T, RET_DK, axis=0).reshape(pairs, 2 * RET_DK, chunk)
    cd = np.broadcast_to(chunk_dec[:, None, None], (RET_HEADS, 1, RET_DV))
    cd2 = side_by_side(cd)
    f = lambda a: np.ascontiguousarray(a, dtype=np.float32)
    return f(dmat2), f(rd2), f(updt), f(cd2), f(cd)


def _const_spec(shape, single_buffer=True):
    zeros = (0,) * len(shape)
    mode = dict(pipeline_mode=pl.Buffered(1)) if single_buffer else {}
    return pl.BlockSpec(shape, lambda j: zeros, **mode)


def _layer_spec(shape, layer, single_buffer=True):
    zeros = (0,) * len(shape)
    mode = dict(pipeline_mode=pl.Buffered(1)) if single_buffer else {}
    return pl.BlockSpec((None,) + tuple(shape), lambda j: (layer,) + zeros, **mode)


def _params(n_axes):
    return pltpu.CompilerParams(dimension_semantics=("arbitrary",) * n_axes,
                                vmem_limit_bytes=V7X_VMEM_LIMIT_BYTES)


def _stacked_call(kern, *, name, grid, in_specs, args, out_specs, out_shape, layer, stacked, prev, scratch_shapes):
    def spec(n_layers, block, index_fn):
        return pl.BlockSpec((n_layers,) + tuple(block), lambda *g: (0,) + tuple(index_fn(*g)))

    in_specs, out_specs, out_shape, args = list(in_specs), list(out_specs), list(out_shape), list(args)
    for dims, block, index_fn in stacked:
        out_specs.append(spec(layer + 1, block, index_fn))
        out_shape.append(jax.ShapeDtypeStruct((layer + 1,) + tuple(dims), F32))
        if layer > 0:
            in_specs.append(spec(layer, block, index_fn))
    if layer > 0:
        args += list(prev)
    return pl.pallas_call(
        kern, grid=grid, in_specs=in_specs, out_specs=out_specs, out_shape=out_shape,
        scratch_shapes=scratch_shapes, compiler_params=_params(len(grid)), name=name)(*args)


def _ada_kernel(cs_ref, cp_ref, w_ref, b_ref, os_ref, op_ref):
    layer = pl.program_id(0)
    bias = b_ref[0:1, :]
    for d in range(1, DEPTH):
        bias = jnp.where(layer == d, b_ref[d:d + 1, :], bias)
    w = w_ref[...].astype(BF16)
    os_ref[...] = jnp.dot(_silu(cs_ref[...]).astype(BF16), w, preferred_element_type=F32) + bias
    mod_p = jnp.dot(_silu(cp_ref[...]).astype(BF16), w, preferred_element_type=F32) + bias
    for b in range(cp_ref.shape[0]):
        op_ref[b] = mod_p[b:b + 1, :]


def _ada_modulation(c_sample, c_prompt, ada_w, ada_b):
    ns, bp = c_sample.shape[0], c_prompt.shape[0]
    ncol = 6 * D_MODEL
    blk = 2 * D_MODEL
    return pl.pallas_call(
        _ada_kernel,
        grid=(DEPTH, ncol // blk),
        in_specs=[
            pl.BlockSpec((ns, D_MODEL), lambda l, j: (0, 0)),
            pl.BlockSpec((bp, D_MODEL), lambda l, j: (0, 0)),
            pl.BlockSpec((None, D_MODEL, blk), lambda l, j: (l, 0, j)),
            pl.BlockSpec((DEPTH, blk), lambda l, j: (0, j)),
        ],
        out_specs=[pl.BlockSpec((None, ns, blk), lambda l, j: (l, 0, j)),
                   pl.BlockSpec((None, bp, 1, blk), lambda l, j: (l, 0, 0, j))],
        out_shape=[jax.ShapeDtypeStruct((DEPTH, ns, ncol), F32),
                   jax.ShapeDtypeStruct((DEPTH, bp, 1, ncol), F32)],
        compiler_params=_params(2),
        name="ada_mod",
    )(c_sample, c_prompt, ada_w, ada_b)


N_PROMPT_IN = 24

def _prompt_layer_kernel(*refs, tile, n_tiles, tiles_per_seq, layer):
    (xc_ref, xp_ref, modc_ref, modp_ref, gpm_ref, gqm_ref, gpf_ref, gqf_ref,
     win_ref, wout_ref, w1_hbm, w3_hbm, w2_hbm,
     rope_ref, dmat_ref, rd_ref, upd_ref, cd_ref,
     gn_ref, caw_ref, cab_ref, lng_ref, lnb_ref, ccw_ref) = refs[:N_PROMPT_IN]
    n_prev = 3 if layer > 0 else 0
    prev_states = refs[N_PROMPT_IN:N_PROMPT_IN + n_prev]
    (y_ref, sret_all, sca_all, scc_all,
     h_scr, proj_scr, mixin_scr, mix_scr, hf_scr, p_scr, ubuf, ua_scr, zbuf,
     sc_scr, qs_scr, sbd_scr, w1_ref, w3_ref, w2_ref, w_sem) = refs[N_PROMPT_IN + n_prev:]
    sret_ref, sca_ref, scc_ref = sret_all.at[layer], sca_all.at[layer], scc_all.at[layer]
    gpm_ref, gqm_ref, gpf_ref, gqf_ref, gn_ref, cab_ref, lng_ref, lnb_ref = (
        ref.at[pl.ds(layer, 1)] for ref in (gpm_ref, gqm_ref, gpf_ref, gqf_ref, gn_ref, cab_ref, lng_ref, lnb_ref))
    mod_chunk = lambda ref, k: ref[:, pl.ds(k * D_MODEL, D_MODEL)]
    step = pl.program_id(0)
    row_blocks = [pl.ds(r * ROW_BLK, ROW_BLK) for r in range(tile // ROW_BLK)]
    first_tap = HIST_A - (CONV_A_K - 1)
    first_tap_c = HIST_C - (CONV_C_K - 1)
    span = SUBLANES * CONV_STRIDE

    ffn_weight_copies = [pltpu.make_async_copy(src, dst, w_sem.at[k])
                         for k, (src, dst) in enumerate(((w1_hbm, w1_ref), (w3_hbm, w3_ref), (w2_hbm, w2_ref)))]

    @pl.when(step == 0)
    def _():
        for copy in ffn_weight_copies:
            copy.start()

    @pl.when(step == 1)
    def _():
        for copy in ffn_weight_copies:
            copy.wait()

    @pl.when((step < n_tiles) & (lax.rem(step, tiles_per_seq) == 0))
    def _():
        for prev_ref, all_ref in zip(prev_states, (sret_all, sca_all, scc_all)):
            all_ref[0:layer] = prev_ref[...]
        sret_ref[...] = jnp.zeros_like(sret_ref)
        sbd_scr[...] = jnp.zeros_like(sbd_scr)
        ubuf[:, 0:HIST_A, :] = jnp.zeros((2, HIST_A, LANES), F32)
        zbuf[0:HIST_C, :] = jnp.zeros((HIST_C, CONV_C_WIDTH), F32)

    def mix_prenorm():
        sh = mod_chunk(modc_ref, 0)
        pre_scale = gpm_ref[...] * (1.0 + mod_chunk(modc_ref, 1))
        for rows in row_blocks:
            h_scr[rows, :] = (_rms(xc_ref[rows, :]) * pre_scale + sh).astype(BF16)

    def in_proj(n):
        cols = pl.ds(n * IN_BLK, IN_BLK)
        proj_scr[:, cols] = jnp.dot(h_scr[...], win_ref[:, cols], preferred_element_type=F32)

    def conv_groups():
        taps = [[jnp.broadcast_to(caw_ref[j:j + 1, pl.ds(half * LANES, LANES)], (SUBLANES, LANES))
                 for j in range(CONV_A_K)] for half in range(2)]
        for r in range(tile // CONV_BLK):
            rows = pl.ds(r * CONV_BLK, CONV_BLK)
            dst = pl.ds(HIST_A + r * CONV_BLK, CONV_BLK)
            for half in range(2):
                lo = half * LANES
                u = (proj_scr[rows, pl.ds(OFF_AV + lo, LANES)]
                     * _sigmoid(proj_scr[rows, pl.ds(OFF_AG + lo, LANES)]))
                ubuf[half, dst, :] = u
        for half in range(2):
            for m in range(tile // span):
                base = m * span
                acc = [None] * CONV_STRIDE
                for k in range(CONV_STRIDE + CONV_A_K - 1):
                    win = ubuf[half, pl.ds(base + first_tap + k, SUBLANES, stride=CONV_STRIDE), :]
                    for t in range(CONV_STRIDE):
                        j = k - t
                        if 0 <= j < CONV_A_K:
                            term = taps[half][j] * win
                            acc[t] = term if acc[t] is None else acc[t] + term
                for t in range(CONV_STRIDE):
                    ua_scr[half, pl.ds(base + t, SUBLANES, stride=CONV_STRIDE), :] = acc[t]
        for r in range(tile // CONV_BLK):
            rows = pl.ds(r * CONV_BLK, CONV_BLK)
            ua = jnp.concatenate([ua_scr[0, rows, :], ua_scr[1, rows, :]], axis=-1) + cab_ref[...]
            o_a = _silu(_layernorm(ua, lng_ref[...], lnb_ref[...]))
            mixin_scr[rows, MIX_A:MIX_A + CONV_A_WIDTH] = o_a.astype(BF16)
        for r in range(tile // CONV_BLK):
            rows = pl.ds(r * CONV_BLK, CONV_BLK)
            z = proj_scr[rows, OFF_CC:OFF_CC + CONV_C_WIDTH] * proj_scr[rows, OFF_CX:OFF_CX + CONV_C_WIDTH]
            zbuf[pl.ds(HIST_C + r * CONV_BLK, CONV_BLK), :] = z
        for r in range(tile // CONV_BLK):
            rows = pl.ds(r * CONV_BLK, CONV_BLK)
            zc = ccw_ref[0:1, :] * zbuf[pl.ds(r * CONV_BLK + first_tap_c, CONV_BLK), :]
            for j in range(1, CONV_C_K):
                zc = zc + ccw_ref[j:j + 1, :] * zbuf[pl.ds(r * CONV_BLK + first_tap_c + j, CONV_BLK), :]
            o_c = proj_scr[rows, OFF_CB:OFF_CB + CONV_C_WIDTH] * zc
            mixin_scr[rows, MIX_C:MIX_C + CONV_C_WIDTH] = o_c.astype(BF16)
        hist_a = pl.ds(tile + first_tap, CONV_A_K - 1)
        sca_ref[...] = jnp.concatenate([ubuf[0, hist_a, :], ubuf[1, hist_a, :]], axis=-1)
        scc_ref[...] = zbuf[pl.ds(tile + first_tap_c, CONV_C_K - 1), :]
        ubuf[:, 0:HIST_A, :] = ubuf[:, pl.ds(tile, HIST_A), :]
        zbuf[0:HIST_C, :] = zbuf[pl.ds(tile, HIST_C), :]

    def retention_scores(i, pair, buf):
        first_half, _ = _first_half_mask(RET_CHUNK)
        rows = pl.ds(i * RET_CHUNK, RET_CHUNK)
        cq, sq, ck, sk = (rope_ref[rows, pl.ds(t * LANES, LANES)] for t in range(4))
        qr = _rope_half(proj_scr[rows, pl.ds(OFF_Q + pair * LANES, LANES)], cq, sq, first_half)
        kr = _rope_half(proj_scr[rows, pl.ds(OFF_K + pair * LANES, LANES)], ck, sk, first_half)
        qr_b = qr.astype(BF16)
        k_t = kr.T
        head0_rows = lax.broadcasted_iota(jnp.int32, (2 * RET_DK, RET_CHUNK), 0) < RET_DK
        k_bd = jnp.concatenate([jnp.where(head0_rows, k_t, 0.0), jnp.where(head0_rows, 0.0, k_t)], axis=1)
        scores = jnp.dot(qr_b, k_bd.astype(BF16), preferred_element_type=F32) * dmat_ref[pair]
        sc_scr[buf] = scores.astype(BF16)
        s_bd = sbd_scr[pair]
        qs_scr[buf] = jnp.dot(qr_b, s_bd.astype(BF16), preferred_element_type=F32) * rd_ref[pair]
        v_pair = proj_scr[rows, pl.ds(OFF_V + 2 * pair * RET_DV, 2 * RET_DV)].astype(BF16)
        kv = jnp.dot((k_t * upd_ref[pair]).astype(BF16), v_pair, preferred_element_type=F32)
        r0 = lax.broadcasted_iota(jnp.int32, (2 * RET_DK, 2 * RET_DV), 0) < RET_DK
        c0 = lax.broadcasted_iota(jnp.int32, (2 * RET_DK, 2 * RET_DV), 1) < RET_DV
        s_new = s_bd * cd_ref[pair] + jnp.where(r0 == c0, kv, 0.0)
        sbd_scr[pair] = s_new
        sret_ref[2 * pair] = s_new[0:RET_DK, 0:RET_DV]
        sret_ref[2 * pair + 1] = s_new[RET_DK:2 * RET_DK, RET_DV:2 * RET_DV]

    def retention_values(i, pair, buf):
        rows = pl.ds(i * RET_CHUNK, RET_CHUNK)
        for hl in range(2):
            h = 2 * pair + hl
            vh = proj_scr[rows, pl.ds(OFF_V + h * RET_DV, RET_DV)].astype(BF16)
            o = (jnp.dot(sc_scr[buf, :, pl.ds(hl * RET_CHUNK, RET_CHUNK)], vh, preferred_element_type=F32)
                 + qs_scr[buf, :, pl.ds(hl * RET_DV, RET_DV)])
            gate = proj_scr[rows, pl.ds(OFF_G + h * RET_DV, RET_DV)]
            o = _layernorm(o, gn_ref[:, pl.ds(h * RET_DV, RET_DV)]) * _silu(gate)
            mixin_scr[rows, pl.ds(h * RET_DV, RET_DV)] = o.astype(BF16)

    def out_proj():
        mix_scr[...] = jnp.dot(mixin_scr[...], wout_ref[...], preferred_element_type=F32)

    in_tiles = [functools.partial(in_proj, n) for n in range(IN_WIDTH // IN_BLK)]
    items = [(i, pair) for i in range(tile // RET_CHUNK) for pair in range(RET_HEADS // 2)]
    ret_items = []
    for k, (i, pair) in enumerate(items):
        ret_items.append(functools.partial(retention_scores, i, pair, k % 2))
        if k > 0:
            ret_items.append(functools.partial(retention_values, *items[k - 1], (k - 1) % 2))
    ret_items.append(functools.partial(retention_values, *items[-1], (len(items) - 1) % 2))
    n_qkvg = OFF_AV // IN_BLK

    def ffn_norms():
        post_scale = gqm_ref[...] * mod_chunk(modp_ref, 2)
        sh = mod_chunk(modp_ref, 3)
        pre_scale = gpf_ref[...] * (1.0 + mod_chunk(modp_ref, 4))
        for rows in row_blocks:
            x1 = xp_ref[rows, :] + _rms(mix_scr[rows, :]) * post_scale
            y_ref[rows, :] = x1
            hf_scr[rows, :] = (_rms(x1) * pre_scale + sh).astype(BF16)

    def hidden(j):
        p_scr[:, pl.ds(j * FF_BLK, FF_BLK)] = _swiglu_block(hf_scr[...], w1_ref, w3_ref, j * FF_BLK)

    def down(n):
        cols = pl.ds(n * FF_BLK, FF_BLK)
        proj_scr[:, cols] = jnp.dot(p_scr[...], w2_ref[:, cols], preferred_element_type=F32)

    def ffn_finish():
        post_scale = gqf_ref[...] * mod_chunk(modp_ref, 5)
        for rows in row_blocks:
            y_ref[rows, :] = y_ref[rows, :] + _rms(proj_scr[rows, 0:D_MODEL]) * post_scale

    hidden_blocks = [functools.partial(hidden, j) for j in range(D_FF // FF_BLK)]
    down_tiles = [functools.partial(down, n) for n in range(D_MODEL // FF_BLK)]

    @pl.when(step == 0)
    def _():
        mix_prenorm()
        _emit(in_tiles)
        _emit(ret_items)
        conv_groups()
        out_proj()

    @pl.when((step > 0) & (step < n_tiles))
    def _():
        mix_prenorm()
        ffn_norms()
        _emit(in_tiles[:n_qkvg])
        _emit_interleaved(in_tiles[n_qkvg:] + hidden_blocks, ret_items)
        conv_groups()
        _emit(down_tiles)
        out_proj()
        ffn_finish()

    @pl.when(step == n_tiles)
    def _():
        ffn_norms()
        _emit(hidden_blocks)
        _emit(down_tiles)
        ffn_finish()


def _prompt_layer(layer, prev, x, mod_p, g_pm, g_qm, g_pf, g_qf, w_in, w_out, w1, w3, w2, tables,
                  gn, caw, cab, lng, lnb, ccw):
    bsz, seq, _ = x.shape
    tile = TOK_TILE
    tps = seq // tile
    n_tiles = bsz * tps
    rope_tab, dmat, rd, upd, cd, _ = tables
    cur = lambda i: jnp.minimum(i, n_tiles - 1)
    prv = lambda i: jnp.maximum(i - 1, 0)
    tok_cur = pl.BlockSpec((None, tile, D_MODEL), lambda i: (cur(i) // tps, cur(i) % tps, 0))
    tok_prev = pl.BlockSpec((None, tile, D_MODEL), lambda i: (prv(i) // tps, prv(i) % tps, 0))
    mod_cur = pl.BlockSpec((None, None, 1, 6 * D_MODEL), lambda i: (layer, cur(i) // tps, 0, 0))
    mod_prev = pl.BlockSpec((None, None, 1, 6 * D_MODEL), lambda i: (layer, prv(i) // tps, 0, 0))
    rope = pl.BlockSpec((tile, 4 * LANES), lambda i: (cur(i) % tps, 0))
    cs = _const_spec
    ls = functools.partial(_layer_spec, layer=layer)
    hbm = pl.BlockSpec(memory_space=pl.ANY)
    assert n_tiles >= 2
    args = (x, x, mod_p, mod_p, g_pm, g_qm, g_pf, g_qf, w_in, w_out, w1, w3, w2,
            rope_tab, dmat, rd, upd, cd, gn, caw, cab, lng, lnb, ccw)
    seq_block = lambda i: (cur(i) // tps,)
    return _stacked_call(
        functools.partial(_prompt_layer_kernel, tile=tile, n_tiles=n_tiles, tiles_per_seq=tps, layer=layer),
        name="prompt_layer", grid=(n_tiles + 1,), args=args, layer=layer, prev=prev,
        stacked=[
            ((bsz, RET_HEADS, RET_DK, RET_DV), (None, RET_HEADS, RET_DK, RET_DV),
             lambda i: seq_block(i) + (0, 0, 0)),
            ((bsz, CONV_A_K - 1, CONV_A_WIDTH), (None, CONV_A_K - 1, CONV_A_WIDTH),
             lambda i: seq_block(i) + (0, 0)),
            ((bsz, CONV_C_K - 1, CONV_C_WIDTH), (None, CONV_C_K - 1, CONV_C_WIDTH),
             lambda i: seq_block(i) + (0, 0)),
        ],
        in_specs=[
            tok_cur, tok_prev, mod_cur, mod_prev,
            cs((DEPTH, D_MODEL)), cs((DEPTH, D_MODEL)), cs((DEPTH, D_MODEL)), cs((DEPTH, D_MODEL)),
            cs((D_MODEL, IN_WIDTH)), cs((D_MODEL, D_MODEL)),
            hbm, hbm, hbm,
            rope,
            cs(dmat.shape), cs(rd.shape), cs(upd.shape), cs(cd.shape),
            cs((DEPTH, RET_WIDTH)), ls((CONV_A_K, CONV_A_WIDTH)), cs((DEPTH, CONV_A_WIDTH)),
            cs((DEPTH, CONV_A_WIDTH)), cs((DEPTH, CONV_A_WIDTH)), ls((CONV_C_K, CONV_C_WIDTH)),
        ],
        out_specs=[tok_prev],
        out_shape=[jax.ShapeDtypeStruct(x.shape, F32)],
        scratch_shapes=[
            pltpu.VMEM((tile, D_MODEL), BF16),
            pltpu.VMEM((tile, IN_WIDTH), F32),
            pltpu.VMEM((tile, D_MODEL), BF16),
            pltpu.VMEM((tile, D_MODEL), F32),
            pltpu.VMEM((tile, D_MODEL), BF16),
            pltpu.VMEM((tile, D_FF), BF16),
            pltpu.VMEM((2, tile + HIST_A, LANES), F32),
            pltpu.VMEM((2, tile, LANES), F32),
            pltpu.VMEM((tile + HIST_C, CONV_C_WIDTH), F32),
            pltpu.VMEM((2, RET_CHUNK, 2 * RET_CHUNK), BF16),
            pltpu.VMEM((2, RET_CHUNK, 2 * RET_DV), F32),
            pltpu.VMEM((RET_HEADS // 2, 2 * RET_DK, 2 * RET_DV), F32),
            pltpu.VMEM((D_MODEL, D_FF), BF16),
            pltpu.VMEM((D_MODEL, D_FF), BF16),
            pltpu.VMEM((D_FF, D_MODEL), BF16),
            pltpu.SemaphoreType.DMA((3,)),
        ])


def _sample_pre_kernel(x_ref, mod_ref, gpre_ref, win_ref, rope_ref, after_ref, proj_ref, qt_ref, kt_ref, winb_ref,
                       h_scr, *, layer):
    j = pl.program_id(0)
    n = x_ref.shape[0]

    @pl.when(j == 0)
    def _():
        sh = mod_ref[:, 0:D_MODEL]
        sc1 = 1.0 + mod_ref[:, D_MODEL:2 * D_MODEL]
        h_scr[...] = (_rms(x_ref[...]) * (gpre_ref[pl.ds(layer, 1), :] * sc1) + sh).astype(BF16)

    w_blk = win_ref[...].astype(BF16)
    winb_ref[...] = w_blk
    proj_ref[...] = jnp.dot(h_scr[...], w_blk, preferred_element_type=F32)

    def rope_transposed(off, cos, sin, dst_ref):
        first_half, _ = _first_half_mask(n)
        for pair in range(RET_HEADS // 2):
            src = proj_ref[:, pl.ds(off + pair * LANES, LANES)]
            dst_ref[pl.ds(pair * LANES, LANES), :] = _rope_half(src, cos, sin, first_half).T

    @pl.when(j == 0)
    def _():
        rope_transposed(OFF_Q, rope_ref[0:1, :], rope_ref[1:2, :], qt_ref)
        rope_transposed(OFF_K, rope_ref[2:3, :], rope_ref[3:4, :], kt_ref)


def _sample_pre(layer, xs, mod, gpre, w_in, rope_s, after):
    n = xs.shape[0]
    cs = functools.partial(_const_spec, single_buffer=False)
    ls = functools.partial(_layer_spec, layer=layer, single_buffer=False)
    return pl.pallas_call(
        functools.partial(_sample_pre_kernel, layer=layer),
        grid=(IN_WIDTH // PRE_BLK,),
        in_specs=[cs((n, D_MODEL)), ls((n, 6 * D_MODEL)), cs((DEPTH, D_MODEL)),
                  pl.BlockSpec((None, D_MODEL, PRE_BLK), lambda j: (layer, 0, j)), cs(rope_s.shape),
                  pl.BlockSpec(memory_space=pl.ANY)],
        out_specs=[pl.BlockSpec((n, PRE_BLK), lambda j: (0, j)), cs((QK_WIDTH, n)), cs((QK_WIDTH, n)),
                   pl.BlockSpec((D_MODEL, PRE_BLK), lambda j: (0, j))],
        out_shape=[jax.ShapeDtypeStruct((n, IN_WIDTH), F32),
                   jax.ShapeDtypeStruct((QK_WIDTH, n), F32),
                   jax.ShapeDtypeStruct((QK_WIDTH, n), F32),
                   jax.ShapeDtypeStruct((D_MODEL, IN_WIDTH), BF16)],
        scratch_shapes=[pltpu.VMEM((n, D_MODEL), BF16)],
        compiler_params=_params(1),
        name="sample_pre",
    )(xs, mod, gpre, w_in, rope_s, after)


def _sample_state_kernel(*refs, layer):
    (proj_ref, qt_ref, kt_ref, sin_ref, bufa_ref, bufc_ref, cd_ref,
     gn_ref, caw_ref, cab_ref, lng_ref, lnb_ref, ccw_ref) = refs[:13]
    n_prev = 3 if layer > 0 else 0
    prev_states = refs[13:13 + n_prev]
    mix_ref, sout_all, outa_all, outc_all, o_scr = refs[13 + n_prev:]
    for prev_ref, all_ref in zip(prev_states, (sout_all, outa_all, outc_all)):
        all_ref[0:layer] = prev_ref[...]
    sout_ref, outa_ref, outc_ref = sout_all.at[layer], outa_all.at[layer], outc_all.at[layer]
    gn_ref, cab_ref, lng_ref, lnb_ref = (ref.at[pl.ds(layer, 1)] for ref in (gn_ref, cab_ref, lng_ref, lnb_ref))
    blk = pl.program_id(0)
    n = qt_ref.shape[1]
    lane = lax.broadcasted_iota(jnp.int32, (QK_WIDTH, n), 1)

    for bl in range(SEQ_BLK):
        onehot = lane == (blk * SEQ_BLK + bl)
        qcol = jnp.sum(jnp.where(onehot, qt_ref[...], 0.0), axis=1, keepdims=True)
        kcol = jnp.sum(jnp.where(onehot, kt_ref[...], 0.0), axis=1, keepdims=True)
        row = pl.ds(bl, 1)
        for h in range(RET_HEADS):
            vrow = proj_ref[row, pl.ds(OFF_V + h * RET_DV, RET_DV)]
            s_new = (sin_ref[bl, h] * cd_ref[h]
                     + kcol[h * RET_DK:(h + 1) * RET_DK, :] * vrow)
            sout_ref[bl, h] = s_new
            o_scr[row, pl.ds(h * RET_DV, RET_DV)] = jnp.sum(
                qcol[h * RET_DK:(h + 1) * RET_DK, :] * s_new, axis=0, keepdims=True)
    for h in range(RET_HEADS):
        cols = pl.ds(h * RET_DV, RET_DV)
        o = _layernorm(o_scr[:, cols], gn_ref[:, cols]) * _silu(proj_ref[:, pl.ds(OFF_G + h * RET_DV, RET_DV)])
        mix_ref[:, cols] = o

    u = proj_ref[:, OFF_AV:OFF_AV + CONV_A_WIDTH] * _sigmoid(proj_ref[:, OFF_AG:OFF_AG + CONV_A_WIDTH])
    ua = caw_ref[CONV_A_K - 1:CONV_A_K, :] * u
    for j in range(CONV_A_K - 1):
        ua = ua + caw_ref[j:j + 1, :] * bufa_ref[j]
    outa_ref[0:CONV_A_K - 2] = bufa_ref[1:CONV_A_K - 1]
    outa_ref[CONV_A_K - 2] = u
    mix_ref[:, MIX_A:MIX_A + CONV_A_WIDTH] = _silu(_layernorm(ua + cab_ref[...], lng_ref[...], lnb_ref[...]))

    z = proj_ref[:, OFF_CC:OFF_CC + CONV_C_WIDTH] * proj_ref[:, OFF_CX:OFF_CX + CONV_C_WIDTH]
    zc = ccw_ref[0:1, :] * bufc_ref[0] + ccw_ref[1:2, :] * bufc_ref[1] + ccw_ref[2:3, :] * z
    outc_ref[0] = bufc_ref[1]
    outc_ref[1] = z
    mix_ref[:, MIX_C:MIX_C + CONV_C_WIDTH] = proj_ref[:, OFF_CB:OFF_CB + CONV_C_WIDTH] * zc


def _sample_state(layer, prev, proj, qt, kt, s_ret, buf_a_t, buf_c_t, cd, gn, caw, cab, lng, lnb, ccw):
    n = proj.shape[0]
    cs = functools.partial(_const_spec, single_buffer=False)
    ls = functools.partial(_layer_spec, layer=layer, single_buffer=False)
    s_spec = pl.BlockSpec((None, SEQ_BLK, RET_HEADS, RET_DK, RET_DV), lambda j: (layer, j, 0, 0, 0))
    a_spec = pl.BlockSpec((None, CONV_A_K - 1, SEQ_BLK, CONV_A_WIDTH), lambda j: (layer, 0, j, 0))
    c_spec = pl.BlockSpec((None, CONV_C_K - 1, SEQ_BLK, CONV_C_WIDTH), lambda j: (layer, 0, j, 0))
    args = (proj, qt, kt, s_ret, buf_a_t, buf_c_t, cd, gn, caw, cab, lng, lnb, ccw)
    return _stacked_call(
        functools.partial(_sample_state_kernel, layer=layer),
        name="sample_state", grid=(n // SEQ_BLK,), args=args, layer=layer, prev=prev,
        stacked=[
            ((n, RET_HEADS, RET_DK, RET_DV), (SEQ_BLK, RET_HEADS, RET_DK, RET_DV), lambda j: (j, 0, 0, 0)),
            ((CONV_A_K - 1, n, CONV_A_WIDTH), (CONV_A_K - 1, SEQ_BLK, CONV_A_WIDTH), lambda j: (0, j, 0)),
            ((CONV_C_K - 1, n, CONV_C_WIDTH), (CONV_C_K - 1, SEQ_BLK, CONV_C_WIDTH), lambda j: (0, j, 0)),
        ],
        in_specs=[
            pl.BlockSpec((SEQ_BLK, IN_WIDTH), lambda j: (j, 0)),
            cs((QK_WIDTH, n)), cs((QK_WIDTH, n)),
            s_spec, a_spec, c_spec,
            cs(cd.shape), cs((DEPTH, RET_WIDTH)), ls((CONV_A_K, CONV_A_WIDTH)), cs((DEPTH, CONV_A_WIDTH)),
            cs((DEPTH, CONV_A_WIDTH)), cs((DEPTH, CONV_A_WIDTH)), ls((CONV_C_K, CONV_C_WIDTH)),
        ],
        out_specs=[pl.BlockSpec((SEQ_BLK, D_MODEL), lambda j: (j, 0))],
        out_shape=[jax.ShapeDtypeStruct((n, D_MODEL), F32)],
        scratch_shapes=[pltpu.VMEM((SEQ_BLK, RET_WIDTH), F32)])


def _sample_post_kernel(x_ref, mod_ref, mix_ref, wout_ref, gpm_ref, gpf_ref, gqf_ref, w1_ref, w3_ref, w2_ref,
                        y_ref, woutb_ref, w1b_ref, w3b_ref, w2b_ref, x1_scr, h_scr, f_scr, *, n_steps, layer):
    j = pl.program_id(0)
    gpm_ref, gpf_ref, gqf_ref = (ref.at[pl.ds(layer, 1)] for ref in (gpm_ref, gpf_ref, gqf_ref))

    @pl.when(j == 0)
    def _():
        gt_m = mod_ref[:, 2 * D_MODEL:3 * D_MODEL]
        wout_b = wout_ref[...].astype(BF16)
        woutb_ref[...] = wout_b
        mix = jnp.dot(mix_ref[...].astype(BF16), wout_b, preferred_element_type=F32)
        x1 = x_ref[...] + _rms(mix) * (gpm_ref[...] * gt_m)
        x1_scr[...] = x1
        sh = mod_ref[:, 3 * D_MODEL:4 * D_MODEL]
        sc1 = 1.0 + mod_ref[:, 4 * D_MODEL:5 * D_MODEL]
        h_scr[...] = (_rms(x1) * (gpf_ref[...] * sc1) + sh).astype(BF16)
        f_scr[...] = jnp.zeros_like(f_scr)

    w1_b, w3_b, w2_b = w1_ref[...].astype(BF16), w3_ref[...].astype(BF16), w2_ref[...].astype(BF16)
    w1b_ref[...] = w1_b
    w3b_ref[...] = w3_b
    w2b_ref[...] = w2_b
    h = h_scr[...]
    a = jnp.dot(h, w1_b, preferred_element_type=F32)
    b = jnp.dot(h, w3_b, preferred_element_type=F32)
    p = (_silu(a) * b).astype(BF16)
    f_scr[...] += jnp.dot(p, w2_b, preferred_element_type=F32)

    @pl.when(j == n_steps - 1)
    def _():
        gt_f = mod_ref[:, 5 * D_MODEL:6 * D_MODEL]
        y_ref[...] = x1_scr[...] + _rms(f_scr[...]) * (gqf_ref[...] * gt_f)


def _sample_post(layer, xs, mod, mix, w_out, gpost_m, gpre_f, gpost_f, w1, w3, w2):
    n = xs.shape[0]
    n_steps = D_FF // FF_BLK
    cs = functools.partial(_const_spec, single_buffer=False)
    ls = functools.partial(_layer_spec, layer=layer, single_buffer=False)
    return pl.pallas_call(
        functools.partial(_sample_post_kernel, n_steps=n_steps, layer=layer),
        grid=(n_steps,),
        in_specs=[
            cs((n, D_MODEL)), ls((n, 6 * D_MODEL)), cs((n, D_MODEL)), ls((D_MODEL, D_MODEL)),
            cs((DEPTH, D_MODEL)), cs((DEPTH, D_MODEL)), cs((DEPTH, D_MODEL)),
            pl.BlockSpec((None, D_MODEL, FF_BLK), lambda j: (layer, 0, j)),
            pl.BlockSpec((None, D_MODEL, FF_BLK), lambda j: (layer, 0, j)),
            pl.BlockSpec((None, FF_BLK, D_MODEL), lambda j: (layer, j, 0)),
        ],
        out_specs=[
            cs((n, D_MODEL)), cs((D_MODEL, D_MODEL)),
            pl.BlockSpec((D_MODEL, FF_BLK), lambda j: (0, j)),
            pl.BlockSpec((D_MODEL, FF_BLK), lambda j: (0, j)),
            pl.BlockSpec((FF_BLK, D_MODEL), lambda j: (j, 0)),
        ],
        out_shape=[
            jax.ShapeDtypeStruct((n, D_MODEL), F32),
            jax.ShapeDtypeStruct((D_MODEL, D_MODEL), BF16),
            jax.ShapeDtypeStruct((D_MODEL, D_FF), BF16),
            jax.ShapeDtypeStruct((D_MODEL, D_FF), BF16),
            jax.ShapeDtypeStruct((D_FF, D_MODEL), BF16),
        ],
        scratch_shapes=[
            pltpu.VMEM((n, D_MODEL), F32),
            pltpu.VMEM((n, D_MODEL), BF16),
            pltpu.VMEM((n, D_MODEL), F32),
        ],
        compiler_params=_params(1),
        name="sample_post",
    )(xs, mod, mix, w_out, gpost_m, gpre_f, gpost_f, w1, w3, w2)


def kernel(x_prompt, x_sample, c_prompt, c_sample, state_ret, state_conv_a, state_conv_c, ada_w, ada_b, norm_pre_mix, norm_post_mix, norm_pre_ffn, norm_post_ffn, w_in, w_out, ret_gn_g, conv_a_w, conv_a_b, conv_a_ln_g, conv_a_ln_b, conv_c_w, ffn_w1, ffn_w3, ffn_w2):
    bp, lp, _ = x_prompt.shape
    ns = x_sample.shape[0]
    assert x_sample.shape[1] == 1 and ns % SEQ_BLK == 0
    assert OFF_K + QK_WIDTH <= PRE_BLK and IN_WIDTH % PRE_BLK == 0
    assert lp % TOK_TILE == 0 and TOK_TILE % RET_CHUNK == 0
    assert TOK_TILE % CONV_BLK == 0 and CONV_BLK % (SUBLANES * CONV_STRIDE) == 0

    k_scale = RET_DK ** -0.5
    cq, sq = _rope_tables(np.arange(lp), 1.0)
    ck, sk = _rope_tables(np.arange(lp), k_scale)
    tables_p = (np.concatenate([cq, sq, ck, sk], axis=1),) + _decay_tables(RET_CHUNK)
    cqs, sqs = _rope_tables([PAST_LEN], 1.0)
    cks, sks = _rope_tables([PAST_LEN], k_scale)
    rope_s = np.concatenate([cqs, sqs, cks, sks, np.zeros((4, LANES), np.float32)], axis=0)
    cd_s = _decay_tables(1)[4]

    mod_s, mod_p = _ada_modulation(c_sample, c_prompt, ada_w, ada_b)

    gn, cab, lng, lnb = ret_gn_g, conv_a_b, conv_a_ln_g, conv_a_ln_b
    g_pm, g_qm, g_pf, g_qf = norm_pre_mix, norm_post_mix, norm_pre_ffn, norm_post_ffn
    conv_a_t = jnp.transpose(state_conv_a, (0, 2, 1, 3))
    conv_c_t = jnp.transpose(state_conv_c, (0, 2, 1, 3))

    yp = x_prompt
    ys = x_sample.reshape(ns, D_MODEL)
    st_p = st_s = ()
    for l in range(DEPTH):
        proj, qt, kt, w_in_b = _sample_pre(l, ys, mod_s, g_pm, w_in, rope_s, yp)
        mix, *st_s = _sample_state(l, st_s, proj, qt, kt, state_ret, conv_a_t, conv_c_t, cd_s,
                                   gn, conv_a_w, cab, lng, lnb, conv_c_w)
        ys, w_out_b, w1_b, w3_b, w2_b = _sample_post(l, ys, mod_s, mix, w_out, g_qm, g_pf, g_qf,
                                                    ffn_w1, ffn_w3, ffn_w2)
        yp, *st_p = _prompt_layer(l, st_p, yp, mod_p, g_pm, g_qm, g_pf, g_qf, w_in_b, w_out_b, w1_b, w3_b, w2_b,
                                  tables_p, gn, conv_a_w, cab, lng, lnb, conv_c_w)

    sret_s, sca_s, scc_s = st_s
    sca_s = jnp.transpose(sca_s, (0, 2, 1, 3))
    scc_s = jnp.transpose(scc_s, (0, 2, 1, 3))
    return (yp, ys.reshape(ns, 1, D_MODEL)) + tuple(st_p) + (sret_s, sca_s, scc_s)
```

```python
import functools

import numpy as np
import jax
import jax.numpy as jnp
from jax import lax
from jax.experimental import pallas as pl
from jax.experimental.pallas import tpu as pltpu

D_MODEL = 1024
DEPTH = 2
PAST_LEN = 16384
RET_HEADS = 4
RET_WIDTH = D_MODEL // 2
RET_DV = RET_WIDTH // RET_HEADS
RET_DK = RET_DV // 2
QK_WIDTH = RET_HEADS * RET_DK
CONV_A_WIDTH = D_MODEL // 4
CONV_A_K = 31
CONV_C_WIDTH = D_MODEL - RET_WIDTH - CONV_A_WIDTH
CONV_C_K = 3
IN_WIDTH = 2 * QK_WIDTH + 2 * RET_WIDTH + 2 * CONV_A_WIDTH + 3 * CONV_C_WIDTH
D_FF = ((8 * D_MODEL // 3 + 255) // 256) * 256
RET_CHUNK = 128
ROPE_BASE = 10000.0
EPS = 1e-6

OFF_Q = 0
OFF_K = OFF_Q + QK_WIDTH
OFF_V = OFF_K + QK_WIDTH
OFF_G = OFF_V + RET_WIDTH
OFF_AV = OFF_G + RET_WIDTH
OFF_AG = OFF_AV + CONV_A_WIDTH
OFF_CB = OFF_AG + CONV_A_WIDTH
OFF_CC = OFF_CB + CONV_C_WIDTH
OFF_CX = OFF_CC + CONV_C_WIDTH
MIX_A = RET_WIDTH
MIX_C = RET_WIDTH + CONV_A_WIDTH

LANES = 128
SUBLANES = 8
V7X_VMEM_LIMIT_BYTES = 60 * 1024 * 1024

TOK_TILE = 512
ROW_BLK = 32
CONV_BLK = 64
CONV_STRIDE = 4
HIST_A = 32
HIST_C = 8
SEQ_BLK = 16
FF_BLK = 256
IN_BLK = 256
PRE_BLK = 1408

F32 = jnp.float32
BF16 = jnp.bfloat16


def _sigmoid(x):
    return 0.5 * jnp.tanh(0.5 * x) + 0.5


def _silu(x):
    return x * _sigmoid(x)


def _rms(x):
    return x * lax.rsqrt(jnp.mean(x * x, axis=-1, keepdims=True) + EPS)


def _layernorm(x, g, b=None):
    mu = jnp.mean(x, axis=-1, keepdims=True)
    d = x - mu
    var = jnp.mean(d * d, axis=-1, keepdims=True)
    y = d * lax.rsqrt(var + EPS) * g
    return y if b is None else y + b


def _rope_half(x, cos, sin, first_half):
    partner = jnp.where(first_half, pltpu.roll(x, 96, 1), pltpu.roll(x, 32, 1))
    return x * cos + partner * sin


def _first_half_mask(rows):
    lane = lax.broadcasted_iota(jnp.int32, (rows, LANES), 1)
    return (lane & (RET_DK - 1)) < (RET_DK // 2), lane < RET_DK


def _swiglu_block(h, w1_ref, w3_ref, col0):
    cols = pl.ds(col0, FF_BLK)
    a = jnp.dot(h, w1_ref[:, cols], preferred_element_type=F32)
    b = jnp.dot(h, w3_ref[:, cols], preferred_element_type=F32)
    return (_silu(a) * b).astype(BF16)


def _emit(items):
    for item in items:
        item()


def _emit_interleaved(primary, secondary):
    n, m = len(primary), len(secondary)
    done = 0
    for i, item in enumerate(primary):
        item()
        upto = ((i + 1) * m) // n
        _emit(secondary[done:upto])
        done = upto


def _rope_tables(pos, k_scale):
    half = RET_DK // 2
    inv = ROPE_BASE ** (-np.arange(half, dtype=np.float64) / half)
    ang = np.asarray(pos, np.float64)[:, None] * inv[None, :]
    cos = np.tile(np.cos(ang), (1, 4))
    sin = np.tile(np.concatenate([-np.sin(ang), np.sin(ang)], axis=1), (1, 2))
    return (cos * k_scale).astype(np.float32), (sin * k_scale).astype(np.float32)


def _decay_tables(chunk):
    log_g = np.log(1.0 - np.exp2(-5.0 - np.arange(RET_HEADS, dtype=np.float64)))
    idx = np.arange(chunk, dtype=np.float64)
    diff = idx[:, None] - idx[None, :]
    dmat = np.where(diff[None] >= 0, np.exp(np.maximum(diff, 0.0)[None] * log_g[:, None, None]), 0.0)
    read_dec = np.exp((idx + 1.0)[:, None] * log_g[None, :])
    upd_dec = np.exp((chunk - 1.0 - idx)[:, None] * log_g[None, :])
    chunk_dec = np.exp(chunk * log_g)
    pairs = RET_HEADS // 2
    side_by_side = lambda a: a.reshape(pairs, 2, *a.shape[1:]).transpose(0, 2, 1, 3).reshape(pairs, a.shape[1], -1)
    dmat2 = side_by_side(dmat)
    rd2 = side_by_side(np.broadcast_to(read_dec.T[:, :, None], (RET_HEADS, chunk, RET_DV)))
    updt = np.repeat(upd_dec.T, RET_DK, axis=0).reshape(pairs, 2 * RET_DK, chunk)
    cd = np.broadcast_to(chunk_dec[:, None, None], (RET_HEADS, 1, RET_DV))
    cd2 = side_by_side(cd)
    f = lambda a: np.ascontiguousarray(a, dtype=np.float32)
    return f(dmat2), f(rd2), f(updt), f(cd2), f(cd)


def _const_spec(shape, single_buffer=True):
    zeros = (0,) * len(shape)
    mode = dict(pipeline_mode=pl.Buffered(1)) if single_buffer else {}
    return pl.BlockSpec(shape, lambda j: zeros, **mode)


def _layer_spec(shape, layer, single_buffer=True):
    zeros = (0,) * len(shape)
    mode = dict(pipeline_mode=pl.Buffered(1)) if single_buffer else {}
    return pl.BlockSpec((None,) + tuple(shape), lambda j: (layer,) + zeros, **mode)


def _params(n_axes):
    return pltpu.CompilerParams(dimension_semantics=("arbitrary",) * n_axes,
                                vmem_limit_bytes=V7X_VMEM_LIMIT_BYTES)


def _stacked_call(kern, *, name, grid, in_specs, args, out_specs, out_shape, layer, stacked, prev, scratch_shapes):
    def spec(n_layers, block, index_fn):
        return pl.BlockSpec((n_layers,) + tuple(block), lambda *g: (0,) + tuple(index_fn(*g)))

    in_specs, out_specs, out_shape, args = list(in_specs), list(out_specs), list(out_shape), list(args)
    for dims, block, index_fn in stacked:
        out_specs.append(spec(layer + 1, block, index_fn))
        out_shape.append(jax.ShapeDtypeStruct((layer + 1,) + tuple(dims), F32))
        if layer > 0:
            in_specs.append(spec(layer, block, index_fn))
    if layer > 0:
        args += list(prev)
    return pl.pallas_call(
        kern, grid=grid, in_specs=in_specs, out_specs=out_specs, out_shape=out_shape,
        scratch_shapes=scratch_shapes, compiler_params=_params(len(grid)), name=name)(*args)


def _ada_kernel(cs_ref, cp_ref, w_ref, b_ref, os_ref, op_ref):
    layer = pl.program_id(0)
    bias = b_ref[0:1, :]
    for d in range(1, DEPTH):
        bias = jnp.where(layer == d, b_ref[d:d + 1, :], bias)
    w = w_ref[...].astype(BF16)
    os_ref[...] = jnp.dot(_silu(cs_ref[...]).astype(BF16), w, preferred_element_type=F32) + bias
    mod_p = jnp.dot(_silu(cp_ref[...]).astype(BF16), w, preferred_element_type=F32) + bias
    for b in range(cp_ref.shape[0]):
        op_ref[b] = mod_p[b:b + 1, :]


def _ada_modulation(c_sample, c_prompt, ada_w, ada_b):
    ns, bp = c_sample.shape[0], c_prompt.shape[0]
    ncol = 6 * D_MODEL
    blk = 2 * D_MODEL
    return pl.pallas_call(
        _ada_kernel,
        grid=(DEPTH, ncol // blk),
        in_specs=[
            pl.BlockSpec((ns, D_MODEL), lambda l, j: (0, 0)),
            pl.BlockSpec((bp, D_MODEL), lambda l, j: (0, 0)),
            pl.BlockSpec((None, D_MODEL, blk), lambda l, j: (l, 0, j)),
            pl.BlockSpec((DEPTH, blk), lambda l, j: (0, j)),
        ],
        out_specs=[pl.BlockSpec((None, ns, blk), lambda l, j: (l, 0, j)),
                   pl.BlockSpec((None, bp, 1, blk), lambda l, j: (l, 0, 0, j))],
        out_shape=[jax.ShapeDtypeStruct((DEPTH, ns, ncol), F32),
                   jax.ShapeDtypeStruct((DEPTH, bp, 1, ncol), F32)],
        compiler_params=_params(2),
        name="ada_mod",
    )(c_sample, c_prompt, ada_w, ada_b)


N_PROMPT_IN = 24

def _prompt_layer_kernel(*refs, tile, n_tiles, tiles_per_seq, layer):
    (xc_ref, xp_ref, modc_ref, modp_ref, gpm_ref, gqm_ref, gpf_ref, gqf_ref,
     win_ref, wout_ref, w1_hbm, w3_hbm, w2_hbm,
     rope_ref, dmat_ref, rd_ref, upd_ref, cd_ref,
     gn_ref, caw_ref, cab_ref, lng_ref, lnb_ref, ccw_ref) = refs[:N_PROMPT_IN]
    n_prev = 3 if layer > 0 else 0
    prev_states = refs[N_PROMPT_IN:N_PROMPT_IN + n_prev]
    (y_ref, sret_all, sca_all, scc_all,
     h_scr, proj_scr, mixin_scr, mix_scr, hf_scr, p_scr, ubuf, ua_scr, zbuf,
     sc_scr, qs_scr, sbd_scr, w1_ref, w3_ref, w2_ref, w_sem) = refs[N_PROMPT_IN + n_prev:]
    sret_ref, sca_ref, scc_ref = sret_all.at[layer], sca_all.at[layer], scc_all.at[layer]
    gpm_ref, gqm_ref, gpf_ref, gqf_ref, gn_ref, cab_ref, lng_ref, lnb_ref = (
        ref.at[pl.ds(layer, 1)] for ref in (gpm_ref, gqm_ref, gpf_ref, gqf_ref, gn_ref, cab_ref, lng_ref, lnb_ref))
    mod_chunk = lambda ref, k: ref[:, pl.ds(k * D_MODEL, D_MODEL)]
    step = pl.program_id(0)
    row_blocks = [pl.ds(r * ROW_BLK, ROW_BLK) for r in range(tile // ROW_BLK)]
    first_tap = HIST_A - (CONV_A_K - 1)
    first_tap_c = HIST_C - (CONV_C_K - 1)
    span = SUBLANES * CONV_STRIDE

    ffn_weight_copies = [pltpu.make_async_copy(src, dst, w_sem.at[k])
                         for k, (src, dst) in enumerate(((w1_hbm, w1_ref), (w3_hbm, w3_ref), (w2_hbm, w2_ref)))]

    @pl.when(step == 0)
    def _():
        for copy in ffn_weight_copies:
            copy.start()

    @pl.when(step == 1)
    def _():
        for copy in ffn_weight_copies:
            copy.wait()

    @pl.when((step < n_tiles) & (lax.rem(step, tiles_per_seq) == 0))
    def _():
        for prev_ref, all_ref in zip(prev_states, (sret_all, sca_all, scc_all)):
            all_ref[0:layer] = prev_ref[...]
        sret_ref[...] = jnp.zeros_like(sret_ref)
        sbd_scr[...] = jnp.zeros_like(sbd_scr)
        ubuf[:, 0:HIST_A, :] = jnp.zeros((2, HIST_A, LANES), F32)
        zbuf[0:HIST_C, :] = jnp.zeros((HIST_C, CONV_C_WIDTH), F32)

    def mix_prenorm():
        sh = mod_chunk(modc_ref, 0)
        pre_scale = gpm_ref[...] * (1.0 + mod_chunk(modc_ref, 1))
        for rows in row_blocks:
            h_scr[rows, :] = (_rms(xc_ref[rows, :]) * pre_scale + sh).astype(BF16)

    def in_proj(n):
        cols = pl.ds(n * IN_BLK, IN_BLK)
        proj_scr[:, cols] = jnp.dot(h_scr[...], win_ref[:, cols], preferred_element_type=F32)

    def conv_groups():
        taps = [[jnp.broadcast_to(caw_ref[j:j + 1, pl.ds(half * LANES, LANES)], (SUBLANES, LANES))
                 for j in range(CONV_A_K)] for half in range(2)]
        for r in range(tile // CONV_BLK):
            rows = pl.ds(r * CONV_BLK, CONV_BLK)
            dst = pl.ds(HIST_A + r * CONV_BLK, CONV_BLK)
            for half in range(2):
                lo = half * LANES
                u = (proj_scr[rows, pl.ds(OFF_AV + lo, LANES)]
                     * _sigmoid(proj_scr[rows, pl.ds(OFF_AG + lo, LANES)]))
                ubuf[half, dst, :] = u
        for half in range(2):
            for m in range(tile // span):
                base = m * span
                acc = [None] * CONV_STRIDE
                for k in range(CONV_STRIDE + CONV_A_K - 1):
                    win = ubuf[half, pl.ds(base + first_tap + k, SUBLANES, stride=CONV_STRIDE), :]
                    for t in range(CONV_STRIDE):
                        j = k - t
                        if 0 <= j < CONV_A_K:
                            term = taps[half][j] * win
                            acc[t] = term if acc[t] is None else acc[t] + term
                for t in range(CONV_STRIDE):
                    ua_scr[half, pl.ds(base + t, SUBLANES, stride=CONV_STRIDE), :] = acc[t]
        for r in range(tile // CONV_BLK):
            rows = pl.ds(r * CONV_BLK, CONV_BLK)
            ua = jnp.concatenate([ua_scr[0, rows, :], ua_scr[1, rows, :]], axis=-1) + cab_ref[...]
            o_a = _silu(_layernorm(ua, lng_ref[...], lnb_ref[...]))
            mixin_scr[rows, MIX_A:MIX_A + CONV_A_WIDTH] = o_a.astype(BF16)
        for r in range(tile // CONV_BLK):
            rows = pl.ds(r * CONV_BLK, CONV_BLK)
            z = proj_scr[rows, OFF_CC:OFF_CC + CONV_C_WIDTH] * proj_scr[rows, OFF_CX:OFF_CX + CONV_C_WIDTH]
            zbuf[pl.ds(HIST_C + r * CONV_BLK, CONV_BLK), :] = z
        for r in range(tile // CONV_BLK):
            rows = pl.ds(r * CONV_BLK, CONV_BLK)
            zc = ccw_ref[0:1, :] * zbuf[pl.ds(r * CONV_BLK + first_tap_c, CONV_BLK), :]
            for j in range(1, CONV_C_K):
                zc = zc + ccw_ref[j:j + 1, :] * zbuf[pl.ds(r * CONV_BLK + first_tap_c + j, CONV_BLK), :]
            o_c = proj_scr[rows, OFF_CB:OFF_CB + CONV_C_WIDTH] * zc
            mixin_scr[rows, MIX_C:MIX_C + CONV_C_WIDTH] = o_c.astype(BF16)
        hist_a = pl.ds(tile + first_tap, CONV_A_K - 1)
        sca_ref[...] = jnp.concatenate([ubuf[0, hist_a, :], ubuf[1, hist_a, :]], axis=-1)
        scc_ref[...] = zbuf[pl.ds(tile + first_tap_c, CONV_C_K - 1), :]
        ubuf[:, 0:HIST_A, :] = ubuf[:, pl.ds(tile, HIST_A), :]
        zbuf[0:HIST_C, :] = zbuf[pl.ds(tile, HIST_C), :]

    def retention_scores(i, pair, buf):
        first_half, _ = _first_half_mask(RET_CHUNK)
        rows = pl.ds(i * RET_CHUNK, RET_CHUNK)
        cq, sq, ck, sk = (rope_ref[rows, pl.ds(t * LANES, LANES)] for t in range(4))
        qr = _rope_half(proj_scr[rows, pl.ds(OFF_Q + pair * LANES, LANES)], cq, sq, first_half)
        kr = _rope_half(proj_scr[rows, pl.ds(OFF_K + pair * LANES, LANES)], ck, sk, first_half)
        qr_b = qr.astype(BF16)
        k_t = kr.T
        head0_rows = lax.broadcasted_iota(jnp.int32, (2 * RET_DK, RET_CHUNK), 0) < RET_DK
        k_bd = jnp.concatenate([jnp.where(head0_rows, k_t, 0.0), jnp.where(head0_rows, 0.0, k_t)], axis=1)
        scores = jnp.dot(qr_b, k_bd.astype(BF16), preferred_element_type=F32) * dmat_ref[pair]
        sc_scr[buf] = scores.astype(BF16)
        s_bd = sbd_scr[pair]
        qs_scr[buf] = jnp.dot(qr_b, s_bd.astype(BF16), preferred_element_type=F32) * rd_ref[pair]
        v_pair = proj_scr[rows, pl.ds(OFF_V + 2 * pair * RET_DV, 2 * RET_DV)].astype(BF16)
        kv = jnp.dot((k_t * upd_ref[pair]).astype(BF16), v_pair, preferred_element_type=F32)
        r0 = lax.broadcasted_iota(jnp.int32, (2 * RET_DK, 2 * RET_DV), 0) < RET_DK
        c0 = lax.broadcasted_iota(jnp.int32, (2 * RET_DK, 2 * RET_DV), 1) < RET_DV
        s_new = s_bd * cd_ref[pair] + jnp.where(r0 == c0, kv, 0.0)
        sbd_scr[pair] = s_new
        sret_ref[2 * pair] = s_new[0:RET_DK, 0:RET_DV]
        sret_ref[2 * pair + 1] = s_new[RET_DK:2 * RET_DK, RET_DV:2 * RET_DV]

    def retention_values(i, pair, buf):
        rows = pl.ds(i * RET_CHUNK, RET_CHUNK)
        for hl in range(2):
            h = 2 * pair + hl
            vh = proj_scr[rows, pl.ds(OFF_V + h * RET_DV, RET_DV)].astype(BF16)
            o = (jnp.dot(sc_scr[buf, :, pl.ds(hl * RET_CHUNK, RET_CHUNK)], vh, preferred_element_type=F32)
                 + qs_scr[buf, :, pl.ds(hl * RET_DV, RET_DV)])
            gate = proj_scr[rows, pl.ds(OFF_G + h * RET_DV, RET_DV)]
            o = _layernorm(o, gn_ref[:, pl.ds(h * RET_DV, RET_DV)]) * _silu(gate)
            mixin_scr[rows, pl.ds(h * RET_DV, RET_DV)] = o.astype(BF16)

    def out_proj():
        mix_scr[...] = jnp.dot(mixin_scr[...], wout_ref[...], preferred_element_type=F32)

    in_tiles = [functools.partial(in_proj, n) for n in range(IN_WIDTH // IN_BLK)]
    items = [(i, pair) for i in range(tile // RET_CHUNK) for pair in range(RET_HEADS // 2)]
    ret_items = []
    for k, (i, pair) in enumerate(items):
        ret_items.append(functools.partial(retention_scores, i, pair, k % 2))
        if k > 0:
            ret_items.append(functools.partial(retention_values, *items[k - 1], (k - 1) % 2))
    ret_items.append(functools.partial(retention_values, *items[-1], (len(items) - 1) % 2))
    n_qkvg = OFF_AV // IN_BLK

    def ffn_norms():
        post_scale = gqm_ref[...] * mod_chunk(modp_ref, 2)
        sh = mod_chunk(modp_ref, 3)
        pre_scale = gpf_ref[...] * (1.0 + mod_chunk(modp_ref, 4))
        for rows in row_blocks:
            x1 = xp_ref[rows, :] + _rms(mix_scr[rows, :]) * post_scale
            y_ref[rows, :] = x1
            hf_scr[rows, :] = (_rms(x1) * pre_scale + sh).astype(BF16)

    def hidden(j):
        p_scr[:, pl.ds(j * FF_BLK, FF_BLK)] = _swiglu_block(hf_scr[...], w1_ref, w3_ref, j * FF_BLK)

    def down(n):
        cols = pl.ds(n * FF_BLK, FF_BLK)
        proj_scr[:, cols] = jnp.dot(p_scr[...], w2_ref[:, cols], preferred_element_type=F32)

    def ffn_finish():
        post_scale = gqf_ref[...] * mod_chunk(modp_ref, 5)
        for rows in row_blocks:
            y_ref[rows, :] = y_ref[rows, :] + _rms(proj_scr[rows, 0:D_MODEL]) * post_scale

    hidden_blocks = [functools.partial(hidden, j) for j in range(D_FF // FF_BLK)]
    down_tiles = [functools.partial(down, n) for n in range(D_MODEL // FF_BLK)]

    @pl.when(step == 0)
    def _():
        mix_prenorm()
        _emit(in_tiles)
        _emit(ret_items)
        conv_groups()
        out_proj()

    @pl.when((step > 0) & (step < n_tiles))
    def _():
        mix_prenorm()
        ffn_norms()
        _emit(in_tiles[:n_qkvg])
        _emit_interleaved(in_tiles[n_qkvg:] + hidden_blocks, ret_items)
        conv_groups()
        _emit(down_tiles)
        out_proj()
        ffn_finish()

    @pl.when(step == n_tiles)
    def _():
        ffn_norms()
        _emit(hidden_blocks)
        _emit(down_tiles)
        ffn_finish()


def _prompt_layer(layer, prev, x, mod_p, g_pm, g_qm, g_pf, g_qf, w_in, w_out, w1, w3, w2, tables,
                  gn, caw, cab, lng, lnb, ccw):
    bsz, seq, _ = x.shape
    tile = TOK_TILE
    tps = seq // tile
    n_tiles = bsz * tps
    rope_tab, dmat, rd, upd, cd, _ = tables
    cur = lambda i: jnp.minimum(i, n_tiles - 1)
    prv = lambda i: jnp.maximum(i - 1, 0)
    tok_cur = pl.BlockSpec((None, tile, D_MODEL), lambda i: (cur(i) // tps, cur(i) % tps, 0))
    tok_prev = pl.BlockSpec((None, tile, D_MODEL), lambda i: (prv(i) // tps, prv(i) % tps, 0))
    mod_cur = pl.BlockSpec((None, None, 1, 6 * D_MODEL), lambda i: (layer, cur(i) // tps, 0, 0))
    mod_prev = pl.BlockSpec((None, None, 1, 6 * D_MODEL), lambda i: (layer, prv(i) // tps, 0, 0))
    rope = pl.BlockSpec((tile, 4 * LANES), lambda i: (cur(i) % tps, 0))
    cs = _const_spec
    ls = functools.partial(_layer_spec, layer=layer)
    hbm = pl.BlockSpec(memory_space=pl.ANY)
    assert n_tiles >= 2
    args = (x, x, mod_p, mod_p, g_pm, g_qm, g_pf, g_qf, w_in, w_out, w1, w3, w2,
            rope_tab, dmat, rd, upd, cd, gn, caw, cab, lng, lnb, ccw)
    seq_block = lambda i: (cur(i) // tps,)
    return _stacked_call(
        functools.partial(_prompt_layer_kernel, tile=tile, n_tiles=n_tiles, tiles_per_seq=tps, layer=layer),
        name="prompt_layer", grid=(n_tiles + 1,), args=args, layer=layer, prev=prev,
        stacked=[
            ((bsz, RET_HEADS, RET_DK, RET_DV), (None, RET_HEADS, RET_DK, RET_DV),
             lambda i: seq_block(i) + (0, 0, 0)),
            ((bsz, CONV_A_K - 1, CONV_A_WIDTH), (None, CONV_A_K - 1, CONV_A_WIDTH),
             lambda i: seq_block(i) + (0, 0)),
            ((bsz, CONV_C_K - 1, CONV_C_WIDTH), (None, CONV_C_K - 1, CONV_C_WIDTH),
             lambda i: seq_block(i) + (0, 0)),
        ],
        in_specs=[
            tok_cur, tok_prev, mod_cur, mod_prev,
            cs((DEPTH, D_MODEL)), cs((DEPTH, D_MODEL)), cs((DEPTH, D_MODEL)), cs((DEPTH, D_MODEL)),
            cs((D_MODEL, IN_WIDTH)), cs((D_MODEL, D_MODEL)),
            hbm, hbm, hbm,
            rope,
            cs(dmat.shape), cs(rd.shape), cs(upd.shape), cs(cd.shape),
            cs((DEPTH, RET_WIDTH)), ls((CONV_A_K, CONV_A_WIDTH)), cs((DEPTH, CONV_A_WIDTH)),
            cs((DEPTH, CONV_A_WIDTH)), cs((DEPTH, CONV_A_WIDTH)), ls((CONV_C_K, CONV_C_WIDTH)),
        ],
        out_specs=[tok_prev],
        out_shape=[jax.ShapeDtypeStruct(x.shape, F32)],
        scratch_shapes=[
            pltpu.VMEM((tile, D_MODEL), BF16),
            pltpu.VMEM((tile, IN_WIDTH), F32),
            pltpu.VMEM((tile, D_MODEL), BF16),
            pltpu.VMEM((tile, D_MODEL), F32),
            pltpu.VMEM((tile, D_MODEL), BF16),
            pltpu.VMEM((tile, D_FF), BF16),
            pltpu.VMEM((2, tile + HIST_A, LANES), F32),
            pltpu.VMEM((2, tile, LANES), F32),
            pltpu.VMEM((tile + HIST_C, CONV_C_WIDTH), F32),
            pltpu.VMEM((2, RET_CHUNK, 2 * RET_CHUNK), BF16),
            pltpu.VMEM((2, RET_CHUNK, 2 * RET_DV), F32),
            pltpu.VMEM((RET_HEADS // 2, 2 * RET_DK, 2 * RET_DV), F32),
            pltpu.VMEM((D_MODEL, D_FF), BF16),
            pltpu.VMEM((D_MODEL, D_FF), BF16),
            pltpu.VMEM((D_FF, D_MODEL), BF16),
            pltpu.SemaphoreType.DMA((3,)),
        ])


def _sample_pre_kernel(x_ref, mod_ref, gpre_ref, win_ref, rope_ref, after_ref, proj_ref, qt_ref, kt_ref, winb_ref,
                       h_scr, *, layer):
    j = pl.program_id(0)
    n = x_ref.shape[0]

    @pl.when(j == 0)
    def _():
        sh = mod_ref[:, 0:D_MODEL]
        sc1 = 1.0 + mod_ref[:, D_MODEL:2 * D_MODEL]
        h_scr[...] = (_rms(x_ref[...]) * (gpre_ref[pl.ds(layer, 1), :] * sc1) + sh).astype(BF16)

    w_blk = win_ref[...].astype(BF16)
    winb_ref[...] = w_blk
    proj_ref[...] = jnp.dot(h_scr[...], w_blk, preferred_element_type=F32)

    def rope_transposed(off, cos, sin, dst_ref):
        first_half, _ = _first_half_mask(n)
        for pair in range(RET_HEADS // 2):
            src = proj_ref[:, pl.ds(off + pair * LANES, LANES)]
            dst_ref[pl.ds(pair * LANES, LANES), :] = _rope_half(src, cos, sin, first_half).T

    @pl.when(j == 0)
    def _():
        rope_transposed(OFF_Q, rope_ref[0:1, :], rope_ref[1:2, :], qt_ref)
        rope_transposed(OFF_K, rope_ref[2:3, :], rope_ref[3:4, :], kt_ref)


def _sample_pre(layer, xs, mod, gpre, w_in, rope_s, after):
    n = xs.shape[0]
    cs = functools.partial(_const_spec, single_buffer=False)
    ls = functools.partial(_layer_spec, layer=layer, single_buffer=False)
    return pl.pallas_call(
        functools.partial(_sample_pre_kernel, layer=layer),
        grid=(IN_WIDTH // PRE_BLK,),
        in_specs=[cs((n, D_MODEL)), ls((n, 6 * D_MODEL)), cs((DEPTH, D_MODEL)),
                  pl.BlockSpec((None, D_MODEL, PRE_BLK), lambda j: (layer, 0, j)), cs(rope_s.shape),
                  pl.BlockSpec(memory_space=pl.ANY)],
        out_specs=[pl.BlockSpec((n, PRE_BLK), lambda j: (0, j)), cs((QK_WIDTH, n)), cs((QK_WIDTH, n)),
                   pl.BlockSpec((D_MODEL, PRE_BLK), lambda j: (0, j))],
        out_shape=[jax.ShapeDtypeStruct((n, IN_WIDTH), F32),
                   jax.ShapeDtypeStruct((QK_WIDTH, n), F32),
                   jax.ShapeDtypeStruct((QK_WIDTH, n), F32),
                   jax.ShapeDtypeStruct((D_MODEL, IN_WIDTH), BF16)],
        scratch_shapes=[pltpu.VMEM((n, D_MODEL), BF16)],
        compiler_params=_params(1),
        name="sample_pre",
    )(xs, mod, gpre, w_in, rope_s, after)


def _sample_state_kernel(*refs, layer):
    (proj_ref, qt_ref, kt_ref, sin_ref, bufa_ref, bufc_ref, cd_ref,
     gn_ref, caw_ref, cab_ref, lng_ref, lnb_ref, ccw_ref) = refs[:13]
    n_prev = 3 if layer > 0 else 0
    prev_states = refs[13:13 + n_prev]
    mix_ref, sout_all, outa_all, outc_all, o_scr = refs[13 + n_prev:]
    for prev_ref, all_ref in zip(prev_states, (sout_all, outa_all, outc_all)):
        all_ref[0:layer] = prev_ref[...]
    sout_ref, outa_ref, outc_ref = sout_all.at[layer], outa_all.at[layer], outc_all.at[layer]
    gn_ref, cab_ref, lng_ref, lnb_ref = (ref.at[pl.ds(layer, 1)] for ref in (gn_ref, cab_ref, lng_ref, lnb_ref))
    blk = pl.program_id(0)
    n = qt_ref.shape[1]
    lane = lax.broadcasted_iota(jnp.int32, (QK_WIDTH, n), 1)

    for bl in range(SEQ_BLK):
        onehot = lane == (blk * SEQ_BLK + bl)
        qcol = jnp.sum(jnp.where(onehot, qt_ref[...], 0.0), axis=1, keepdims=True)
        kcol = jnp.sum(jnp.where(onehot, kt_ref[...], 0.0), axis=1, keepdims=True)
        row = pl.ds(bl, 1)
        for h in range(RET_HEADS):
            vrow = proj_ref[row, pl.ds(OFF_V + h * RET_DV, RET_DV)]
            s_new = (sin_ref[bl, h] * cd_ref[h]
                     + kcol[h * RET_DK:(h + 1) * RET_DK, :] * vrow)
            sout_ref[bl, h] = s_new
            o_scr[row, pl.ds(h * RET_DV, RET_DV)] = jnp.sum(
                qcol[h * RET_DK:(h + 1) * RET_DK, :] * s_new, axis=0, keepdims=True)
    for h in range(RET_HEADS):
        cols = pl.ds(h * RET_DV, RET_DV)
        o = _layernorm(o_scr[:, cols], gn_ref[:, cols]) * _silu(proj_ref[:, pl.ds(OFF_G + h * RET_DV, RET_DV)])
        mix_ref[:, cols] = o

    u = proj_ref[:, OFF_AV:OFF_AV + CONV_A_WIDTH] * _sigmoid(proj_ref[:, OFF_AG:OFF_AG + CONV_A_WIDTH])
    ua = caw_ref[CONV_A_K - 1:CONV_A_K, :] * u
    for j in range(CONV_A_K - 1):
        ua = ua + caw_ref[j:j + 1, :] * bufa_ref[j]
    outa_ref[0:CONV_A_K - 2] = bufa_ref[1:CONV_A_K - 1]
    outa_ref[CONV_A_K - 2] = u
    mix_ref[:, MIX_A:MIX_A + CONV_A_WIDTH] = _silu(_layernorm(ua + cab_ref[...], lng_ref[...], lnb_ref[...]))

    z = proj_ref[:, OFF_CC:OFF_CC + CONV_C_WIDTH] * proj_ref[:, OFF_CX:OFF_CX + CONV_C_WIDTH]
    zc = ccw_ref[0:1, :] * bufc_ref[0] + ccw_ref[1:2, :] * bufc_ref[1] + ccw_ref[2:3, :] * z
    outc_ref[0] = bufc_ref[1]
    outc_ref[1] = z
    mix_ref[:, MIX_C:MIX_C + CONV_C_WIDTH] = proj_ref[:, OFF_CB:OFF_CB + CONV_C_WIDTH] * zc


def _sample_state(layer, prev, proj, qt, kt, s_ret, buf_a_t, buf_c_t, cd, gn, caw, cab, lng, lnb, ccw):
    n = proj.shape[0]
    cs = functools.partial(_const_spec, single_buffer=False)
    ls = functools.partial(_layer_spec, layer=layer, single_buffer=False)
    s_spec = pl.BlockSpec((None, SEQ_BLK, RET_HEADS, RET_DK, RET_DV), lambda j: (layer, j, 0, 0, 0))
    a_spec = pl.BlockSpec((None, CONV_A_K - 1, SEQ_BLK, CONV_A_WIDTH), lambda j: (layer, 0, j, 0))
    c_spec = pl.BlockSpec((None, CONV_C_K - 1, SEQ_BLK, CONV_C_WIDTH), lambda j: (layer, 0, j, 0))
    args = (proj, qt, kt, s_ret, buf_a_t, buf_c_t, cd, gn, caw, cab, lng, lnb, ccw)
    return _stacked_call(
        functools.partial(_sample_state_kernel, layer=layer),
        name="sample_state", grid=(n // SEQ_BLK,), args=args, layer=layer, prev=prev,
        stacked=[
            ((n, RET_HEADS, RET_DK, RET_DV), (SEQ_BLK, RET_HEADS, RET_DK, RET_DV), lambda j: (j, 0, 0, 0)),
            ((CONV_A_K - 1, n, CONV_A_WIDTH), (CONV_A_K - 1, SEQ_BLK, CONV_A_WIDTH), lambda j: (0, j, 0)),
            ((CONV_C_K - 1, n, CONV_C_WIDTH), (CONV_C_K - 1, SEQ_BLK, CONV_C_WIDTH), lambda j: (0, j, 0)),
        ],
        in_specs=[
            pl.BlockSpec((SEQ_BLK, IN_WIDTH), lambda j: (j, 0)),
            cs((QK_WIDTH, n)), cs((QK_WIDTH, n)),
            s_spec, a_spec, c_spec,
            cs(cd.shape), cs((DEPTH, RET_WIDTH)), ls((CONV_A_K, CONV_A_WIDTH)), cs((DEPTH, CONV_A_WIDTH)),
            cs((DEPTH, CONV_A_WIDTH)), cs((DEPTH, CONV_A_WIDTH)), ls((CONV_C_K, CONV_C_WIDTH)),
        ],
        out_specs=[pl.BlockSpec((SEQ_BLK, D_MODEL), lambda j: (j, 0))],
        out_shape=[jax.ShapeDtypeStruct((n, D_MODEL), F32)],
        scratch_shapes=[pltpu.VMEM((SEQ_BLK, RET_WIDTH), F32)])


def _sample_post_kernel(x_ref, mod_ref, mix_ref, wout_ref, gpm_ref, gpf_ref, gqf_ref, w1_ref, w3_ref, w2_ref,
                        y_ref, woutb_ref, w1b_ref, w3b_ref, w2b_ref, x1_scr, h_scr, f_scr, *, n_steps, layer):
    j = pl.program_id(0)
    gpm_ref, gpf_ref, gqf_ref = (ref.at[pl.ds(layer, 1)] for ref in (gpm_ref, gpf_ref, gqf_ref))

    @pl.when(j == 0)
    def _():
        gt_m = mod_ref[:, 2 * D_MODEL:3 * D_MODEL]
        wout_b = wout_ref[...].astype(BF16)
        woutb_ref[...] = wout_b
        mix = jnp.dot(mix_ref[...].astype(BF16), wout_b, preferred_element_type=F32)
        x1 = x_ref[...] + _rms(mix) * (gpm_ref[...] * gt_m)
        x1_scr[...] = x1
        sh = mod_ref[:, 3 * D_MODEL:4 * D_MODEL]
        sc1 = 1.0 + mod_ref[:, 4 * D_MODEL:5 * D_MODEL]
        h_scr[...] = (_rms(x1) * (gpf_ref[...] * sc1) + sh).astype(BF16)
        f_scr[...] = jnp.zeros_like(f_scr)

    w1_b, w3_b, w2_b = w1_ref[...].astype(BF16), w3_ref[...].astype(BF16), w2_ref[...].astype(BF16)
    w1b_ref[...] = w1_b
    w3b_ref[...] = w3_b
    w2b_ref[...] = w2_b
    h = h_scr[...]
    a = jnp.dot(h, w1_b, preferred_element_type=F32)
    b = jnp.dot(h, w3_b, preferred_element_type=F32)
    p = (_silu(a) * b).astype(BF16)
    f_scr[...] += jnp.dot(p, w2_b, preferred_element_type=F32)

    @pl.when(j == n_steps - 1)
    def _():
        gt_f = mod_ref[:, 5 * D_MODEL:6 * D_MODEL]
        y_ref[...] = x1_scr[...] + _rms(f_scr[...]) * (gqf_ref[...] * gt_f)


def _sample_post(layer, xs, mod, mix, w_out, gpost_m, gpre_f, gpost_f, w1, w3, w2):
    n = xs.shape[0]
    n_steps = D_FF // FF_BLK
    cs = functools.partial(_const_spec, single_buffer=False)
    ls = functools.partial(_layer_spec, layer=layer, single_buffer=False)
    return pl.pallas_call(
        functools.partial(_sample_post_kernel, n_steps=n_steps, layer=layer),
        grid=(n_steps,),
        in_specs=[
            cs((n, D_MODEL)), ls((n, 6 * D_MODEL)), cs((n, D_MODEL)), ls((D_MODEL, D_MODEL)),
            cs((DEPTH, D_MODEL)), cs((DEPTH, D_MODEL)), cs((DEPTH, D_MODEL)),
            pl.BlockSpec((None, D_MODEL, FF_BLK), lambda j: (layer, 0, j)),
            pl.BlockSpec((None, D_MODEL, FF_BLK), lambda j: (layer, 0, j)),
            pl.BlockSpec((None, FF_BLK, D_MODEL), lambda j: (layer, j, 0)),
        ],
        out_specs=[
            cs((n, D_MODEL)), cs((D_MODEL, D_MODEL)),
            pl.BlockSpec((D_MODEL, FF_BLK), lambda j: (0, j)),
            pl.BlockSpec((D_MODEL, FF_BLK), lambda j: (0, j)),
            pl.BlockSpec((FF_BLK, D_MODEL), lambda j: (j, 0)),
        ],
        out_shape=[
            jax.ShapeDtypeStruct((n, D_MODEL), F32),
            jax.ShapeDtypeStruct((D_MODEL, D_MODEL), BF16),
            jax.ShapeDtypeStruct((D_MODEL, D_FF), BF16),
            jax.ShapeDtypeStruct((D_MODEL, D_FF), BF16),
            jax.ShapeDtypeStruct((D_FF, D_MODEL), BF16),
        ],
        scratch_shapes=[
            pltpu.VMEM((n, D_MODEL), F32),
            pltpu.VMEM((n, D_MODEL), BF16),
            pltpu.VMEM((n, D_MODEL), F32),
        ],
        compiler_params=_params(1),
        name="sample_post",
    )(xs, mod, mix, w_out, gpost_m, gpre_f, gpost_f, w1, w3, w2)


def kernel(x_prompt, x_sample, c_prompt, c_sample, state_ret, state_conv_a, state_conv_c, ada_w, ada_b, norm_pre_mix, norm_post_mix, norm_pre_ffn, norm_post_ffn, w_in, w_out, ret_gn_g, conv_a_w, conv_a_b, conv_a_ln_g, conv_a_ln_b, conv_c_w, ffn_w1, ffn_w3, ffn_w2):
    bp, lp, _ = x_prompt.shape
    ns = x_sample.shape[0]
    assert x_sample.shape[1] == 1 and ns % SEQ_BLK == 0
    assert OFF_K + QK_WIDTH <= PRE_BLK and IN_WIDTH % PRE_BLK == 0
    assert lp % TOK_TILE == 0 and TOK_TILE % RET_CHUNK == 0
    assert TOK_TILE % CONV_BLK == 0 and CONV_BLK % (SUBLANES * CONV_STRIDE) == 0

    k_scale = RET_DK ** -0.5
    cq, sq = _rope_tables(np.arange(lp), 1.0)
    ck, sk = _rope_tables(np.arange(lp), k_scale)
    tables_p = (np.concatenate([cq, sq, ck, sk], axis=1),) + _decay_tables(RET_CHUNK)
    cqs, sqs = _rope_tables([PAST_LEN], 1.0)
    cks, sks = _rope_tables([PAST_LEN], k_scale)
    rope_s = np.concatenate([cqs, sqs, cks, sks, np.zeros((4, LANES), np.float32)], axis=0)
    cd_s = _decay_tables(1)[4]

    mod_s, mod_p = _ada_modulation(c_sample, c_prompt, ada_w, ada_b)

    gn, cab, lng, lnb = ret_gn_g, conv_a_b, conv_a_ln_g, conv_a_ln_b
    g_pm, g_qm, g_pf, g_qf = norm_pre_mix, norm_post_mix, norm_pre_ffn, norm_post_ffn
    conv_a_t = jnp.transpose(state_conv_a, (0, 2, 1, 3))
    conv_c_t = jnp.transpose(state_conv_c, (0, 2, 1, 3))

    yp = x_prompt
    ys = x_sample.reshape(ns, D_MODEL)
    st_p = st_s = ()
    for l in range(DEPTH):
        proj, qt, kt, w_in_b = _sample_pre(l, ys, mod_s, g_pm, w_in, rope_s, yp)
        mix, *st_s = _sample_state(l, st_s, proj, qt, kt, state_ret, conv_a_t, conv_c_t, cd_s,
                                   gn, conv_a_w, cab, lng, lnb, conv_c_w)
        ys, w_out_b, w1_b, w3_b, w2_b = _sample_post(l, ys, mod_s, mix, w_out, g_qm, g_pf, g_qf,
                                                    ffn_w1, ffn_w3, ffn_w2)
        yp, *st_p = _prompt_layer(l, st_p, yp, mod_p, g_pm, g_qm, g_pf, g_qf, w_in_b, w_out_b, w1_b, w3_b, w2_b,
                                  tables_p, gn, conv_a_w, cab, lng, lnb, conv_c_w)

    sret_s, sca_s, scc_s = st_s
    sca_s = jnp.transpose(sca_s, (0, 2, 1, 3))
    scc_s = jnp.transpose(scc_s, (0, 2, 1, 3))
    return (yp, ys.reshape(ns, 1, D_MODEL)) + tuple(st_p) + (sret_s, sca_s, scc_s)
```
